```python
import jax
import jax.numpy as jnp
from jax import lax
import numpy as np

D_MODEL = 1024
BATCH = 4
SEQ = 8192
DEPTH = 2

GRID_W = 64
CTX_LEN = 256
NORM_EPS = 1e-6
LRU_WIDTH = 512
LRU_BLOCKS = 8
LRU_BLOCK = LRU_WIDTH // LRU_BLOCKS
LRU_C = 8.0
CONV_W = 4
N_Q_HEADS = 8
N_KV_HEADS = 2
HEAD_DIM = 64
Q_GROUP = N_Q_HEADS // N_KV_HEADS
Q_BLOCK = 128
ROPE_BASE = 10000.0
ATTN_SCALE = HEAD_DIM ** -0.5
IN_WIDTH = 2 * LRU_WIDTH + (N_Q_HEADS + 2 * N_KV_HEADS) * HEAD_DIM
MIX_WIDTH = LRU_WIDTH + N_Q_HEADS * HEAD_DIM
RWKV_HEAD = 64
RWKV_HEADS = D_MODEL // RWKV_HEAD
DECAY_LORA = 64
AAA_LORA = 64
GATE_LORA = 160
GN_EPS = 64e-5
N_EXPERTS = 16
EXPERT_FF = 2048
CAPACITY_FACTOR = 2

kernel_name = 'hybrid_rglru_gqa_rwkv7_ecmoe_dit'


def rms_norm(x, gain):
    xf = x.astype(jnp.float32)
    y = xf * lax.rsqrt(jnp.mean(xf * xf, axis=-1, keepdims=True) + NORM_EPS)
    return (y * gain.astype(jnp.float32)).astype(x.dtype)


def ada_params(cond, w_mod, b_mod):
    m = jax.nn.silu(cond) @ w_mod + b_mod
    return jnp.split(m[:, None, :], 6, axis=-1)


def axial_rope(n_tokens):
    n_rows = n_tokens // GRID_W
    row = jnp.repeat(jnp.arange(n_rows, dtype=jnp.float32), GRID_W)
    col = jnp.tile(jnp.arange(GRID_W, dtype=jnp.float32), n_rows)
    axis_dim = HEAD_DIM // 2
    inv_freq = ROPE_BASE ** (-jnp.arange(0, axis_dim, 2, dtype=jnp.float32) / axis_dim)
    ang = jnp.concatenate([row[:, None] * inv_freq, col[:, None] * inv_freq], axis=-1)
    return jnp.cos(ang), jnp.sin(ang)


def apply_rope(x, cos, sin):
    xf = x.astype(jnp.float32)
    x1, x2 = xf[..., 0::2], xf[..., 1::2]
    c = cos[None, :, None, :]
    s = sin[None, :, None, :]
    out = jnp.stack([x1 * c - x2 * s, x1 * s + x2 * c], axis=-1).reshape(x.shape)
    return out.astype(x.dtype)


def attend(q, k, v):
    s = jnp.einsum('bhgqd,bkhd->bhgqk', q, k).astype(jnp.float32) * ATTN_SCALE
    p = jax.nn.softmax(s, axis=-1).astype(v.dtype)
    return jnp.einsum('bhgqk,bkhd->bhgqd', p, v)


def centred_depthwise_conv(x, w, b):
    left = CONV_W // 2
    y = lax.conv_general_dilated(x, w[:, None, :].astype(x.dtype), window_strides=(1,),
                                 padding=[(left, CONV_W - 1 - left)],
                                 dimension_numbers=('NWC', 'WIO', 'NWC'),
                                 feature_group_count=x.shape[-1])
    return y + b


def rglru_coeffs(xc, w_i, b_i, w_r, b_r, lam):
    B, T, _ = xc.shape
    f32 = jnp.float32
    xf = xc.astype(f32)
    xb = xf.reshape(B, T, LRU_BLOCKS, LRU_BLOCK)
    i_gate = jax.nn.sigmoid(jnp.einsum('btnd,nde->btne', xb, w_i.astype(f32)).reshape(B, T, LRU_WIDTH) + b_i)
    r_gate = jax.nn.sigmoid(jnp.einsum('btnd,nde->btne', xb, w_r.astype(f32)).reshape(B, T, LRU_WIDTH) + b_r)
    log_a = LRU_C * r_gate * jax.nn.log_sigmoid(lam.astype(f32))
    a = jnp.exp(log_a)
    b = jnp.sqrt(-jnp.expm1(2.0 * log_a)) * (i_gate * xf)
    return a, b


def linear_scan(a, b, h0, reverse):
    i0 = -1 if reverse else 0
    b = b.at[:, i0].add(a[:, i0] * h0)

    def combine(left, right):
        a1, b1 = left
        a2, b2 = right
        return a1 * a2, a2 * b1 + b2

    _, h = lax.associative_scan(combine, (a, b), reverse=reverse, axis=1)
    return h


def mixer_rglru_gqa(h_lat, h_ctx, w_in, conv_w, conv_b, lru_wi, lru_bi, lru_wr, lru_br, lru_lam,
                    q_gain, k_gain, w_out, ctx_out=True):
    f32 = jnp.float32
    cuts = np.cumsum([LRU_WIDTH, LRU_WIDTH, N_Q_HEADS * HEAD_DIM, N_KV_HEADS * HEAD_DIM]).tolist()

    def project(h):
        B, T, _ = h.shape
        xa, ga, q, k, v = jnp.split(h @ w_in, cuts, axis=-1)
        q = rms_norm(q.reshape(B, T, N_Q_HEADS, HEAD_DIM), q_gain)
        k = rms_norm(k.reshape(B, T, N_KV_HEADS, HEAD_DIM), k_gain)
        v = v.reshape(B, T, N_KV_HEADS, HEAD_DIM)
        return centred_depthwise_conv(xa, conv_w, conv_b), ga, q, k, v

    xc_c, ga_c, q_c, k_c, v_c = project(h_ctx)
    xc_l, ga_l, q_l, k_l, v_l = project(h_lat)
    B, T, _ = h_lat.shape
    cos, sin = axial_rope(T)
    q_l = apply_rope(q_l, cos, sin)
    k_l = apply_rope(k_l, cos, sin)

    rec_l = jnp.zeros(xc_l.shape, f32)
    rec_c = jnp.zeros(xc_c.shape, f32)
    for d, reverse in enumerate((False, True)):
        a_c, b_c = rglru_coeffs(xc_c, lru_wi[d], lru_bi[d], lru_wr[d], lru_br[d], lru_lam[d])
        hs_c = linear_scan(a_c, b_c, jnp.zeros((B, LRU_WIDTH), f32), reverse)
        h_last = hs_c[:, 0] if reverse else hs_c[:, -1]
        a_l, b_l = rglru_coeffs(xc_l, lru_wi[d], lru_bi[d], lru_wr[d], lru_br[d], lru_lam[d])
        rec_l = rec_l + linear_scan(a_l, b_l, h_last, reverse)
        if ctx_out:
            rec_c = rec_c + hs_c
    y_a_l = rec_l.astype(h_lat.dtype) * jax.nn.gelu(ga_l)

    k_all = jnp.concatenate([k_l, k_c], axis=1)
    v_all = jnp.concatenate([v_l, v_c], axis=1)
    n_blocks = T // Q_BLOCK
    qb = q_l.reshape(B, n_blocks, Q_BLOCK, N_KV_HEADS, Q_GROUP, HEAD_DIM).transpose(1, 0, 3, 4, 2, 5)
    o = lax.map(lambda blk: attend(blk, k_all, v_all), qb)
    y_b_l = o.transpose(1, 0, 4, 2, 3, 5).reshape(B, T, N_Q_HEADS * HEAD_DIM)
    y_lat = jnp.concatenate([y_a_l, y_b_l], axis=-1) @ w_out
    if not ctx_out:
        return y_lat, None

    Lc = h_ctx.shape[1]
    qc = q_c.reshape(B, Lc, N_KV_HEADS, Q_GROUP, HEAD_DIM).transpose(0, 2, 3, 1, 4)
    y_b_c = attend(qc, k_c, v_c).transpose(0, 3, 1, 2, 4).reshape(B, Lc, N_Q_HEADS * HEAD_DIM)
    y_a_c = rec_c.astype(h_ctx.dtype) * jax.nn.gelu(ga_c)
    y_ctx = jnp.concatenate([y_a_c, y_b_c], axis=-1) @ w_out
    return y_lat, y_ctx


def centred_shift_delta(h):
    hp = jnp.pad(h, ((0, 0), (1, 1), (0, 0)))
    return 0.5 * (hp[:, :-2] + hp[:, 2:]) - h


def wkv_scan(S0, seqs, reverse, emit):
    xs = tuple(jnp.swapaxes(t, 0, 1) for t in seqs)

    def step(S, inp):
        r, w, k, v, a, b = inp
        sa = jnp.einsum('bhvk,bhk->bhv', S, a)
        S = S * w[:, :, None, :] + sa[..., None] * b[:, :, None, :] + v[..., None] * k[:, :, None, :]
        y = jnp.einsum('bhvk,bhk->bhv', S, r) if emit else None
        return S, y

    S, ys = lax.scan(step, S0, xs, reverse=reverse)
    return S, (jnp.swapaxes(ys, 0, 1) if emit else None)


def mixer_rwkv7(h_lat, h_ctx, mu, w_r, w_k, w_v, w0, w1, w2, a0, a1, a2, g1, g2, k_k, k_a, r_k,
                gn_g, gn_b, w_o, ctx_out=True):
    f32 = jnp.float32
    kk_scale = k_k.astype(f32).reshape(RWKV_HEADS, RWKV_HEAD)
    ka = k_a.astype(f32).reshape(RWKV_HEADS, RWKV_HEAD)
    rk = r_k.astype(f32)

    def prepare(h):
        B, T, _ = h.shape
        xx = centred_shift_delta(h)
        xr, xw, xk, xv, xa, xg = [h + xx * mu[j] for j in range(6)]
        heads = lambda t: t.astype(f32).reshape(B, T, RWKV_HEADS, RWKV_HEAD)
        r = heads(xr @ w_r)
        k = heads(xk @ w_k)
        v = heads(xv @ w_v)
        g = jax.nn.sigmoid(xg @ g1) @ g2
        kk = k * kk_scale
        kk = kk * lax.rsqrt(jnp.maximum(jnp.sum(kk * kk, axis=-1, keepdims=True), 1e-24))
        per_dir = []
        for d in range(2):
            w_pre = (w0[d] + jnp.tanh(xw @ w1[d]) @ w2[d]).astype(f32)
            decay = heads(jnp.exp(-jnp.exp(-jax.nn.softplus(-w_pre) - 0.5)))
            a = heads(jax.nn.sigmoid(a0[d] + (xa @ a1[d]) @ a2[d]))
            per_dir.append((decay, k * (1.0 + (a - 1.0) * ka), -kk, kk * a))
        return r, v, g, per_dir

    def read_out(y, bonus, g):
        B, T = y.shape[:2]
        mean = jnp.mean(y, axis=-1, keepdims=True)
        var = jnp.mean(jnp.square(y - mean), axis=-1, keepdims=True)
        yn = ((y - mean) * lax.rsqrt(var + GN_EPS)).reshape(B, T, D_MODEL) * gn_g + gn_b
        out = (yn + bonus.reshape(B, T, D_MODEL)) * g
        return out.astype(g.dtype) @ w_o

    r_c, v_c, g_c, dirs_c = prepare(h_ctx)
    r_l, v_l, g_l, dirs_l = prepare(h_lat)
    B = h_lat.shape[0]
    S_zero = jnp.zeros((B, RWKV_HEADS, RWKV_HEAD, RWKV_HEAD), f32)
    y_l = jnp.zeros_like(v_l)
    bonus_l = jnp.zeros_like(v_l)
    y_c = jnp.zeros_like(v_c)
    bonus_c = jnp.zeros_like(v_c)
    for d, reverse in enumerate((False, True)):
        dec_c, kd_c, ia_c, ib_c = dirs_c[d]
        S_c, ys_c = wkv_scan(S_zero, (r_c, dec_c, kd_c, v_c, ia_c, ib_c), reverse, ctx_out)
        dec_l, kd_l, ia_l, ib_l = dirs_l[d]
        _, ys_l = wkv_scan(S_c, (r_l, dec_l, kd_l, v_l, ia_l, ib_l), reverse, True)
        y_l = y_l + ys_l
        bonus_l = bonus_l + jnp.sum(r_l * kd_l * rk, axis=-1, keepdims=True) * v_l
        if ctx_out:
            y_c = y_c + ys_c
            bonus_c = bonus_c + jnp.sum(r_c * kd_c * rk, axis=-1, keepdims=True) * v_c
    y_lat = read_out(y_l, bonus_l, g_l)
    if not ctx_out:
        return y_lat, None
    return y_lat, read_out(y_c, bonus_c, g_c)


def expert_choice_ffn(h, w_router, we_gate, we_up, we_down):
    N = h.shape[1]
    cap = max(1, CAPACITY_FACTOR * N // N_EXPERTS)

    def route(hs):
        aff = jax.nn.softmax((hs @ w_router).astype(jnp.float32), axis=-1)
        gate, idx = lax.top_k(aff.T, cap)
        xe = hs[idx]
        hid = jax.nn.silu(jnp.einsum('ecd,edf->ecf', xe, we_gate)) * jnp.einsum('ecd,edf->ecf', xe, we_up)
        ye = jnp.einsum('ecf,efd->ecd', hid, we_down) * gate[..., None].astype(hs.dtype)
        return jnp.zeros_like(hs).at[idx.reshape(-1)].add(ye.reshape(-1, hs.shape[-1]))

    return lax.map(route, h)


def setup_inputs(seed: int = 0) -> dict:
    key = jax.random.key(seed)
    keys = iter(jax.random.split(key, 64))
    f32 = jnp.float32
    D = D_MODEL

    def nrm(shape, scale):
        return jax.random.normal(next(keys), shape, f32) * scale

    def gain(shape):
        return 1.0 + nrm(shape, 0.01)

    inp = {}
    inp['x'] = nrm((BATCH, SEQ, D), 1.0)
    inp['c'] = nrm((BATCH, D), 1.0)
    inp['ctx'] = nrm((BATCH, CTX_LEN, D), 1.0)
    inp['c_ctx'] = nrm((D,), 1.0)

    def common(p):
        inp[p + 'w_mod'] = nrm((D, 6 * D), 0.5 * D ** -0.5)
        inp[p + 'b_mod'] = nrm((6 * D,), 0.01)
        inp[p + 'norm_mix'] = gain((D,))
        inp[p + 'norm_ffn'] = gain((D,))

    def moe(p):
        inp[p + 'router'] = nrm((D, N_EXPERTS), D ** -0.5)
        inp[p + 'we_gate'] = nrm((N_EXPERTS, D, EXPERT_FF), D ** -0.5)
        inp[p + 'we_up'] = nrm((N_EXPERTS, D, EXPERT_FF), D ** -0.5)
        inp[p + 'we_down'] = nrm((N_EXPERTS, EXPERT_FF, D), EXPERT_FF ** -0.5)

    common('l0_')
    inp['l0_w_in'] = nrm((D, IN_WIDTH), D ** -0.5)
    inp['l0_conv_w'] = nrm((CONV_W, LRU_WIDTH), CONV_W ** -0.5)
    inp['l0_conv_b'] = nrm((LRU_WIDTH,), 0.01)
    inp['l0_lru_wi'] = nrm((2, LRU_BLOCKS, LRU_BLOCK, LRU_BLOCK), LRU_BLOCK ** -0.5)
    inp['l0_lru_bi'] = nrm((2, LRU_WIDTH), 0.01)
    inp['l0_lru_wr'] = nrm((2, LRU_BLOCKS, LRU_BLOCK, LRU_BLOCK), LRU_BLOCK ** -0.5)
    inp['l0_lru_br'] = nrm((2, LRU_WIDTH), 0.01)
    a_c = jax.random.uniform(next(keys), (2, LRU_WIDTH), f32, 0.9, 0.999)
    sig = a_c ** (1.0 / LRU_C)
    inp['l0_lru_lam'] = jnp.log(sig) - jnp.log1p(-sig)
    inp['l0_q_gain'] = gain((HEAD_DIM,))
    inp['l0_k_gain'] = gain((HEAD_DIM,))
    inp['l0_w_out'] = nrm((MIX_WIDTH, D), MIX_WIDTH ** -0.5)
    moe('l0_')

    common('l1_')
    inp['l1_mu'] = jax.random.uniform(next(keys), (6, D), f32)
    inp['l1_w_r'] = nrm((D, D), D ** -0.5)
    inp['l1_w_k'] = nrm((D, D), D ** -0.5)
    inp['l1_w_v'] = nrm((D, D), D ** -0.5)
    inp['l1_w0'] = jax.random.uniform(next(keys), (2, D), f32, -6.0, -1.0)
    inp['l1_w1'] = nrm((2, D, DECAY_LORA), D ** -0.5)
    inp['l1_w2'] = nrm((2, DECAY_LORA, D), 0.1 * DECAY_LORA ** -0.5)
    inp['l1_a0'] = nrm((2, D), 0.1)
    inp['l1_a1'] = nrm((2, D, AAA_LORA), D ** -0.5)
    inp['l1_a2'] = nrm((2, AAA_LORA, D), 0.1 * AAA_LORA ** -0.5)
    inp['l1_g1'] = nrm((D, GATE_LORA), D ** -0.5)
    inp['l1_g2'] = nrm((GATE_LORA, D), GATE_LORA ** -0.5)
    inp['l1_k_k'] = 1.0 + nrm((D,), 0.1)
    inp['l1_k_a'] = 1.0 + nrm((D,), 0.1)
    inp['l1_r_k'] = nrm((RWKV_HEADS, RWKV_HEAD), 0.1)
    inp['l1_gn_g'] = gain((D,))
    inp['l1_gn_b'] = nrm((D,), 0.01)
    inp['l1_w_o'] = nrm((D, D), D ** -0.5)
    moe('l1_')
    return inp


def reference(x, c, ctx, c_ctx,
              l0_w_mod, l0_b_mod, l0_norm_mix, l0_norm_ffn, l0_w_in, l0_conv_w, l0_conv_b,
              l0_lru_wi, l0_lru_bi, l0_lru_wr, l0_lru_br, l0_lru_lam, l0_q_gain, l0_k_gain, l0_w_out,
              l0_router, l0_we_gate, l0_we_up, l0_we_down,
              l1_w_mod, l1_b_mod, l1_norm_mix, l1_norm_ffn, l1_mu, l1_w_r, l1_w_k, l1_w_v,
              l1_w0, l1_w1, l1_w2, l1_a0, l1_a1, l1_a2, l1_g1, l1_g2, l1_k_k, l1_k_a, l1_r_k,
              l1_gn_g, l1_gn_b, l1_w_o, l1_router, l1_we_gate, l1_we_up, l1_we_down):
    layers = [
        ((l0_w_mod, l0_b_mod, l0_norm_mix, l0_norm_ffn),
         (l0_w_in, l0_conv_w, l0_conv_b, l0_lru_wi, l0_lru_bi, l0_lru_wr, l0_lru_br, l0_lru_lam,
          l0_q_gain, l0_k_gain, l0_w_out),
         (l0_router, l0_we_gate, l0_we_up, l0_we_down)),
        ((l1_w_mod, l1_b_mod, l1_norm_mix, l1_norm_ffn),
         (l1_mu, l1_w_r, l1_w_k, l1_w_v, l1_w0, l1_w1, l1_w2, l1_a0, l1_a1, l1_a2, l1_g1, l1_g2,
          l1_k_k, l1_k_a, l1_r_k, l1_gn_g, l1_gn_b, l1_w_o),
         (l1_router, l1_we_gate, l1_we_up, l1_we_down)),
    ]
    for i in range(DEPTH):
        (w_mod, b_mod, norm_mix, norm_ffn), mix_params, moe_params = layers[i]
        last = i == DEPTH - 1
        sh1, sc1, gt1, sh2, sc2, gt2 = ada_params(c, w_mod, b_mod)
        csh1, csc1, cgt1, csh2, csc2, cgt2 = ada_params(c_ctx[None], w_mod, b_mod)
        h_lat = rms_norm(x, norm_mix) * (1.0 + sc1) + sh1
        h_ctx = rms_norm(ctx, norm_mix) * (1.0 + csc1) + csh1
        mixer = mixer_rglru_gqa if i % 2 == 0 else mixer_rwkv7
        y_lat, y_ctx = mixer(h_lat, h_ctx, *mix_params, ctx_out=not last)
        x = x + gt1 * y_lat
        x = x + gt2 * expert_choice_ffn(rms_norm(x, norm_ffn) * (1.0 + sc2) + sh2, *moe_params)
        if not last:
            ctx = ctx + cgt1 * y_ctx
            ctx = ctx + cgt2 * expert_choice_ffn(rms_norm(ctx, norm_ffn) * (1.0 + csc2) + csh2, *moe_params)
    return x
```

```python
import functools

import jax
import jax.numpy as jnp
import numpy as np
from jax import lax
from jax.experimental import pallas as pl
from jax.experimental.pallas import tpu as pltpu

F32 = jnp.float32
BF16 = jnp.bfloat16
I32 = jnp.int32

D_MODEL = 1024
LANES = 128
SUBLANES = 8
GRID_W = 64
NORM_EPS = 1e-6
LRU_WIDTH = 512
LRU_BLOCKS = 8
LRU_BLOCK = LRU_WIDTH // LRU_BLOCKS
LRU_C = 8.0
CONV_W = 4
N_Q_HEADS = 8
N_KV_HEADS = 2
HEAD_DIM = 64
Q_GROUP = N_Q_HEADS // N_KV_HEADS
ROPE_BASE = 10000.0
ATTN_SCALE = HEAD_DIM ** -0.5
Q_WIDTH = N_Q_HEADS * HEAD_DIM
KV_WIDTH = N_KV_HEADS * HEAD_DIM
IN_WIDTH = 2 * LRU_WIDTH + Q_WIDTH + 2 * KV_WIDTH
RWKV_HEAD = 64
RWKV_HEADS = D_MODEL // RWKV_HEAD
HEAD_PAIRS = D_MODEL // LANES
DECAY_LORA = 64
AAA_LORA = 64
GATE_LORA = 160
GATE_LORA_PAD = 256
GN_EPS = 64e-5
N_EXPERTS = 16
EXPERT_FF = 2048
CAPACITY_FACTOR = 2
ROUTE_PAD = LANES
VMEM_LIMIT = 60 * 1024 * 1024


def _cparams(*sem):
    return pltpu.CompilerParams(dimension_semantics=sem, vmem_limit_bytes=VMEM_LIMIT)


def _dot(a, b):
    return jnp.dot(a, b, preferred_element_type=F32)


def _dot_nt(a, b):
    return lax.dot_general(a, b, (((1,), (1,)), ((), ())), preferred_element_type=F32)


def _dot_tn(a, b):
    return lax.dot_general(a, b, (((0,), (0,)), ((), ())), preferred_element_type=F32)


def _split2(x):
    hi = x.astype(BF16)
    lo = (x - hi.astype(F32)).astype(BF16)
    return hi, lo


def _split3(x):
    hi = x.astype(BF16)
    r = x - hi.astype(F32)
    mid = r.astype(BF16)
    lo = (r - mid.astype(F32)).astype(BF16)
    return hi, mid, lo


def _dot_f32(a, b):
    a3 = _split3(a)
    b3 = _split3(b)
    out = None
    for i in range(3):
        for j in range(3 - i):
            d = _dot(a3[i], b3[j])
            out = d if out is None else out + d
    return out


def _softplus(x):
    return jnp.maximum(x, 0.0) + jnp.log1p(jnp.exp(-jnp.abs(x)))


def _gelu_tanh(x):
    c = np.float32(np.sqrt(2.0 / np.pi))
    return 0.5 * x * (1.0 + jnp.tanh(c * (x + 0.044715 * (x * x * x))))


def _norm_mod(x, gain, sc, sh):
    ms = jnp.mean(x * x, axis=-1, keepdims=True)
    y = x * lax.rsqrt(ms + NORM_EPS)
    return (y * gain) * (1.0 + sc) + sh


def _seg_ones(width):
    i = np.arange(width)[:, None] // HEAD_DIM
    j = np.arange(width)[None, :] // HEAD_DIM
    return jnp.asarray(i == j, dtype=BF16)


def _head_sum(x, seg):
    outs = []
    for c in range(x.shape[1] // LANES):
        hi, lo = _split2(x[:, c * LANES:(c + 1) * LANES])
        outs.append(_dot(hi, seg) + _dot(lo, seg))
    return outs[0] if len(outs) == 1 else jnp.concatenate(outs, axis=1)


def _ada_kernel(c_ref, w_ref, b_ref, o_ref):
    s = c_ref[...]
    s = s * jax.nn.sigmoid(s)
    o_ref[...] = _dot_f32(s, w_ref[...]) + b_ref[...]


def _ada(cond8, w_mod, b_mod):
    n = w_mod.shape[1]
    tn = 1536
    return pl.pallas_call(
        _ada_kernel,
        out_shape=jax.ShapeDtypeStruct((SUBLANES, n), F32),
        grid=(n // tn,),
        in_specs=[pl.BlockSpec((SUBLANES, D_MODEL), lambda j: (0, 0)),
                  pl.BlockSpec((D_MODEL, tn), lambda j: (0, j)),
                  pl.BlockSpec((1, tn), lambda j: (0, j))],
        out_specs=pl.BlockSpec((SUBLANES, tn), lambda j: (0, j)),
        compiler_params=_cparams("arbitrary"),
        name="ada_params",
    )(cond8, w_mod, b_mod.reshape(1, n))


def _swap_pairs(x):
    lane = lax.broadcasted_iota(I32, x.shape, 1)
    nxt = pltpu.roll(x, LANES - 1, 1)
    prv = pltpu.roll(x, 1, 1)
    return jnp.where(lane % 2 == 0, nxt, prv)


def _proj0_kernel(x_ref, sc_ref, sh_ref, gain_ref, w_ref, qg_ref, kg_ref, seg_ref, cos_ref, sin_ref,
                  xa_ref, ga_ref, q_ref, k_ref, v_ref, *, rope):
    h = _norm_mod(x_ref[0], gain_ref[...], sc_ref[0], sh_ref[0])
    res = _dot(h.astype(BF16), w_ref[...])
    xa_ref[0] = res[:, :LRU_WIDTH]
    ga_ref[0] = res[:, LRU_WIDTH:2 * LRU_WIDTH]
    q0 = 2 * LRU_WIDTH
    seg = seg_ref[...]

    def head_norm_rope(z, gain):
        ms = _head_sum(z * z, seg) * (1.0 / HEAD_DIM)
        zn = z * lax.rsqrt(ms + NORM_EPS) * gain
        if not rope:
            return zn
        c = cos_ref[...]
        s = sin_ref[...]
        outs = []
        for t in range(zn.shape[1] // LANES):
            zt = zn[:, t * LANES:(t + 1) * LANES]
            outs.append(zt * c + _swap_pairs(zt) * s)
        return outs[0] if len(outs) == 1 else jnp.concatenate(outs, axis=1)

    q = head_norm_rope(res[:, q0:q0 + Q_WIDTH], qg_ref[...])
    q_ref[0] = (q * ATTN_SCALE).astype(BF16)
    k = head_norm_rope(res[:, q0 + Q_WIDTH:q0 + Q_WIDTH + KV_WIDTH], kg_ref[...])
    k_ref[0] = k.astype(BF16)
    v_ref[0] = res[:, q0 + Q_WIDTH + KV_WIDTH:].astype(BF16)


def _proj0(x, sc, sh, gain, w_in, q_gain, k_gain, cos_t, sin_t, rope):
    b, t, _ = x.shape
    tm = min(512, t)
    qg = jnp.tile(q_gain, N_Q_HEADS).reshape(1, Q_WIDTH)
    kg = jnp.tile(k_gain, N_KV_HEADS).reshape(1, KV_WIDTH)
    seg = _seg_ones(LANES)
    const = lambda shape: pl.BlockSpec(shape, lambda bi, i: (0,) * len(shape))
    tok = lambda w: pl.BlockSpec((1, tm, w), lambda bi, i: (bi, i, 0))
    per_b = pl.BlockSpec((1, 1, D_MODEL), lambda bi, i: (bi, 0, 0))
    return pl.pallas_call(
        functools.partial(_proj0_kernel, rope=rope),
        out_shape=(jax.ShapeDtypeStruct((b, t, LRU_WIDTH), F32),
                   jax.ShapeDtypeStruct((b, t, LRU_WIDTH), F32),
                   jax.ShapeDtypeStruct((b, t, Q_WIDTH), BF16),
                   jax.ShapeDtypeStruct((b, t, KV_WIDTH), BF16),
                   jax.ShapeDtypeStruct((b, t, KV_WIDTH), BF16)),
        grid=(b, t // tm),
        in_specs=[tok(D_MODEL), per_b, per_b, const((1, D_MODEL)), const((D_MODEL, IN_WIDTH)),
                  const((1, Q_WIDTH)), const((1, KV_WIDTH)), const((LANES, LANES)),
                  pl.BlockSpec((tm, LANES), lambda bi, i: (i, 0)),
                  pl.BlockSpec((tm, LANES), lambda bi, i: (i, 0))],
        out_specs=(tok(LRU_WIDTH), tok(LRU_WIDTH), tok(Q_WIDTH), tok(KV_WIDTH), tok(KV_WIDTH)),
        compiler_params=_cparams("parallel", "arbitrary"),
        name="l0_in_proj",
    )(x, sc, sh, gain.reshape(1, D_MODEL), w_in, qg, kg, seg, cos_t, sin_t)


def _rope_tables(t):
    n_rows = t // GRID_W
    row = jnp.repeat(jnp.arange(n_rows, dtype=F32), GRID_W)
    col = jnp.tile(jnp.arange(GRID_W, dtype=F32), n_rows)
    axis_dim = HEAD_DIM // 2
    inv_freq = ROPE_BASE ** (-jnp.arange(0, axis_dim, 2, dtype=F32) / axis_dim)
    ang = jnp.concatenate([row[:, None] * inv_freq, col[:, None] * inv_freq], axis=-1)
    cos = jnp.repeat(jnp.cos(ang), 2, axis=-1)
    sin = jnp.repeat(jnp.sin(ang), 2, axis=-1)
    sign = jnp.tile(jnp.asarray([-1.0, 1.0], F32), HEAD_DIM // 2)
    return jnp.tile(cos, (1, 2)), jnp.tile(sin * sign, (1, 2))


def _scan_rows(a, b, h0, rev):
    n = a.shape[0]
    row = lax.broadcasted_iota(I32, a.shape, 0) % SUBLANES
    for d in (1, 2, 4):
        if rev:
            a_s = pltpu.roll(a, n - d, 0)
            b_s = pltpu.roll(b, n - d, 0)
            m = row < SUBLANES - d
        else:
            a_s = pltpu.roll(a, d, 0)
            b_s = pltpu.roll(b, d, 0)
            m = row >= d
        b = jnp.where(m, a * b_s + b, b)
        a = jnp.where(m, a * a_s, a)
    groups = n // SUBLANES
    outs = [None] * groups
    h = h0
    for g in (range(groups - 1, -1, -1) if rev else range(groups)):
        hg = a[g * SUBLANES:(g + 1) * SUBLANES] * h + b[g * SUBLANES:(g + 1) * SUBLANES]
        outs[g] = hg
        h = hg[0:1] if rev else hg[SUBLANES - 1:SUBLANES]
    return jnp.concatenate(outs, axis=0), h


def _rglru_kernel(xc_ref, gc_ref, xl_ref, gl_ref, cw_ref, cb_ref, wg_ref, bg_ref, lam_ref,
                  yc_ref, yl_ref, rec_c, rec_l, *, tt_c, tt_l):
    cw = cw_ref[...]
    cb = cb_ref[...]

    def coeffs(x_ref, t0, tt, d):
        n = x_ref.shape[1]
        main = x_ref[0, pl.ds(t0, tt), :]
        prev = x_ref[0, pl.ds(pl.multiple_of(jnp.maximum(t0 - SUBLANES, 0), SUBLANES), SUBLANES), :]
        prev = jnp.where(t0 > 0, prev, 0.0)
        nxt = x_ref[0, pl.ds(pl.multiple_of(jnp.minimum(t0 + tt, n - SUBLANES), SUBLANES), SUBLANES), :]
        nxt = jnp.where(t0 + tt < n, nxt, 0.0)
        xe = jnp.concatenate([prev, main, nxt], axis=0)
        o = SUBLANES - CONV_W // 2
        xc = cb
        for j in range(CONV_W):
            xc = xc + cw[j:j + 1] * xe[o + j:o + j + tt]
        g = _dot(xc.astype(BF16), wg_ref[d, 0]) + bg_ref[d, 0]
        i_gate = jax.nn.sigmoid(g[:, :LANES])
        r_gate = jax.nn.sigmoid(g[:, LANES:])
        log_a = LRU_C * r_gate * (-_softplus(-lam_ref[d, 0]))
        a = jnp.exp(log_a)
        bco = jnp.sqrt(-jnp.tanh(log_a) * (a * a + 1.0)) * (i_gate * xc)
        return a, bco

    def sweep(x_ref, tt, d, rev, h, emit):
        nch = x_ref.shape[1] // tt

        def body(i, h):
            ci = nch - 1 - i if rev else i
            t0 = pl.multiple_of(ci * tt, tt)
            a, bco = coeffs(x_ref, t0, tt, d)
            hs, h = _scan_rows(a, bco, h, rev)
            emit(t0, tt, hs)
            return h

        return lax.fori_loop(0, nch, body, h)

    def store_rec(rec):
        def emit(t0, tt, hs):
            rec[pl.ds(t0, tt), :] = hs
        return emit

    def store_out(rec, g_ref, y_ref):
        def emit(t0, tt, hs):
            tot = rec[pl.ds(t0, tt), :] + hs
            y_ref[0, pl.ds(t0, tt), :] = (tot * _gelu_tanh(g_ref[0, pl.ds(t0, tt), :])).astype(y_ref.dtype)
        return emit

    zero = jnp.zeros((1, LANES), F32)
    h = sweep(xc_ref, tt_c, 0, False, zero, store_rec(rec_c))
    sweep(xl_ref, tt_l, 0, False, h, store_rec(rec_l))
    h = sweep(xc_ref, tt_c, 1, True, zero, store_out(rec_c, gc_ref, yc_ref))
    sweep(xl_ref, tt_l, 1, True, h, store_out(rec_l, gl_ref, yl_ref))


def _rglru(xa_c, ga_c, xa_l, ga_l, conv_w, conv_b, lru_wi, lru_bi, lru_wr, lru_br, lru_lam):
    b, lc, _ = xa_c.shape
    t = xa_l.shape[1]
    nt = LRU_WIDTH // LANES
    per_tile = LANES // LRU_BLOCK

    def block_diag(w):
        w = w.reshape(2, nt, per_tile, LRU_BLOCK, LRU_BLOCK)
        eye = jnp.eye(per_tile, dtype=w.dtype)
        return jnp.einsum('dtpij,pq->dtpiqj', w, eye).reshape(2, nt, LANES, LANES)

    wg = jnp.concatenate([block_diag(lru_wi), block_diag(lru_wr)], axis=-1).astype(BF16)
    bg = jnp.concatenate([lru_bi.reshape(2, nt, 1, LANES), lru_br.reshape(2, nt, 1, LANES)], axis=-1)
    lam = lru_lam.reshape(2, nt, 1, LANES)
    tt_c = min(256, lc)
    tt_l = min(256, t)
    seq = lambda n: pl.BlockSpec((1, n, LANES), lambda bi, j: (bi, 0, j))
    return pl.pallas_call(
        functools.partial(_rglru_kernel, tt_c=tt_c, tt_l=tt_l),
        out_shape=(jax.ShapeDtypeStruct((b, lc, LRU_WIDTH), BF16),
                   jax.ShapeDtypeStruct((b, t, LRU_WIDTH), BF16)),
        grid=(b, nt),
        in_specs=[seq(lc), seq(lc), seq(t), seq(t),
                  pl.BlockSpec((CONV_W, LANES), lambda bi, j: (0, j)),
                  pl.BlockSpec((1, LANES), lambda bi, j: (0, j)),
                  pl.BlockSpec((2, 1, LANES, 2 * LANES), lambda bi, j: (0, j, 0, 0)),
                  pl.BlockSpec((2, 1, 1, 2 * LANES), lambda bi, j: (0, j, 0, 0)),
                  pl.BlockSpec((2, 1, 1, LANES), lambda bi, j: (0, j, 0, 0))],
        out_specs=(seq(lc), seq(t)),
        scratch_shapes=[pltpu.VMEM((lc, LANES), F32), pltpu.VMEM((t, LANES), F32)],
        compiler_params=_cparams("parallel", "arbitrary"),
        name="l0_rglru",
    )(xa_c, ga_c, xa_l, ga_l, conv_w, conv_b.reshape(1, LRU_WIDTH), wg, bg, lam)


def _attn_kernel(q_ref, *refs, n_seg):
    kv = refs[:2 * n_seg]
    o_ref = refs[2 * n_seg]
    outs = []
    for h in range(Q_GROUP):
        qh = q_ref[0, :, h * HEAD_DIM:(h + 1) * HEAD_DIM]
        ss = [_dot(qh, kv[2 * i][0, 0]) for i in range(n_seg)]
        m = ss[0].max(axis=-1, keepdims=True)
        for s in ss[1:]:
            m = jnp.maximum(m, s.max(axis=-1, keepdims=True))
        l = None
        o = None
        for i, s in enumerate(ss):
            p = jnp.exp(s - m)
            li = p.sum(axis=-1, keepdims=True)
            oi = _dot(p.astype(BF16), kv[2 * i + 1][0, 0])
            l = li if l is None else l + li
            o = oi if o is None else o + oi
        outs.append(o / l)
    o_ref[0] = jnp.concatenate(outs, axis=1).astype(o_ref.dtype)


def _attention(q, segs):
    b, t, _ = q.shape
    tq = min(128, t)
    gw = Q_GROUP * HEAD_DIM
    in_specs = [pl.BlockSpec((1, tq, gw), lambda bi, g, i: (bi, i, g))]
    args = [q]
    for kt, v in segs:
        tk = kt.shape[-1]
        in_specs.append(pl.BlockSpec((1, 1, HEAD_DIM, tk), lambda bi, g, i: (bi, g, 0, 0)))
        in_specs.append(pl.BlockSpec((1, 1, tk, HEAD_DIM), lambda bi, g, i: (bi, g, 0, 0)))
        args += [kt, v]
    return pl.pallas_call(
        functools.partial(_attn_kernel, n_seg=len(segs)),
        out_shape=jax.ShapeDtypeStruct((b, t, Q_WIDTH), BF16),
        grid=(b, N_KV_HEADS, t // tq),
        in_specs=in_specs,
        out_specs=pl.BlockSpec((1, tq, gw), lambda bi, g, i: (bi, i, g)),
        compiler_params=_cparams("parallel", "parallel", "arbitrary"),
        name="l0_attention",
    )(*args)


def _kv_layout(k, v):
    b, t, _ = k.shape
    kt = k.reshape(b, t, N_KV_HEADS, HEAD_DIM).transpose(0, 2, 3, 1)
    vv = v.reshape(b, t, N_KV_HEADS, HEAD_DIM).transpose(0, 2, 1, 3)
    return kt, vv


def _resmm_kernel(x_ref, g_ref, *refs, n):
    acc = None
    for i in range(n):
        d = _dot(refs[i][0], refs[n + i][...])
        acc = d if acc is None else acc + d
    o_ref = refs[2 * n]
    o_ref[0] = x_ref[0] + g_ref[0] * acc


def _residual_matmul(x, gate, acts, weights):
    b, t, _ = x.shape
    tm = min(512, t)
    n = len(acts)
    tok = lambda w: pl.BlockSpec((1, tm, w), lambda bi, i: (bi, i, 0))
    in_specs = [tok(D_MODEL), pl.BlockSpec((1, 1, D_MODEL), lambda bi, i: (bi, 0, 0))]
    in_specs += [tok(a.shape[-1]) for a in acts]
    in_specs += [pl.BlockSpec(w.shape, lambda bi, i: (0, 0)) for w in weights]
    return pl.pallas_call(
        functools.partial(_resmm_kernel, n=n),
        out_shape=jax.ShapeDtypeStruct((b, t, D_MODEL), F32),
        grid=(b, t // tm),
        in_specs=in_specs,
        out_specs=tok(D_MODEL),
        compiler_params=_cparams("parallel", "arbitrary"),
        name="residual_proj",
    )(x, gate, *acts, *weights)


def _router_kernel(x_ref, sc_ref, sh_ref, gain_ref, wr_ref, h_ref, aff_ref):
    h = _norm_mod(x_ref[0], gain_ref[...], sc_ref[0], sh_ref[0])
    tm = h.shape[0]
    logits = _dot_f32(h, wr_ref[...])
    lane = lax.broadcasted_iota(I32, (tm, ROUTE_PAD), 1)
    z = jnp.where(lane < N_EXPERTS, logits, -jnp.inf)
    e = jnp.exp(z - z.max(axis=1, keepdims=True))
    aff = e / e.sum(axis=1, keepdims=True)
    h_ref[0, :, :D_MODEL] = h
    h_ref[0, :, D_MODEL:] = aff
    aff_ref[0] = aff.T[:N_EXPERTS]


def _router(x, sc, sh, gain, w_router):
    b, t, _ = x.shape
    tm = min(512, t)
    wr = jnp.pad(w_router, ((0, 0), (0, ROUTE_PAD - N_EXPERTS)))
    tok = lambda w: pl.BlockSpec((1, tm, w), lambda bi, i: (bi, i, 0))
    per_b = pl.BlockSpec((1, 1, D_MODEL), lambda bi, i: (bi, 0, 0))
    return pl.pallas_call(
        _router_kernel,
        out_shape=(jax.ShapeDtypeStruct((b, t, D_MODEL + ROUTE_PAD), F32),
                   jax.ShapeDtypeStruct((b, N_EXPERTS, t), F32)),
        grid=(b, t // tm),
        in_specs=[tok(D_MODEL), per_b, per_b,
                  pl.BlockSpec((1, D_MODEL), lambda bi, i: (0, 0)),
                  pl.BlockSpec((D_MODEL, ROUTE_PAD), lambda bi, i: (0, 0))],
        out_specs=(tok(D_MODEL + ROUTE_PAD), pl.BlockSpec((1, N_EXPERTS, tm), lambda bi, i: (bi, 0, i))),
        compiler_params=_cparams("parallel", "arbitrary"),
        name="moe_router",
    )(x, sc, sh, gain.reshape(1, D_MODEL), wr)


def _cumsum_lanes(x01, tri):
    outs = []
    off = jnp.zeros((x01.shape[0], 1), F32)
    for j in range(x01.shape[1] // LANES):
        cj = _dot(x01[:, j * LANES:(j + 1) * LANES].astype(BF16), tri) + off
        outs.append(cj)
        off = cj[:, LANES - 1:LANES]
    return outs[0] if len(outs) == 1 else jnp.concatenate(outs, axis=1)


def _select_kernel(aff_ref, tri_ref, idx_ref, pos_s, *, cap, rb):
    aff = aff_ref[0]
    n = aff.shape[1]
    keys = pltpu.bitcast(aff, I32)

    def bit_step(i, tau):
        cand = tau | jnp.left_shift(jnp.int32(1), 30 - i)
        cnt = jnp.sum((keys >= cand).astype(I32), axis=1, keepdims=True)
        return jnp.where(cnt >= cap, cand, tau)

    tau = lax.fori_loop(0, 31, bit_step, jnp.zeros((N_EXPERTS, 1), I32))
    gt = keys > tau
    eq = keys == tau
    need = (cap - jnp.sum(gt.astype(I32), axis=1, keepdims=True)).astype(F32)
    tri = tri_ref[...]
    c_eq = _cumsum_lanes(eq.astype(F32), tri)
    sel = gt | (eq & (c_eq <= need))
    c_sel = _cumsum_lanes(sel.astype(F32), tri)
    pos = jnp.where(sel, c_sel - 1.0, -1.0)
    for ei in range(N_EXPERTS):
        pos_s[ei] = pos[ei:ei + 1]

    n_tiles = n // LANES
    n_rb = cap // rb

    def per_expert(e, carry):
        def per_rb(r, carry):
            r0 = jnp.asarray(r * rb, F32)
            slot = lax.broadcasted_iota(I32, (rb, LANES), 0).astype(F32) + r0
            acc = jnp.zeros((rb, LANES), F32)
            for j in range(n_tiles):
                p = pos_s[e, :, j * LANES:(j + 1) * LANES]
                tok = (lax.broadcasted_iota(I32, (1, LANES), 1) + j * LANES).astype(F32)
                acc = acc + jnp.where(p == slot, tok, 0.0)
            tot = jnp.sum(acc, axis=1, keepdims=True)
            idx_ref[0, pl.ds(e, 1), pl.ds(pl.multiple_of(r * rb, rb), rb), :] = (
                jnp.broadcast_to(tot, (rb, LANES)).astype(I32)[None])
            return carry
        return lax.fori_loop(0, n_rb, per_rb, carry)

    lax.fori_loop(0, N_EXPERTS, per_expert, 0)


def _select(aff_t, cap):
    b, _, n = aff_t.shape
    rb = min(64, cap)
    tri = jnp.asarray(np.triu(np.ones((LANES, LANES), np.float32)), dtype=BF16)
    out = pl.pallas_call(
        functools.partial(_select_kernel, cap=cap, rb=rb),
        out_shape=jax.ShapeDtypeStruct((b, N_EXPERTS, cap, LANES), I32),
        grid=(b,),
        in_specs=[pl.BlockSpec((1, N_EXPERTS, n), lambda bi: (bi, 0, 0)),
                  pl.BlockSpec((LANES, LANES), lambda bi: (0, 0))],
        out_specs=pl.BlockSpec((1, N_EXPERTS, cap, LANES), lambda bi: (bi, 0, 0, 0)),
        scratch_shapes=[pltpu.VMEM((N_EXPERTS, 1, n), F32)],
        compiler_params=_cparams("parallel"),
        name="moe_select",
    )(aff_t, tri)
    return out[..., 0]


def _moe_kernel(idx_ref, h_hbm, x_hbm, gt_ref, wg_ref, wu_ref, wd_ref, o_hbm, xe, orow, sems, *, cap, fc):
    del x_hbm
    e = pl.program_id(0)
    b = pl.program_id(1)

    def gather_h(r):
        tok = idx_ref[0, 0, 0, r]
        return pltpu.make_async_copy(h_hbm.at[b, pl.ds(tok, 1), :], xe.at[pl.ds(r, 1), :], sems.at[0])

    def gather_o(r):
        tok = idx_ref[0, 0, 0, r]
        return pltpu.make_async_copy(o_hbm.at[b, pl.ds(tok, 1), :], orow.at[pl.ds(r, 1), :], sems.at[1])

    def scatter_o(r):
        tok = idx_ref[0, 0, 0, r]
        return pltpu.make_async_copy(orow.at[pl.ds(r, 1), :], o_hbm.at[b, pl.ds(tok, 1), :], sems.at[2])

    def for_rows(fn):
        def body(r, c):
            fn(r)
            return c
        lax.fori_loop(0, cap, body, 0)

    for_rows(lambda r: gather_h(r).start())
    for_rows(lambda r: gather_o(r).start())
    for_rows(lambda r: gather_h(r).wait())

    xb = xe[:, :D_MODEL].astype(BF16)
    acc = jnp.zeros((cap, D_MODEL), F32)
    for c in range(EXPERT_FF // fc):
        g = _dot(xb, wg_ref[0, :, c * fc:(c + 1) * fc])
        u = _dot(xb, wu_ref[0, :, c * fc:(c + 1) * fc])
        hid = (g * jax.nn.sigmoid(g)) * u
        acc = acc + _dot(hid.astype(BF16), wd_ref[0, c * fc:(c + 1) * fc, :])
    lane = lax.broadcasted_iota(I32, (cap, ROUTE_PAD), 1)
    gate = jnp.sum(jnp.where(lane == e, xe[:, D_MODEL:], 0.0), axis=1, keepdims=True)

    for_rows(lambda r: gather_o(r).wait())
    orow[...] = orow[...] + gt_ref[0] * (acc * gate)
    for_rows(lambda r: scatter_o(r).start())
    for_rows(lambda r: scatter_o(r).wait())


def _moe_apply(idx, h_ext, x, gt, wg, wu, wd):
    b, n, _ = x.shape
    cap = idx.shape[-1]
    fc = 512
    idx4 = idx.reshape(b, N_EXPERTS, 1, cap)
    wspec = lambda shape: pl.BlockSpec((1,) + shape, lambda e, bi: (e, 0, 0))
    return pl.pallas_call(
        functools.partial(_moe_kernel, cap=cap, fc=fc),
        out_shape=jax.ShapeDtypeStruct((b, n, D_MODEL), F32),
        grid=(N_EXPERTS, b),
        in_specs=[pl.BlockSpec((1, 1, 1, cap), lambda e, bi: (bi, e, 0, 0), memory_space=pltpu.SMEM),
                  pl.BlockSpec(memory_space=pl.ANY),
                  pl.BlockSpec(memory_space=pl.ANY),
                  pl.BlockSpec((1, 1, D_MODEL), lambda e, bi: (bi, 0, 0)),
                  wspec((D_MODEL, EXPERT_FF)), wspec((D_MODEL, EXPERT_FF)), wspec((EXPERT_FF, D_MODEL))],
        out_specs=pl.BlockSpec(memory_space=pl.ANY),
        scratch_shapes=[pltpu.VMEM((cap, D_MODEL + ROUTE_PAD), F32),
                        pltpu.VMEM((cap, D_MODEL), F32),
                        pltpu.SemaphoreType.DMA((3,))],
        input_output_aliases={2: 0},
        compiler_params=pltpu.CompilerParams(dimension_semantics=("arbitrary", "arbitrary"),
                                             vmem_limit_bytes=VMEM_LIMIT, has_side_effects=True),
        name="moe_experts",
    )(idx4, h_ext, x, gt, wg, wu, wd)


def _moe(x, sc, sh, gt, gain, w_router, wg, wu, wd):
    n = x.shape[1]
    cap = max(1, CAPACITY_FACTOR * n // N_EXPERTS)
    h_ext, aff_t = _router(x, sc, sh, gain, w_router)
    idx = _select(aff_t, cap)
    return _moe_apply(idx, h_ext, x, gt, wg, wu, wd)


def _rwkv_prep_kernel(xm_ref, xp_ref, xn_ref, sc_ref, sh_ref, gain_ref, mu_ref, wr_ref, wk_ref, wv_ref,
                      w1_ref, w2_ref, a1_ref, a2_ref, g1_ref, g2_ref, w0_ref, a0_ref, kks_ref, ka_ref, rk_ref,
                      seg_ref, r_ref, v_ref, g_ref, kk_ref, bonus_ref, lw0_ref, lw1_ref, kd0_ref, kd1_ref,
                      ag0_ref, ag1_ref):
    i = pl.program_id(1)
    last = pl.num_programs(1) - 1
    tm = xm_ref.shape[1]
    gain = gain_ref[...]
    sc = sc_ref[0]
    sh = sh_ref[0]
    xe = jnp.concatenate([xp_ref[0], xm_ref[0], xn_ref[0]], axis=0)
    he = _norm_mod(xe, gain, sc, sh)
    row = lax.broadcasted_iota(I32, (tm, 1), 0)
    h = he[SUBLANES:SUBLANES + tm]
    hm1 = jnp.where((row == 0) & (i == 0), 0.0, he[SUBLANES - 1:SUBLANES - 1 + tm])
    hp1 = jnp.where((row == tm - 1) & (i == last), 0.0, he[SUBLANES + 1:SUBLANES + 1 + tm])
    xx = 0.5 * (hm1 + hp1) - h
    mu = mu_ref[...]
    mix = lambda j: (h + xx * mu[j:j + 1]).astype(BF16)
    r = _dot(mix(0), wr_ref[...])
    k = _dot(mix(2), wk_ref[...])
    v = _dot(mix(3), wv_ref[...])
    tw = jnp.tanh(_dot(mix(1), w1_ref[...])).astype(BF16)
    la = _dot(mix(4), a1_ref[...]).astype(BF16)
    g = _dot(jax.nn.sigmoid(_dot(mix(5), g1_ref[...])).astype(BF16), g2_ref[...])
    seg = seg_ref[...]
    kkv = k * kks_ref[...]
    kk = kkv * lax.rsqrt(jnp.maximum(_head_sum(kkv * kkv, seg), 1e-24))
    r_ref[0] = r.astype(r_ref.dtype)
    v_ref[0] = v.astype(v_ref.dtype)
    g_ref[0] = g.astype(g_ref.dtype)
    kk_ref[0] = kk.astype(kk_ref.dtype)
    bonus = None
    for d, (lw_ref, kd_ref, ag_ref) in enumerate(((lw0_ref, kd0_ref, ag0_ref), (lw1_ref, kd1_ref, ag1_ref))):
        w_pre = w0_ref[d] + _dot(tw, w2_ref[d])
        lw_ref[0] = -jnp.exp(-_softplus(-w_pre) - 0.5)
        a = jax.nn.sigmoid(a0_ref[d] + _dot(la, a2_ref[d]))
        kd = k * (1.0 + (a - 1.0) * ka_ref[...])
        kd_ref[0] = kd.astype(kd_ref.dtype)
        ag_ref[0] = a.astype(ag_ref.dtype)
        bd = _head_sum(r * kd * rk_ref[...], seg) * v
        bonus = bd if bonus is None else bonus + bd
    bonus_ref[0] = bonus


def _rwkv_prep(x, sc, sh, gain, p):
    b, t, _ = x.shape
    tm = min(256, t)
    nb8 = tm // SUBLANES
    tok = lambda: pl.BlockSpec((1, tm, D_MODEL), lambda bi, i: (bi, i, 0))
    prev = pl.BlockSpec((1, SUBLANES, D_MODEL), lambda bi, i: (bi, jnp.maximum(i * nb8 - 1, 0), 0))
    nxt = pl.BlockSpec((1, SUBLANES, D_MODEL),
                       lambda bi, i: (bi, jnp.minimum((i + 1) * nb8, t // SUBLANES - 1), 0))
    per_b = pl.BlockSpec((1, 1, D_MODEL), lambda bi, i: (bi, 0, 0))
    const = lambda a: pl.BlockSpec(a.shape, lambda bi, i: (0,) * a.ndim)
    consts = [p['gain'], p['mu'], p['w_r'], p['w_k'], p['w_v'], p['w1'], p['w2'], p['a1'], p['a2'], p['g1'],
              p['g2'], p['w0'], p['a0'], p['kks'], p['ka'], p['rk'], p['seg']]
    bf = jax.ShapeDtypeStruct((b, t, D_MODEL), BF16)
    f32 = jax.ShapeDtypeStruct((b, t, D_MODEL), F32)
    return pl.pallas_call(
        _rwkv_prep_kernel,
        out_shape=(bf, bf, bf, bf, f32, f32, f32, bf, bf, bf, bf),
        grid=(b, t // tm),
        in_specs=[tok(), prev, nxt, per_b, per_b] + [const(a) for a in consts],
        out_specs=tuple(tok() for _ in range(11)),
        compiler_params=_cparams("parallel", "arbitrary"),
        name="l1_rwkv_prep",
    )(x, x, x, sc, sh, *consts)


def _rwkv_params(gain, mu, w_r, w_k, w_v, w0, w1, w2, a0, a1, a2, g1, g2, k_k, k_a, r_k):
    def pad_dir(w):
        z = jnp.zeros_like(w[0])
        return jnp.stack([jnp.concatenate([w[0], z], axis=0), jnp.concatenate([z, w[1]], axis=0)])

    row = lambda a: a.reshape(1, D_MODEL)
    return dict(
        gain=row(gain), mu=mu, w_r=w_r.astype(BF16), w_k=w_k.astype(BF16), w_v=w_v.astype(BF16),
        w1=jnp.concatenate([w1[0], w1[1]], axis=1).astype(BF16), w2=pad_dir(w2).astype(BF16),
        a1=jnp.concatenate([a1[0], a1[1]], axis=1).astype(BF16), a2=pad_dir(a2).astype(BF16),
        g1=jnp.pad(g1, ((0, 0), (0, GATE_LORA_PAD - GATE_LORA))).astype(BF16),
        g2=jnp.pad(g2, ((0, GATE_LORA_PAD - GATE_LORA), (0, 0))).astype(BF16),
        w0=w0.reshape(2, 1, D_MODEL), a0=a0.reshape(2, 1, D_MODEL),
        kks=row(k_k), ka=row(k_a), rk=row(r_k), seg=_seg_ones(LANES))


SOLVE_BASE = 8


def _solve_unit_tri(nmat, x):
    c = nmat.shape[0]
    rowi = lax.broadcasted_iota(I32, (c, c), 0)
    coli = lax.broadcasted_iota(I32, (c, c), 1)

    def same_block(shift):
        return (rowi >> shift) == (coli >> shift)

    k = int(np.log2(SOLVE_BASE))
    m = jnp.where(same_block(k), nmat, 0.0).astype(BF16)
    tinv = jnp.where(rowi == coli, 1.0, 0.0) + m.astype(F32)
    for _ in range(k - 1):
        m = _dot(m, m).astype(BF16)
        tinv = tinv + _dot(m, tinv.astype(BF16))
    while (1 << k) < c:
        off = jnp.where(same_block(k + 1) & jnp.logical_not(same_block(k)), nmat, 0.0).astype(BF16)
        tb = tinv.astype(BF16)
        tinv = tinv + _dot(_dot(tb, off).astype(BF16), tb)
        k += 1
    return _dot(tinv.astype(BF16), x.astype(BF16))


def _wkv_stage(r_ref, v_ref, kk_ref, lw_ref, kd_ref, ag_ref, tri, ws, zs, vs, gts, d, rev, chunk):
    lw = lw_ref[0]
    p3 = _split3(lw)
    cum = _dot(tri, p3[0]) + _dot(tri, p3[1]) + _dot(tri, p3[2])
    g_in = jnp.exp(cum)
    g_ex = jnp.exp(cum - lw)
    g_inv = jnp.exp(-cum)
    kk = kk_ref[0].astype(F32)
    a_t = -(kk * g_ex)
    r_t = r_ref[0].astype(F32) * g_in
    b_t = kk * ag_ref[0].astype(F32) * g_inv
    k_t = kd_ref[0].astype(F32) * g_inv
    g_tot = jnp.exp(cum[0:1] if rev else cum[chunk - 1:chunk])
    v = v_ref[0]
    for p in range(HEAD_PAIRS):
        sl = slice(p * LANES, (p + 1) * LANES)
        ws[d, p, :chunk] = a_t[:, sl].astype(BF16)
        ws[d, p, chunk:] = r_t[:, sl].astype(BF16)
        zs[d, p, :chunk] = b_t[:, sl].astype(BF16)
        zs[d, p, chunk:] = k_t[:, sl].astype(BF16)
        vs[d, p] = v[:, sl]
        gts[d, p] = g_tot[:, sl]


def _wkv_pair(s_ref, y_s, ws, zs, vs, gts, d, p, rev, chunk):
    w = ws[d, p]
    z = zs[d, p]
    vp = vs[d, p]
    s_old = s_ref[d, p]
    s_bf = s_old.astype(BF16)
    rowi = lax.broadcasted_iota(I32, (chunk, chunk), 0)
    coli = lax.broadcasted_iota(I32, (chunk, chunk), 1)
    strict = rowi < coli if rev else rowi > coli
    incl = rowi <= coli if rev else rowi >= coli
    lane = lax.broadcasted_iota(I32, (1, LANES), 1)
    ds = jnp.zeros((RWKV_HEAD, LANES), F32)
    ys = []
    for j in range(2):
        mine = (lane >= j * RWKV_HEAD) & (lane < (j + 1) * RWKV_HEAD)
        wj = jnp.where(mine, w, jnp.zeros_like(w))
        zj = jnp.where(mine, z, jnp.zeros_like(z))
        pm = _dot_nt(wj, z)
        nmat = jnp.where(strict, pm[:chunk, :chunk], 0.0)
        ak = jnp.where(strict, pm[:chunk, chunk:], 0.0)
        rb = jnp.where(incl, pm[chunk:, :chunk], 0.0)
        rk = jnp.where(incl, pm[chunk:, chunk:], 0.0)
        wst = _dot_nt(wj, s_bf)
        vj = vp[:, j * RWKV_HEAD:(j + 1) * RWKV_HEAD]
        u = _solve_unit_tri(nmat, wst[:chunk] + _dot(ak.astype(BF16), vj))
        ub = u.astype(BF16)
        ys.append(wst[chunk:] + _dot(rb.astype(BF16), ub) + _dot(rk.astype(BF16), vj))
        ds = ds + _dot_tn(jnp.concatenate([ub, vj], axis=0), zj)
    y_s[d, p] = jnp.concatenate(ys, axis=1)
    s_ref[d, p] = (s_old + ds) * gts[d, p]


def _wkv_kernel(*refs, chunk, emit_y, has_init):
    ins = refs[:12]
    tri_ref = refs[12]
    pos = 13
    if has_init:
        s0_ref = refs[pos]
        pos += 1
    if emit_y:
        yf_ref, yb_ref = refs[pos:pos + 2]
        pos += 2
    sfin_ref = refs[pos]
    s_ref, y_s, ws, zs, vs, gts = refs[pos + 1:]
    i = pl.program_id(1)

    @pl.when(i == 0)
    def _():
        if has_init:
            s_ref[...] = s0_ref[0]
        else:
            s_ref[...] = jnp.zeros(s_ref.shape, F32)

    for d, rev in ((0, False), (1, True)):
        _wkv_stage(*ins[6 * d:6 * d + 6], tri_ref[d], ws, zs, vs, gts, d, rev, chunk)

    def pair_body(p, carry):
        _wkv_pair(s_ref, y_s, ws, zs, vs, gts, 0, p, False, chunk)
        _wkv_pair(s_ref, y_s, ws, zs, vs, gts, 1, p, True, chunk)
        return carry

    lax.fori_loop(0, HEAD_PAIRS, pair_body, 0)
    if emit_y:
        for d, y_ref in ((0, yf_ref), (1, yb_ref)):
            for p in range(HEAD_PAIRS):
                y_ref[0, :, p * LANES:(p + 1) * LANES] = y_s[d, p]

    @pl.when(i == pl.num_programs(1) - 1)
    def _():
        sfin_ref[0] = s_ref[...]


def _wkv(r, v, kk, lw, kd, ag, s0, emit_y, chunk=64):
    b, t, _ = r.shape
    n = t // chunk
    fwd = pl.BlockSpec((1, chunk, D_MODEL), lambda bi, i: (bi, i, 0))
    bwd = pl.BlockSpec((1, chunk, D_MODEL), lambda bi, i: (bi, n - 1 - i, 0))
    tri = jnp.asarray(np.stack([np.tril(np.ones((chunk, chunk), np.float32)),
                                np.triu(np.ones((chunk, chunk), np.float32))]), dtype=BF16)
    state_shape = (2, HEAD_PAIRS, RWKV_HEAD, LANES)
    state_spec = pl.BlockSpec((1,) + state_shape, lambda bi, i: (bi, 0, 0, 0, 0))
    args = [r, v, kk, lw[0], kd[0], ag[0], r, v, kk, lw[1], kd[1], ag[1], tri]
    in_specs = [fwd] * 6 + [bwd] * 6 + [pl.BlockSpec((2, chunk, chunk), lambda bi, i: (0, 0, 0))]
    if s0 is not None:
        args.append(s0)
        in_specs.append(state_spec)
    out_shape = [jax.ShapeDtypeStruct((b,) + state_shape, F32)]
    out_specs = [state_spec]
    if emit_y:
        out_shape = [jax.ShapeDtypeStruct((b, t, D_MODEL), F32)] * 2 + out_shape
        out_specs = [fwd, bwd] + out_specs
    res = pl.pallas_call(
        functools.partial(_wkv_kernel, chunk=chunk, emit_y=emit_y, has_init=s0 is not None),
        out_shape=tuple(out_shape),
        grid=(b, n),
        in_specs=in_specs,
        out_specs=tuple(out_specs),
        scratch_shapes=[pltpu.VMEM(state_shape, F32),
                        pltpu.VMEM((2, HEAD_PAIRS, chunk, LANES), F32),
                        pltpu.VMEM((2, HEAD_PAIRS, 2 * chunk, LANES), BF16),
                        pltpu.VMEM((2, HEAD_PAIRS, 2 * chunk, LANES), BF16),
                        pltpu.VMEM((2, HEAD_PAIRS, chunk, LANES), BF16),
                        pltpu.VMEM((2, HEAD_PAIRS, 1, LANES), F32)],
        compiler_params=_cparams("parallel", "arbitrary"),
        name="l1_wkv_scan",
    )(*args)
    return res


def _readout_kernel(x_ref, gt_ref, yf_ref, yb_ref, bonus_ref, g_ref, gng_ref, gnb_ref, seg_ref, wo_ref, o_ref):
    seg = seg_ref[...]
    y = yf_ref[0] + yb_ref[0]
    mean = _head_sum(y, seg) * (1.0 / RWKV_HEAD)
    c = y - mean
    var = _head_sum(c * c, seg) * (1.0 / RWKV_HEAD)
    yn = c * lax.rsqrt(var + GN_EPS) * gng_ref[...] + gnb_ref[...]
    out = (yn + bonus_ref[0]) * g_ref[0].astype(F32)
    o_ref[0] = x_ref[0] + gt_ref[0] * _dot(out.astype(BF16), wo_ref[...])


def _readout(x, gt, y_f, y_b, bonus, g, gn_g, gn_b, w_o):
    b, t, _ = x.shape
    tm = min(256, t)
    tok = pl.BlockSpec((1, tm, D_MODEL), lambda bi, i: (bi, i, 0))
    row = pl.BlockSpec((1, D_MODEL), lambda bi, i: (0, 0))
    return pl.pallas_call(
        _readout_kernel,
        out_shape=jax.ShapeDtypeStruct((b, t, D_MODEL), F32),
        grid=(b, t // tm),
        in_specs=[tok, pl.BlockSpec((1, 1, D_MODEL), lambda bi, i: (bi, 0, 0)), tok, tok, tok, tok, row, row,
                  pl.BlockSpec((LANES, LANES), lambda bi, i: (0, 0)),
                  pl.BlockSpec((D_MODEL, D_MODEL), lambda bi, i: (0, 0))],
        out_specs=tok,
        compiler_params=_cparams("parallel", "arbitrary"),
        name="l1_readout",
    )(x, gt, y_f, y_b, bonus, g, gn_g.reshape(1, D_MODEL), gn_b.reshape(1, D_MODEL), _seg_ones(LANES), w_o)


def _modulation(c, c_ctx, w_mod, b_mod):
    b = c.shape[0]
    cond = jnp.zeros((SUBLANES, D_MODEL), F32).at[:b].set(c).at[b].set(c_ctx)
    m = _ada(cond, w_mod, b_mod)
    lat = [m[:b, j * D_MODEL:(j + 1) * D_MODEL].reshape(b, 1, D_MODEL) for j in range(6)]
    ctx = [jnp.broadcast_to(m[b, j * D_MODEL:(j + 1) * D_MODEL].reshape(1, 1, D_MODEL), (b, 1, D_MODEL))
           for j in range(6)]
    return lat, ctx


def kernel(x, c, ctx, c_ctx, l0_w_mod, l0_b_mod, l0_norm_mix, l0_norm_ffn, l0_w_in, l0_conv_w, l0_conv_b, l0_lru_wi, l0_lru_bi, l0_lru_wr, l0_lru_br, l0_lru_lam, l0_q_gain, l0_k_gain, l0_w_out, l0_router, l0_we_gate, l0_we_up, l0_we_down, l1_w_mod, l1_b_mod, l1_norm_mix, l1_norm_ffn, l1_mu, l1_w_r, l1_w_k, l1_w_v, l1_w0, l1_w1, l1_w2, l1_a0, l1_a1, l1_a2, l1_g1, l1_g2, l1_k_k, l1_k_a, l1_r_k, l1_gn_g, l1_gn_b, l1_w_o, l1_router, l1_we_gate, l1_we_up, l1_we_down):
    t = x.shape[1]

    (sh1, sc1, gt1, sh2, sc2, gt2), (csh1, csc1, cgt1, csh2, csc2, cgt2) = _modulation(c, c_ctx, l0_w_mod, l0_b_mod)
    w_in = l0_w_in.astype(BF16)
    cos_t, sin_t = _rope_tables(t)
    xa_l, ga_l, q_l, k_l, v_l = _proj0(x, sc1, sh1, l0_norm_mix, w_in, l0_q_gain, l0_k_gain, cos_t, sin_t, True)
    lc = ctx.shape[1]
    xa_c, ga_c, q_c, k_c, v_c = _proj0(ctx, csc1, csh1, l0_norm_mix, w_in, l0_q_gain, l0_k_gain,
                                       cos_t[:lc], sin_t[:lc], False)
    ya_c, ya_l = _rglru(xa_c, ga_c, xa_l, ga_l, l0_conv_w, l0_conv_b, l0_lru_wi, l0_lru_bi, l0_lru_wr,
                        l0_lru_br, l0_lru_lam)
    seg_l = _kv_layout(k_l, v_l)
    seg_c = _kv_layout(k_c, v_c)
    yb_l = _attention(q_l, [seg_l, seg_c])
    yb_c = _attention(q_c, [seg_c])
    w_out = l0_w_out.astype(BF16)
    w_oa, w_ob = w_out[:LRU_WIDTH], w_out[LRU_WIDTH:]
    x = _residual_matmul(x, gt1, [ya_l, yb_l], [w_oa, w_ob])
    ctx = _residual_matmul(ctx, cgt1, [ya_c, yb_c], [w_oa, w_ob])
    wg, wu, wd = l0_we_gate.astype(BF16), l0_we_up.astype(BF16), l0_we_down.astype(BF16)
    x = _moe(x, sc2, sh2, gt2, l0_norm_ffn, l0_router, wg, wu, wd)
    ctx = _moe(ctx, csc2, csh2, cgt2, l0_norm_ffn, l0_router, wg, wu, wd)

    (sh1, sc1, gt1, sh2, sc2, gt2), (csh1, csc1, _, _, _, _) = _modulation(c, c_ctx, l1_w_mod, l1_b_mod)
    p = _rwkv_params(l1_norm_mix, l1_mu, l1_w_r, l1_w_k, l1_w_v, l1_w0, l1_w1, l1_w2, l1_a0, l1_a1, l1_a2,
                     l1_g1, l1_g2, l1_k_k, l1_k_a, l1_r_k)
    r_c, v_c, _, kk_c, _, lw0_c, lw1_c, kd0_c, kd1_c, ag0_c, ag1_c = _rwkv_prep(ctx, csc1, csh1, l1_norm_mix, p)
    r_l, v_l, g_l, kk_l, bonus_l, lw0, lw1, kd0, kd1, ag0, ag1 = _rwkv_prep(x, sc1, sh1, l1_norm_mix, p)
    (s_ctx,) = _wkv(r_c, v_c, kk_c, (lw0_c, lw1_c), (kd0_c, kd1_c), (ag0_c, ag1_c), None, False)
    y_f, y_b, _ = _wkv(r_l, v_l, kk_l, (lw0, lw1), (kd0, kd1), (ag0, ag1), s_ctx, True)
    x = _readout(x, gt1, y_f, y_b, bonus_l, g_l, l1_gn_g, l1_gn_b, l1_w_o.astype(BF16))
    wg, wu, wd = l1_we_gate.astype(BF16), l1_we_up.astype(BF16), l1_we_down.astype(BF16)
    x = _moe(x, sc2, sh2, gt2, l1_norm_ffn, l1_router, wg, wu, wd)
    return x
```

```python
import functools

import jax
import jax.numpy as jnp
import numpy as np
from jax import lax
from jax.experimental import pallas as pl
from jax.experimental.pallas import tpu as pltpu

F32 = jnp.float32
BF16 = jnp.bfloat16
I32 = jnp.int32

D_MODEL = 1024
LANES = 128
SUBLANES = 8
GRID_W = 64
NORM_EPS = 1e-6
LRU_WIDTH = 512
LRU_BLOCKS = 8
LRU_BLOCK = LRU_WIDTH // LRU_BLOCKS
LRU_C = 8.0
CONV_W = 4
N_Q_HEADS = 8
N_KV_HEADS = 2
HEAD_DIM = 64
Q_GROUP = N_Q_HEADS // N_KV_HEADS
ROPE_BASE = 10000.0
ATTN_SCALE = HEAD_DIM ** -0.5
Q_WIDTH = N_Q_HEADS * HEAD_DIM
KV_WIDTH = N_KV_HEADS * HEAD_DIM
IN_WIDTH = 2 * LRU_WIDTH + Q_WIDTH + 2 * KV_WIDTH
RWKV_HEAD = 64
RWKV_HEADS = D_MODEL // RWKV_HEAD
HEAD_PAIRS = D_MODEL // LANES
DECAY_LORA = 64
AAA_LORA = 64
GATE_LORA = 160
GATE_LORA_PAD = 256
GN_EPS = 64e-5
N_EXPERTS = 16
EXPERT_FF = 2048
CAPACITY_FACTOR = 2
ROUTE_PAD = LANES
VMEM_LIMIT = 60 * 1024 * 1024
ROW_DMA_UNROLL = 8


def _cparams(*sem):
    return pltpu.CompilerParams(dimension_semantics=sem, vmem_limit_bytes=VMEM_LIMIT)


def _dot(a, b):
    return jnp.dot(a, b, preferred_element_type=F32)


def _dot_nt(a, b):
    return lax.dot_general(a, b, (((1,), (1,)), ((), ())), preferred_element_type=F32)


def _dot_tn(a, b):
    return lax.dot_general(a, b, (((0,), (0,)), ((), ())), preferred_element_type=F32)


def _split2(x):
    hi = x.astype(BF16)
    lo = (x - hi.astype(F32)).astype(BF16)
    return hi, lo


def _split3(x):
    hi = x.astype(BF16)
    r = x - hi.astype(F32)
    mid = r.astype(BF16)
    lo = (r - mid.astype(F32)).astype(BF16)
    return hi, mid, lo


def _dot_f32(a, b):
    a3 = _split3(a)
    b3 = _split3(b)
    out = None
    for i in range(3):
        for j in range(3 - i):
            d = _dot(a3[i], b3[j])
            out = d if out is None else out + d
    return out


def _softplus(x):
    return jnp.maximum(x, 0.0) + jnp.log1p(jnp.exp(-jnp.abs(x)))


def _gelu_tanh(x):
    c = np.float32(np.sqrt(2.0 / np.pi))
    return 0.5 * x * (1.0 + jnp.tanh(c * (x + 0.044715 * (x * x * x))))


def _norm_mod(x, gain, sc, sh):
    ms = jnp.mean(x * x, axis=-1, keepdims=True)
    y = x * lax.rsqrt(ms + NORM_EPS)
    return (y * gain) * (1.0 + sc) + sh


def _seg_ones(width):
    i = np.arange(width)[:, None] // HEAD_DIM
    j = np.arange(width)[None, :] // HEAD_DIM
    return jnp.asarray(i == j, dtype=BF16)


def _head_sum(x, seg):
    outs = []
    for c in range(x.shape[1] // LANES):
        hi, lo = _split2(x[:, c * LANES:(c + 1) * LANES])
        outs.append(_dot(hi, seg) + _dot(lo, seg))
    return outs[0] if len(outs) == 1 else jnp.concatenate(outs, axis=1)


def _ada_kernel(c_ref, w_ref, b_ref, o_ref):
    s = c_ref[...]
    s = s * jax.nn.sigmoid(s)
    o_ref[...] = _dot_f32(s, w_ref[...]) + b_ref[...]


def _ada(cond8, w_mod, b_mod):
    n = w_mod.shape[1]
    tn = 1536
    return pl.pallas_call(
        _ada_kernel,
        out_shape=jax.ShapeDtypeStruct((SUBLANES, n), F32),
        grid=(n // tn,),
        in_specs=[pl.BlockSpec((SUBLANES, D_MODEL), lambda j: (0, 0)),
                  pl.BlockSpec((D_MODEL, tn), lambda j: (0, j)),
                  pl.BlockSpec((1, tn), lambda j: (0, j))],
        out_specs=pl.BlockSpec((SUBLANES, tn), lambda j: (0, j)),
        compiler_params=_cparams("arbitrary"),
        name="ada_params",
    )(cond8, w_mod, b_mod.reshape(1, n))


def _swap_pairs(x):
    lane = lax.broadcasted_iota(I32, x.shape, 1)
    nxt = pltpu.roll(x, LANES - 1, 1)
    prv = pltpu.roll(x, 1, 1)
    return jnp.where(lane % 2 == 0, nxt, prv)


def _proj0_kernel(x_ref, sc_ref, sh_ref, gain_ref, w_ref, qg_ref, kg_ref, seg_ref, cos_ref, sin_ref,
                  xa_ref, ga_ref, q_ref, k_ref, v_ref, *, rope):
    h = _norm_mod(x_ref[0], gain_ref[...], sc_ref[0], sh_ref[0])
    res = _dot(h.astype(BF16), w_ref[...])
    xa_ref[0] = res[:, :LRU_WIDTH]
    ga_ref[0] = res[:, LRU_WIDTH:2 * LRU_WIDTH]
    q0 = 2 * LRU_WIDTH
    seg = seg_ref[...]

    def head_norm_rope(z, gain):
        ms = _head_sum(z * z, seg) * (1.0 / HEAD_DIM)
        zn = z * lax.rsqrt(ms + NORM_EPS) * gain
        if not rope:
            return zn
        c = cos_ref[...]
        s = sin_ref[...]
        outs = []
        for t in range(zn.shape[1] // LANES):
            zt = zn[:, t * LANES:(t + 1) * LANES]
            outs.append(zt * c + _swap_pairs(zt) * s)
        return outs[0] if len(outs) == 1 else jnp.concatenate(outs, axis=1)

    q = head_norm_rope(res[:, q0:q0 + Q_WIDTH], qg_ref[...])
    q_ref[0] = (q * ATTN_SCALE).astype(BF16)
    k = head_norm_rope(res[:, q0 + Q_WIDTH:q0 + Q_WIDTH + KV_WIDTH], kg_ref[...])
    k_ref[0] = k.astype(BF16)
    v_ref[0] = res[:, q0 + Q_WIDTH + KV_WIDTH:].astype(BF16)


def _proj0(x, sc, sh, gain, w_in, q_gain, k_gain, cos_t, sin_t, rope):
    b, t, _ = x.shape
    tm = min(512, t)
    qg = jnp.tile(q_gain, N_Q_HEADS).reshape(1, Q_WIDTH)
    kg = jnp.tile(k_gain, N_KV_HEADS).reshape(1, KV_WIDTH)
    seg = _seg_ones(LANES)
    const = lambda shape: pl.BlockSpec(shape, lambda bi, i: (0,) * len(shape))
    tok = lambda w: pl.BlockSpec((1, tm, w), lambda bi, i: (bi, i, 0))
    per_b = pl.BlockSpec((1, 1, D_MODEL), lambda bi, i: (bi, 0, 0))
    return pl.pallas_call(
        functools.partial(_proj0_kernel, rope=rope),
        out_shape=(jax.ShapeDtypeStruct((b, t, LRU_WIDTH), F32),
                   jax.ShapeDtypeStruct((b, t, LRU_WIDTH), F32),
                   jax.ShapeDtypeStruct((b, t, Q_WIDTH), BF16),
                   jax.ShapeDtypeStruct((b, t, KV_WIDTH), BF16),
                   jax.ShapeDtypeStruct((b, t, KV_WIDTH), BF16)),
        grid=(b, t // tm),
        in_specs=[tok(D_MODEL), per_b, per_b, const((1, D_MODEL)), const((D_MODEL, IN_WIDTH)),
                  const((1, Q_WIDTH)), const((1, KV_WIDTH)), const((LANES, LANES)),
                  pl.BlockSpec((tm, LANES), lambda bi, i: (i, 0)),
                  pl.BlockSpec((tm, LANES), lambda bi, i: (i, 0))],
        out_specs=(tok(LRU_WIDTH), tok(LRU_WIDTH), tok(Q_WIDTH), tok(KV_WIDTH), tok(KV_WIDTH)),
        compiler_params=_cparams("parallel", "arbitrary"),
        name="l0_in_proj",
    )(x, sc, sh, gain.reshape(1, D_MODEL), w_in, qg, kg, seg, cos_t, sin_t)


def _rope_tables(t):
    n_rows = t // GRID_W
    row = jnp.repeat(jnp.arange(n_rows, dtype=F32), GRID_W)
    col = jnp.tile(jnp.arange(GRID_W, dtype=F32), n_rows)
    axis_dim = HEAD_DIM // 2
    inv_freq = ROPE_BASE ** (-jnp.arange(0, axis_dim, 2, dtype=F32) / axis_dim)
    ang = jnp.concatenate([row[:, None] * inv_freq, col[:, None] * inv_freq], axis=-1)
    cos = jnp.repeat(jnp.cos(ang), 2, axis=-1)
    sin = jnp.repeat(jnp.sin(ang), 2, axis=-1)
    sign = jnp.tile(jnp.asarray([-1.0, 1.0], F32), HEAD_DIM // 2)
    return jnp.tile(cos, (1, 2)), jnp.tile(sin * sign, (1, 2))


def _scan_rows(a, b, h0, rev):
    n = a.shape[0]
    row = lax.broadcasted_iota(I32, a.shape, 0) % SUBLANES
    for d in (1, 2, 4):
        if rev:
            a_s = pltpu.roll(a, n - d, 0)
            b_s = pltpu.roll(b, n - d, 0)
            m = row < SUBLANES - d
        else:
            a_s = pltpu.roll(a, d, 0)
            b_s = pltpu.roll(b, d, 0)
            m = row >= d
        b = jnp.where(m, a * b_s + b, b)
        a = jnp.where(m, a * a_s, a)
    groups = n // SUBLANES
    outs = [None] * groups
    h = h0
    for g in (range(groups - 1, -1, -1) if rev else range(groups)):
        hg = a[g * SUBLANES:(g + 1) * SUBLANES] * h + b[g * SUBLANES:(g + 1) * SUBLANES]
        outs[g] = hg
        h = hg[0:1] if rev else hg[SUBLANES - 1:SUBLANES]
    return jnp.concatenate(outs, axis=0), h


def _rglru_kernel(xc_ref, gc_ref, xl_ref, gl_ref, cw_ref, cb_ref, wg_ref, bg_ref, lam_ref,
                  yc_ref, yl_ref, rec_c, rec_l, *, tt_c, tt_l):
    cw = cw_ref[...]
    cb = cb_ref[...]

    def coeffs(x_ref, t0, tt, d):
        n = x_ref.shape[1]
        main = x_ref[0, pl.ds(t0, tt), :]
        prev = x_ref[0, pl.ds(pl.multiple_of(jnp.maximum(t0 - SUBLANES, 0), SUBLANES), SUBLANES), :]
        prev = jnp.where(t0 > 0, prev, 0.0)
        nxt = x_ref[0, pl.ds(pl.multiple_of(jnp.minimum(t0 + tt, n - SUBLANES), SUBLANES), SUBLANES), :]
        nxt = jnp.where(t0 + tt < n, nxt, 0.0)
        xe = jnp.concatenate([prev, main, nxt], axis=0)
        o = SUBLANES - CONV_W // 2
        xc = cb
        for j in range(CONV_W):
            xc = xc + cw[j:j + 1] * xe[o + j:o + j + tt]
        g = _dot(xc.astype(BF16), wg_ref[d, 0]) + bg_ref[d, 0]
        i_gate = jax.nn.sigmoid(g[:, :LANES])
        r_gate = jax.nn.sigmoid(g[:, LANES:])
        log_a = LRU_C * r_gate * (-_softplus(-lam_ref[d, 0]))
        a = jnp.exp(log_a)
        bco = jnp.sqrt(-jnp.tanh(log_a) * (a * a + 1.0)) * (i_gate * xc)
        return a, bco

    def sweep(x_ref, tt, d, rev, h, emit):
        nch = x_ref.shape[1] // tt

        def body(i, h):
            ci = nch - 1 - i if rev else i
            t0 = pl.multiple_of(ci * tt, tt)
            a, bco = coeffs(x_ref, t0, tt, d)
            hs, h = _scan_rows(a, bco, h, rev)
            emit(t0, tt, hs)
            return h

        return lax.fori_loop(0, nch, body, h)

    def store_rec(rec):
        def emit(t0, tt, hs):
            rec[pl.ds(t0, tt), :] = hs
        return emit

    def store_out(rec, g_ref, y_ref):
        def emit(t0, tt, hs):
            tot = rec[pl.ds(t0, tt), :] + hs
            y_ref[0, pl.ds(t0, tt), :] = (tot * _gelu_tanh(g_ref[0, pl.ds(t0, tt), :])).astype(y_ref.dtype)
        return emit

    zero = jnp.zeros((1, LANES), F32)
    h = sweep(xc_ref, tt_c, 0, False, zero, store_rec(rec_c))
    sweep(xl_ref, tt_l, 0, False, h, store_rec(rec_l))
    h = sweep(xc_ref, tt_c, 1, True, zero, store_out(rec_c, gc_ref, yc_ref))
    sweep(xl_ref, tt_l, 1, True, h, store_out(rec_l, gl_ref, yl_ref))


def _rglru(xa_c, ga_c, xa_l, ga_l, conv_w, conv_b, lru_wi, lru_bi, lru_wr, lru_br, lru_lam):
    b, lc, _ = xa_c.shape
    t = xa_l.shape[1]
    nt = LRU_WIDTH // LANES
    per_tile = LANES // LRU_BLOCK

    def block_diag(w):
        w = w.reshape(2, nt, per_tile, LRU_BLOCK, LRU_BLOCK)
        eye = jnp.eye(per_tile, dtype=w.dtype)
        return jnp.einsum('dtpij,pq->dtpiqj', w, eye).reshape(2, nt, LANES, LANES)

    wg = jnp.concatenate([block_diag(lru_wi), block_diag(lru_wr)], axis=-1).astype(BF16)
    bg = jnp.concatenate([lru_bi.reshape(2, nt, 1, LANES), lru_br.reshape(2, nt, 1, LANES)], axis=-1)
    lam = lru_lam.reshape(2, nt, 1, LANES)
    tt_c = min(256, lc)
    tt_l = min(256, t)
    seq = lambda n: pl.BlockSpec((1, n, LANES), lambda bi, j: (bi, 0, j))
    return pl.pallas_call(
        functools.partial(_rglru_kernel, tt_c=tt_c, tt_l=tt_l),
        out_shape=(jax.ShapeDtypeStruct((b, lc, LRU_WIDTH), BF16),
                   jax.ShapeDtypeStruct((b, t, LRU_WIDTH), BF16)),
        grid=(b, nt),
        in_specs=[seq(lc), seq(lc), seq(t), seq(t),
                  pl.BlockSpec((CONV_W, LANES), lambda bi, j: (0, j)),
                  pl.BlockSpec((1, LANES), lambda bi, j: (0, j)),
                  pl.BlockSpec((2, 1, LANES, 2 * LANES), lambda bi, j: (0, j, 0, 0)),
                  pl.BlockSpec((2, 1, 1, 2 * LANES), lambda bi, j: (0, j, 0, 0)),
                  pl.BlockSpec((2, 1, 1, LANES), lambda bi, j: (0, j, 0, 0))],
        out_specs=(seq(lc), seq(t)),
        scratch_shapes=[pltpu.VMEM((lc, LANES), F32), pltpu.VMEM((t, LANES), F32)],
        compiler_params=_cparams("parallel", "arbitrary"),
        name="l0_rglru",
    )(xa_c, ga_c, xa_l, ga_l, conv_w, conv_b.reshape(1, LRU_WIDTH), wg, bg, lam)


def _attn_kernel(q_ref, *refs, n_seg):
    kv = refs[:2 * n_seg]
    o_ref = refs[2 * n_seg]
    outs = []
    for h in range(Q_GROUP):
        qh = q_ref[0, :, h * HEAD_DIM:(h + 1) * HEAD_DIM]
        ss = [_dot(qh, kv[2 * i][0, 0]) for i in range(n_seg)]
        m = ss[0].max(axis=-1, keepdims=True)
        for s in ss[1:]:
            m = jnp.maximum(m, s.max(axis=-1, keepdims=True))
        l = None
        o = None
        for i, s in enumerate(ss):
            p = jnp.exp(s - m)
            li = p.sum(axis=-1, keepdims=True)
            oi = _dot(p.astype(BF16), kv[2 * i + 1][0, 0])
            l = li if l is None else l + li
            o = oi if o is None else o + oi
        outs.append(o / l)
    o_ref[0] = jnp.concatenate(outs, axis=1).astype(o_ref.dtype)


def _attention(q, segs):
    b, t, _ = q.shape
    tq = min(128, t)
    gw = Q_GROUP * HEAD_DIM
    in_specs = [pl.BlockSpec((1, tq, gw), lambda bi, g, i: (bi, i, g))]
    args = [q]
    for kt, v in segs:
        tk = kt.shape[-1]
        in_specs.append(pl.BlockSpec((1, 1, HEAD_DIM, tk), lambda bi, g, i: (bi, g, 0, 0)))
        in_specs.append(pl.BlockSpec((1, 1, tk, HEAD_DIM), lambda bi, g, i: (bi, g, 0, 0)))
        args += [kt, v]
    return pl.pallas_call(
        functools.partial(_attn_kernel, n_seg=len(segs)),
        out_shape=jax.ShapeDtypeStruct((b, t, Q_WIDTH), BF16),
        grid=(b, N_KV_HEADS, t // tq),
        in_specs=in_specs,
        out_specs=pl.BlockSpec((1, tq, gw), lambda bi, g, i: (bi, i, g)),
        compiler_params=_cparams("parallel", "parallel", "arbitrary"),
        name="l0_attention",
    )(*args)


def _kv_layout(k, v):
    b, t, _ = k.shape
    kt = k.reshape(b, t, N_KV_HEADS, HEAD_DIM).transpose(0, 2, 3, 1)
    vv = v.reshape(b, t, N_KV_HEADS, HEAD_DIM).transpose(0, 2, 1, 3)
    return kt, vv


def _resmm_kernel(x_ref, g_ref, *refs, n):
    acc = None
    for i in range(n):
        d = _dot(refs[i][0], refs[n + i][...])
        acc = d if acc is None else acc + d
    o_ref = refs[2 * n]
    o_ref[0] = x_ref[0] + g_ref[0] * acc


def _residual_matmul(x, gate, acts, weights):
    b, t, _ = x.shape
    tm = min(512, t)
    n = len(acts)
    tok = lambda w: pl.BlockSpec((1, tm, w), lambda bi, i: (bi, i, 0))
    in_specs = [tok(D_MODEL), pl.BlockSpec((1, 1, D_MODEL), lambda bi, i: (bi, 0, 0))]
    in_specs += [tok(a.shape[-1]) for a in acts]
    in_specs += [pl.BlockSpec(w.shape, lambda bi, i: (0, 0)) for w in weights]
    return pl.pallas_call(
        functools.partial(_resmm_kernel, n=n),
        out_shape=jax.ShapeDtypeStruct((b, t, D_MODEL), F32),
        grid=(b, t // tm),
        in_specs=in_specs,
        out_specs=tok(D_MODEL),
        compiler_params=_cparams("parallel", "arbitrary"),
        name="residual_proj",
    )(x, gate, *acts, *weights)


def _router_kernel(x_ref, sc_ref, sh_ref, gain_ref, wr_ref, h_ref, aff_ref):
    h = _norm_mod(x_ref[0], gain_ref[...], sc_ref[0], sh_ref[0])
    tm = h.shape[0]
    logits = _dot_f32(h, wr_ref[...])
    lane = lax.broadcasted_iota(I32, (tm, ROUTE_PAD), 1)
    z = jnp.where(lane < N_EXPERTS, logits, -jnp.inf)
    e = jnp.exp(z - z.max(axis=1, keepdims=True))
    aff = e / e.sum(axis=1, keepdims=True)
    h_ref[0, :, :D_MODEL] = h
    h_ref[0, :, D_MODEL:] = aff
    aff_ref[0] = aff.T[:N_EXPERTS]


def _router(x, sc, sh, gain, w_router):
    b, t, _ = x.shape
    tm = min(512, t)
    wr = jnp.pad(w_router, ((0, 0), (0, ROUTE_PAD - N_EXPERTS)))
    tok = lambda w: pl.BlockSpec((1, tm, w), lambda bi, i: (bi, i, 0))
    per_b = pl.BlockSpec((1, 1, D_MODEL), lambda bi, i: (bi, 0, 0))
    return pl.pallas_call(
        _router_kernel,
        out_shape=(jax.ShapeDtypeStruct((b, t, D_MODEL + ROUTE_PAD), F32),
                   jax.ShapeDtypeStruct((b, N_EXPERTS, t), F32)),
        grid=(b, t // tm),
        in_specs=[tok(D_MODEL), per_b, per_b,
                  pl.BlockSpec((1, D_MODEL), lambda bi, i: (0, 0)),
                  pl.BlockSpec((D_MODEL, ROUTE_PAD), lambda bi, i: (0, 0))],
        out_specs=(tok(D_MODEL + ROUTE_PAD), pl.BlockSpec((1, N_EXPERTS, tm), lambda bi, i: (bi, 0, i))),
        compiler_params=_cparams("parallel", "arbitrary"),
        name="moe_router",
    )(x, sc, sh, gain.reshape(1, D_MODEL), wr)


def _cumsum_lanes(x01, tri):
    outs = []
    off = jnp.zeros((x01.shape[0], 1), F32)
    for j in range(x01.shape[1] // LANES):
        cj = _dot(x01[:, j * LANES:(j + 1) * LANES].astype(BF16), tri) + off
        outs.append(cj)
        off = cj[:, LANES - 1:LANES]
    return outs[0] if len(outs) == 1 else jnp.concatenate(outs, axis=1)


def _select_kernel(aff_ref, tri_ref, idx_ref, pos_s, *, cap, rb):
    aff = aff_ref[0]
    n = aff.shape[1]
    keys = pltpu.bitcast(aff, I32)

    def bit_step(i, tau):
        cand = tau | jnp.left_shift(jnp.int32(1), 30 - i)
        cnt = jnp.sum((keys >= cand).astype(I32), axis=1, keepdims=True)
        return jnp.where(cnt >= cap, cand, tau)

    tau = lax.fori_loop(0, 31, bit_step, jnp.zeros((N_EXPERTS, 1), I32))
    gt = keys > tau
    eq = keys == tau
    need = (cap - jnp.sum(gt.astype(I32), axis=1, keepdims=True)).astype(F32)
    tri = tri_ref[...]
    c_eq = _cumsum_lanes(eq.astype(F32), tri)
    sel = gt | (eq & (c_eq <= need))
    c_sel = _cumsum_lanes(sel.astype(F32), tri)
    pos = jnp.where(sel, c_sel - 1.0, -1.0)
    for ei in range(N_EXPERTS):
        pos_s[ei] = pos[ei:ei + 1]

    n_tiles = n // LANES
    n_rb = cap // rb

    def per_expert(e, carry):
        def per_rb(r, carry):
            r0 = jnp.asarray(r * rb, F32)
            slot = lax.broadcasted_iota(I32, (rb, LANES), 0).astype(F32) + r0
            acc = jnp.zeros((rb, LANES), F32)
            for j in range(n_tiles):
                p = pos_s[e, :, j * LANES:(j + 1) * LANES]
                tok = (lax.broadcasted_iota(I32, (1, LANES), 1) + j * LANES).astype(F32)
                acc = acc + jnp.where(p == slot, tok, 0.0)
            tot = jnp.sum(acc, axis=1, keepdims=True)
            idx_ref[0, pl.ds(e, 1), pl.ds(pl.multiple_of(r * rb, rb), rb), :] = (
                jnp.broadcast_to(tot, (rb, LANES)).astype(I32)[None])
            return carry
        return lax.fori_loop(0, n_rb, per_rb, carry)

    lax.fori_loop(0, N_EXPERTS, per_expert, 0)


def _select(aff_t, cap):
    b, _, n = aff_t.shape
    rb = min(64, cap)
    tri = jnp.asarray(np.triu(np.ones((LANES, LANES), np.float32)), dtype=BF16)
    out = pl.pallas_call(
        functools.partial(_select_kernel, cap=cap, rb=rb),
        out_shape=jax.ShapeDtypeStruct((b, N_EXPERTS, cap, LANES), I32),
        grid=(b,),
        in_specs=[pl.BlockSpec((1, N_EXPERTS, n), lambda bi: (bi, 0, 0)),
                  pl.BlockSpec((LANES, LANES), lambda bi: (0, 0))],
        out_specs=pl.BlockSpec((1, N_EXPERTS, cap, LANES), lambda bi: (bi, 0, 0, 0)),
        scratch_shapes=[pltpu.VMEM((N_EXPERTS, 1, n), F32)],
        compiler_params=_cparams("parallel"),
        name="moe_select",
    )(aff_t, tri)
    return out[..., 0]


def _moe_kernel(idx_ref, h_hbm, x_hbm, gt_ref, wg_ref, wu_ref, wd_ref, o_hbm, xe, orow, sems, *, cap, fc):
    del x_hbm
    e = pl.program_id(0)
    b = pl.program_id(1)

    def gather_h(r):
        tok = idx_ref[0, 0, 0, r]
        return pltpu.make_async_copy(h_hbm.at[b, pl.ds(tok, 1), :], xe.at[pl.ds(r, 1), :], sems.at[0])

    def gather_o(r):
        tok = idx_ref[0, 0, 0, r]
        return pltpu.make_async_copy(o_hbm.at[b, pl.ds(tok, 1), :], orow.at[pl.ds(r, 1), :], sems.at[1])

    def scatter_o(r):
        tok = idx_ref[0, 0, 0, r]
        return pltpu.make_async_copy(orow.at[pl.ds(r, 1), :], o_hbm.at[b, pl.ds(tok, 1), :], sems.at[2])

    def for_rows(fn):
        def body(r, c):
            fn(r)
            return c
        lax.fori_loop(0, cap, body, 0, unroll=ROW_DMA_UNROLL)

    def all_rows(hbm, vmem, sem):
        return pltpu.make_async_copy(hbm.at[b, pl.ds(0, cap), :], vmem, sem)

    def start_gathers(r):
        gather_h(r).start()
        gather_o(r).start()

    for_rows(start_gathers)
    all_rows(h_hbm, xe, sems.at[0]).wait()

    xb = xe[:, :D_MODEL].astype(BF16)
    acc = jnp.zeros((cap, D_MODEL), F32)
    for c in range(EXPERT_FF // fc):
        g = _dot(xb, wg_ref[0, :, c * fc:(c + 1) * fc])
        u = _dot(xb, wu_ref[0, :, c * fc:(c + 1) * fc])
        hid = (g * jax.nn.sigmoid(g)) * u
        acc = acc + _dot(hid.astype(BF16), wd_ref[0, c * fc:(c + 1) * fc, :])
    lane = lax.broadcasted_iota(I32, (cap, ROUTE_PAD), 1)
    gate = jnp.sum(jnp.where(lane == e, xe[:, D_MODEL:], 0.0), axis=1, keepdims=True)

    all_rows(o_hbm, orow, sems.at[1]).wait()
    orow[...] = orow[...] + gt_ref[0] * (acc * gate)
    for_rows(lambda r: scatter_o(r).start())
    pltpu.make_async_copy(orow, o_hbm.at[b, pl.ds(0, cap), :], sems.at[2]).wait()


def _moe_apply(idx, h_ext, x, gt, wg, wu, wd):
    b, n, _ = x.shape
    cap = idx.shape[-1]
    fc = 512
    idx4 = idx.reshape(b, N_EXPERTS, 1, cap)
    wspec = lambda shape: pl.BlockSpec((1,) + shape, lambda e, bi: (e, 0, 0))
    return pl.pallas_call(
        functools.partial(_moe_kernel, cap=cap, fc=fc),
        out_shape=jax.ShapeDtypeStruct((b, n, D_MODEL), F32),
        grid=(N_EXPERTS, b),
        in_specs=[pl.BlockSpec((1, 1, 1, cap), lambda e, bi: (bi, e, 0, 0), memory_space=pltpu.SMEM),
                  pl.BlockSpec(memory_space=pl.ANY),
                  pl.BlockSpec(memory_space=pl.ANY),
                  pl.BlockSpec((1, 1, D_MODEL), lambda e, bi: (bi, 0, 0)),
                  wspec((D_MODEL, EXPERT_FF)), wspec((D_MODEL, EXPERT_FF)), wspec((EXPERT_FF, D_MODEL))],
        out_specs=pl.BlockSpec(memory_space=pl.ANY),
        scratch_shapes=[pltpu.VMEM((cap, D_MODEL + ROUTE_PAD), F32),
                        pltpu.VMEM((cap, D_MODEL), F32),
                        pltpu.SemaphoreType.DMA((3,))],
        input_output_aliases={2: 0},
        compiler_params=pltpu.CompilerParams(dimension_semantics=("arbitrary", "arbitrary"),
                                             vmem_limit_bytes=VMEM_LIMIT, has_side_effects=True),
        name="moe_experts",
    )(idx4, h_ext, x, gt, wg, wu, wd)


def _moe(x, sc, sh, gt, gain, w_router, wg, wu, wd):
    n = x.shape[1]
    cap = max(1, CAPACITY_FACTOR * n // N_EXPERTS)
    h_ext, aff_t = _router(x, sc, sh, gain, w_router)
    idx = _select(aff_t, cap)
    return _moe_apply(idx, h_ext, x, gt, wg, wu, wd)


def _rwkv_prep_kernel(xm_ref, xp_ref, xn_ref, sc_ref, sh_ref, gain_ref, mu_ref, wr_ref, wk_ref, wv_ref,
                      w1_ref, w2_ref, a1_ref, a2_ref, g1_ref, g2_ref, w0_ref, a0_ref, kks_ref, ka_ref, rk_ref,
                      seg_ref, r_ref, v_ref, g_ref, kk_ref, bonus_ref, lw0_ref, lw1_ref, kd0_ref, kd1_ref,
                      ag0_ref, ag1_ref):
    i = pl.program_id(1)
    last = pl.num_programs(1) - 1
    tm = xm_ref.shape[1]
    gain = gain_ref[...]
    sc = sc_ref[0]
    sh = sh_ref[0]
    xe = jnp.concatenate([xp_ref[0], xm_ref[0], xn_ref[0]], axis=0)
    he = _norm_mod(xe, gain, sc, sh)
    row = lax.broadcasted_iota(I32, (tm, 1), 0)
    h = he[SUBLANES:SUBLANES + tm]
    hm1 = jnp.where((row == 0) & (i == 0), 0.0, he[SUBLANES - 1:SUBLANES - 1 + tm])
    hp1 = jnp.where((row == tm - 1) & (i == last), 0.0, he[SUBLANES + 1:SUBLANES + 1 + tm])
    xx = 0.5 * (hm1 + hp1) - h
    mu = mu_ref[...]
    mix = lambda j: (h + xx * mu[j:j + 1]).astype(BF16)
    r = _dot(mix(0), wr_ref[...])
    k = _dot(mix(2), wk_ref[...])
    v = _dot(mix(3), wv_ref[...])
    tw = jnp.tanh(_dot(mix(1), w1_ref[...])).astype(BF16)
    la = _dot(mix(4), a1_ref[...]).astype(BF16)
    g = _dot(jax.nn.sigmoid(_dot(mix(5), g1_ref[...])).astype(BF16), g2_ref[...])
    seg = seg_ref[...]
    kkv = k * kks_ref[...]
    kk = kkv * lax.rsqrt(jnp.maximum(_head_sum(kkv * kkv, seg), 1e-24))
    r_ref[0] = r.astype(r_ref.dtype)
    v_ref[0] = v.astype(v_ref.dtype)
    g_ref[0] = g.astype(g_ref.dtype)
    kk_ref[0] = kk.astype(kk_ref.dtype)
    bonus = None
    for d, (lw_ref, kd_ref, ag_ref) in enumerate(((lw0_ref, kd0_ref, ag0_ref), (lw1_ref, kd1_ref, ag1_ref))):
        w_pre = w0_ref[d] + _dot(tw, w2_ref[d])
        lw_ref[0] = -jnp.exp(-_softplus(-w_pre) - 0.5)
        a = jax.nn.sigmoid(a0_ref[d] + _dot(la, a2_ref[d]))
        kd = k * (1.0 + (a - 1.0) * ka_ref[...])
        kd_ref[0] = kd.astype(kd_ref.dtype)
        ag_ref[0] = a.astype(ag_ref.dtype)
        bd = _head_sum(r * kd * rk_ref[...], seg) * v
        bonus = bd if bonus is None else bonus + bd
    bonus_ref[0] = bonus


def _rwkv_prep(x, sc, sh, gain, p):
    b, t, _ = x.shape
    tm = min(256, t)
    nb8 = tm // SUBLANES
    tok = lambda: pl.BlockSpec((1, tm, D_MODEL), lambda bi, i: (bi, i, 0))
    prev = pl.BlockSpec((1, SUBLANES, D_MODEL), lambda bi, i: (bi, jnp.maximum(i * nb8 - 1, 0), 0))
    nxt = pl.BlockSpec((1, SUBLANES, D_MODEL),
                       lambda bi, i: (bi, jnp.minimum((i + 1) * nb8, t // SUBLANES - 1), 0))
    per_b = pl.BlockSpec((1, 1, D_MODEL), lambda bi, i: (bi, 0, 0))
    const = lambda a: pl.BlockSpec(a.shape, lambda bi, i: (0,) * a.ndim)
    consts = [p['gain'], p['mu'], p['w_r'], p['w_k'], p['w_v'], p['w1'], p['w2'], p['a1'], p['a2'], p['g1'],
              p['g2'], p['w0'], p['a0'], p['kks'], p['ka'], p['rk'], p['seg']]
    bf = jax.ShapeDtypeStruct((b, t, D_MODEL), BF16)
    f32 = jax.ShapeDtypeStruct((b, t, D_MODEL), F32)
    return pl.pallas_call(
        _rwkv_prep_kernel,
        out_shape=(bf, bf, bf, bf, f32, f32, f32, bf, bf, bf, bf),
        grid=(b, t // tm),
        in_specs=[tok(), prev, nxt, per_b, per_b] + [const(a) for a in consts],
        out_specs=tuple(tok() for _ in range(11)),
        compiler_params=_cparams("parallel", "arbitrary"),
        name="l1_rwkv_prep",
    )(x, x, x, sc, sh, *consts)


def _rwkv_params(gain, mu, w_r, w_k, w_v, w0, w1, w2, a0, a1, a2, g1, g2, k_k, k_a, r_k):
    def pad_dir(w):
        z = jnp.zeros_like(w[0])
        return jnp.stack([jnp.concatenate([w[0], z], axis=0), jnp.concatenate([z, w[1]], axis=0)])

    row = lambda a: a.reshape(1, D_MODEL)
    return dict(
        gain=row(gain), mu=mu, w_r=w_r.astype(BF16), w_k=w_k.astype(BF16), w_v=w_v.astype(BF16),
        w1=jnp.concatenate([w1[0], w1[1]], axis=1).astype(BF16), w2=pad_dir(w2).astype(BF16),
        a1=jnp.concatenate([a1[0], a1[1]], axis=1).astype(BF16), a2=pad_dir(a2).astype(BF16),
        g1=jnp.pad(g1, ((0, 0), (0, GATE_LORA_PAD - GATE_LORA))).astype(BF16),
        g2=jnp.pad(g2, ((0, GATE_LORA_PAD - GATE_LORA), (0, 0))).astype(BF16),
        w0=w0.reshape(2, 1, D_MODEL), a0=a0.reshape(2, 1, D_MODEL),
        kks=row(k_k), ka=row(k_a), rk=row(r_k), seg=_seg_ones(LANES))


SOLVE_BASE = 8


def _solve_unit_tri(nmat, x, top):
    n = nmat.shape[0]
    rowi = lax.broadcasted_iota(I32, (n, n), 0)
    coli = lax.broadcasted_iota(I32, (n, n), 1)

    def same_block(shift):
        return (rowi >> shift) == (coli >> shift)

    k = int(np.log2(SOLVE_BASE))
    m = jnp.where(same_block(k), nmat, 0.0).astype(BF16)
    tinv = jnp.where(rowi == coli, 1.0, 0.0) + m.astype(F32)
    for _ in range(k - 1):
        m = _dot(m, m).astype(BF16)
        yield None
        tinv = tinv + _dot(m, tinv.astype(BF16))
        yield None
    while (1 << k) < top:
        off = jnp.where(same_block(k + 1) & jnp.logical_not(same_block(k)), nmat, 0.0).astype(BF16)
        tb = tinv.astype(BF16)
        half = _dot(tb, off).astype(BF16)
        yield None
        tinv = tinv + _dot(half, tb)
        yield None
        k += 1
    yield _dot(tinv.astype(BF16), x.astype(BF16))


def _wkv_stage(r_ref, v_ref, kk_ref, lw_ref, kd_ref, ag_ref, tri, ws, zs, vs, gts, d, rev, chunk):
    lw = lw_ref[0]
    p3 = _split3(lw)
    cum = _dot(tri, p3[0]) + _dot(tri, p3[1]) + _dot(tri, p3[2])
    g_in = jnp.exp(cum)
    g_ex = jnp.exp(cum - lw)
    g_inv = jnp.exp(-cum)
    kk = kk_ref[0].astype(F32)
    a_t = -(kk * g_ex)
    r_t = r_ref[0].astype(F32) * g_in
    b_t = kk * ag_ref[0].astype(F32) * g_inv
    k_t = kd_ref[0].astype(F32) * g_inv
    g_tot = jnp.exp(cum[0:1] if rev else cum[chunk - 1:chunk])
    first = (lax.broadcasted_iota(I32, (1, D_MODEL), 1) % LANES) < RWKV_HEAD
    stacked = lambda x: (jnp.where(first, x, jnp.zeros_like(x)), jnp.where(first, jnp.zeros_like(x), x))
    a2, r2, b2, k2 = (stacked(x.astype(BF16)) for x in (a_t, r_t, b_t, k_t))
    v2 = stacked(v_ref[0])
    c2 = 2 * chunk
    for p in range(HEAD_PAIRS):
        sl = slice(p * LANES, (p + 1) * LANES)
        for h in range(2):
            ws[d, p, h * chunk:(h + 1) * chunk] = a2[h][:, sl]
            ws[d, p, c2 + h * chunk:c2 + (h + 1) * chunk] = r2[h][:, sl]
            zs[d, p, h * chunk:(h + 1) * chunk] = b2[h][:, sl]
            zs[d, p, c2 + h * chunk:c2 + (h + 1) * chunk] = k2[h][:, sl]
            vs[d, p, h * chunk:(h + 1) * chunk] = v2[h][:, sl]
        gts[d, p] = g_tot[:, sl]


def _wkv_chain(s_ref, y_ref, ws, zs, vs, gts, d, p, rev, chunk):
    n = 2 * chunk
    w = ws[d, p]
    z = zs[d, p]
    v = vs[d, p]
    s_old = s_ref[d, p]
    pm = _dot_nt(w, z)
    yield
    wst = _dot_nt(w, s_old.astype(BF16))
    yield
    ti = lax.broadcasted_iota(I32, (n, n), 0) & (chunk - 1)
    tj = lax.broadcasted_iota(I32, (n, n), 1) & (chunk - 1)
    strict = ti < tj if rev else ti > tj
    incl = ti <= tj if rev else ti >= tj
    nmat = jnp.where(strict, pm[:n, :n], 0.0)
    ak = jnp.where(strict, pm[:n, n:], 0.0).astype(BF16)
    rbk = jnp.concatenate([jnp.where(incl, pm[n:, :n], 0.0), jnp.where(incl, pm[n:, n:], 0.0)], axis=1).astype(BF16)
    x = wst[:n] + _dot(ak, v)
    yield
    u = None
    for u in _solve_unit_tri(nmat, x, chunk):
        yield
    uv = jnp.concatenate([u.astype(BF16), v], axis=0)
    y = wst[n:] + _dot(rbk, uv)
    yield
    if y_ref is not None:
        y_ref[0, :, p * LANES:(p + 1) * LANES] = y[:chunk] + y[chunk:]
    ds = _dot_tn(uv, z)
    yield
    s_ref[d, p] = (s_old + ds) * gts[d, p]


def _wkv_kernel(*refs, chunk, emit_y, has_init):
    ins = refs[:12]
    tri_ref = refs[12]
    pos = 13
    if has_init:
        s0_ref = refs[pos]
        pos += 1
    yf_ref = yb_ref = None
    if emit_y:
        yf_ref, yb_ref = refs[pos:pos + 2]
        pos += 2
    sfin_ref = refs[pos]
    s_ref, ws, zs, vs, gts = refs[pos + 1:]
    i = pl.program_id(1)

    @pl.when(i == 0)
    def _():
        if has_init:
            s_ref[...] = s0_ref[0]
        else:
            s_ref[...] = jnp.zeros(s_ref.shape, F32)

    for d, rev in ((0, False), (1, True)):
        _wkv_stage(*ins[6 * d:6 * d + 6], tri_ref[d], ws, zs, vs, gts, d, rev, chunk)

    chains = [_wkv_chain(s_ref, (yf_ref, yb_ref)[d], ws, zs, vs, gts, d, p, bool(d), chunk)
              for p in range(HEAD_PAIRS) for d in range(2)]
    while chains:
        alive = []
        for ch in chains:
            try:
                next(ch)
                alive.append(ch)
            except StopIteration:
                pass
        chains = alive

    @pl.when(i == pl.num_programs(1) - 1)
    def _():
        sfin_ref[0] = s_ref[...]


def _wkv(r, v, kk, lw, kd, ag, s0, emit_y, chunk=64):
    b, t, _ = r.shape
    n = t // chunk
    fwd = pl.BlockSpec((1, chunk, D_MODEL), lambda bi, i: (bi, i, 0))
    bwd = pl.BlockSpec((1, chunk, D_MODEL), lambda bi, i: (bi, n - 1 - i, 0))
    tri = jnp.asarray(np.stack([np.tril(np.ones((chunk, chunk), np.float32)),
                                np.triu(np.ones((chunk, chunk), np.float32))]), dtype=BF16)
    state_shape = (2, HEAD_PAIRS, 2 * RWKV_HEAD, LANES)
    state_spec = pl.BlockSpec((1,) + state_shape, lambda bi, i: (bi, 0, 0, 0, 0))
    args = [r, v, kk, lw[0], kd[0], ag[0], r, v, kk, lw[1], kd[1], ag[1], tri]
    in_specs = [fwd] * 6 + [bwd] * 6 + [pl.BlockSpec((2, chunk, chunk), lambda bi, i: (0, 0, 0))]
    if s0 is not None:
        args.append(s0)
        in_specs.append(state_spec)
    out_shape = [jax.ShapeDtypeStruct((b,) + state_shape, F32)]
    out_specs = [state_spec]
    if emit_y:
        out_shape = [jax.ShapeDtypeStruct((b, t, D_MODEL), F32)] * 2 + out_shape
        out_specs = [fwd, bwd] + out_specs
    res = pl.pallas_call(
        functools.partial(_wkv_kernel, chunk=chunk, emit_y=emit_y, has_init=s0 is not None),
        out_shape=tuple(out_shape),
        grid=(b, n),
        in_specs=in_specs,
        out_specs=tuple(out_specs),
        scratch_shapes=[pltpu.VMEM(state_shape, F32),
                        pltpu.VMEM((2, HEAD_PAIRS, 4 * chunk, LANES), BF16),
                        pltpu.VMEM((2, HEAD_PAIRS, 4 * chunk, LANES), BF16),
                        pltpu.VMEM((2, HEAD_PAIRS, 2 * chunk, LANES), BF16),
                        pltpu.VMEM((2, HEAD_PAIRS, 1, LANES), F32)],
        compiler_params=_cparams("parallel", "arbitrary"),
        name="l1_wkv_scan",
    )(*args)
    return res


def _readout_kernel(x_ref, gt_ref, yf_ref, yb_ref, bonus_ref, g_ref, gng_ref, gnb_ref, seg_ref, wo_ref, o_ref):
    seg = seg_ref[...]
    y = yf_ref[0] + yb_ref[0]
    mean = _head_sum(y, seg) * (1.0 / RWKV_HEAD)
    c = y - mean
    var = _head_sum(c * c, seg) * (1.0 / RWKV_HEAD)
    yn = c * lax.rsqrt(var + GN_EPS) * gng_ref[...] + gnb_ref[...]
    out = (yn + bonus_ref[0]) * g_ref[0].astype(F32)
    o_ref[0] = x_ref[0] + gt_ref[0] * _dot(out.astype(BF16), wo_ref[...])


def _readout(x, gt, y_f, y_b, bonus, g, gn_g, gn_b, w_o):
    b, t, _ = x.shape
    tm = min(256, t)
    tok = pl.BlockSpec((1, tm, D_MODEL), lambda bi, i: (bi, i, 0))
    row = pl.BlockSpec((1, D_MODEL), lambda bi, i: (0, 0))
    return pl.pallas_call(
        _readout_kernel,
        out_shape=jax.ShapeDtypeStruct((b, t, D_MODEL), F32),
        grid=(b, t // tm),
        in_specs=[tok, pl.BlockSpec((1, 1, D_MODEL), lambda bi, i: (bi, 0, 0)), tok, tok, tok, tok, row, row,
                  pl.BlockSpec((LANES, LANES), lambda bi, i: (0, 0)),
                  pl.BlockSpec((D_MODEL, D_MODEL), lambda bi, i: (0, 0))],
        out_specs=tok,
        compiler_params=_cparams("parallel", "arbitrary"),
        name="l1_readout",
    )(x, gt, y_f, y_b, bonus, g, gn_g.reshape(1, D_MODEL), gn_b.reshape(1, D_MODEL), _seg_ones(LANES), w_o)


def _modulation(c, c_ctx, w_mod, b_mod):
    b = c.shape[0]
    cond = jnp.zeros((SUBLANES, D_MODEL), F32).at[:b].set(c).at[b].set(c_ctx)
    m = _ada(cond, w_mod, b_mod)
    lat = [m[:b, j * D_MODEL:(j + 1) * D_MODEL].reshape(b, 1, D_MODEL) for j in range(6)]
    ctx = [jnp.broadcast_to(m[b, j * D_MODEL:(j + 1) * D_MODEL].reshape(1, 1, D_MODEL), (b, 1, D_MODEL))
           for j in range(6)]
    return lat, ctx


def kernel(x, c, ctx, c_ctx, l0_w_mod, l0_b_mod, l0_norm_mix, l0_norm_ffn, l0_w_in, l0_conv_w, l0_conv_b, l0_lru_wi, l0_lru_bi, l0_lru_wr, l0_lru_br, l0_lru_lam, l0_q_gain, l0_k_gain, l0_w_out, l0_router, l0_we_gate, l0_we_up, l0_we_down, l1_w_mod, l1_b_mod, l1_norm_mix, l1_norm_ffn, l1_mu, l1_w_r, l1_w_k, l1_w_v, l1_w0, l1_w1, l1_w2, l1_a0, l1_a1, l1_a2, l1_g1, l1_g2, l1_k_k, l1_k_a, l1_r_k, l1_gn_g, l1_gn_b, l1_w_o, l1_router, l1_we_gate, l1_we_up, l1_we_down):
    t = x.shape[1]

    (sh1, sc1, gt1, sh2, sc2, gt2), (csh1, csc1, cgt1, csh2, csc2, cgt2) = _modulation(c, c_ctx, l0_w_mod, l0_b_mod)
    w_in = l0_w_in.astype(BF16)
    cos_t, sin_t = _rope_tables(t)
    xa_l, ga_l, q_l, k_l, v_l = _proj0(x, sc1, sh1, l0_norm_mix, w_in, l0_q_gain, l0_k_gain, cos_t, sin_t, True)
    lc = ctx.shape[1]
    xa_c, ga_c, q_c, k_c, v_c = _proj0(ctx, csc1, csh1, l0_norm_mix, w_in, l0_q_gain, l0_k_gain,
                                       cos_t[:lc], sin_t[:lc], False)
    ya_c, ya_l = _rglru(xa_c, ga_c, xa_l, ga_l, l0_conv_w, l0_conv_b, l0_lru_wi, l0_lru_bi, l0_lru_wr,
                        l0_lru_br, l0_lru_lam)
    seg_l = _kv_layout(k_l, v_l)
    seg_c = _kv_layout(k_c, v_c)
    yb_l = _attention(q_l, [seg_l, seg_c])
    yb_c = _attention(q_c, [seg_c])
    w_out = l0_w_out.astype(BF16)
    w_oa, w_ob = w_out[:LRU_WIDTH], w_out[LRU_WIDTH:]
    x = _residual_matmul(x, gt1, [ya_l, yb_l], [w_oa, w_ob])
    ctx = _residual_matmul(ctx, cgt1, [ya_c, yb_c], [w_oa, w_ob])
    wg, wu, wd = l0_we_gate.astype(BF16), l0_we_up.astype(BF16), l0_we_down.astype(BF16)
    x = _moe(x, sc2, sh2, gt2, l0_norm_ffn, l0_router, wg, wu, wd)
    ctx = _moe(ctx, csc2, csh2, cgt2, l0_norm_ffn, l0_router, wg, wu, wd)

    (sh1, sc1, gt1, sh2, sc2, gt2), (csh1, csc1, _, _, _, _) = _modulation(c, c_ctx, l1_w_mod, l1_b_mod)
    p = _rwkv_params(l1_norm_mix, l1_mu, l1_w_r, l1_w_k, l1_w_v, l1_w0, l1_w1, l1_w2, l1_a0, l1_a1, l1_a2,
                     l1_g1, l1_g2, l1_k_k, l1_k_a, l1_r_k)
    r_c, v_c, _, kk_c, _, lw0_c, lw1_c, kd0_c, kd1_c, ag0_c, ag1_c = _rwkv_prep(ctx, csc1, csh1, l1_norm_mix, p)
    r_l, v_l, g_l, kk_l, bonus_l, lw0, lw1, kd0, kd1, ag0, ag1 = _rwkv_prep(x, sc1, sh1, l1_norm_mix, p)
    (s_ctx,) = _wkv(r_c, v_c, kk_c, (lw0_c, lw1_c), (kd0_c, kd1_c), (ag0_c, ag1_c), None, False)
    y_f, y_b, _ = _wkv(r_l, v_l, kk_l, (lw0, lw1), (kd0, kd1), (ag0, ag1), s_ctx, True)
    x = _readout(x, gt1, y_f, y_b, bonus_l, g_l, l1_gn_g, l1_gn_b, l1_w_o.astype(BF16))
    wg, wu, wd = l1_we_gate.astype(BF16), l1_we_up.astype(BF16), l1_we_down.astype(BF16)
    x = _moe(x, sc2, sh2, gt2, l1_norm_ffn, l1_router, wg, wu, wd)
    return x
```

```python
import functools

import jax
import jax.numpy as jnp
import numpy as np
from jax import lax
from jax.experimental import pallas as pl
from jax.experimental.pallas import tpu as pltpu

F32 = jnp.float32
BF16 = jnp.bfloat16
I32 = jnp.int32

D_MODEL = 1024
LANES = 128
SUBLANES = 8
GRID_W = 64
NORM_EPS = 1e-6
LRU_WIDTH = 512
LRU_BLOCKS = 8
LRU_BLOCK = LRU_WIDTH // LRU_BLOCKS
LRU_C = 8.0
CONV_W = 4
N_Q_HEADS = 8
N_KV_HEADS = 2
HEAD_DIM = 64
Q_GROUP = N_Q_HEADS // N_KV_HEADS
ROPE_BASE = 10000.0
ATTN_SCALE = HEAD_DIM ** -0.5
Q_PRESCALE = ATTN_SCALE * float(np.log2(np.e))
ATTN_KV_CHUNK = 1024
Q_WIDTH = N_Q_HEADS * HEAD_DIM
KV_WIDTH = N_KV_HEADS * HEAD_DIM
IN_WIDTH = 2 * LRU_WIDTH + Q_WIDTH + 2 * KV_WIDTH
RWKV_HEAD = 64
RWKV_HEADS = D_MODEL // RWKV_HEAD
HEAD_PAIRS = D_MODEL // LANES
DECAY_LORA = 64
AAA_LORA = 64
GATE_LORA = 160
GATE_LORA_PAD = 256
GN_EPS = 64e-5
N_EXPERTS = 16
EXPERT_FF = 2048
CAPACITY_FACTOR = 2
ROUTE_PAD = LANES
VMEM_LIMIT = 60 * 1024 * 1024
ROW_DMA_UNROLL = 8


def _cparams(*sem):
    return pltpu.CompilerParams(dimension_semantics=sem, vmem_limit_bytes=VMEM_LIMIT)


def _dot(a, b):
    return jnp.dot(a, b, preferred_element_type=F32)


def _dot_nt(a, b):
    return lax.dot_general(a, b, (((1,), (1,)), ((), ())), preferred_element_type=F32)


def _dot_tn(a, b):
    return lax.dot_general(a, b, (((0,), (0,)), ((), ())), preferred_element_type=F32)


def _split2(x):
    hi = x.astype(BF16)
    lo = (x - hi.astype(F32)).astype(BF16)
    return hi, lo


def _split3(x):
    hi = x.astype(BF16)
    r = x - hi.astype(F32)
    mid = r.astype(BF16)
    lo = (r - mid.astype(F32)).astype(BF16)
    return hi, mid, lo


def _dot_f32(a, b):
    a3 = _split3(a)
    b3 = _split3(b)
    out = None
    for i in range(3):
        for j in range(3 - i):
            d = _dot(a3[i], b3[j])
            out = d if out is None else out + d
    return out


def _softplus(x):
    return jnp.maximum(x, 0.0) + jnp.log1p(jnp.exp(-jnp.abs(x)))


def _gelu_tanh(x):
    c = np.float32(np.sqrt(2.0 / np.pi))
    return 0.5 * x * (1.0 + jnp.tanh(c * (x + 0.044715 * (x * x * x))))


def _norm_mod(x, gain, sc, sh):
    ms = jnp.mean(x * x, axis=-1, keepdims=True)
    y = x * lax.rsqrt(ms + NORM_EPS)
    return (y * gain) * (1.0 + sc) + sh


def _seg_ones(width):
    i = np.arange(width)[:, None] // HEAD_DIM
    j = np.arange(width)[None, :] // HEAD_DIM
    return jnp.asarray(i == j, dtype=BF16)


def _head_sum(x, seg):
    outs = []
    for c in range(x.shape[1] // LANES):
        hi, lo = _split2(x[:, c * LANES:(c + 1) * LANES])
        outs.append(_dot(hi, seg) + _dot(lo, seg))
    return outs[0] if len(outs) == 1 else jnp.concatenate(outs, axis=1)


def _ada_kernel(c_ref, w_ref, b_ref, o_ref):
    s = c_ref[...]
    s = s * jax.nn.sigmoid(s)
    o_ref[...] = _dot_f32(s, w_ref[...]) + b_ref[...]


def _ada(cond8, w_mod, b_mod):
    n = w_mod.shape[1]
    tn = 1536
    return pl.pallas_call(
        _ada_kernel,
        out_shape=jax.ShapeDtypeStruct((SUBLANES, n), F32),
        grid=(n // tn,),
        in_specs=[pl.BlockSpec((SUBLANES, D_MODEL), lambda j: (0, 0)),
                  pl.BlockSpec((D_MODEL, tn), lambda j: (0, j)),
                  pl.BlockSpec((1, tn), lambda j: (0, j))],
        out_specs=pl.BlockSpec((SUBLANES, tn), lambda j: (0, j)),
        compiler_params=_cparams("arbitrary"),
        name="ada_params",
    )(cond8, w_mod, b_mod.reshape(1, n))


def _swap_pairs(x):
    lane = lax.broadcasted_iota(I32, x.shape, 1)
    nxt = pltpu.roll(x, LANES - 1, 1)
    prv = pltpu.roll(x, 1, 1)
    return jnp.where(lane % 2 == 0, nxt, prv)


def _proj0_kernel(x_ref, sc_ref, sh_ref, gain_ref, w_ref, qg_ref, kg_ref, seg_ref, cos_ref, sin_ref,
                  xa_ref, ga_ref, q_ref, k_ref, v_ref, *, rope):
    h = _norm_mod(x_ref[0], gain_ref[...], sc_ref[0], sh_ref[0])
    res = _dot(h.astype(BF16), w_ref[...])
    xa_ref[0] = res[:, :LRU_WIDTH]
    ga_ref[0] = res[:, LRU_WIDTH:2 * LRU_WIDTH]
    q0 = 2 * LRU_WIDTH
    seg = seg_ref[...]

    def head_norm_rope(z, gain):
        ms = _head_sum(z * z, seg) * (1.0 / HEAD_DIM)
        zn = z * lax.rsqrt(ms + NORM_EPS) * gain
        if not rope:
            return zn
        c = cos_ref[...]
        s = sin_ref[...]
        outs = []
        for t in range(zn.shape[1] // LANES):
            zt = zn[:, t * LANES:(t + 1) * LANES]
            outs.append(zt * c + _swap_pairs(zt) * s)
        return outs[0] if len(outs) == 1 else jnp.concatenate(outs, axis=1)

    q = head_norm_rope(res[:, q0:q0 + Q_WIDTH], qg_ref[...])
    q_ref[0] = (q * Q_PRESCALE).astype(BF16)
    k = head_norm_rope(res[:, q0 + Q_WIDTH:q0 + Q_WIDTH + KV_WIDTH], kg_ref[...])
    k_ref[0] = k.astype(BF16)
    v_ref[0] = res[:, q0 + Q_WIDTH + KV_WIDTH:].astype(BF16)


def _proj0(x, sc, sh, gain, w_in, q_gain, k_gain, cos_t, sin_t, rope):
    b, t, _ = x.shape
    tm = min(512, t)
    qg = jnp.tile(q_gain, N_Q_HEADS).reshape(1, Q_WIDTH)
    kg = jnp.tile(k_gain, N_KV_HEADS).reshape(1, KV_WIDTH)
    seg = _seg_ones(LANES)
    const = lambda shape: pl.BlockSpec(shape, lambda bi, i: (0,) * len(shape))
    tok = lambda w: pl.BlockSpec((1, tm, w), lambda bi, i: (bi, i, 0))
    per_b = pl.BlockSpec((1, 1, D_MODEL), lambda bi, i: (bi, 0, 0))
    return pl.pallas_call(
        functools.partial(_proj0_kernel, rope=rope),
        out_shape=(jax.ShapeDtypeStruct((b, t, LRU_WIDTH), F32),
                   jax.ShapeDtypeStruct((b, t, LRU_WIDTH), F32),
                   jax.ShapeDtypeStruct((b, t, Q_WIDTH), BF16),
                   jax.ShapeDtypeStruct((b, t, KV_WIDTH), BF16),
                   jax.ShapeDtypeStruct((b, t, KV_WIDTH), BF16)),
        grid=(b, t // tm),
        in_specs=[tok(D_MODEL), per_b, per_b, const((1, D_MODEL)), const((D_MODEL, IN_WIDTH)),
                  const((1, Q_WIDTH)), const((1, KV_WIDTH)), const((LANES, LANES)),
                  pl.BlockSpec((tm, LANES), lambda bi, i: (i, 0)),
                  pl.BlockSpec((tm, LANES), lambda bi, i: (i, 0))],
        out_specs=(tok(LRU_WIDTH), tok(LRU_WIDTH), tok(Q_WIDTH), tok(KV_WIDTH), tok(KV_WIDTH)),
        compiler_params=_cparams("parallel", "arbitrary"),
        name="l0_in_proj",
    )(x, sc, sh, gain.reshape(1, D_MODEL), w_in, qg, kg, seg, cos_t, sin_t)


def _rope_tables(t):
    n_rows = t // GRID_W
    row = jnp.repeat(jnp.arange(n_rows, dtype=F32), GRID_W)
    col = jnp.tile(jnp.arange(GRID_W, dtype=F32), n_rows)
    axis_dim = HEAD_DIM // 2
    inv_freq = ROPE_BASE ** (-jnp.arange(0, axis_dim, 2, dtype=F32) / axis_dim)
    ang = jnp.concatenate([row[:, None] * inv_freq, col[:, None] * inv_freq], axis=-1)
    cos = jnp.repeat(jnp.cos(ang), 2, axis=-1)
    sin = jnp.repeat(jnp.sin(ang), 2, axis=-1)
    sign = jnp.tile(jnp.asarray([-1.0, 1.0], F32), HEAD_DIM // 2)
    return jnp.tile(cos, (1, 2)), jnp.tile(sin * sign, (1, 2))


def _scan_rows(a, b, h0, rev):
    n = a.shape[0]
    row = lax.broadcasted_iota(I32, a.shape, 0) % SUBLANES
    for d in (1, 2, 4):
        if rev:
            a_s = pltpu.roll(a, n - d, 0)
            b_s = pltpu.roll(b, n - d, 0)
            m = row < SUBLANES - d
        else:
            a_s = pltpu.roll(a, d, 0)
            b_s = pltpu.roll(b, d, 0)
            m = row >= d
        b = jnp.where(m, a * b_s + b, b)
        a = jnp.where(m, a * a_s, a)
    groups = n // SUBLANES
    outs = [None] * groups
    h = h0
    for g in (range(groups - 1, -1, -1) if rev else range(groups)):
        hg = a[g * SUBLANES:(g + 1) * SUBLANES] * h + b[g * SUBLANES:(g + 1) * SUBLANES]
        outs[g] = hg
        h = hg[0:1] if rev else hg[SUBLANES - 1:SUBLANES]
    return jnp.concatenate(outs, axis=0), h


def _rglru_kernel(xc_ref, gc_ref, xl_ref, gl_ref, cw_ref, cb_ref, wg_ref, bg_ref, lam_ref,
                  yc_ref, yl_ref, rec_c, rec_l, *, tt_c, tt_l):
    cw = cw_ref[...]
    cb = cb_ref[...]

    def coeffs(x_ref, t0, tt, d):
        n = x_ref.shape[1]
        main = x_ref[0, pl.ds(t0, tt), :]
        prev = x_ref[0, pl.ds(pl.multiple_of(jnp.maximum(t0 - SUBLANES, 0), SUBLANES), SUBLANES), :]
        prev = jnp.where(t0 > 0, prev, 0.0)
        nxt = x_ref[0, pl.ds(pl.multiple_of(jnp.minimum(t0 + tt, n - SUBLANES), SUBLANES), SUBLANES), :]
        nxt = jnp.where(t0 + tt < n, nxt, 0.0)
        xe = jnp.concatenate([prev, main, nxt], axis=0)
        o = SUBLANES - CONV_W // 2
        xc = cb
        for j in range(CONV_W):
            xc = xc + cw[j:j + 1] * xe[o + j:o + j + tt]
        g = _dot(xc.astype(BF16), wg_ref[d, 0]) + bg_ref[d, 0]
        i_gate = jax.nn.sigmoid(g[:, :LANES])
        r_gate = jax.nn.sigmoid(g[:, LANES:])
        log_a = LRU_C * r_gate * (-_softplus(-lam_ref[d, 0]))
        a = jnp.exp(log_a)
        bco = jnp.sqrt(-jnp.tanh(log_a) * (a * a + 1.0)) * (i_gate * xc)
        return a, bco

    def sweep(x_ref, tt, d, rev, h, emit):
        nch = x_ref.shape[1] // tt

        def body(i, h):
            ci = nch - 1 - i if rev else i
            t0 = pl.multiple_of(ci * tt, tt)
            a, bco = coeffs(x_ref, t0, tt, d)
            hs, h = _scan_rows(a, bco, h, rev)
            emit(t0, tt, hs)
            return h

        return lax.fori_loop(0, nch, body, h)

    def store_rec(rec):
        def emit(t0, tt, hs):
            rec[pl.ds(t0, tt), :] = hs
        return emit

    def store_out(rec, g_ref, y_ref):
        def emit(t0, tt, hs):
            tot = rec[pl.ds(t0, tt), :] + hs
            y_ref[0, pl.ds(t0, tt), :] = (tot * _gelu_tanh(g_ref[0, pl.ds(t0, tt), :])).astype(y_ref.dtype)
        return emit

    zero = jnp.zeros((1, LANES), F32)
    h = sweep(xc_ref, tt_c, 0, False, zero, store_rec(rec_c))
    sweep(xl_ref, tt_l, 0, False, h, store_rec(rec_l))
    h = sweep(xc_ref, tt_c, 1, True, zero, store_out(rec_c, gc_ref, yc_ref))
    sweep(xl_ref, tt_l, 1, True, h, store_out(rec_l, gl_ref, yl_ref))


def _rglru(xa_c, ga_c, xa_l, ga_l, conv_w, conv_b, lru_wi, lru_bi, lru_wr, lru_br, lru_lam):
    b, lc, _ = xa_c.shape
    t = xa_l.shape[1]
    nt = LRU_WIDTH // LANES
    per_tile = LANES // LRU_BLOCK

    def block_diag(w):
        w = w.reshape(2, nt, per_tile, LRU_BLOCK, LRU_BLOCK)
        eye = jnp.eye(per_tile, dtype=w.dtype)
        return jnp.einsum('dtpij,pq->dtpiqj', w, eye).reshape(2, nt, LANES, LANES)

    wg = jnp.concatenate([block_diag(lru_wi), block_diag(lru_wr)], axis=-1).astype(BF16)
    bg = jnp.concatenate([lru_bi.reshape(2, nt, 1, LANES), lru_br.reshape(2, nt, 1, LANES)], axis=-1)
    lam = lru_lam.reshape(2, nt, 1, LANES)
    tt_c = min(256, lc)
    tt_l = min(256, t)
    seq = lambda n: pl.BlockSpec((1, n, LANES), lambda bi, j: (bi, 0, j))
    return pl.pallas_call(
        functools.partial(_rglru_kernel, tt_c=tt_c, tt_l=tt_l),
        out_shape=(jax.ShapeDtypeStruct((b, lc, LRU_WIDTH), BF16),
                   jax.ShapeDtypeStruct((b, t, LRU_WIDTH), BF16)),
        grid=(b, nt),
        in_specs=[seq(lc), seq(lc), seq(t), seq(t),
                  pl.BlockSpec((CONV_W, LANES), lambda bi, j: (0, j)),
                  pl.BlockSpec((1, LANES), lambda bi, j: (0, j)),
                  pl.BlockSpec((2, 1, LANES, 2 * LANES), lambda bi, j: (0, j, 0, 0)),
                  pl.BlockSpec((2, 1, 1, 2 * LANES), lambda bi, j: (0, j, 0, 0)),
                  pl.BlockSpec((2, 1, 1, LANES), lambda bi, j: (0, j, 0, 0))],
        out_specs=(seq(lc), seq(t)),
        scratch_shapes=[pltpu.VMEM((lc, LANES), F32), pltpu.VMEM((t, LANES), F32)],
        compiler_params=_cparams("parallel", "arbitrary"),
        name="l0_rglru",
    )(xa_c, ga_c, xa_l, ga_l, conv_w, conv_b.reshape(1, LRU_WIDTH), wg, bg, lam)


def _attn_kernel(q_ref, *refs, n_seg, kv_chunk):
    kv = refs[:2 * n_seg]
    o_ref = refs[2 * n_seg]
    g = pl.program_id(1)
    q = q_ref[0]
    tq = q.shape[0]
    lane = lax.broadcasted_iota(I32, (1, KV_WIDTH), 1)
    mine = (lane >= g * HEAD_DIM) & (lane < (g + 1) * HEAD_DIM)
    rows = []
    for h in range(Q_GROUP):
        qh = q[:, h * HEAD_DIM:(h + 1) * HEAD_DIM]
        both = jnp.concatenate([qh] * N_KV_HEADS, axis=1)
        rows.append(jnp.where(mine, both, jnp.zeros_like(both)))
    qpad = jnp.concatenate(rows, axis=0)
    nq = Q_GROUP * tq
    m = jnp.full((1, nq), -jnp.inf, F32)
    l = jnp.zeros((1, nq), F32)
    acc = jnp.zeros((HEAD_DIM, nq), F32)
    chunks = []
    for i in range(n_seg):
        tk = kv[2 * i].shape[1]
        ck = min(kv_chunk, tk)
        chunks += [(kv[2 * i], kv[2 * i + 1], c * ck, ck) for c in range(tk // ck)]
    scores = lambda ch: _dot_nt(ch[0][0, ch[2]:ch[2] + ch[3], :], qpad)
    s_next = scores(chunks[0])
    for j, (_, vt_ref, c0, ck) in enumerate(chunks):
        s = s_next
        if j + 1 < len(chunks):
            s_next = scores(chunks[j + 1])
        m_new = jnp.maximum(m, s.max(axis=0, keepdims=True))
        alpha = jnp.exp2(m - m_new)
        p = jnp.exp2(s - m_new)
        l = alpha * l + p.sum(axis=0, keepdims=True)
        acc = alpha * acc + _dot(vt_ref[0, 0, :, c0:c0 + ck], p.astype(BF16))
        m = m_new
    out_t = acc / l
    outs = [out_t[:, h * tq:(h + 1) * tq].T for h in range(Q_GROUP)]
    o_ref[0] = jnp.concatenate(outs, axis=1).astype(o_ref.dtype)


def _attention(q, segs):
    b, t, _ = q.shape
    tq = min(128, t)
    gw = Q_GROUP * HEAD_DIM
    in_specs = [pl.BlockSpec((1, tq, gw), lambda bi, g, i: (bi, i, g))]
    args = [q]
    for k, vt in segs:
        tk = k.shape[1]
        in_specs.append(pl.BlockSpec((1, tk, KV_WIDTH), lambda bi, g, i: (bi, 0, 0)))
        in_specs.append(pl.BlockSpec((1, 1, HEAD_DIM, tk), lambda bi, g, i: (bi, g, 0, 0)))
        args += [k, vt]
    return pl.pallas_call(
        functools.partial(_attn_kernel, n_seg=len(segs), kv_chunk=ATTN_KV_CHUNK),
        out_shape=jax.ShapeDtypeStruct((b, t, Q_WIDTH), BF16),
        grid=(b, N_KV_HEADS, t // tq),
        in_specs=in_specs,
        out_specs=pl.BlockSpec((1, tq, gw), lambda bi, g, i: (bi, i, g)),
        compiler_params=_cparams("parallel", "parallel", "arbitrary"),
        name="l0_attention",
    )(*args)


def _kv_layout(k, v):
    b, t, _ = k.shape
    return k, v.reshape(b, t, N_KV_HEADS, HEAD_DIM).transpose(0, 2, 3, 1)


def _resmm_kernel(x_ref, g_ref, *refs, n):
    acc = None
    for i in range(n):
        d = _dot(refs[i][0], refs[n + i][...])
        acc = d if acc is None else acc + d
    o_ref = refs[2 * n]
    o_ref[0] = x_ref[0] + g_ref[0] * acc


def _residual_matmul(x, gate, acts, weights):
    b, t, _ = x.shape
    tm = min(512, t)
    n = len(acts)
    tok = lambda w: pl.BlockSpec((1, tm, w), lambda bi, i: (bi, i, 0))
    in_specs = [tok(D_MODEL), pl.BlockSpec((1, 1, D_MODEL), lambda bi, i: (bi, 0, 0))]
    in_specs += [tok(a.shape[-1]) for a in acts]
    in_specs += [pl.BlockSpec(w.shape, lambda bi, i: (0, 0)) for w in weights]
    return pl.pallas_call(
        functools.partial(_resmm_kernel, n=n),
        out_shape=jax.ShapeDtypeStruct((b, t, D_MODEL), F32),
        grid=(b, t // tm),
        in_specs=in_specs,
        out_specs=tok(D_MODEL),
        compiler_params=_cparams("parallel", "arbitrary"),
        name="residual_proj",
    )(x, gate, *acts, *weights)


def _router_kernel(x_ref, sc_ref, sh_ref, gain_ref, wr_ref, h_ref, aff_ref):
    h = _norm_mod(x_ref[0], gain_ref[...], sc_ref[0], sh_ref[0])
    tm = h.shape[0]
    logits = _dot_f32(h, wr_ref[...])
    lane = lax.broadcasted_iota(I32, (tm, ROUTE_PAD), 1)
    z = jnp.where(lane < N_EXPERTS, logits, -jnp.inf)
    e = jnp.exp(z - z.max(axis=1, keepdims=True))
    aff = e / e.sum(axis=1, keepdims=True)
    h_ref[0, :, :D_MODEL] = h
    h_ref[0, :, D_MODEL:] = aff
    aff_ref[0] = aff.T[:N_EXPERTS]


def _router(x, sc, sh, gain, w_router):
    b, t, _ = x.shape
    tm = min(512, t)
    wr = jnp.pad(w_router, ((0, 0), (0, ROUTE_PAD - N_EXPERTS)))
    tok = lambda w: pl.BlockSpec((1, tm, w), lambda bi, i: (bi, i, 0))
    per_b = pl.BlockSpec((1, 1, D_MODEL), lambda bi, i: (bi, 0, 0))
    return pl.pallas_call(
        _router_kernel,
        out_shape=(jax.ShapeDtypeStruct((b, t, D_MODEL + ROUTE_PAD), F32),
                   jax.ShapeDtypeStruct((b, N_EXPERTS, t), F32)),
        grid=(b, t // tm),
        in_specs=[tok(D_MODEL), per_b, per_b,
                  pl.BlockSpec((1, D_MODEL), lambda bi, i: (0, 0)),
                  pl.BlockSpec((D_MODEL, ROUTE_PAD), lambda bi, i: (0, 0))],
        out_specs=(tok(D_MODEL + ROUTE_PAD), pl.BlockSpec((1, N_EXPERTS, tm), lambda bi, i: (bi, 0, i))),
        compiler_params=_cparams("parallel", "arbitrary"),
        name="moe_router",
    )(x, sc, sh, gain.reshape(1, D_MODEL), wr)


def _cumsum_lanes(x01, tri):
    outs = []
    off = jnp.zeros((x01.shape[0], 1), F32)
    for j in range(x01.shape[1] // LANES):
        cj = _dot(x01[:, j * LANES:(j + 1) * LANES].astype(BF16), tri) + off
        outs.append(cj)
        off = cj[:, LANES - 1:LANES]
    return outs[0] if len(outs) == 1 else jnp.concatenate(outs, axis=1)


def _select_kernel(aff_ref, tri_ref, tle_ref, idx_ref, pos_s, bnd_v, bnd_s, sem, *, cap, rb):
    aff = aff_ref[0]
    n = aff.shape[1]
    keys = pltpu.bitcast(aff, I32)

    def bit_step(i, tau):
        cand = tau | jnp.left_shift(jnp.int32(1), 30 - i)
        cnt = jnp.sum((keys >= cand).astype(I32), axis=1, keepdims=True)
        return jnp.where(cnt >= cap, cand, tau)

    tau = lax.fori_loop(0, 31, bit_step, jnp.zeros((N_EXPERTS, 1), I32))
    gt = keys > tau
    eq = keys == tau
    need = (cap - jnp.sum(gt.astype(I32), axis=1, keepdims=True)).astype(F32)
    tri = tri_ref[...]
    c_eq = _cumsum_lanes(eq.astype(F32), tri)
    sel = gt | (eq & (c_eq <= need))
    c_sel = _cumsum_lanes(sel.astype(F32), tri)
    pos = jnp.where(sel, c_sel - 1.0, -1.0)
    n_tiles = n // LANES
    n_rb = cap // rb
    for j in range(n_tiles):
        for ei in range(N_EXPERTS):
            pos_s[j, ei] = pos[ei:ei + 1, j * LANES:(j + 1) * LANES]

    ends = _dot(sel.astype(BF16), tle_ref[...])
    lane = lax.broadcasted_iota(I32, (N_EXPERTS, LANES), 1)
    real = lane < n_tiles
    count = lambda m: jnp.sum(jnp.where(real & m, 1, 0), axis=1, keepdims=True)
    bnd = jnp.zeros((N_EXPERTS, LANES), I32)
    for r in range(n_rb):
        lo = count(ends <= float(r * rb))
        hi = jnp.minimum(count(ends < float((r + 1) * rb)), n_tiles - 1)
        bnd = jnp.where(lane == r, lo, bnd)
        bnd = jnp.where(lane == LANES // 2 + r, hi, bnd)
    bnd_v[...] = bnd
    to_smem = pltpu.make_async_copy(bnd_v, bnd_s, sem)
    to_smem.start()
    to_smem.wait()

    def per_expert(e, carry):
        def per_rb(r, carry):
            r0 = jnp.asarray(r * rb, F32)
            slot = lax.broadcasted_iota(I32, (rb, LANES), 0).astype(F32) + r0

            def per_tile(j, acc):
                p = pos_s[j, e]
                tok = (lax.broadcasted_iota(I32, (1, LANES), 1) + j * LANES).astype(F32)
                return acc + jnp.where(p == slot, tok, 0.0)

            acc = lax.fori_loop(bnd_s[e, r], bnd_s[e, LANES // 2 + r] + 1, per_tile,
                                jnp.zeros((rb, LANES), F32))
            tot = jnp.sum(acc, axis=1, keepdims=True)
            idx_ref[0, pl.ds(e, 1), pl.ds(pl.multiple_of(r * rb, rb), rb), :] = (
                jnp.broadcast_to(tot, (rb, LANES)).astype(I32)[None])
            return carry
        return lax.fori_loop(0, n_rb, per_rb, carry)

    lax.fori_loop(0, N_EXPERTS, per_expert, 0)


def _select(aff_t, cap):
    b, _, n = aff_t.shape
    rb = min(64, cap)
    assert cap // rb <= LANES // 2 and n // LANES <= LANES
    tri = jnp.asarray(np.triu(np.ones((LANES, LANES), np.float32)), dtype=BF16)
    tile_le = jnp.asarray(np.arange(n)[:, None] // LANES <= np.arange(LANES)[None, :], dtype=BF16)
    out = pl.pallas_call(
        functools.partial(_select_kernel, cap=cap, rb=rb),
        out_shape=jax.ShapeDtypeStruct((b, N_EXPERTS, cap, LANES), I32),
        grid=(b,),
        in_specs=[pl.BlockSpec((1, N_EXPERTS, n), lambda bi: (bi, 0, 0)),
                  pl.BlockSpec((LANES, LANES), lambda bi: (0, 0)),
                  pl.BlockSpec((n, LANES), lambda bi: (0, 0))],
        out_specs=pl.BlockSpec((1, N_EXPERTS, cap, LANES), lambda bi: (bi, 0, 0, 0)),
        scratch_shapes=[pltpu.VMEM((n // LANES, N_EXPERTS, 1, LANES), F32),
                        pltpu.VMEM((N_EXPERTS, LANES), I32),
                        pltpu.SMEM((N_EXPERTS, LANES), I32),
                        pltpu.SemaphoreType.DMA(())],
        compiler_params=_cparams("arbitrary"),
        name="moe_select",
    )(aff_t, tri, tile_le)
    return out[..., 0]


def _moe_kernel(idx_ref, h_hbm, x_hbm, gt_ref, wg_ref, wu_ref, wd_ref, o_hbm, xe, orow, sems, *, cap, fc):
    del x_hbm
    e = pl.program_id(0)
    b = pl.program_id(1)

    def gather_h(r):
        tok = idx_ref[0, 0, 0, r]
        return pltpu.make_async_copy(h_hbm.at[b, pl.ds(tok, 1), :], xe.at[pl.ds(r, 1), :], sems.at[0])

    def gather_o(r):
        tok = idx_ref[0, 0, 0, r]
        return pltpu.make_async_copy(o_hbm.at[b, pl.ds(tok, 1), :], orow.at[pl.ds(r, 1), :], sems.at[1])

    def scatter_o(r):
        tok = idx_ref[0, 0, 0, r]
        return pltpu.make_async_copy(orow.at[pl.ds(r, 1), :], o_hbm.at[b, pl.ds(tok, 1), :], sems.at[2])

    def for_rows(fn):
        def body(r, c):
            fn(r)
            return c
        lax.fori_loop(0, cap, body, 0, unroll=ROW_DMA_UNROLL)

    def all_rows(hbm, vmem, sem):
        return pltpu.make_async_copy(hbm.at[b, pl.ds(0, cap), :], vmem, sem)

    def start_gathers(r):
        gather_h(r).start()
        gather_o(r).start()

    for_rows(start_gathers)
    all_rows(h_hbm, xe, sems.at[0]).wait()

    xb = xe[:, :D_MODEL].astype(BF16)
    acc = jnp.zeros((cap, D_MODEL), F32)
    for c in range(EXPERT_FF // fc):
        g = _dot(xb, wg_ref[0, :, c * fc:(c + 1) * fc])
        u = _dot(xb, wu_ref[0, :, c * fc:(c + 1) * fc])
        hid = (g * jax.nn.sigmoid(g)) * u
        acc = acc + _dot(hid.astype(BF16), wd_ref[0, c * fc:(c + 1) * fc, :])
    lane = lax.broadcasted_iota(I32, (cap, ROUTE_PAD), 1)
    gate = jnp.sum(jnp.where(lane == e, xe[:, D_MODEL:], 0.0), axis=1, keepdims=True)

    all_rows(o_hbm, orow, sems.at[1]).wait()
    orow[...] = orow[...] + gt_ref[0] * (acc * gate)
    for_rows(lambda r: scatter_o(r).start())
    pltpu.make_async_copy(orow, o_hbm.at[b, pl.ds(0, cap), :], sems.at[2]).wait()


def _moe_apply(idx, h_ext, x, gt, wg, wu, wd):
    b, n, _ = x.shape
    cap = idx.shape[-1]
    fc = 512
    idx4 = idx.reshape(b, N_EXPERTS, 1, cap)
    wspec = lambda shape: pl.BlockSpec((1,) + shape, lambda e, bi: (e, 0, 0))
    return pl.pallas_call(
        functools.partial(_moe_kernel, cap=cap, fc=fc),
        out_shape=jax.ShapeDtypeStruct((b, n, D_MODEL), F32),
        grid=(N_EXPERTS, b),
        in_specs=[pl.BlockSpec((1, 1, 1, cap), lambda e, bi: (bi, e, 0, 0), memory_space=pltpu.SMEM),
                  pl.BlockSpec(memory_space=pl.ANY),
                  pl.BlockSpec(memory_space=pl.ANY),
                  pl.BlockSpec((1, 1, D_MODEL), lambda e, bi: (bi, 0, 0)),
                  wspec((D_MODEL, EXPERT_FF)), wspec((D_MODEL, EXPERT_FF)), wspec((EXPERT_FF, D_MODEL))],
        out_specs=pl.BlockSpec(memory_space=pl.ANY),
        scratch_shapes=[pltpu.VMEM((cap, D_MODEL + ROUTE_PAD), F32),
                        pltpu.VMEM((cap, D_MODEL), F32),
                        pltpu.SemaphoreType.DMA((3,))],
        input_output_aliases={2: 0},
        compiler_params=pltpu.CompilerParams(dimension_semantics=("arbitrary", "arbitrary"),
                                             vmem_limit_bytes=VMEM_LIMIT, has_side_effects=True),
        name="moe_experts",
    )(idx4, h_ext, x, gt, wg, wu, wd)


def _moe(x, sc, sh, gt, gain, w_router, wg, wu, wd):
    n = x.shape[1]
    cap = max(1, CAPACITY_FACTOR * n // N_EXPERTS)
    h_ext, aff_t = _router(x, sc, sh, gain, w_router)
    idx = _select(aff_t, cap)
    return _moe_apply(idx, h_ext, x, gt, wg, wu, wd)


def _rwkv_prep_kernel(xm_ref, xp_ref, xn_ref, sc_ref, sh_ref, gain_ref, mu_ref, wr_ref, wk_ref, wv_ref,
                      w1_ref, w2_ref, a1_ref, a2_ref, g1_ref, g2_ref, w0_ref, a0_ref, kks_ref, ka_ref, rk_ref,
                      seg_ref, r_ref, v_ref, g_ref, kk_ref, bonus_ref, lw0_ref, lw1_ref, kd0_ref, kd1_ref,
                      ag0_ref, ag1_ref):
    i = pl.program_id(1)
    last = pl.num_programs(1) - 1
    tm = xm_ref.shape[1]
    gain = gain_ref[...]
    sc = sc_ref[0]
    sh = sh_ref[0]
    xe = jnp.concatenate([xp_ref[0], xm_ref[0], xn_ref[0]], axis=0)
    he = _norm_mod(xe, gain, sc, sh)
    row = lax.broadcasted_iota(I32, (tm, 1), 0)
    h = he[SUBLANES:SUBLANES + tm]
    hm1 = jnp.where((row == 0) & (i == 0), 0.0, he[SUBLANES - 1:SUBLANES - 1 + tm])
    hp1 = jnp.where((row == tm - 1) & (i == last), 0.0, he[SUBLANES + 1:SUBLANES + 1 + tm])
    xx = 0.5 * (hm1 + hp1) - h
    mu = mu_ref[...]
    mix = lambda j: (h + xx * mu[j:j + 1]).astype(BF16)
    r = _dot(mix(0), wr_ref[...])
    k = _dot(mix(2), wk_ref[...])
    v = _dot(mix(3), wv_ref[...])
    tw = jnp.tanh(_dot(mix(1), w1_ref[...])).astype(BF16)
    la = _dot(mix(4), a1_ref[...]).astype(BF16)
    g = _dot(jax.nn.sigmoid(_dot(mix(5), g1_ref[...])).astype(BF16), g2_ref[...])
    seg = seg_ref[...]
    kkv = k * kks_ref[...]
    kk = kkv * lax.rsqrt(jnp.maximum(_head_sum(kkv * kkv, seg), 1e-24))
    r_ref[0] = r.astype(r_ref.dtype)
    v_ref[0] = v.astype(v_ref.dtype)
    g_ref[0] = g.astype(g_ref.dtype)
    kk_ref[0] = kk.astype(kk_ref.dtype)
    bonus = None
    for d, (lw_ref, kd_ref, ag_ref) in enumerate(((lw0_ref, kd0_ref, ag0_ref), (lw1_ref, kd1_ref, ag1_ref))):
        w_pre = w0_ref[d] + _dot(tw, w2_ref[d])
        lw_ref[0] = -jnp.exp(-_softplus(-w_pre) - 0.5)
        a = jax.nn.sigmoid(a0_ref[d] + _dot(la, a2_ref[d]))
        kd = k * (1.0 + (a - 1.0) * ka_ref[...])
        kd_ref[0] = kd.astype(kd_ref.dtype)
        ag_ref[0] = a.astype(ag_ref.dtype)
        bd = _head_sum(r * kd * rk_ref[...], seg) * v
        bonus = bd if bonus is None else bonus + bd
    bonus_ref[0] = bonus


def _rwkv_prep(x, sc, sh, gain, p):
    b, t, _ = x.shape
    tm = min(256, t)
    nb8 = tm // SUBLANES
    tok = lambda: pl.BlockSpec((1, tm, D_MODEL), lambda bi, i: (bi, i, 0))
    prev = pl.BlockSpec((1, SUBLANES, D_MODEL), lambda bi, i: (bi, jnp.maximum(i * nb8 - 1, 0), 0))
    nxt = pl.BlockSpec((1, SUBLANES, D_MODEL),
                       lambda bi, i: (bi, jnp.minimum((i + 1) * nb8, t // SUBLANES - 1), 0))
    per_b = pl.BlockSpec((1, 1, D_MODEL), lambda bi, i: (bi, 0, 0))
    const = lambda a: pl.BlockSpec(a.shape, lambda bi, i: (0,) * a.ndim)
    consts = [p['gain'], p['mu'], p['w_r'], p['w_k'], p['w_v'], p['w1'], p['w2'], p['a1'], p['a2'], p['g1'],
              p['g2'], p['w0'], p['a0'], p['kks'], p['ka'], p['rk'], p['seg']]
    bf = jax.ShapeDtypeStruct((b, t, D_MODEL), BF16)
    f32 = jax.ShapeDtypeStruct((b, t, D_MODEL), F32)
    return pl.pallas_call(
        _rwkv_prep_kernel,
        out_shape=(bf, bf, bf, bf, f32, f32, f32, bf, bf, bf, bf),
        grid=(b, t // tm),
        in_specs=[tok(), prev, nxt, per_b, per_b] + [const(a) for a in consts],
        out_specs=tuple(tok() for _ in range(11)),
        compiler_params=_cparams("parallel", "arbitrary"),
        name="l1_rwkv_prep",
    )(x, x, x, sc, sh, *consts)


def _rwkv_params(gain, mu, w_r, w_k, w_v, w0, w1, w2, a0, a1, a2, g1, g2, k_k, k_a, r_k):
    def pad_dir(w):
        z = jnp.zeros_like(w[0])
        return jnp.stack([jnp.concatenate([w[0], z], axis=0), jnp.concatenate([z, w[1]], axis=0)])

    row = lambda a: a.reshape(1, D_MODEL)
    return dict(
        gain=row(gain), mu=mu, w_r=w_r.astype(BF16), w_k=w_k.astype(BF16), w_v=w_v.astype(BF16),
        w1=jnp.concatenate([w1[0], w1[1]], axis=1).astype(BF16), w2=pad_dir(w2).astype(BF16),
        a1=jnp.concatenate([a1[0], a1[1]], axis=1).astype(BF16), a2=pad_dir(a2).astype(BF16),
        g1=jnp.pad(g1, ((0, 0), (0, GATE_LORA_PAD - GATE_LORA))).astype(BF16),
        g2=jnp.pad(g2, ((0, GATE_LORA_PAD - GATE_LORA), (0, 0))).astype(BF16),
        w0=w0.reshape(2, 1, D_MODEL), a0=a0.reshape(2, 1, D_MODEL),
        kks=row(k_k), ka=row(k_a), rk=row(r_k), seg=_seg_ones(LANES))


SOLVE_BASE = 8


def _solve_unit_tri(nmat, x, top):
    n = nmat.shape[0]
    rowi = lax.broadcasted_iota(I32, (n, n), 0)
    coli = lax.broadcasted_iota(I32, (n, n), 1)

    def same_block(shift):
        return (rowi >> shift) == (coli >> shift)

    k = int(np.log2(SOLVE_BASE))
    m = jnp.where(same_block(k), nmat, 0.0).astype(BF16)
    tinv = jnp.where(rowi == coli, 1.0, 0.0) + m.astype(F32)
    for _ in range(k - 1):
        m = _dot(m, m).astype(BF16)
        yield None
        tinv = tinv + _dot(m, tinv.astype(BF16))
        yield None
    while (1 << k) < top:
        off = jnp.where(same_block(k + 1) & jnp.logical_not(same_block(k)), nmat, 0.0).astype(BF16)
        tb = tinv.astype(BF16)
        half = _dot(tb, off).astype(BF16)
        yield None
        tinv = tinv + _dot(half, tb)
        yield None
        k += 1
    yield _dot(tinv.astype(BF16), x.astype(BF16))


def _wkv_stage(r_ref, v_ref, kk_ref, lw_ref, kd_ref, ag_ref, tri, ws, zs, vs, gts, d, rev, chunk):
    lw = lw_ref[0]
    p3 = _split3(lw)
    cum = _dot(tri, p3[0]) + _dot(tri, p3[1]) + _dot(tri, p3[2])
    g_in = jnp.exp(cum)
    g_ex = jnp.exp(cum - lw)
    g_inv = jnp.exp(-cum)
    kk = kk_ref[0].astype(F32)
    a_t = -(kk * g_ex)
    r_t = r_ref[0].astype(F32) * g_in
    b_t = kk * ag_ref[0].astype(F32) * g_inv
    k_t = kd_ref[0].astype(F32) * g_inv
    g_tot = jnp.exp(cum[0:1] if rev else cum[chunk - 1:chunk])
    first = (lax.broadcasted_iota(I32, (1, D_MODEL), 1) % LANES) < RWKV_HEAD
    stacked = lambda x: (jnp.where(first, x, jnp.zeros_like(x)), jnp.where(first, jnp.zeros_like(x), x))
    a2, r2, b2, k2 = (stacked(x.astype(BF16)) for x in (a_t, r_t, b_t, k_t))
    v2 = stacked(v_ref[0])
    c2 = 2 * chunk
    for p in range(HEAD_PAIRS):
        sl = slice(p * LANES, (p + 1) * LANES)
        for h in range(2):
            ws[d, p, h * chunk:(h + 1) * chunk] = a2[h][:, sl]
            ws[d, p, c2 + h * chunk:c2 + (h + 1) * chunk] = r2[h][:, sl]
            zs[d, p, h * chunk:(h + 1) * chunk] = b2[h][:, sl]
            zs[d, p, c2 + h * chunk:c2 + (h + 1) * chunk] = k2[h][:, sl]
            vs[d, p, h * chunk:(h + 1) * chunk] = v2[h][:, sl]
        gts[d, p] = g_tot[:, sl]


def _wkv_chain(s_ref, y_ref, ws, zs, vs, gts, d, p, rev, chunk):
    n = 2 * chunk
    w = ws[d, p]
    z = zs[d, p]
    v = vs[d, p]
    s_old = s_ref[d, p]
    pm = _dot_nt(w, z)
    yield
    wst = _dot_nt(w, s_old.astype(BF16))
    yield
    ti = lax.broadcasted_iota(I32, (n, n), 0) & (chunk - 1)
    tj = lax.broadcasted_iota(I32, (n, n), 1) & (chunk - 1)
    strict = ti < tj if rev else ti > tj
    incl = ti <= tj if rev else ti >= tj
    nmat = jnp.where(strict, pm[:n, :n], 0.0)
    ak = jnp.where(strict, pm[:n, n:], 0.0).astype(BF16)
    rbk = jnp.concatenate([jnp.where(incl, pm[n:, :n], 0.0), jnp.where(incl, pm[n:, n:], 0.0)], axis=1).astype(BF16)
    x = wst[:n] + _dot(ak, v)
    yield
    u = None
    for u in _solve_unit_tri(nmat, x, chunk):
        yield
    uv = jnp.concatenate([u.astype(BF16), v], axis=0)
    y = wst[n:] + _dot(rbk, uv)
    yield
    if y_ref is not None:
        y_ref[0, :, p * LANES:(p + 1) * LANES] = y[:chunk] + y[chunk:]
    ds = _dot_tn(uv, z)
    yield
    s_ref[d, p] = (s_old + ds) * gts[d, p]


def _wkv_kernel(*refs, chunk, emit_y, has_init):
    ins = refs[:12]
    tri_ref = refs[12]
    pos = 13
    if has_init:
        s0_ref = refs[pos]
        pos += 1
    yf_ref = yb_ref = None
    if emit_y:
        yf_ref, yb_ref = refs[pos:pos + 2]
        pos += 2
    sfin_ref = refs[pos]
    s_ref, ws, zs, vs, gts = refs[pos + 1:]
    i = pl.program_id(1)

    @pl.when(i == 0)
    def _():
        if has_init:
            s_ref[...] = s0_ref[0]
        else:
            s_ref[...] = jnp.zeros(s_ref.shape, F32)

    for d, rev in ((0, False), (1, True)):
        _wkv_stage(*ins[6 * d:6 * d + 6], tri_ref[d], ws, zs, vs, gts, d, rev, chunk)

    chains = [_wkv_chain(s_ref, (yf_ref, yb_ref)[d], ws, zs, vs, gts, d, p, bool(d), chunk)
              for p in range(HEAD_PAIRS) for d in range(2)]
    while chains:
        alive = []
        for ch in chains:
            try:
                next(ch)
                alive.append(ch)
            except StopIteration:
                pass
        chains = alive

    @pl.when(i == pl.num_programs(1) - 1)
    def _():
        sfin_ref[0] = s_ref[...]


def _wkv(r, v, kk, lw, kd, ag, s0, emit_y, chunk=64):
    b, t, _ = r.shape
    n = t // chunk
    fwd = pl.BlockSpec((1, chunk, D_MODEL), lambda bi, i: (bi, i, 0))
    bwd = pl.BlockSpec((1, chunk, D_MODEL), lambda bi, i: (bi, n - 1 - i, 0))
    tri = jnp.asarray(np.stack([np.tril(np.ones((chunk, chunk), np.float32)),
                                np.triu(np.ones((chunk, chunk), np.float32))]), dtype=BF16)
    state_shape = (2, HEAD_PAIRS, 2 * RWKV_HEAD, LANES)
    state_spec = pl.BlockSpec((1,) + state_shape, lambda bi, i: (bi, 0, 0, 0, 0))
    args = [r, v, kk, lw[0], kd[0], ag[0], r, v, kk, lw[1], kd[1], ag[1], tri]
    in_specs = [fwd] * 6 + [bwd] * 6 + [pl.BlockSpec((2, chunk, chunk), lambda bi, i: (0, 0, 0))]
    if s0 is not None:
        args.append(s0)
        in_specs.append(state_spec)
    out_shape = [jax.ShapeDtypeStruct((b,) + state_shape, F32)]
    out_specs = [state_spec]
    if emit_y:
        out_shape = [jax.ShapeDtypeStruct((b, t, D_MODEL), F32)] * 2 + out_shape
        out_specs = [fwd, bwd] + out_specs
    res = pl.pallas_call(
        functools.partial(_wkv_kernel, chunk=chunk, emit_y=emit_y, has_init=s0 is not None),
        out_shape=tuple(out_shape),
        grid=(b, n),
        in_specs=in_specs,
        out_specs=tuple(out_specs),
        scratch_shapes=[pltpu.VMEM(state_shape, F32),
                        pltpu.VMEM((2, HEAD_PAIRS, 4 * chunk, LANES), BF16),
                        pltpu.VMEM((2, HEAD_PAIRS, 4 * chunk, LANES), BF16),
                        pltpu.VMEM((2, HEAD_PAIRS, 2 * chunk, LANES), BF16),
                        pltpu.VMEM((2, HEAD_PAIRS, 1, LANES), F32)],
        compiler_params=_cparams("parallel", "arbitrary"),
        name="l1_wkv_scan",
    )(*args)
    return res


def _readout_kernel(x_ref, gt_ref, yf_ref, yb_ref, bonus_ref, g_ref, gng_ref, gnb_ref, seg_ref, wo_ref, o_ref):
    seg = seg_ref[...]
    y = yf_ref[0] + yb_ref[0]
    mean = _head_sum(y, seg) * (1.0 / RWKV_HEAD)
    c = y - mean
    var = _head_sum(c * c, seg) * (1.0 / RWKV_HEAD)
    yn = c * lax.rsqrt(var + GN_EPS) * gng_ref[...] + gnb_ref[...]
    out = (yn + bonus_ref[0]) * g_ref[0].astype(F32)
    o_ref[0] = x_ref[0] + gt_ref[0] * _dot(out.astype(BF16), wo_ref[...])


def _readout(x, gt, y_f, y_b, bonus, g, gn_g, gn_b, w_o):
    b, t, _ = x.shape
    tm = min(256, t)
    tok = pl.BlockSpec((1, tm, D_MODEL), lambda bi, i: (bi, i, 0))
    row = pl.BlockSpec((1, D_MODEL), lambda bi, i: (0, 0))
    return pl.pallas_call(
        _readout_kernel,
        out_shape=jax.ShapeDtypeStruct((b, t, D_MODEL), F32),
        grid=(b, t // tm),
        in_specs=[tok, pl.BlockSpec((1, 1, D_MODEL), lambda bi, i: (bi, 0, 0)), tok, tok, tok, tok, row, row,
                  pl.BlockSpec((LANES, LANES), lambda bi, i: (0, 0)),
                  pl.BlockSpec((D_MODEL, D_MODEL), lambda bi, i: (0, 0))],
        out_specs=tok,
        compiler_params=_cparams("parallel", "arbitrary"),
        name="l1_readout",
    )(x, gt, y_f, y_b, bonus, g, gn_g.reshape(1, D_MODEL), gn_b.reshape(1, D_MODEL), _seg_ones(LANES), w_o)


def _modulation(c, c_ctx, w_mod, b_mod):
    b = c.shape[0]
    cond = jnp.zeros((SUBLANES, D_MODEL), F32).at[:b].set(c).at[b].set(c_ctx)
    m = _ada(cond, w_mod, b_mod)
    lat = [m[:b, j * D_MODEL:(j + 1) * D_MODEL].reshape(b, 1, D_MODEL) for j in range(6)]
    ctx = [jnp.broadcast_to(m[b, j * D_MODEL:(j + 1) * D_MODEL].reshape(1, 1, D_MODEL), (b, 1, D_MODEL))
           for j in range(6)]
    return lat, ctx


def kernel(x, c, ctx, c_ctx, l0_w_mod, l0_b_mod, l0_norm_mix, l0_norm_ffn, l0_w_in, l0_conv_w, l0_conv_b, l0_lru_wi, l0_lru_bi, l0_lru_wr, l0_lru_br, l0_lru_lam, l0_q_gain, l0_k_gain, l0_w_out, l0_router, l0_we_gate, l0_we_up, l0_we_down, l1_w_mod, l1_b_mod, l1_norm_mix, l1_norm_ffn, l1_mu, l1_w_r, l1_w_k, l1_w_v, l1_w0, l1_w1, l1_w2, l1_a0, l1_a1, l1_a2, l1_g1, l1_g2, l1_k_k, l1_k_a, l1_r_k, l1_gn_g, l1_gn_b, l1_w_o, l1_router, l1_we_gate, l1_we_up, l1_we_down):
    t = x.shape[1]

    (sh1, sc1, gt1, sh2, sc2, gt2), (csh1, csc1, cgt1, csh2, csc2, cgt2) = _modulation(c, c_ctx, l0_w_mod, l0_b_mod)
    w_in = l0_w_in.astype(BF16)
    cos_t, sin_t = _rope_tables(t)
    xa_l, ga_l, q_l, k_l, v_l = _proj0(x, sc1, sh1, l0_norm_mix, w_in, l0_q_gain, l0_k_gain, cos_t, sin_t, True)
    lc = ctx.shape[1]
    xa_c, ga_c, q_c, k_c, v_c = _proj0(ctx, csc1, csh1, l0_norm_mix, w_in, l0_q_gain, l0_k_gain,
                                       cos_t[:lc], sin_t[:lc], False)
    ya_c, ya_l = _rglru(xa_c, ga_c, xa_l, ga_l, l0_conv_w, l0_conv_b, l0_lru_wi, l0_lru_bi, l0_lru_wr,
                        l0_lru_br, l0_lru_lam)
    seg_l = _kv_layout(k_l, v_l)
    seg_c = _kv_layout(k_c, v_c)
    yb_l = _attention(q_l, [seg_l, seg_c])
    yb_c = _attention(q_c, [seg_c])
    w_out = l0_w_out.astype(BF16)
    w_oa, w_ob = w_out[:LRU_WIDTH], w_out[LRU_WIDTH:]
    x = _residual_matmul(x, gt1, [ya_l, yb_l], [w_oa, w_ob])
    ctx = _residual_matmul(ctx, cgt1, [ya_c, yb_c], [w_oa, w_ob])
    wg, wu, wd = l0_we_gate.astype(BF16), l0_we_up.astype(BF16), l0_we_down.astype(BF16)
    x = _moe(x, sc2, sh2, gt2, l0_norm_ffn, l0_router, wg, wu, wd)
    ctx = _moe(ctx, csc2, csh2, cgt2, l0_norm_ffn, l0_router, wg, wu, wd)

    (sh1, sc1, gt1, sh2, sc2, gt2), (csh1, csc1, _, _, _, _) = _modulation(c, c_ctx, l1_w_mod, l1_b_mod)
    p = _rwkv_params(l1_norm_mix, l1_mu, l1_w_r, l1_w_k, l1_w_v, l1_w0, l1_w1, l1_w2, l1_a0, l1_a1, l1_a2,
                     l1_g1, l1_g2, l1_k_k, l1_k_a, l1_r_k)
    r_c, v_c, _, kk_c, _, lw0_c, lw1_c, kd0_c, kd1_c, ag0_c, ag1_c = _rwkv_prep(ctx, csc1, csh1, l1_norm_mix, p)
    r_l, v_l, g_l, kk_l, bonus_l, lw0, lw1, kd0, kd1, ag0, ag1 = _rwkv_prep(x, sc1, sh1, l1_norm_mix, p)
    (s_ctx,) = _wkv(r_c, v_c, kk_c, (lw0_c, lw1_c), (kd0_c, kd1_c), (ag0_c, ag1_c), None, False)
    y_f, y_b, _ = _wkv(r_l, v_l, kk_l, (lw0, lw1), (kd0, kd1), (ag0, ag1), s_ctx, True)
    x = _readout(x, gt1, y_f, y_b, bonus_l, g_l, l1_gn_g, l1_gn_b, l1_w_o.astype(BF16))
    wg, wu, wd = l1_we_gate.astype(BF16), l1_we_up.astype(BF16), l1_we_down.astype(BF16)
    x = _moe(x, sc2, sh2, gt2, l1_norm_ffn, l1_router, wg, wu, wd)
    return x
```

```python
import functools

import jax
import jax.numpy as jnp
import numpy as np
from jax import lax
from jax.experimental import pallas as pl
from jax.experimental.pallas import tpu as pltpu

F32 = jnp.float32
BF16 = jnp.bfloat16
I32 = jnp.int32

D_MODEL = 1024
LANES = 128
SUBLANES = 8
GRID_W = 64
NORM_EPS = 1e-6
LRU_WIDTH = 512
LRU_BLOCKS = 8
LRU_BLOCK = LRU_WIDTH // LRU_BLOCKS
LRU_C = 8.0
CONV_W = 4
N_Q_HEADS = 8
N_KV_HEADS = 2
HEAD_DIM = 64
Q_GROUP = N_Q_HEADS // N_KV_HEADS
ROPE_BASE = 10000.0
ATTN_SCALE = HEAD_DIM ** -0.5
Q_PRESCALE = ATTN_SCALE * float(np.log2(np.e))
ATTN_KV_CHUNK = 1024
ATTN_Q_TILE = 128
Q_WIDTH = N_Q_HEADS * HEAD_DIM
KV_WIDTH = N_KV_HEADS * HEAD_DIM
IN_WIDTH = 2 * LRU_WIDTH + Q_WIDTH + 2 * KV_WIDTH
RWKV_HEAD = 64
RWKV_HEADS = D_MODEL // RWKV_HEAD
HEAD_PAIRS = D_MODEL // LANES
DECAY_LORA = 64
AAA_LORA = 64
GATE_LORA = 160
GATE_LORA_PAD = 256
GN_EPS = 64e-5
N_EXPERTS = 16
EXPERT_FF = 2048
CAPACITY_FACTOR = 2
ROUTE_PAD = LANES
VMEM_LIMIT = 60 * 1024 * 1024
MOE_TILE = 256
MOE_ALIGN_LOG2 = 4
MOE_WIN = 64 + (1 << MOE_ALIGN_LOG2)
MOE_SLACK = 128
MOE_COL_SPLIT = 3


def _cparams(*sem):
    return pltpu.CompilerParams(dimension_semantics=sem, vmem_limit_bytes=VMEM_LIMIT)


def _dot(a, b):
    return jnp.dot(a, b, preferred_element_type=F32)


def _dot_nt(a, b):
    return lax.dot_general(a, b, (((1,), (1,)), ((), ())), preferred_element_type=F32)


def _dot_tn(a, b):
    return lax.dot_general(a, b, (((0,), (0,)), ((), ())), preferred_element_type=F32)


def _split2(x):
    hi = x.astype(BF16)
    lo = (x - hi.astype(F32)).astype(BF16)
    return hi, lo


def _split3(x):
    hi = x.astype(BF16)
    r = x - hi.astype(F32)
    mid = r.astype(BF16)
    lo = (r - mid.astype(F32)).astype(BF16)
    return hi, mid, lo


def _dot_f32(a, b, pieces=3):
    split = _split3 if pieces == 3 else _split2
    ap = split(a)
    bp = split(b)
    out = None
    for i in range(pieces):
        for j in range(pieces - i):
            d = _dot(ap[i], bp[j])
            out = d if out is None else out + d
    return out


def _softplus(x):
    return jnp.maximum(x, 0.0) + jnp.log1p(jnp.exp(-jnp.abs(x)))


def _gelu_tanh(x):
    c = np.float32(np.sqrt(2.0 / np.pi))
    return 0.5 * x * (1.0 + jnp.tanh(c * (x + 0.044715 * (x * x * x))))


def _norm_mod(x, gain, sc, sh):
    ms = jnp.mean(x * x, axis=-1, keepdims=True)
    y = x * lax.rsqrt(ms + NORM_EPS)
    return (y * gain) * (1.0 + sc) + sh


def _seg_ones(width):
    i = np.arange(width)[:, None] // HEAD_DIM
    j = np.arange(width)[None, :] // HEAD_DIM
    return jnp.asarray(i == j, dtype=BF16)


def _head_sum(x, seg, pieces=2):
    outs = []
    for c in range(x.shape[1] // LANES):
        xc = x[:, c * LANES:(c + 1) * LANES]
        if pieces == 1:
            outs.append(_dot(xc.astype(BF16), seg))
        else:
            hi, lo = _split2(xc)
            outs.append(_dot(hi, seg) + _dot(lo, seg))
    return outs[0] if len(outs) == 1 else jnp.concatenate(outs, axis=1)


def _ada_kernel(c_ref, w_ref, b_ref, o_ref):
    s = c_ref[...]
    s = s * jax.nn.sigmoid(s)
    o_ref[...] = _dot_f32(s, w_ref[...]) + b_ref[...]


def _ada(cond8, w_mod, b_mod):
    n = w_mod.shape[1]
    tn = 1536
    return pl.pallas_call(
        _ada_kernel,
        out_shape=jax.ShapeDtypeStruct((SUBLANES, n), F32),
        grid=(n // tn,),
        in_specs=[pl.BlockSpec((SUBLANES, D_MODEL), lambda j: (0, 0)),
                  pl.BlockSpec((D_MODEL, tn), lambda j: (0, j)),
                  pl.BlockSpec((1, tn), lambda j: (0, j))],
        out_specs=pl.BlockSpec((SUBLANES, tn), lambda j: (0, j)),
        compiler_params=_cparams("arbitrary"),
        name="ada_params",
    )(cond8, w_mod, b_mod.reshape(1, n))


def _swap_pairs(x):
    lane = lax.broadcasted_iota(I32, x.shape, 1)
    nxt = pltpu.roll(x, LANES - 1, 1)
    prv = pltpu.roll(x, 1, 1)
    return jnp.where(lane % 2 == 0, nxt, prv)


def _proj0_kernel(x_ref, sc_ref, sh_ref, gain_ref, w_ref, qg_ref, kg_ref, seg_ref, cos_ref, sin_ref,
                  xa_ref, ga_ref, q_ref, k_ref, v_ref, *, rope):
    h = _norm_mod(x_ref[0], gain_ref[...], sc_ref[0], sh_ref[0])
    res = _dot(h.astype(BF16), w_ref[...])
    xa_ref[0] = res[:, :LRU_WIDTH]
    ga_ref[0] = res[:, LRU_WIDTH:2 * LRU_WIDTH]
    q0 = 2 * LRU_WIDTH
    seg = seg_ref[...]

    def head_norm_rope(z, gain):
        ms = _head_sum(z * z, seg) * (1.0 / HEAD_DIM)
        zn = z * lax.rsqrt(ms + NORM_EPS) * gain
        if not rope:
            return zn
        c = cos_ref[...]
        s = sin_ref[...]
        outs = []
        for t in range(zn.shape[1] // LANES):
            zt = zn[:, t * LANES:(t + 1) * LANES]
            outs.append(zt * c + _swap_pairs(zt) * s)
        return outs[0] if len(outs) == 1 else jnp.concatenate(outs, axis=1)

    q = head_norm_rope(res[:, q0:q0 + Q_WIDTH], qg_ref[...])
    q_ref[0] = (q * Q_PRESCALE).astype(BF16)
    k = head_norm_rope(res[:, q0 + Q_WIDTH:q0 + Q_WIDTH + KV_WIDTH], kg_ref[...])
    k_ref[0] = k.astype(BF16)
    v_ref[0] = res[:, q0 + Q_WIDTH + KV_WIDTH:].astype(BF16)


def _proj0(x, sc, sh, gain, w_in, q_gain, k_gain, cos_t, sin_t, rope):
    b, t, _ = x.shape
    tm = min(512, t)
    qg = jnp.tile(q_gain, N_Q_HEADS).reshape(1, Q_WIDTH)
    kg = jnp.tile(k_gain, N_KV_HEADS).reshape(1, KV_WIDTH)
    seg = _seg_ones(LANES)
    const = lambda shape: pl.BlockSpec(shape, lambda bi, i: (0,) * len(shape))
    tok = lambda w: pl.BlockSpec((1, tm, w), lambda bi, i: (bi, i, 0))
    per_b = pl.BlockSpec((1, 1, D_MODEL), lambda bi, i: (bi, 0, 0))
    return pl.pallas_call(
        functools.partial(_proj0_kernel, rope=rope),
        out_shape=(jax.ShapeDtypeStruct((b, t, LRU_WIDTH), F32),
                   jax.ShapeDtypeStruct((b, t, LRU_WIDTH), F32),
                   jax.ShapeDtypeStruct((b, t, Q_WIDTH), BF16),
                   jax.ShapeDtypeStruct((b, t, KV_WIDTH), BF16),
                   jax.ShapeDtypeStruct((b, t, KV_WIDTH), BF16)),
        grid=(b, t // tm),
        in_specs=[tok(D_MODEL), per_b, per_b, const((1, D_MODEL)), const((D_MODEL, IN_WIDTH)),
                  const((1, Q_WIDTH)), const((1, KV_WIDTH)), const((LANES, LANES)),
                  pl.BlockSpec((tm, LANES), lambda bi, i: (i, 0)),
                  pl.BlockSpec((tm, LANES), lambda bi, i: (i, 0))],
        out_specs=(tok(LRU_WIDTH), tok(LRU_WIDTH), tok(Q_WIDTH), tok(KV_WIDTH), tok(KV_WIDTH)),
        compiler_params=_cparams("parallel", "arbitrary"),
        name="l0_in_proj",
    )(x, sc, sh, gain.reshape(1, D_MODEL), w_in, qg, kg, seg, cos_t, sin_t)


def _rope_tables(t):
    n_rows = t // GRID_W
    row = jnp.repeat(jnp.arange(n_rows, dtype=F32), GRID_W)
    col = jnp.tile(jnp.arange(GRID_W, dtype=F32), n_rows)
    axis_dim = HEAD_DIM // 2
    inv_freq = ROPE_BASE ** (-jnp.arange(0, axis_dim, 2, dtype=F32) / axis_dim)
    ang = jnp.concatenate([row[:, None] * inv_freq, col[:, None] * inv_freq], axis=-1)
    cos = jnp.repeat(jnp.cos(ang), 2, axis=-1)
    sin = jnp.repeat(jnp.sin(ang), 2, axis=-1)
    sign = jnp.tile(jnp.asarray([-1.0, 1.0], F32), HEAD_DIM // 2)
    return jnp.tile(cos, (1, 2)), jnp.tile(sin * sign, (1, 2))


def _scan_rows(a, b, h0, rev):
    n = a.shape[0]
    row = lax.broadcasted_iota(I32, a.shape, 0) % SUBLANES
    for d in (1, 2, 4):
        if rev:
            a_s = pltpu.roll(a, n - d, 0)
            b_s = pltpu.roll(b, n - d, 0)
            m = row < SUBLANES - d
        else:
            a_s = pltpu.roll(a, d, 0)
            b_s = pltpu.roll(b, d, 0)
            m = row >= d
        b = jnp.where(m, a * b_s + b, b)
        a = jnp.where(m, a * a_s, a)
    groups = n // SUBLANES
    outs = [None] * groups
    h = h0
    for g in (range(groups - 1, -1, -1) if rev else range(groups)):
        hg = a[g * SUBLANES:(g + 1) * SUBLANES] * h + b[g * SUBLANES:(g + 1) * SUBLANES]
        outs[g] = hg
        h = hg[0:1] if rev else hg[SUBLANES - 1:SUBLANES]
    return jnp.concatenate(outs, axis=0), h


def _rglru_kernel(xc_ref, gc_ref, xl_ref, gl_ref, cw_ref, cb_ref, wg_ref, bg_ref, lam_ref,
                  yc_ref, yl_ref, rec_c, rec_l, *, tt_c, tt_l):
    cw = cw_ref[...]
    cb = cb_ref[...]

    def coeffs(x_ref, t0, tt, d):
        n = x_ref.shape[1]
        main = x_ref[0, pl.ds(t0, tt), :]
        prev = x_ref[0, pl.ds(pl.multiple_of(jnp.maximum(t0 - SUBLANES, 0), SUBLANES), SUBLANES), :]
        prev = jnp.where(t0 > 0, prev, 0.0)
        nxt = x_ref[0, pl.ds(pl.multiple_of(jnp.minimum(t0 + tt, n - SUBLANES), SUBLANES), SUBLANES), :]
        nxt = jnp.where(t0 + tt < n, nxt, 0.0)
        xe = jnp.concatenate([prev, main, nxt], axis=0)
        o = SUBLANES - CONV_W // 2
        xc = cb
        for j in range(CONV_W):
            xc = xc + cw[j:j + 1] * xe[o + j:o + j + tt]
        g = _dot(xc.astype(BF16), wg_ref[d, 0]) + bg_ref[d, 0]
        i_gate = jax.nn.sigmoid(g[:, :LANES])
        r_gate = jax.nn.sigmoid(g[:, LANES:])
        log_a = LRU_C * r_gate * (-_softplus(-lam_ref[d, 0]))
        a = jnp.exp(log_a)
        bco = jnp.sqrt(-jnp.tanh(log_a) * (a * a + 1.0)) * (i_gate * xc)
        return a, bco

    def sweep(x_ref, tt, d, rev, h, emit):
        nch = x_ref.shape[1] // tt

        def body(i, h):
            ci = nch - 1 - i if rev else i
            t0 = pl.multiple_of(ci * tt, tt)
            a, bco = coeffs(x_ref, t0, tt, d)
            hs, h = _scan_rows(a, bco, h, rev)
            emit(t0, tt, hs)
            return h

        return lax.fori_loop(0, nch, body, h)

    def store_rec(rec):
        def emit(t0, tt, hs):
            rec[pl.ds(t0, tt), :] = hs
        return emit

    def store_out(rec, g_ref, y_ref):
        def emit(t0, tt, hs):
            tot = rec[pl.ds(t0, tt), :] + hs
            y_ref[0, pl.ds(t0, tt), :] = (tot * _gelu_tanh(g_ref[0, pl.ds(t0, tt), :])).astype(y_ref.dtype)
        return emit

    zero = jnp.zeros((1, LANES), F32)
    h = sweep(xc_ref, tt_c, 0, False, zero, store_rec(rec_c))
    sweep(xl_ref, tt_l, 0, False, h, store_rec(rec_l))
    h = sweep(xc_ref, tt_c, 1, True, zero, store_out(rec_c, gc_ref, yc_ref))
    sweep(xl_ref, tt_l, 1, True, h, store_out(rec_l, gl_ref, yl_ref))


def _rglru(xa_c, ga_c, xa_l, ga_l, conv_w, conv_b, lru_wi, lru_bi, lru_wr, lru_br, lru_lam):
    b, lc, _ = xa_c.shape
    t = xa_l.shape[1]
    nt = LRU_WIDTH // LANES
    per_tile = LANES // LRU_BLOCK

    def block_diag(w):
        w = w.reshape(2, nt, per_tile, LRU_BLOCK, LRU_BLOCK)
        eye = jnp.eye(per_tile, dtype=w.dtype)
        return jnp.einsum('dtpij,pq->dtpiqj', w, eye).reshape(2, nt, LANES, LANES)

    wg = jnp.concatenate([block_diag(lru_wi), block_diag(lru_wr)], axis=-1).astype(BF16)
    bg = jnp.concatenate([lru_bi.reshape(2, nt, 1, LANES), lru_br.reshape(2, nt, 1, LANES)], axis=-1)
    lam = lru_lam.reshape(2, nt, 1, LANES)
    tt_c = min(256, lc)
    tt_l = min(256, t)
    seq = lambda n: pl.BlockSpec((1, n, LANES), lambda bi, j: (bi, 0, j))
    return pl.pallas_call(
        functools.partial(_rglru_kernel, tt_c=tt_c, tt_l=tt_l),
        out_shape=(jax.ShapeDtypeStruct((b, lc, LRU_WIDTH), BF16),
                   jax.ShapeDtypeStruct((b, t, LRU_WIDTH), BF16)),
        grid=(b, nt),
        in_specs=[seq(lc), seq(lc), seq(t), seq(t),
                  pl.BlockSpec((CONV_W, LANES), lambda bi, j: (0, j)),
                  pl.BlockSpec((1, LANES), lambda bi, j: (0, j)),
                  pl.BlockSpec((2, 1, LANES, 2 * LANES), lambda bi, j: (0, j, 0, 0)),
                  pl.BlockSpec((2, 1, 1, 2 * LANES), lambda bi, j: (0, j, 0, 0)),
                  pl.BlockSpec((2, 1, 1, LANES), lambda bi, j: (0, j, 0, 0))],
        out_specs=(seq(lc), seq(t)),
        scratch_shapes=[pltpu.VMEM((lc, LANES), F32), pltpu.VMEM((t, LANES), F32)],
        compiler_params=_cparams("parallel", "arbitrary"),
        name="l0_rglru",
    )(xa_c, ga_c, xa_l, ga_l, conv_w, conv_b.reshape(1, LRU_WIDTH), wg, bg, lam)


def _attn_kernel(q_ref, *refs, n_seg, kv_chunk):
    kv = refs[:2 * n_seg]
    o_ref = refs[2 * n_seg]
    g = pl.program_id(1)
    q = q_ref[0]
    tq = q.shape[0]
    lane = lax.broadcasted_iota(I32, (1, KV_WIDTH), 1)
    mine = (lane >= g * HEAD_DIM) & (lane < (g + 1) * HEAD_DIM)
    rows = []
    for h in range(Q_GROUP):
        qh = q[:, h * HEAD_DIM:(h + 1) * HEAD_DIM]
        both = jnp.concatenate([qh] * N_KV_HEADS, axis=1)
        rows.append(jnp.where(mine, both, jnp.zeros_like(both)))
    qpad = jnp.concatenate(rows, axis=0)
    nq = Q_GROUP * tq
    m = jnp.full((1, nq), -jnp.inf, F32)
    l = jnp.zeros((1, nq), F32)
    acc = jnp.zeros((HEAD_DIM, nq), F32)
    chunks = []
    for i in range(n_seg):
        tk = kv[2 * i].shape[1]
        ck = min(kv_chunk, tk)
        chunks += [(kv[2 * i], kv[2 * i + 1], c * ck, ck) for c in range(tk // ck)]
    scores = lambda ch: _dot_nt(ch[0][0, ch[2]:ch[2] + ch[3], :], qpad)
    s_next = scores(chunks[0])
    for j, (_, vt_ref, c0, ck) in enumerate(chunks):
        s = s_next
        if j + 1 < len(chunks):
            s_next = scores(chunks[j + 1])
        m_new = jnp.maximum(m, s.max(axis=0, keepdims=True))
        alpha = jnp.exp2(m - m_new)
        p = jnp.exp2(s - m_new)
        l = alpha * l + p.sum(axis=0, keepdims=True)
        acc = alpha * acc + _dot(vt_ref[0, 0, :, c0:c0 + ck], p.astype(BF16))
        m = m_new
    out_t = acc / l
    outs = [out_t[:, h * tq:(h + 1) * tq].T for h in range(Q_GROUP)]
    o_ref[0] = jnp.concatenate(outs, axis=1).astype(o_ref.dtype)


def _attention(q, segs):
    b, t, _ = q.shape
    tq = min(ATTN_Q_TILE, t)
    gw = Q_GROUP * HEAD_DIM
    in_specs = [pl.BlockSpec((1, tq, gw), lambda bi, g, i: (bi, i, g))]
    args = [q]
    for k, vt in segs:
        tk = k.shape[1]
        in_specs.append(pl.BlockSpec((1, tk, KV_WIDTH), lambda bi, g, i: (bi, 0, 0)))
        in_specs.append(pl.BlockSpec((1, 1, HEAD_DIM, tk), lambda bi, g, i: (bi, g, 0, 0)))
        args += [k, vt]
    return pl.pallas_call(
        functools.partial(_attn_kernel, n_seg=len(segs), kv_chunk=ATTN_KV_CHUNK),
        out_shape=jax.ShapeDtypeStruct((b, t, Q_WIDTH), BF16),
        grid=(b, N_KV_HEADS, t // tq),
        in_specs=in_specs,
        out_specs=pl.BlockSpec((1, tq, gw), lambda bi, g, i: (bi, i, g)),
        compiler_params=_cparams("parallel", "parallel", "arbitrary"),
        name="l0_attention",
    )(*args)


def _kv_layout(k, v):
    b, t, _ = k.shape
    return k, v.reshape(b, t, N_KV_HEADS, HEAD_DIM).transpose(0, 2, 3, 1)


def _resmm_kernel(x_ref, g_ref, *refs, n):
    acc = None
    for i in range(n):
        d = _dot(refs[i][0], refs[n + i][...])
        acc = d if acc is None else acc + d
    o_ref = refs[2 * n]
    o_ref[0] = x_ref[0] + g_ref[0] * acc


def _residual_matmul(x, gate, acts, weights):
    b, t, _ = x.shape
    tm = min(512, t)
    n = len(acts)
    tok = lambda w: pl.BlockSpec((1, tm, w), lambda bi, i: (bi, i, 0))
    in_specs = [tok(D_MODEL), pl.BlockSpec((1, 1, D_MODEL), lambda bi, i: (bi, 0, 0))]
    in_specs += [tok(a.shape[-1]) for a in acts]
    in_specs += [pl.BlockSpec(w.shape, lambda bi, i: (0, 0)) for w in weights]
    return pl.pallas_call(
        functools.partial(_resmm_kernel, n=n),
        out_shape=jax.ShapeDtypeStruct((b, t, D_MODEL), F32),
        grid=(b, t // tm),
        in_specs=in_specs,
        out_specs=tok(D_MODEL),
        compiler_params=_cparams("parallel", "arbitrary"),
        name="residual_proj",
    )(x, gate, *acts, *weights)


def _router_kernel(x_ref, sc_ref, sh_ref, gain_ref, wr_ref, h_ref, aff_ref):
    h = _norm_mod(x_ref[0], gain_ref[...], sc_ref[0], sh_ref[0])
    tm = h.shape[0]
    logits = _dot_f32(h, wr_ref[...], 2)
    lane = lax.broadcasted_iota(I32, (tm, ROUTE_PAD), 1)
    z = jnp.where(lane < N_EXPERTS, logits, -jnp.inf)
    e = jnp.exp(z - z.max(axis=1, keepdims=True))
    aff = e / e.sum(axis=1, keepdims=True)
    h_ref[0, :, :D_MODEL] = h.astype(BF16)
    hi = aff.astype(BF16).astype(F32)
    h_ref[0, :, D_MODEL:] = (hi + pltpu.roll(aff - hi, N_EXPERTS, 1)).astype(BF16)
    aff_ref[0] = aff.T[:N_EXPERTS]


def _router(x, sc, sh, gain, w_router):
    b, t, _ = x.shape
    tm = min(512, t)
    wr = jnp.pad(w_router, ((0, 0), (0, ROUTE_PAD - N_EXPERTS)))
    tok = lambda w: pl.BlockSpec((1, tm, w), lambda bi, i: (bi, i, 0))
    per_b = pl.BlockSpec((1, 1, D_MODEL), lambda bi, i: (bi, 0, 0))
    return pl.pallas_call(
        _router_kernel,
        out_shape=(jax.ShapeDtypeStruct((b, t, D_MODEL + ROUTE_PAD), BF16),
                   jax.ShapeDtypeStruct((b, N_EXPERTS, t), F32)),
        grid=(b, t // tm),
        in_specs=[tok(D_MODEL), per_b, per_b,
                  pl.BlockSpec((1, D_MODEL), lambda bi, i: (0, 0)),
                  pl.BlockSpec((D_MODEL, ROUTE_PAD), lambda bi, i: (0, 0))],
        out_specs=(tok(D_MODEL + ROUTE_PAD), pl.BlockSpec((1, N_EXPERTS, tm), lambda bi, i: (bi, 0, i))),
        compiler_params=_cparams("parallel", "arbitrary"),
        name="moe_router",
    )(x, sc, sh, gain.reshape(1, D_MODEL), wr)


def _cumsum_lanes(x01, tri):
    outs = []
    off = jnp.zeros((x01.shape[0], 1), F32)
    for j in range(x01.shape[1] // LANES):
        cj = _dot(x01[:, j * LANES:(j + 1) * LANES].astype(BF16), tri) + off
        outs.append(cj)
        off = cj[:, LANES - 1:LANES]
    return outs[0] if len(outs) == 1 else jnp.concatenate(outs, axis=1)


def _select_kernel(aff_ref, tri_ref, tile_ref, pos_ref, tab_ref, *, cap):
    aff = aff_ref[0]
    n = aff.shape[1]
    keys = pltpu.bitcast(aff, I32)

    def bit_step(i, tau):
        cand = tau | jnp.left_shift(jnp.int32(1), 30 - i)
        cnt = jnp.sum((keys >= cand).astype(I32), axis=1, keepdims=True)
        return jnp.where(cnt >= cap, cand, tau)

    tau = lax.fori_loop(0, 31, bit_step, jnp.zeros((N_EXPERTS, 1), I32))
    gt = keys > tau
    eq = keys == tau
    need = (cap - jnp.sum(gt.astype(I32), axis=1, keepdims=True)).astype(F32)
    tri = tri_ref[...]
    c_eq = _cumsum_lanes(eq.astype(F32), tri)
    sel = gt | (eq & (c_eq <= need))
    c_sel = _cumsum_lanes(sel.astype(F32), tri)
    pos_ref[0] = jnp.where(sel, c_sel - 1.0, -1.0)
    tab_ref[0] = _dot(sel.astype(BF16), tile_ref[...]).astype(I32)


def _select(aff_t, cap):
    b, _, n = aff_t.shape
    assert n % MOE_TILE == 0 and n // MOE_TILE <= LANES // 2
    tri = jnp.asarray(np.triu(np.ones((LANES, LANES), np.float32)), dtype=BF16)
    tok = np.arange(n)[:, None]
    j = np.arange(LANES // 2)[None, :]
    tile_tab = jnp.asarray(np.concatenate([tok < j * MOE_TILE, tok // MOE_TILE == j], axis=1), dtype=BF16)
    return pl.pallas_call(
        functools.partial(_select_kernel, cap=cap),
        out_shape=(jax.ShapeDtypeStruct((b, N_EXPERTS, n), F32),
                   jax.ShapeDtypeStruct((b, N_EXPERTS, LANES), I32)),
        grid=(b,),
        in_specs=[pl.BlockSpec((1, N_EXPERTS, n), lambda bi: (bi, 0, 0)),
                  pl.BlockSpec((LANES, LANES), lambda bi: (0, 0)),
                  pl.BlockSpec((n, LANES), lambda bi: (0, 0))],
        out_specs=(pl.BlockSpec((1, N_EXPERTS, n), lambda bi: (bi, 0, 0)),
                   pl.BlockSpec((1, N_EXPERTS, LANES), lambda bi: (bi, 0, 0))),
        compiler_params=_cparams("parallel"),
        name="moe_select",
    )(aff_t, tri, tile_tab)


def _tile_windows(tab_ref, pos_ref, j, win):
    slot = lax.broadcasted_iota(I32, (win, MOE_TILE), 0).astype(F32)
    out = []
    for e in range(N_EXPERTS):
        start = tab_ref[0, e, j]
        count = tab_ref[0, e, LANES // 2 + j]
        a0 = pl.multiple_of(lax.shift_left(lax.shift_right_logical(start, MOE_ALIGN_LOG2), MOE_ALIGN_LOG2),
                            1 << MOE_ALIGN_LOG2)
        p = pos_ref[0, e:e + 1, :]
        onehot = lambda first, p=p: (p == slot + jnp.asarray(first, F32)).astype(BF16)
        extra = jnp.maximum(start - a0 + count - 1, 0) // win
        out.append((a0, onehot, extra))
    return out


def _moe_gather_kernel(tab_ref, h_ref, pos_ref, o_ref, *, win):
    j = pl.program_id(2)

    @pl.when(j == 0)
    def _():
        o_ref[...] = jnp.zeros(o_ref.shape, o_ref.dtype)

    h = h_ref[0]
    wins = _tile_windows(tab_ref, pos_ref, j, win)
    x = _dot(jnp.concatenate([oh(a0) for a0, oh, _ in wins], axis=0), h).astype(BF16)
    for e, (a0, oh, extra) in enumerate(wins):
        o_ref[0, e, pl.ds(a0, win), :] = o_ref[0, e, pl.ds(a0, win), :] + x[e * win:(e + 1) * win]

        def more(k, c, e=e, a0=a0, oh=oh):
            ak = pl.multiple_of(a0 + (k + 1) * win, 1 << MOE_ALIGN_LOG2)
            o_ref[0, e, pl.ds(ak, win), :] = o_ref[0, e, pl.ds(ak, win), :] + _dot(oh(ak), h).astype(BF16)
            return c

        lax.fori_loop(0, extra, more, 0)


def _moe_gather(h_ext, pos, tab, cap):
    b, n, width = h_ext.shape
    cols = width // MOE_COL_SPLIT
    rows = cap + MOE_SLACK
    return pl.pallas_call(
        functools.partial(_moe_gather_kernel, win=MOE_WIN),
        out_shape=jax.ShapeDtypeStruct((b, N_EXPERTS, rows, width), BF16),
        grid=(b, MOE_COL_SPLIT, n // MOE_TILE),
        in_specs=[pl.BlockSpec((1, N_EXPERTS, LANES), lambda bi, c, j: (bi, 0, 0), memory_space=pltpu.SMEM),
                  pl.BlockSpec((1, MOE_TILE, cols), lambda bi, c, j: (bi, j, c)),
                  pl.BlockSpec((1, N_EXPERTS, MOE_TILE), lambda bi, c, j: (bi, 0, j))],
        out_specs=pl.BlockSpec((1, N_EXPERTS, rows, cols), lambda bi, c, j: (bi, 0, 0, c)),
        compiler_params=_cparams("parallel", "parallel", "arbitrary"),
        name="moe_gather",
    )(tab, h_ext, pos)


def _moe_ffn_kernel(x_ref, wg_ref, wu_ref, wd_ref, y_ref, *, cap, fc):
    e = pl.program_id(0)
    x = x_ref[0, 0]
    xb = x[:, :D_MODEL]
    acc = jnp.zeros((cap, D_MODEL), F32)
    for c in range(EXPERT_FF // fc):
        g = _dot(xb, wg_ref[0, :, c * fc:(c + 1) * fc])
        u = _dot(xb, wu_ref[0, :, c * fc:(c + 1) * fc])
        hid = (g * jax.nn.sigmoid(g)) * u
        acc = acc + _dot(hid.astype(BF16), wd_ref[0, c * fc:(c + 1) * fc, :])
    lane = lax.broadcasted_iota(I32, (cap, ROUTE_PAD), 1)
    mine = (lane == e) | (lane == e + N_EXPERTS)
    gate = jnp.sum(jnp.where(mine, x[:, D_MODEL:].astype(F32), 0.0), axis=1, keepdims=True)
    y_ref[0, 0, :cap] = (acc * gate).astype(y_ref.dtype)
    y_ref[0, 0, cap:] = jnp.zeros((y_ref.shape[2] - cap, D_MODEL), y_ref.dtype)


def _moe_ffn(xe, wg, wu, wd, cap):
    b, _, rows, width = xe.shape
    wspec = lambda shape: pl.BlockSpec((1,) + shape, lambda e, bi: (e, 0, 0))
    return pl.pallas_call(
        functools.partial(_moe_ffn_kernel, cap=cap, fc=512),
        out_shape=jax.ShapeDtypeStruct((b, N_EXPERTS, rows, D_MODEL), BF16),
        grid=(N_EXPERTS, b),
        in_specs=[pl.BlockSpec((1, 1, cap, width), lambda e, bi: (bi, e, 0, 0)),
                  wspec((D_MODEL, EXPERT_FF)), wspec((D_MODEL, EXPERT_FF)), wspec((EXPERT_FF, D_MODEL))],
        out_specs=pl.BlockSpec((1, 1, rows, D_MODEL), lambda e, bi: (bi, e, 0, 0)),
        compiler_params=_cparams("arbitrary", "arbitrary"),
        name="moe_experts",
    )(xe, wg, wu, wd)


def _moe_combine_kernel(tab_ref, x_ref, gt_ref, pos_ref, y_ref, o_ref, acc_s, *, win):
    j = pl.program_id(2)
    wins = _tile_windows(tab_ref, pos_ref, j, win)
    onehots = jnp.concatenate([oh(a0) for a0, oh, _ in wins], axis=0)
    rows = jnp.concatenate([y_ref[0, e, pl.ds(a0, win), :] for e, (a0, _, _) in enumerate(wins)], axis=0)
    acc_s[...] = _dot_tn(onehots, rows)
    for e, (a0, oh, extra) in enumerate(wins):
        def more(k, c, e=e, a0=a0, oh=oh):
            ak = pl.multiple_of(a0 + (k + 1) * win, 1 << MOE_ALIGN_LOG2)
            acc_s[...] += _dot_tn(oh(ak), y_ref[0, e, pl.ds(ak, win), :])
            return c

        lax.fori_loop(0, extra, more, 0)
    o_ref[0] = x_ref[0] + gt_ref[0] * acc_s[...]


def _moe_combine(x, gt, pos, tab, y):
    b, n, _ = x.shape
    rows = y.shape[2]
    cols = D_MODEL // 2
    tile = lambda: pl.BlockSpec((1, MOE_TILE, cols), lambda bi, c, j: (bi, j, c))
    return pl.pallas_call(
        functools.partial(_moe_combine_kernel, win=MOE_WIN),
        out_shape=jax.ShapeDtypeStruct((b, n, D_MODEL), F32),
        grid=(b, 2, n // MOE_TILE),
        in_specs=[pl.BlockSpec((1, N_EXPERTS, LANES), lambda bi, c, j: (bi, 0, 0), memory_space=pltpu.SMEM),
                  tile(),
                  pl.BlockSpec((1, 1, cols), lambda bi, c, j: (bi, 0, c)),
                  pl.BlockSpec((1, N_EXPERTS, MOE_TILE), lambda bi, c, j: (bi, 0, j)),
                  pl.BlockSpec((1, N_EXPERTS, rows, cols), lambda bi, c, j: (bi, 0, 0, c))],
        out_specs=tile(),
        scratch_shapes=[pltpu.VMEM((MOE_TILE, cols), F32)],
        compiler_params=_cparams("parallel", "parallel", "arbitrary"),
        name="moe_combine",
    )(tab, x, gt, pos, y)


def _moe(x, sc, sh, gt, gain, w_router, wg, wu, wd):
    n = x.shape[1]
    cap = max(1, CAPACITY_FACTOR * n // N_EXPERTS)
    h_ext, aff_t = _router(x, sc, sh, gain, w_router)
    pos, tab = _select(aff_t, cap)
    y = _moe_ffn(_moe_gather(h_ext, pos, tab, cap), wg, wu, wd, cap)
    return _moe_combine(x, gt, pos, tab, y)


def _rwkv_prep_kernel(xm_ref, xp_ref, xn_ref, sc_ref, sh_ref, gain_ref, mu_ref, wr_ref, wk_ref, wv_ref,
                      w1_ref, w2_ref, a1_ref, a2_ref, g1_ref, g2_ref, w0_ref, a0_ref, kks_ref, ka_ref, rk_ref,
                      seg_ref, r_ref, v_ref, g_ref, kk_ref, bonus_ref, lw0_ref, lw1_ref, kd0_ref, kd1_ref,
                      ag0_ref, ag1_ref):
    i = pl.program_id(1)
    last = pl.num_programs(1) - 1
    tm = xm_ref.shape[1]
    gain = gain_ref[...]
    sc = sc_ref[0]
    sh = sh_ref[0]
    xe = jnp.concatenate([xp_ref[0], xm_ref[0], xn_ref[0]], axis=0)
    he = _norm_mod(xe, gain, sc, sh)
    row = lax.broadcasted_iota(I32, (tm, 1), 0)
    h = he[SUBLANES:SUBLANES + tm]
    hm1 = jnp.where((row == 0) & (i == 0), 0.0, he[SUBLANES - 1:SUBLANES - 1 + tm])
    hp1 = jnp.where((row == tm - 1) & (i == last), 0.0, he[SUBLANES + 1:SUBLANES + 1 + tm])
    xx = 0.5 * (hm1 + hp1) - h
    mu = mu_ref[...]
    mix = lambda j: (h + xx * mu[j:j + 1]).astype(BF16)
    r = _dot(mix(0), wr_ref[...])
    k = _dot(mix(2), wk_ref[...])
    v = _dot(mix(3), wv_ref[...])
    tw = jnp.tanh(_dot(mix(1), w1_ref[...])).astype(BF16)
    la = _dot(mix(4), a1_ref[...]).astype(BF16)
    g = _dot(jax.nn.sigmoid(_dot(mix(5), g1_ref[...])).astype(BF16), g2_ref[...])
    seg = seg_ref[...]
    kkv = k * kks_ref[...]
    kk = kkv * lax.rsqrt(jnp.maximum(_head_sum(kkv * kkv, seg, 1), 1e-24))
    r_ref[0] = r.astype(r_ref.dtype)
    v_ref[0] = v.astype(v_ref.dtype)
    g_ref[0] = g.astype(g_ref.dtype)
    kk_ref[0] = kk.astype(kk_ref.dtype)
    bonus = None
    for d, (lw_ref, kd_ref, ag_ref) in enumerate(((lw0_ref, kd0_ref, ag0_ref), (lw1_ref, kd1_ref, ag1_ref))):
        w_pre = w0_ref[d] + _dot(tw, w2_ref[d])
        lw_ref[0] = -jnp.exp(-_softplus(-w_pre) - 0.5)
        a = jax.nn.sigmoid(a0_ref[d] + _dot(la, a2_ref[d]))
        kd = k * (1.0 + (a - 1.0) * ka_ref[...])
        kd_ref[0] = kd.astype(kd_ref.dtype)
        ag_ref[0] = a.astype(ag_ref.dtype)
        bd = _head_sum(r * kd * rk_ref[...], seg, 1) * v
        bonus = bd if bonus is None else bonus + bd
    bonus_ref[0] = bonus


def _rwkv_prep(x, sc, sh, gain, p):
    b, t, _ = x.shape
    tm = min(256, t)
    nb8 = tm // SUBLANES
    tok = lambda: pl.BlockSpec((1, tm, D_MODEL), lambda bi, i: (bi, i, 0))
    prev = pl.BlockSpec((1, SUBLANES, D_MODEL), lambda bi, i: (bi, jnp.maximum(i * nb8 - 1, 0), 0))
    nxt = pl.BlockSpec((1, SUBLANES, D_MODEL),
                       lambda bi, i: (bi, jnp.minimum((i + 1) * nb8, t // SUBLANES - 1), 0))
    per_b = pl.BlockSpec((1, 1, D_MODEL), lambda bi, i: (bi, 0, 0))
    const = lambda a: pl.BlockSpec(a.shape, lambda bi, i: (0,) * a.ndim)
    consts = [p['gain'], p['mu'], p['w_r'], p['w_k'], p['w_v'], p['w1'], p['w2'], p['a1'], p['a2'], p['g1'],
              p['g2'], p['w0'], p['a0'], p['kks'], p['ka'], p['rk'], p['seg']]
    bf = jax.ShapeDtypeStruct((b, t, D_MODEL), BF16)
    f32 = jax.ShapeDtypeStruct((b, t, D_MODEL), F32)
    return pl.pallas_call(
        _rwkv_prep_kernel,
        out_shape=(bf, bf, bf, bf, f32, f32, f32, bf, bf, bf, bf),
        grid=(b, t // tm),
        in_specs=[tok(), prev, nxt, per_b, per_b] + [const(a) for a in consts],
        out_specs=tuple(tok() for _ in range(11)),
        compiler_params=_cparams("parallel", "arbitrary"),
        name="l1_rwkv_prep",
    )(x, x, x, sc, sh, *consts)


def _rwkv_params(gain, mu, w_r, w_k, w_v, w0, w1, w2, a0, a1, a2, g1, g2, k_k, k_a, r_k):
    def pad_dir(w):
        z = jnp.zeros_like(w[0])
        return jnp.stack([jnp.concatenate([w[0], z], axis=0), jnp.concatenate([z, w[1]], axis=0)])

    row = lambda a: a.reshape(1, D_MODEL)
    return dict(
        gain=row(gain), mu=mu, w_r=w_r.astype(BF16), w_k=w_k.astype(BF16), w_v=w_v.astype(BF16),
        w1=jnp.concatenate([w1[0], w1[1]], axis=1).astype(BF16), w2=pad_dir(w2).astype(BF16),
        a1=jnp.concatenate([a1[0], a1[1]], axis=1).astype(BF16), a2=pad_dir(a2).astype(BF16),
        g1=jnp.pad(g1, ((0, 0), (0, GATE_LORA_PAD - GATE_LORA))).astype(BF16),
        g2=jnp.pad(g2, ((0, GATE_LORA_PAD - GATE_LORA), (0, 0))).astype(BF16),
        w0=w0.reshape(2, 1, D_MODEL), a0=a0.reshape(2, 1, D_MODEL),
        kks=row(k_k), ka=row(k_a), rk=row(r_k), seg=_seg_ones(LANES))


SOLVE_BASE = 8


def _solve_unit_tri(nmat, x, top):
    n = nmat.shape[0]
    rowi = lax.broadcasted_iota(I32, (n, n), 0)
    coli = lax.broadcasted_iota(I32, (n, n), 1)

    def same_block(shift):
        return (rowi >> shift) == (coli >> shift)

    k = int(np.log2(SOLVE_BASE))
    m = jnp.where(same_block(k), nmat, 0.0).astype(BF16)
    tinv = jnp.where(rowi == coli, 1.0, 0.0) + m.astype(F32)
    for _ in range(k - 1):
        m = _dot(m, m).astype(BF16)
        yield None
        tinv = tinv + _dot(m, tinv.astype(BF16))
        yield None
    while (1 << k) < top:
        off = jnp.where(same_block(k + 1) & jnp.logical_not(same_block(k)), nmat, 0.0).astype(BF16)
        tb = tinv.astype(BF16)
        half = _dot(tb, off).astype(BF16)
        yield None
        tinv = tinv + _dot(half, tb)
        yield None
        k += 1
    yield _dot(tinv.astype(BF16), x.astype(BF16))


def _wkv_stage(r_ref, v_ref, kk_ref, lw_ref, kd_ref, ag_ref, tri, ws, zs, vs, gts, d, rev, chunk):
    lw = lw_ref[0]
    p3 = _split3(lw)
    cum = _dot(tri, p3[0]) + _dot(tri, p3[1]) + _dot(tri, p3[2])
    g_in = jnp.exp(cum)
    g_ex = jnp.exp(cum - lw)
    g_inv = jnp.exp(-cum)
    kk = kk_ref[0].astype(F32)
    a_t = -(kk * g_ex)
    r_t = r_ref[0].astype(F32) * g_in
    b_t = kk * ag_ref[0].astype(F32) * g_inv
    k_t = kd_ref[0].astype(F32) * g_inv
    g_tot = jnp.exp(cum[0:1] if rev else cum[chunk - 1:chunk])
    first = (lax.broadcasted_iota(I32, (1, D_MODEL), 1) % LANES) < RWKV_HEAD
    stacked = lambda x: (jnp.where(first, x, jnp.zeros_like(x)), jnp.where(first, jnp.zeros_like(x), x))
    a2, r2, b2, k2 = (stacked(x.astype(BF16)) for x in (a_t, r_t, b_t, k_t))
    v2 = stacked(v_ref[0])
    c2 = 2 * chunk
    for p in range(HEAD_PAIRS):
        sl = slice(p * LANES, (p + 1) * LANES)
        for h in range(2):
            ws[d, p, h * chunk:(h + 1) * chunk] = a2[h][:, sl]
            ws[d, p, c2 + h * chunk:c2 + (h + 1) * chunk] = r2[h][:, sl]
            zs[d, p, h * chunk:(h + 1) * chunk] = b2[h][:, sl]
            zs[d, p, c2 + h * chunk:c2 + (h + 1) * chunk] = k2[h][:, sl]
            vs[d, p, h * chunk:(h + 1) * chunk] = v2[h][:, sl]
        gts[d, p] = g_tot[:, sl]


def _wkv_chain(s_ref, y_ref, ws, zs, vs, gts, d, p, rev, chunk):
    n = 2 * chunk
    w = ws[d, p]
    z = zs[d, p]
    v = vs[d, p]
    s_old = s_ref[d, p]
    pm = _dot_nt(w, z)
    yield
    wst = _dot_nt(w, s_old.astype(BF16))
    yield
    ti = lax.broadcasted_iota(I32, (n, n), 0) & (chunk - 1)
    tj = lax.broadcasted_iota(I32, (n, n), 1) & (chunk - 1)
    strict = ti < tj if rev else ti > tj
    incl = ti <= tj if rev else ti >= tj
    nmat = jnp.where(strict, pm[:n, :n], 0.0)
    ak = jnp.where(strict, pm[:n, n:], 0.0).astype(BF16)
    rbk = jnp.concatenate([jnp.where(incl, pm[n:, :n], 0.0), jnp.where(incl, pm[n:, n:], 0.0)], axis=1).astype(BF16)
    x = wst[:n] + _dot(ak, v)
    yield
    u = None
    for u in _solve_unit_tri(nmat, x, chunk):
        yield
    uv = jnp.concatenate([u.astype(BF16), v], axis=0)
    y = wst[n:] + _dot(rbk, uv)
    yield
    if y_ref is not None:
        y_ref[0, :, p * LANES:(p + 1) * LANES] = y[:chunk] + y[chunk:]
    ds = _dot_tn(uv, z)
    yield
    s_ref[d, p] = (s_old + ds) * gts[d, p]


def _wkv_kernel(*refs, chunk, emit_y, has_init):
    ins = refs[:12]
    tri_ref = refs[12]
    pos = 13
    if has_init:
        s0_ref = refs[pos]
        pos += 1
    yf_ref = yb_ref = None
    if emit_y:
        yf_ref, yb_ref = refs[pos:pos + 2]
        pos += 2
    sfin_ref = refs[pos]
    s_ref, ws, zs, vs, gts = refs[pos + 1:]
    i = pl.program_id(1)

    @pl.when(i == 0)
    def _():
        if has_init:
            s_ref[...] = s0_ref[0]
        else:
            s_ref[...] = jnp.zeros(s_ref.shape, F32)

    for d, rev in ((0, False), (1, True)):
        _wkv_stage(*ins[6 * d:6 * d + 6], tri_ref[d], ws, zs, vs, gts, d, rev, chunk)

    chains = [_wkv_chain(s_ref, (yf_ref, yb_ref)[d], ws, zs, vs, gts, d, p, bool(d), chunk)
              for p in range(HEAD_PAIRS) for d in range(2)]
    while chains:
        alive = []
        for ch in chains:
            try:
                next(ch)
                alive.append(ch)
            except StopIteration:
                pass
        chains = alive

    @pl.when(i == pl.num_programs(1) - 1)
    def _():
        sfin_ref[0] = s_ref[...]


def _wkv(r, v, kk, lw, kd, ag, s0, emit_y, chunk=64):
    b, t, _ = r.shape
    n = t // chunk
    fwd = pl.BlockSpec((1, chunk, D_MODEL), lambda bi, i: (bi, i, 0))
    bwd = pl.BlockSpec((1, chunk, D_MODEL), lambda bi, i: (bi, n - 1 - i, 0))
    tri = jnp.asarray(np.stack([np.tril(np.ones((chunk, chunk), np.float32)),
                                np.triu(np.ones((chunk, chunk), np.float32))]), dtype=BF16)
    state_shape = (2, HEAD_PAIRS, 2 * RWKV_HEAD, LANES)
    state_spec = pl.BlockSpec((1,) + state_shape, lambda bi, i: (bi, 0, 0, 0, 0))
    args = [r, v, kk, lw[0], kd[0], ag[0], r, v, kk, lw[1], kd[1], ag[1], tri]
    in_specs = [fwd] * 6 + [bwd] * 6 + [pl.BlockSpec((2, chunk, chunk), lambda bi, i: (0, 0, 0))]
    if s0 is not None:
        args.append(s0)
        in_specs.append(state_spec)
    out_shape = [jax.ShapeDtypeStruct((b,) + state_shape, F32)]
    out_specs = [state_spec]
    if emit_y:
        out_shape = [jax.ShapeDtypeStruct((b, t, D_MODEL), F32)] * 2 + out_shape
        out_specs = [fwd, bwd] + out_specs
    res = pl.pallas_call(
        functools.partial(_wkv_kernel, chunk=chunk, emit_y=emit_y, has_init=s0 is not None),
        out_shape=tuple(out_shape),
        grid=(b, n),
        in_specs=in_specs,
        out_specs=tuple(out_specs),
        scratch_shapes=[pltpu.VMEM(state_shape, F32),
                        pltpu.VMEM((2, HEAD_PAIRS, 4 * chunk, LANES), BF16),
                        pltpu.VMEM((2, HEAD_PAIRS, 4 * chunk, LANES), BF16),
                        pltpu.VMEM((2, HEAD_PAIRS, 2 * chunk, LANES), BF16),
                        pltpu.VMEM((2, HEAD_PAIRS, 1, LANES), F32)],
        compiler_params=_cparams("parallel", "arbitrary"),
        name="l1_wkv_scan",
    )(*args)
    return res


def _readout_kernel(x_ref, gt_ref, yf_ref, yb_ref, bonus_ref, g_ref, gng_ref, gnb_ref, seg_ref, wo_ref, o_ref):
    seg = seg_ref[...]
    y = yf_ref[0] + yb_ref[0]
    mean = _head_sum(y, seg) * (1.0 / RWKV_HEAD)
    c = y - mean
    var = _head_sum(c * c, seg) * (1.0 / RWKV_HEAD)
    yn = c * lax.rsqrt(var + GN_EPS) * gng_ref[...] + gnb_ref[...]
    out = (yn + bonus_ref[0]) * g_ref[0].astype(F32)
    o_ref[0] = x_ref[0] + gt_ref[0] * _dot(out.astype(BF16), wo_ref[...])


def _readout(x, gt, y_f, y_b, bonus, g, gn_g, gn_b, w_o):
    b, t, _ = x.shape
    tm = min(256, t)
    tok = pl.BlockSpec((1, tm, D_MODEL), lambda bi, i: (bi, i, 0))
    row = pl.BlockSpec((1, D_MODEL), lambda bi, i: (0, 0))
    return pl.pallas_call(
        _readout_kernel,
        out_shape=jax.ShapeDtypeStruct((b, t, D_MODEL), F32),
        grid=(b, t // tm),
        in_specs=[tok, pl.BlockSpec((1, 1, D_MODEL), lambda bi, i: (bi, 0, 0)), tok, tok, tok, tok, row, row,
                  pl.BlockSpec((LANES, LANES), lambda bi, i: (0, 0)),
                  pl.BlockSpec((D_MODEL, D_MODEL), lambda bi, i: (0, 0))],
        out_specs=tok,
        compiler_params=_cparams("parallel", "arbitrary"),
        name="l1_readout",
    )(x, gt, y_f, y_b, bonus, g, gn_g.reshape(1, D_MODEL), gn_b.reshape(1, D_MODEL), _seg_ones(LANES), w_o)


def _modulation(c, c_ctx, w_mod, b_mod):
    b = c.shape[0]
    cond = jnp.zeros((SUBLANES, D_MODEL), F32).at[:b].set(c).at[b].set(c_ctx)
    m = _ada(cond, w_mod, b_mod)
    lat = [m[:b, j * D_MODEL:(j + 1) * D_MODEL].reshape(b, 1, D_MODEL) for j in range(6)]
    ctx = [jnp.broadcast_to(m[b, j * D_MODEL:(j + 1) * D_MODEL].reshape(1, 1, D_MODEL), (b, 1, D_MODEL))
           for j in range(6)]
    return lat, ctx


def kernel(x, c, ctx, c_ctx, l0_w_mod, l0_b_mod, l0_norm_mix, l0_norm_ffn, l0_w_in, l0_conv_w, l0_conv_b, l0_lru_wi, l0_lru_bi, l0_lru_wr, l0_lru_br, l0_lru_lam, l0_q_gain, l0_k_gain, l0_w_out, l0_router, l0_we_gate, l0_we_up, l0_we_down, l1_w_mod, l1_b_mod, l1_norm_mix, l1_norm_ffn, l1_mu, l1_w_r, l1_w_k, l1_w_v, l1_w0, l1_w1, l1_w2, l1_a0, l1_a1, l1_a2, l1_g1, l1_g2, l1_k_k, l1_k_a, l1_r_k, l1_gn_g, l1_gn_b, l1_w_o, l1_router, l1_we_gate, l1_we_up, l1_we_down):
    t = x.shape[1]

    (sh1, sc1, gt1, sh2, sc2, gt2), (csh1, csc1, cgt1, csh2, csc2, cgt2) = _modulation(c, c_ctx, l0_w_mod, l0_b_mod)
    w_in = l0_w_in.astype(BF16)
    cos_t, sin_t = _rope_tables(t)
    xa_l, ga_l, q_l, k_l, v_l = _proj0(x, sc1, sh1, l0_norm_mix, w_in, l0_q_gain, l0_k_gain, cos_t, sin_t, True)
    lc = ctx.shape[1]
    xa_c, ga_c, q_c, k_c, v_c = _proj0(ctx, csc1, csh1, l0_norm_mix, w_in, l0_q_gain, l0_k_gain,
                                       cos_t[:lc], sin_t[:lc], False)
    ya_c, ya_l = _rglru(xa_c, ga_c, xa_l, ga_l, l0_conv_w, l0_conv_b, l0_lru_wi, l0_lru_bi, l0_lru_wr,
                        l0_lru_br, l0_lru_lam)
    seg_l = _kv_layout(k_l, v_l)
    seg_c = _kv_layout(k_c, v_c)
    yb_l = _attention(q_l, [seg_l, seg_c])
    yb_c = _attention(q_c, [seg_c])
    w_out = l0_w_out.astype(BF16)
    w_oa, w_ob = w_out[:LRU_WIDTH], w_out[LRU_WIDTH:]
    x = _residual_matmul(x, gt1, [ya_l, yb_l], [w_oa, w_ob])
    ctx = _residual_matmul(ctx, cgt1, [ya_c, yb_c], [w_oa, w_ob])
    wg, wu, wd = l0_we_gate.astype(BF16), l0_we_up.astype(BF16), l0_we_down.astype(BF16)
    x = _moe(x, sc2, sh2, gt2, l0_norm_ffn, l0_router, wg, wu, wd)
    ctx = _moe(ctx, csc2, csh2, cgt2, l0_norm_ffn, l0_router, wg, wu, wd)

    (sh1, sc1, gt1, sh2, sc2, gt2), (csh1, csc1, _, _, _, _) = _modulation(c, c_ctx, l1_w_mod, l1_b_mod)
    p = _rwkv_params(l1_norm_mix, l1_mu, l1_w_r, l1_w_k, l1_w_v, l1_w0, l1_w1, l1_w2, l1_a0, l1_a1, l1_a2,
                     l1_g1, l1_g2, l1_k_k, l1_k_a, l1_r_k)
    r_c, v_c, _, kk_c, _, lw0_c, lw1_c, kd0_c, kd1_c, ag0_c, ag1_c = _rwkv_prep(ctx, csc1, csh1, l1_norm_mix, p)
    r_l, v_l, g_l, kk_l, bonus_l, lw0, lw1, kd0, kd1, ag0, ag1 = _rwkv_prep(x, sc1, sh1, l1_norm_mix, p)
    (s_ctx,) = _wkv(r_c, v_c, kk_c, (lw0_c, lw1_c), (kd0_c, kd1_c), (ag0_c, ag1_c), None, False)
    y_f, y_b, _ = _wkv(r_l, v_l, kk_l, (lw0, lw1), (kd0, kd1), (ag0, ag1), s_ctx, True)
    x = _readout(x, gt1, y_f, y_b, bonus_l, g_l, l1_gn_g, l1_gn_b, l1_w_o.astype(BF16))
    wg, wu, wd = l1_we_gate.astype(BF16), l1_we_up.astype(BF16), l1_we_down.astype(BF16)
    x = _moe(x, sc2, sh2, gt2, l1_norm_ffn, l1_router, wg, wu, wd)
    return x
```

```python
import functools

import jax
import jax.numpy as jnp
import numpy as np
from jax import lax
from jax.experimental import pallas as pl
from jax.experimental.pallas import tpu as pltpu

F32 = jnp.float32
BF16 = jnp.bfloat16
I32 = jnp.int32

D_MODEL = 1024
LANES = 128
SUBLANES = 8
GRID_W = 64
NORM_EPS = 1e-6
LRU_WIDTH = 512
LRU_BLOCKS = 8
LRU_BLOCK = LRU_WIDTH // LRU_BLOCKS
LRU_C = 8.0
CONV_W = 4
N_Q_HEADS = 8
N_KV_HEADS = 2
HEAD_DIM = 64
Q_GROUP = N_Q_HEADS // N_KV_HEADS
ROPE_BASE = 10000.0
ATTN_SCALE = HEAD_DIM ** -0.5
Q_PRESCALE = ATTN_SCALE * float(np.log2(np.e))
ATTN_KV_CHUNK = 1024
ATTN_Q_TILE = 128
ATTN_LOOKAHEAD = 2
Q_WIDTH = N_Q_HEADS * HEAD_DIM
KV_WIDTH = N_KV_HEADS * HEAD_DIM
IN_WIDTH = 2 * LRU_WIDTH + Q_WIDTH + 2 * KV_WIDTH
RWKV_HEAD = 64
RWKV_HEADS = D_MODEL // RWKV_HEAD
HEAD_PAIRS = D_MODEL // LANES
DECAY_LORA = 64
AAA_LORA = 64
GATE_LORA = 160
GATE_LORA_PAD = 256
GN_EPS = 64e-5
N_EXPERTS = 16
EXPERT_FF = 2048
CAPACITY_FACTOR = 2
ROUTE_PAD = LANES
VMEM_LIMIT = 60 * 1024 * 1024
MOE_TILE = 256
MOE_ALIGN_LOG2 = 4
MOE_WIN = 64 + (1 << MOE_ALIGN_LOG2)
MOE_SLACK = 128
MOE_COL_SPLIT = 3
MOE_FFN_ROWS = 1024
MOE_FFN_SLICE = 512


def _cparams(*sem):
    return pltpu.CompilerParams(dimension_semantics=sem, vmem_limit_bytes=VMEM_LIMIT)


def _dot(a, b):
    return jnp.dot(a, b, preferred_element_type=F32)


def _dot_nt(a, b):
    return lax.dot_general(a, b, (((1,), (1,)), ((), ())), preferred_element_type=F32)


def _dot_tn(a, b):
    return lax.dot_general(a, b, (((0,), (0,)), ((), ())), preferred_element_type=F32)


def _split2(x):
    hi = x.astype(BF16)
    lo = (x - hi.astype(F32)).astype(BF16)
    return hi, lo


def _split3(x):
    hi = x.astype(BF16)
    r = x - hi.astype(F32)
    mid = r.astype(BF16)
    lo = (r - mid.astype(F32)).astype(BF16)
    return hi, mid, lo


def _dot_f32(a, b, pieces=3):
    split = _split3 if pieces == 3 else _split2
    ap = split(a)
    bp = split(b)
    out = None
    for i in range(pieces):
        for j in range(pieces - i):
            d = _dot(ap[i], bp[j])
            out = d if out is None else out + d
    return out


def _softplus(x):
    return jnp.maximum(x, 0.0) + jnp.log1p(jnp.exp(-jnp.abs(x)))


def _gelu_tanh(x):
    c = np.float32(np.sqrt(2.0 / np.pi))
    return 0.5 * x * (1.0 + jnp.tanh(c * (x + 0.044715 * (x * x * x))))


def _norm_mod(x, gain, sc, sh):
    ms = jnp.mean(x * x, axis=-1, keepdims=True)
    y = x * lax.rsqrt(ms + NORM_EPS)
    return (y * gain) * (1.0 + sc) + sh


def _seg_ones(width):
    i = np.arange(width)[:, None] // HEAD_DIM
    j = np.arange(width)[None, :] // HEAD_DIM
    return jnp.asarray(i == j, dtype=BF16)


def _head_sum(x, seg, pieces=2):
    outs = []
    for c in range(x.shape[1] // LANES):
        xc = x[:, c * LANES:(c + 1) * LANES]
        if pieces == 1:
            outs.append(_dot(xc.astype(BF16), seg))
        else:
            hi, lo = _split2(xc)
            outs.append(_dot(hi, seg) + _dot(lo, seg))
    return outs[0] if len(outs) == 1 else jnp.concatenate(outs, axis=1)


def _ada_kernel(c_ref, w_ref, b_ref, o_ref):
    s = c_ref[...]
    s = s * jax.nn.sigmoid(s)
    o_ref[...] = _dot_f32(s, w_ref[...]) + b_ref[...]


def _ada(cond8, w_mod, b_mod):
    n = w_mod.shape[1]
    tn = 1536
    return pl.pallas_call(
        _ada_kernel,
        out_shape=jax.ShapeDtypeStruct((SUBLANES, n), F32),
        grid=(n // tn,),
        in_specs=[pl.BlockSpec((SUBLANES, D_MODEL), lambda j: (0, 0)),
                  pl.BlockSpec((D_MODEL, tn), lambda j: (0, j)),
                  pl.BlockSpec((1, tn), lambda j: (0, j))],
        out_specs=pl.BlockSpec((SUBLANES, tn), lambda j: (0, j)),
        compiler_params=_cparams("arbitrary"),
        name="ada_params",
    )(cond8, w_mod, b_mod.reshape(1, n))


def _swap_pairs(x):
    lane = lax.broadcasted_iota(I32, x.shape, 1)
    nxt = pltpu.roll(x, LANES - 1, 1)
    prv = pltpu.roll(x, 1, 1)
    return jnp.where(lane % 2 == 0, nxt, prv)


def _proj0_kernel(x_ref, sc_ref, sh_ref, gain_ref, w_ref, qg_ref, kg_ref, seg_ref, cos_ref, sin_ref,
                  xa_ref, ga_ref, q_ref, k_ref, v_ref, *, rope):
    h = _norm_mod(x_ref[0], gain_ref[...], sc_ref[0], sh_ref[0])
    res = _dot(h.astype(BF16), w_ref[...])
    xa_ref[0] = res[:, :LRU_WIDTH]
    ga_ref[0] = res[:, LRU_WIDTH:2 * LRU_WIDTH]
    q0 = 2 * LRU_WIDTH
    seg = seg_ref[...]

    def head_norm_rope(z, gain):
        ms = _head_sum(z * z, seg) * (1.0 / HEAD_DIM)
        zn = z * lax.rsqrt(ms + NORM_EPS) * gain
        if not rope:
            return zn
        c = cos_ref[...]
        s = sin_ref[...]
        outs = []
        for t in range(zn.shape[1] // LANES):
            zt = zn[:, t * LANES:(t + 1) * LANES]
            outs.append(zt * c + _swap_pairs(zt) * s)
        return outs[0] if len(outs) == 1 else jnp.concatenate(outs, axis=1)

    q = head_norm_rope(res[:, q0:q0 + Q_WIDTH], qg_ref[...])
    q_ref[0] = (q * Q_PRESCALE).astype(BF16)
    k = head_norm_rope(res[:, q0 + Q_WIDTH:q0 + Q_WIDTH + KV_WIDTH], kg_ref[...])
    k_ref[0] = k.astype(BF16)
    v_ref[0] = res[:, q0 + Q_WIDTH + KV_WIDTH:].astype(BF16)


def _proj0(x, sc, sh, gain, w_in, q_gain, k_gain, cos_t, sin_t, rope):
    b, t, _ = x.shape
    tm = min(512, t)
    qg = jnp.tile(q_gain, N_Q_HEADS).reshape(1, Q_WIDTH)
    kg = jnp.tile(k_gain, N_KV_HEADS).reshape(1, KV_WIDTH)
    seg = _seg_ones(LANES)
    const = lambda shape: pl.BlockSpec(shape, lambda bi, i: (0,) * len(shape))
    tok = lambda w: pl.BlockSpec((1, tm, w), lambda bi, i: (bi, i, 0))
    per_b = pl.BlockSpec((1, 1, D_MODEL), lambda bi, i: (bi, 0, 0))
    return pl.pallas_call(
        functools.partial(_proj0_kernel, rope=rope),
        out_shape=(jax.ShapeDtypeStruct((b, t, LRU_WIDTH), F32),
                   jax.ShapeDtypeStruct((b, t, LRU_WIDTH), F32),
                   jax.ShapeDtypeStruct((b, t, Q_WIDTH), BF16),
                   jax.ShapeDtypeStruct((b, t, KV_WIDTH), BF16),
                   jax.ShapeDtypeStruct((b, t, KV_WIDTH), BF16)),
        grid=(b, t // tm),
        in_specs=[tok(D_MODEL), per_b, per_b, const((1, D_MODEL)), const((D_MODEL, IN_WIDTH)),
                  const((1, Q_WIDTH)), const((1, KV_WIDTH)), const((LANES, LANES)),
                  pl.BlockSpec((tm, LANES), lambda bi, i: (i, 0)),
                  pl.BlockSpec((tm, LANES), lambda bi, i: (i, 0))],
        out_specs=(tok(LRU_WIDTH), tok(LRU_WIDTH), tok(Q_WIDTH), tok(KV_WIDTH), tok(KV_WIDTH)),
        compiler_params=_cparams("parallel", "arbitrary"),
        name="l0_in_proj",
    )(x, sc, sh, gain.reshape(1, D_MODEL), w_in, qg, kg, seg, cos_t, sin_t)


def _rope_tables(t):
    n_rows = t // GRID_W
    row = jnp.repeat(jnp.arange(n_rows, dtype=F32), GRID_W)
    col = jnp.tile(jnp.arange(GRID_W, dtype=F32), n_rows)
    axis_dim = HEAD_DIM // 2
    inv_freq = ROPE_BASE ** (-jnp.arange(0, axis_dim, 2, dtype=F32) / axis_dim)
    ang = jnp.concatenate([row[:, None] * inv_freq, col[:, None] * inv_freq], axis=-1)
    cos = jnp.repeat(jnp.cos(ang), 2, axis=-1)
    sin = jnp.repeat(jnp.sin(ang), 2, axis=-1)
    sign = jnp.tile(jnp.asarray([-1.0, 1.0], F32), HEAD_DIM // 2)
    return jnp.tile(cos, (1, 2)), jnp.tile(sin * sign, (1, 2))


def _scan_rows(a, b, h0, rev):
    n = a.shape[0]
    row = lax.broadcasted_iota(I32, a.shape, 0) % SUBLANES
    for d in (1, 2, 4):
        if rev:
            a_s = pltpu.roll(a, n - d, 0)
            b_s = pltpu.roll(b, n - d, 0)
            m = row < SUBLANES - d
        else:
            a_s = pltpu.roll(a, d, 0)
            b_s = pltpu.roll(b, d, 0)
            m = row >= d
        b = jnp.where(m, a * b_s + b, b)
        a = jnp.where(m, a * a_s, a)
    groups = n // SUBLANES
    outs = [None] * groups
    h = h0
    for g in (range(groups - 1, -1, -1) if rev else range(groups)):
        hg = a[g * SUBLANES:(g + 1) * SUBLANES] * h + b[g * SUBLANES:(g + 1) * SUBLANES]
        outs[g] = hg
        h = hg[0:1] if rev else hg[SUBLANES - 1:SUBLANES]
    return jnp.concatenate(outs, axis=0), h


def _rglru_kernel(xc_ref, gc_ref, xl_ref, gl_ref, cw_ref, cb_ref, wg_ref, bg_ref, lam_ref,
                  yc_ref, yl_ref, rec_c, rec_l, *, tt_c, tt_l):
    cw = cw_ref[...]
    cb = cb_ref[...]

    def coeffs(x_ref, t0, tt, d):
        n = x_ref.shape[1]
        main = x_ref[0, pl.ds(t0, tt), :]
        prev = x_ref[0, pl.ds(pl.multiple_of(jnp.maximum(t0 - SUBLANES, 0), SUBLANES), SUBLANES), :]
        prev = jnp.where(t0 > 0, prev, 0.0)
        nxt = x_ref[0, pl.ds(pl.multiple_of(jnp.minimum(t0 + tt, n - SUBLANES), SUBLANES), SUBLANES), :]
        nxt = jnp.where(t0 + tt < n, nxt, 0.0)
        xe = jnp.concatenate([prev, main, nxt], axis=0)
        o = SUBLANES - CONV_W // 2
        xc = cb
        for j in range(CONV_W):
            xc = xc + cw[j:j + 1] * xe[o + j:o + j + tt]
        g = _dot(xc.astype(BF16), wg_ref[d, 0]) + bg_ref[d, 0]
        i_gate = jax.nn.sigmoid(g[:, :LANES])
        r_gate = jax.nn.sigmoid(g[:, LANES:])
        log_a = LRU_C * r_gate * (-_softplus(-lam_ref[d, 0]))
        a = jnp.exp(log_a)
        bco = jnp.sqrt(-jnp.tanh(log_a) * (a * a + 1.0)) * (i_gate * xc)
        return a, bco

    def sweep(x_ref, tt, d, rev, h, emit):
        nch = x_ref.shape[1] // tt

        def body(i, h):
            ci = nch - 1 - i if rev else i
            t0 = pl.multiple_of(ci * tt, tt)
            a, bco = coeffs(x_ref, t0, tt, d)
            hs, h = _scan_rows(a, bco, h, rev)
            emit(t0, tt, hs)
            return h

        return lax.fori_loop(0, nch, body, h)

    def store_rec(rec):
        def emit(t0, tt, hs):
            rec[pl.ds(t0, tt), :] = hs
        return emit

    def store_out(rec, g_ref, y_ref):
        def emit(t0, tt, hs):
            tot = rec[pl.ds(t0, tt), :] + hs
            y_ref[0, pl.ds(t0, tt), :] = (tot * _gelu_tanh(g_ref[0, pl.ds(t0, tt), :])).astype(y_ref.dtype)
        return emit

    zero = jnp.zeros((1, LANES), F32)
    h = sweep(xc_ref, tt_c, 0, False, zero, store_rec(rec_c))
    sweep(xl_ref, tt_l, 0, False, h, store_rec(rec_l))
    h = sweep(xc_ref, tt_c, 1, True, zero, store_out(rec_c, gc_ref, yc_ref))
    sweep(xl_ref, tt_l, 1, True, h, store_out(rec_l, gl_ref, yl_ref))


def _rglru(xa_c, ga_c, xa_l, ga_l, conv_w, conv_b, lru_wi, lru_bi, lru_wr, lru_br, lru_lam):
    b, lc, _ = xa_c.shape
    t = xa_l.shape[1]
    nt = LRU_WIDTH // LANES
    per_tile = LANES // LRU_BLOCK

    def block_diag(w):
        w = w.reshape(2, nt, per_tile, LRU_BLOCK, LRU_BLOCK)
        eye = jnp.eye(per_tile, dtype=w.dtype)
        return jnp.einsum('dtpij,pq->dtpiqj', w, eye).reshape(2, nt, LANES, LANES)

    wg = jnp.concatenate([block_diag(lru_wi), block_diag(lru_wr)], axis=-1).astype(BF16)
    bg = jnp.concatenate([lru_bi.reshape(2, nt, 1, LANES), lru_br.reshape(2, nt, 1, LANES)], axis=-1)
    lam = lru_lam.reshape(2, nt, 1, LANES)
    tt_c = min(256, lc)
    tt_l = min(256, t)
    seq = lambda n: pl.BlockSpec((1, n, LANES), lambda bi, j: (bi, 0, j))
    return pl.pallas_call(
        functools.partial(_rglru_kernel, tt_c=tt_c, tt_l=tt_l),
        out_shape=(jax.ShapeDtypeStruct((b, lc, LRU_WIDTH), BF16),
                   jax.ShapeDtypeStruct((b, t, LRU_WIDTH), BF16)),
        grid=(b, nt),
        in_specs=[seq(lc), seq(lc), seq(t), seq(t),
                  pl.BlockSpec((CONV_W, LANES), lambda bi, j: (0, j)),
                  pl.BlockSpec((1, LANES), lambda bi, j: (0, j)),
                  pl.BlockSpec((2, 1, LANES, 2 * LANES), lambda bi, j: (0, j, 0, 0)),
                  pl.BlockSpec((2, 1, 1, 2 * LANES), lambda bi, j: (0, j, 0, 0)),
                  pl.BlockSpec((2, 1, 1, LANES), lambda bi, j: (0, j, 0, 0))],
        out_specs=(seq(lc), seq(t)),
        scratch_shapes=[pltpu.VMEM((lc, LANES), F32), pltpu.VMEM((t, LANES), F32)],
        compiler_params=_cparams("parallel", "arbitrary"),
        name="l0_rglru",
    )(xa_c, ga_c, xa_l, ga_l, conv_w, conv_b.reshape(1, LRU_WIDTH), wg, bg, lam)


def _attn_kernel(q_ref, *refs, n_seg, kv_chunk):
    kv = refs[:2 * n_seg]
    o_ref = refs[2 * n_seg]
    g = pl.program_id(1)
    q = q_ref[0]
    tq = q.shape[0]
    lane = lax.broadcasted_iota(I32, (1, KV_WIDTH), 1)
    mine = (lane >= g * HEAD_DIM) & (lane < (g + 1) * HEAD_DIM)
    rows = []
    for h in range(Q_GROUP):
        qh = q[:, h * HEAD_DIM:(h + 1) * HEAD_DIM]
        both = jnp.concatenate([qh] * N_KV_HEADS, axis=1)
        rows.append(jnp.where(mine, both, jnp.zeros_like(both)))
    qpad = jnp.concatenate(rows, axis=0)
    nq = Q_GROUP * tq
    m = jnp.full((1, nq), -jnp.inf, F32)
    l = jnp.zeros((1, nq), F32)
    acc = jnp.zeros((HEAD_DIM, nq), F32)
    chunks = []
    for i in range(n_seg):
        tk = kv[2 * i].shape[1]
        ck = min(kv_chunk, tk)
        chunks += [(kv[2 * i], kv[2 * i + 1], c * ck, ck) for c in range(tk // ck)]
    scores = lambda ch: _dot_nt(ch[0][0, ch[2]:ch[2] + ch[3], :], qpad)
    ready = [scores(ch) for ch in chunks[:ATTN_LOOKAHEAD]]
    for j, (_, vt_ref, c0, ck) in enumerate(chunks):
        s = ready.pop(0)
        if j + ATTN_LOOKAHEAD < len(chunks):
            ready.append(scores(chunks[j + ATTN_LOOKAHEAD]))
        m_new = jnp.maximum(m, s.max(axis=0, keepdims=True))
        alpha = jnp.exp2(m - m_new)
        p = jnp.exp2(s - m_new)
        l = alpha * l + p.sum(axis=0, keepdims=True)
        acc = alpha * acc + _dot(vt_ref[0, 0, :, c0:c0 + ck], p.astype(BF16))
        m = m_new
    out_t = acc / l
    outs = [out_t[:, h * tq:(h + 1) * tq].T for h in range(Q_GROUP)]
    o_ref[0] = jnp.concatenate(outs, axis=1).astype(o_ref.dtype)


def _attention(q, segs):
    b, t, _ = q.shape
    tq = min(ATTN_Q_TILE, t)
    gw = Q_GROUP * HEAD_DIM
    in_specs = [pl.BlockSpec((1, tq, gw), lambda bi, g, i: (bi, i, g))]
    args = [q]
    for k, vt in segs:
        tk = k.shape[1]
        in_specs.append(pl.BlockSpec((1, tk, KV_WIDTH), lambda bi, g, i: (bi, 0, 0)))
        in_specs.append(pl.BlockSpec((1, 1, HEAD_DIM, tk), lambda bi, g, i: (bi, g, 0, 0)))
        args += [k, vt]
    return pl.pallas_call(
        functools.partial(_attn_kernel, n_seg=len(segs), kv_chunk=ATTN_KV_CHUNK),
        out_shape=jax.ShapeDtypeStruct((b, t, Q_WIDTH), BF16),
        grid=(b, N_KV_HEADS, t // tq),
        in_specs=in_specs,
        out_specs=pl.BlockSpec((1, tq, gw), lambda bi, g, i: (bi, i, g)),
        compiler_params=_cparams("parallel", "parallel", "arbitrary"),
        name="l0_attention",
    )(*args)


def _kv_layout(k, v):
    b, t, _ = k.shape
    return k, v.reshape(b, t, N_KV_HEADS, HEAD_DIM).transpose(0, 2, 3, 1)


def _resmm_kernel(x_ref, g_ref, *refs, n):
    acc = None
    for i in range(n):
        d = _dot(refs[i][0], refs[n + i][...])
        acc = d if acc is None else acc + d
    o_ref = refs[2 * n]
    o_ref[0] = x_ref[0] + g_ref[0] * acc


def _residual_matmul(x, gate, acts, weights):
    b, t, _ = x.shape
    tm = min(512, t)
    n = len(acts)
    tok = lambda w: pl.BlockSpec((1, tm, w), lambda bi, i: (bi, i, 0))
    in_specs = [tok(D_MODEL), pl.BlockSpec((1, 1, D_MODEL), lambda bi, i: (bi, 0, 0))]
    in_specs += [tok(a.shape[-1]) for a in acts]
    in_specs += [pl.BlockSpec(w.shape, lambda bi, i: (0, 0)) for w in weights]
    return pl.pallas_call(
        functools.partial(_resmm_kernel, n=n),
        out_shape=jax.ShapeDtypeStruct((b, t, D_MODEL), F32),
        grid=(b, t // tm),
        in_specs=in_specs,
        out_specs=tok(D_MODEL),
        compiler_params=_cparams("parallel", "arbitrary"),
        name="residual_proj",
    )(x, gate, *acts, *weights)


def _router_kernel(x_ref, sc_ref, sh_ref, gain_ref, wr_ref, h_ref, aff_ref):
    h = _norm_mod(x_ref[0], gain_ref[...], sc_ref[0], sh_ref[0])
    tm = h.shape[0]
    logits = _dot_f32(h, wr_ref[...], 2)
    lane = lax.broadcasted_iota(I32, (tm, ROUTE_PAD), 1)
    z = jnp.where(lane < N_EXPERTS, logits, -jnp.inf)
    e = jnp.exp(z - z.max(axis=1, keepdims=True))
    aff = e / e.sum(axis=1, keepdims=True)
    h_ref[0, :, :D_MODEL] = h.astype(BF16)
    hi = aff.astype(BF16).astype(F32)
    h_ref[0, :, D_MODEL:] = (hi + pltpu.roll(aff - hi, N_EXPERTS, 1)).astype(BF16)
    aff_ref[0] = aff.T[:N_EXPERTS]


def _router(x, sc, sh, gain, w_router):
    b, t, _ = x.shape
    tm = min(512, t)
    wr = jnp.pad(w_router, ((0, 0), (0, ROUTE_PAD - N_EXPERTS)))
    tok = lambda w: pl.BlockSpec((1, tm, w), lambda bi, i: (bi, i, 0))
    per_b = pl.BlockSpec((1, 1, D_MODEL), lambda bi, i: (bi, 0, 0))
    return pl.pallas_call(
        _router_kernel,
        out_shape=(jax.ShapeDtypeStruct((b, t, D_MODEL + ROUTE_PAD), BF16),
                   jax.ShapeDtypeStruct((b, N_EXPERTS, t), F32)),
        grid=(b, t // tm),
        in_specs=[tok(D_MODEL), per_b, per_b,
                  pl.BlockSpec((1, D_MODEL), lambda bi, i: (0, 0)),
                  pl.BlockSpec((D_MODEL, ROUTE_PAD), lambda bi, i: (0, 0))],
        out_specs=(tok(D_MODEL + ROUTE_PAD), pl.BlockSpec((1, N_EXPERTS, tm), lambda bi, i: (bi, 0, i))),
        compiler_params=_cparams("parallel", "arbitrary"),
        name="moe_router",
    )(x, sc, sh, gain.reshape(1, D_MODEL), wr)


def _cumsum_lanes(x01, tri):
    outs = []
    off = jnp.zeros((x01.shape[0], 1), F32)
    for j in range(x01.shape[1] // LANES):
        cj = _dot(x01[:, j * LANES:(j + 1) * LANES].astype(BF16), tri) + off
        outs.append(cj)
        off = cj[:, LANES - 1:LANES]
    return outs[0] if len(outs) == 1 else jnp.concatenate(outs, axis=1)


def _select_kernel(aff_ref, tri_ref, tile_ref, pos_ref, tab_ref, *, cap):
    aff = aff_ref[0]
    n = aff.shape[1]
    keys = pltpu.bitcast(aff, I32)

    def bit_step(i, tau):
        cand = tau | jnp.left_shift(jnp.int32(1), 30 - i)
        cnt = jnp.sum((keys >= cand).astype(I32), axis=1, keepdims=True)
        return jnp.where(cnt >= cap, cand, tau)

    tau = lax.fori_loop(0, 31, bit_step, jnp.zeros((N_EXPERTS, 1), I32))
    gt = keys > tau
    eq = keys == tau
    need = (cap - jnp.sum(gt.astype(I32), axis=1, keepdims=True)).astype(F32)
    tri = tri_ref[...]
    c_eq = _cumsum_lanes(eq.astype(F32), tri)
    sel = gt | (eq & (c_eq <= need))
    c_sel = _cumsum_lanes(sel.astype(F32), tri)
    pos_ref[0] = jnp.where(sel, c_sel - 1.0, -1.0)
    tab_ref[0] = _dot(sel.astype(BF16), tile_ref[...]).astype(I32)


def _select(aff_t, cap):
    b, _, n = aff_t.shape
    assert n % MOE_TILE == 0 and n // MOE_TILE <= LANES // 2
    tri = jnp.asarray(np.triu(np.ones((LANES, LANES), np.float32)), dtype=BF16)
    tok = np.arange(n)[:, None]
    j = np.arange(LANES // 2)[None, :]
    tile_tab = jnp.asarray(np.concatenate([tok < j * MOE_TILE, tok // MOE_TILE == j], axis=1), dtype=BF16)
    return pl.pallas_call(
        functools.partial(_select_kernel, cap=cap),
        out_shape=(jax.ShapeDtypeStruct((b, N_EXPERTS, n), F32),
                   jax.ShapeDtypeStruct((b, N_EXPERTS, LANES), I32)),
        grid=(b,),
        in_specs=[pl.BlockSpec((1, N_EXPERTS, n), lambda bi: (bi, 0, 0)),
                  pl.BlockSpec((LANES, LANES), lambda bi: (0, 0)),
                  pl.BlockSpec((n, LANES), lambda bi: (0, 0))],
        out_specs=(pl.BlockSpec((1, N_EXPERTS, n), lambda bi: (bi, 0, 0)),
                   pl.BlockSpec((1, N_EXPERTS, LANES), lambda bi: (bi, 0, 0))),
        compiler_params=_cparams("parallel"),
        name="moe_select",
    )(aff_t, tri, tile_tab)


def _tile_windows(tab_ref, pos_ref, j, win):
    slot = lax.broadcasted_iota(I32, (win, MOE_TILE), 0).astype(F32)
    out = []
    for e in range(N_EXPERTS):
        start = tab_ref[0, e, j]
        count = tab_ref[0, e, LANES // 2 + j]
        a0 = pl.multiple_of(lax.shift_left(lax.shift_right_logical(start, MOE_ALIGN_LOG2), MOE_ALIGN_LOG2),
                            1 << MOE_ALIGN_LOG2)
        p = pos_ref[0, e:e + 1, :]
        onehot = lambda first, p=p: (p == slot + jnp.asarray(first, F32)).astype(BF16)
        extra = jnp.maximum(start - a0 + count - 1, 0) // win
        out.append((a0, onehot, extra))
    return out


def _moe_gather_kernel(tab_ref, h_ref, pos_ref, o_ref, *, win):
    j = pl.program_id(2)

    @pl.when(j == 0)
    def _():
        o_ref[...] = jnp.zeros(o_ref.shape, o_ref.dtype)

    h = h_ref[0]
    wins = _tile_windows(tab_ref, pos_ref, j, win)
    x = _dot(jnp.concatenate([oh(a0) for a0, oh, _ in wins], axis=0), h).astype(BF16)
    for e, (a0, oh, extra) in enumerate(wins):
        o_ref[0, e, pl.ds(a0, win), :] = o_ref[0, e, pl.ds(a0, win), :] + x[e * win:(e + 1) * win]

        def more(k, c, e=e, a0=a0, oh=oh):
            ak = pl.multiple_of(a0 + (k + 1) * win, 1 << MOE_ALIGN_LOG2)
            o_ref[0, e, pl.ds(ak, win), :] = o_ref[0, e, pl.ds(ak, win), :] + _dot(oh(ak), h).astype(BF16)
            return c

        lax.fori_loop(0, extra, more, 0)


def _moe_gather(h_ext, pos, tab, cap):
    b, n, width = h_ext.shape
    cols = width // MOE_COL_SPLIT
    rows = cap + MOE_SLACK
    return pl.pallas_call(
        functools.partial(_moe_gather_kernel, win=MOE_WIN),
        out_shape=jax.ShapeDtypeStruct((b, N_EXPERTS, rows, width), BF16),
        grid=(b, MOE_COL_SPLIT, n // MOE_TILE),
        in_specs=[pl.BlockSpec((1, N_EXPERTS, LANES), lambda bi, c, j: (bi, 0, 0), memory_space=pltpu.SMEM),
                  pl.BlockSpec((1, MOE_TILE, cols), lambda bi, c, j: (bi, j, c)),
                  pl.BlockSpec((1, N_EXPERTS, MOE_TILE), lambda bi, c, j: (bi, 0, j))],
        out_specs=pl.BlockSpec((1, N_EXPERTS, rows, cols), lambda bi, c, j: (bi, 0, 0, c)),
        compiler_params=_cparams("parallel", "parallel", "arbitrary"),
        name="moe_gather",
    )(tab, h_ext, pos)


def _moe_ffn_kernel(x_ref, wg_ref, wu_ref, wd_ref, y_ref, acc_s, *, cap):
    e = pl.program_id(0)
    f = pl.program_id(2)
    nb = x_ref.shape[0]
    x = x_ref[:, 0].reshape(nb * cap, x_ref.shape[3])
    xb = x[:, :D_MODEL]

    @pl.when(f == 0)
    def _():
        acc_s[...] = jnp.zeros(acc_s.shape, F32)

    g = _dot(xb, wg_ref[0].astype(BF16))
    u = _dot(xb, wu_ref[0].astype(BF16))
    hid = (g * jax.nn.sigmoid(g)) * u
    acc_s[...] += _dot(hid.astype(BF16), wd_ref[0].astype(BF16))

    @pl.when(f == pl.num_programs(2) - 1)
    def _():
        lane = lax.broadcasted_iota(I32, (nb * cap, ROUTE_PAD), 1)
        mine = (lane == e) | (lane == e + N_EXPERTS)
        gate = jnp.sum(jnp.where(mine, x[:, D_MODEL:].astype(F32), 0.0), axis=1, keepdims=True)
        y_ref[:, 0, :cap] = (acc_s[...] * gate).astype(y_ref.dtype).reshape(nb, cap, D_MODEL)
        slack = y_ref.shape[2] - cap
        y_ref[:, 0, cap:] = jnp.zeros((nb, slack, D_MODEL), y_ref.dtype)


def _moe_ffn(xe, wg, wu, wd, cap):
    b, _, rows, width = xe.shape
    nb = b if b * cap <= MOE_FFN_ROWS else 1
    fc = MOE_FFN_SLICE
    return pl.pallas_call(
        functools.partial(_moe_ffn_kernel, cap=cap),
        out_shape=jax.ShapeDtypeStruct((b, N_EXPERTS, rows, D_MODEL), BF16),
        grid=(N_EXPERTS, b // nb, EXPERT_FF // fc),
        in_specs=[pl.BlockSpec((nb, 1, cap, width), lambda e, bi, f: (bi, e, 0, 0)),
                  pl.BlockSpec((1, D_MODEL, fc), lambda e, bi, f: (e, 0, f)),
                  pl.BlockSpec((1, D_MODEL, fc), lambda e, bi, f: (e, 0, f)),
                  pl.BlockSpec((1, fc, D_MODEL), lambda e, bi, f: (e, f, 0))],
        out_specs=pl.BlockSpec((nb, 1, rows, D_MODEL), lambda e, bi, f: (bi, e, 0, 0)),
        scratch_shapes=[pltpu.VMEM((nb * cap, D_MODEL), F32)],
        compiler_params=_cparams("arbitrary", "arbitrary", "arbitrary"),
        name="moe_experts",
    )(xe, wg, wu, wd)


def _moe_combine_kernel(tab_ref, x_ref, gt_ref, pos_ref, y_ref, o_ref, acc_s, *, win):
    j = pl.program_id(2)
    wins = _tile_windows(tab_ref, pos_ref, j, win)
    onehots = jnp.concatenate([oh(a0) for a0, oh, _ in wins], axis=0)
    rows = jnp.concatenate([y_ref[0, e, pl.ds(a0, win), :] for e, (a0, _, _) in enumerate(wins)], axis=0)
    acc_s[...] = _dot_tn(onehots, rows)
    for e, (a0, oh, extra) in enumerate(wins):
        def more(k, c, e=e, a0=a0, oh=oh):
            ak = pl.multiple_of(a0 + (k + 1) * win, 1 << MOE_ALIGN_LOG2)
            acc_s[...] += _dot_tn(oh(ak), y_ref[0, e, pl.ds(ak, win), :])
            return c

        lax.fori_loop(0, extra, more, 0)
    o_ref[0] = x_ref[0] + gt_ref[0] * acc_s[...]


def _moe_combine(x, gt, pos, tab, y):
    b, n, _ = x.shape
    rows = y.shape[2]
    cols = D_MODEL // 2
    tile = lambda: pl.BlockSpec((1, MOE_TILE, cols), lambda bi, c, j: (bi, j, c))
    return pl.pallas_call(
        functools.partial(_moe_combine_kernel, win=MOE_WIN),
        out_shape=jax.ShapeDtypeStruct((b, n, D_MODEL), F32),
        grid=(b, 2, n // MOE_TILE),
        in_specs=[pl.BlockSpec((1, N_EXPERTS, LANES), lambda bi, c, j: (bi, 0, 0), memory_space=pltpu.SMEM),
                  tile(),
                  pl.BlockSpec((1, 1, cols), lambda bi, c, j: (bi, 0, c)),
                  pl.BlockSpec((1, N_EXPERTS, MOE_TILE), lambda bi, c, j: (bi, 0, j)),
                  pl.BlockSpec((1, N_EXPERTS, rows, cols), lambda bi, c, j: (bi, 0, 0, c))],
        out_specs=tile(),
        scratch_shapes=[pltpu.VMEM((MOE_TILE, cols), F32)],
        compiler_params=_cparams("parallel", "parallel", "arbitrary"),
        name="moe_combine",
    )(tab, x, gt, pos, y)


def _moe(x, sc, sh, gt, gain, w_router, wg, wu, wd):
    n = x.shape[1]
    cap = max(1, CAPACITY_FACTOR * n // N_EXPERTS)
    h_ext, aff_t = _router(x, sc, sh, gain, w_router)
    pos, tab = _select(aff_t, cap)
    y = _moe_ffn(_moe_gather(h_ext, pos, tab, cap), wg, wu, wd, cap)
    return _moe_combine(x, gt, pos, tab, y)


def _rwkv_prep_kernel(xm_ref, xp_ref, xn_ref, sc_ref, sh_ref, gain_ref, mu_ref, wr_ref, wk_ref, wv_ref,
                      w1_ref, w2_ref, a1_ref, a2_ref, g1_ref, g2_ref, w0_ref, a0_ref, kks_ref, ka_ref, rk_ref,
                      seg_ref, r_ref, v_ref, g_ref, kk_ref, bonus_ref, lw0_ref, lw1_ref, kd0_ref, kd1_ref,
                      ag0_ref, ag1_ref):
    i = pl.program_id(1)
    last = pl.num_programs(1) - 1
    tm = xm_ref.shape[1]
    gain = gain_ref[...]
    sc = sc_ref[0]
    sh = sh_ref[0]
    xe = jnp.concatenate([xp_ref[0], xm_ref[0], xn_ref[0]], axis=0)
    he = _norm_mod(xe, gain, sc, sh)
    row = lax.broadcasted_iota(I32, (tm, 1), 0)
    h = he[SUBLANES:SUBLANES + tm]
    hm1 = jnp.where((row == 0) & (i == 0), 0.0, he[SUBLANES - 1:SUBLANES - 1 + tm])
    hp1 = jnp.where((row == tm - 1) & (i == last), 0.0, he[SUBLANES + 1:SUBLANES + 1 + tm])
    xx = 0.5 * (hm1 + hp1) - h
    mu = mu_ref[...]
    mix = lambda j: (h + xx * mu[j:j + 1]).astype(BF16)
    r = _dot(mix(0), wr_ref[...])
    k = _dot(mix(2), wk_ref[...])
    v = _dot(mix(3), wv_ref[...])
    tw = jnp.tanh(_dot(mix(1), w1_ref[...])).astype(BF16)
    la = _dot(mix(4), a1_ref[...]).astype(BF16)
    g = _dot(jax.nn.sigmoid(_dot(mix(5), g1_ref[...])).astype(BF16), g2_ref[...])
    seg = seg_ref[...]
    kkv = k * kks_ref[...]
    kk = kkv * lax.rsqrt(jnp.maximum(_head_sum(kkv * kkv, seg, 1), 1e-24))
    r_ref[0] = r.astype(r_ref.dtype)
    v_ref[0] = v.astype(v_ref.dtype)
    g_ref[0] = g.astype(g_ref.dtype)
    kk_ref[0] = kk.astype(kk_ref.dtype)
    bonus = None
    for d, (lw_ref, kd_ref, ag_ref) in enumerate(((lw0_ref, kd0_ref, ag0_ref), (lw1_ref, kd1_ref, ag1_ref))):
        w_pre = w0_ref[d] + _dot(tw, w2_ref[d])
        lw_ref[0] = -jnp.exp(-_softplus(-w_pre) - 0.5)
        a = jax.nn.sigmoid(a0_ref[d] + _dot(la, a2_ref[d]))
        kd = k * (1.0 + (a - 1.0) * ka_ref[...])
        kd_ref[0] = kd.astype(kd_ref.dtype)
        ag_ref[0] = a.astype(ag_ref.dtype)
        bd = _head_sum(r * kd * rk_ref[...], seg, 1) * v
        bonus = bd if bonus is None else bonus + bd
    bonus_ref[0] = bonus


def _rwkv_prep(x, sc, sh, gain, p):
    b, t, _ = x.shape
    tm = min(256, t)
    nb8 = tm // SUBLANES
    tok = lambda: pl.BlockSpec((1, tm, D_MODEL), lambda bi, i: (bi, i, 0))
    prev = pl.BlockSpec((1, SUBLANES, D_MODEL), lambda bi, i: (bi, jnp.maximum(i * nb8 - 1, 0), 0))
    nxt = pl.BlockSpec((1, SUBLANES, D_MODEL),
                       lambda bi, i: (bi, jnp.minimum((i + 1) * nb8, t // SUBLANES - 1), 0))
    per_b = pl.BlockSpec((1, 1, D_MODEL), lambda bi, i: (bi, 0, 0))
    const = lambda a: pl.BlockSpec(a.shape, lambda bi, i: (0,) * a.ndim)
    consts = [p['gain'], p['mu'], p['w_r'], p['w_k'], p['w_v'], p['w1'], p['w2'], p['a1'], p['a2'], p['g1'],
              p['g2'], p['w0'], p['a0'], p['kks'], p['ka'], p['rk'], p['seg']]
    bf = jax.ShapeDtypeStruct((b, t, D_MODEL), BF16)
    f32 = jax.ShapeDtypeStruct((b, t, D_MODEL), F32)
    return pl.pallas_call(
        _rwkv_prep_kernel,
        out_shape=(bf, bf, bf, bf, f32, f32, f32, bf, bf, bf, bf),
        grid=(b, t // tm),
        in_specs=[tok(), prev, nxt, per_b, per_b] + [const(a) for a in consts],
        out_specs=tuple(tok() for _ in range(11)),
        compiler_params=_cparams("parallel", "arbitrary"),
        name="l1_rwkv_prep",
    )(x, x, x, sc, sh, *consts)


def _rwkv_params(gain, mu, w_r, w_k, w_v, w0, w1, w2, a0, a1, a2, g1, g2, k_k, k_a, r_k):
    def pad_dir(w):
        z = jnp.zeros_like(w[0])
        return jnp.stack([jnp.concatenate([w[0], z], axis=0), jnp.concatenate([z, w[1]], axis=0)])

    row = lambda a: a.reshape(1, D_MODEL)
    return dict(
        gain=row(gain), mu=mu, w_r=w_r.astype(BF16), w_k=w_k.astype(BF16), w_v=w_v.astype(BF16),
        w1=jnp.concatenate([w1[0], w1[1]], axis=1).astype(BF16), w2=pad_dir(w2).astype(BF16),
        a1=jnp.concatenate([a1[0], a1[1]], axis=1).astype(BF16), a2=pad_dir(a2).astype(BF16),
        g1=jnp.pad(g1, ((0, 0), (0, GATE_LORA_PAD - GATE_LORA))).astype(BF16),
        g2=jnp.pad(g2, ((0, GATE_LORA_PAD - GATE_LORA), (0, 0))).astype(BF16),
        w0=w0.reshape(2, 1, D_MODEL), a0=a0.reshape(2, 1, D_MODEL),
        kks=row(k_k), ka=row(k_a), rk=row(r_k), seg=_seg_ones(LANES))


SOLVE_BASE = 8


def _solve_unit_tri(nmat, x, top):
    n = nmat.shape[0]
    rowi = lax.broadcasted_iota(I32, (n, n), 0)
    coli = lax.broadcasted_iota(I32, (n, n), 1)

    def same_block(shift):
        return (rowi >> shift) == (coli >> shift)

    k = int(np.log2(SOLVE_BASE))
    m = jnp.where(same_block(k), nmat, 0.0).astype(BF16)
    tinv = jnp.where(rowi == coli, 1.0, 0.0) + m.astype(F32)
    for _ in range(k - 1):
        m = _dot(m, m).astype(BF16)
        yield None
        tinv = tinv + _dot(m, tinv.astype(BF16))
        yield None
    while (1 << k) < top:
        off = jnp.where(same_block(k + 1) & jnp.logical_not(same_block(k)), nmat, 0.0).astype(BF16)
        tb = tinv.astype(BF16)
        half = _dot(tb, off).astype(BF16)
        yield None
        tinv = tinv + _dot(half, tb)
        yield None
        k += 1
    yield _dot(tinv.astype(BF16), x.astype(BF16))


def _wkv_stage(r_ref, v_ref, kk_ref, lw_ref, kd_ref, ag_ref, tri, ws, zs, vs, gts, d, rev, chunk):
    lw = lw_ref[0]
    p3 = _split3(lw)
    cum = _dot(tri, p3[0]) + _dot(tri, p3[1]) + _dot(tri, p3[2])
    g_in = jnp.exp(cum)
    g_ex = jnp.exp(cum - lw)
    g_inv = jnp.exp(-cum)
    kk = kk_ref[0].astype(F32)
    a_t = -(kk * g_ex)
    r_t = r_ref[0].astype(F32) * g_in
    b_t = kk * ag_ref[0].astype(F32) * g_inv
    k_t = kd_ref[0].astype(F32) * g_inv
    g_tot = jnp.exp(cum[0:1] if rev else cum[chunk - 1:chunk])
    first = (lax.broadcasted_iota(I32, (1, D_MODEL), 1) % LANES) < RWKV_HEAD
    stacked = lambda x: (jnp.where(first, x, jnp.zeros_like(x)), jnp.where(first, jnp.zeros_like(x), x))
    a2, r2, b2, k2 = (stacked(x.astype(BF16)) for x in (a_t, r_t, b_t, k_t))
    v2 = stacked(v_ref[0])
    c2 = 2 * chunk
    for p in range(HEAD_PAIRS):
        sl = slice(p * LANES, (p + 1) * LANES)
        for h in range(2):
            ws[d, p, h * chunk:(h + 1) * chunk] = a2[h][:, sl]
            ws[d, p, c2 + h * chunk:c2 + (h + 1) * chunk] = r2[h][:, sl]
            zs[d, p, h * chunk:(h + 1) * chunk] = b2[h][:, sl]
            zs[d, p, c2 + h * chunk:c2 + (h + 1) * chunk] = k2[h][:, sl]
            vs[d, p, h * chunk:(h + 1) * chunk] = v2[h][:, sl]
        gts[d, p] = g_tot[:, sl]


def _wkv_chain(s_ref, y_ref, ws, zs, vs, gts, d, p, rev, chunk):
    n = 2 * chunk
    w = ws[d, p]
    z = zs[d, p]
    v = vs[d, p]
    s_old = s_ref[d, p]
    pm = _dot_nt(w, z)
    yield
    wst = _dot_nt(w, s_old.astype(BF16))
    yield
    ti = lax.broadcasted_iota(I32, (n, n), 0) & (chunk - 1)
    tj = lax.broadcasted_iota(I32, (n, n), 1) & (chunk - 1)
    strict = ti < tj if rev else ti > tj
    incl = ti <= tj if rev else ti >= tj
    nmat = jnp.where(strict, pm[:n, :n], 0.0)
    ak = jnp.where(strict, pm[:n, n:], 0.0).astype(BF16)
    rbk = jnp.concatenate([jnp.where(incl, pm[n:, :n], 0.0), jnp.where(incl, pm[n:, n:], 0.0)], axis=1).astype(BF16)
    x = wst[:n] + _dot(ak, v)
    yield
    u = None
    for u in _solve_unit_tri(nmat, x, chunk):
        yield
    uv = jnp.concatenate([u.astype(BF16), v], axis=0)
    y = wst[n:] + _dot(rbk, uv)
    yield
    if y_ref is not None:
        y_ref[0, :, p * LANES:(p + 1) * LANES] = y[:chunk] + y[chunk:]
    ds = _dot_tn(uv, z)
    yield
    s_ref[d, p] = (s_old + ds) * gts[d, p]


def _wkv_kernel(*refs, chunk, emit_y, has_init):
    ins = refs[:12]
    tri_ref = refs[12]
    pos = 13
    if has_init:
        s0_ref = refs[pos]
        pos += 1
    yf_ref = yb_ref = None
    if emit_y:
        yf_ref, yb_ref = refs[pos:pos + 2]
        pos += 2
    sfin_ref = refs[pos]
    s_ref, ws, zs, vs, gts = refs[pos + 1:]
    i = pl.program_id(1)

    @pl.when(i == 0)
    def _():
        if has_init:
            s_ref[...] = s0_ref[0]
        else:
            s_ref[...] = jnp.zeros(s_ref.shape, F32)

    for d, rev in ((0, False), (1, True)):
        _wkv_stage(*ins[6 * d:6 * d + 6], tri_ref[d], ws, zs, vs, gts, d, rev, chunk)

    chains = [_wkv_chain(s_ref, (yf_ref, yb_ref)[d], ws, zs, vs, gts, d, p, bool(d), chunk)
              for p in range(HEAD_PAIRS) for d in range(2)]
    while chains:
        alive = []
        for ch in chains:
            try:
                next(ch)
                alive.append(ch)
            except StopIteration:
                pass
        chains = alive

    @pl.when(i == pl.num_programs(1) - 1)
    def _():
        sfin_ref[0] = s_ref[...]


def _wkv(r, v, kk, lw, kd, ag, s0, emit_y, chunk=64):
    b, t, _ = r.shape
    n = t // chunk
    fwd = pl.BlockSpec((1, chunk, D_MODEL), lambda bi, i: (bi, i, 0))
    bwd = pl.BlockSpec((1, chunk, D_MODEL), lambda bi, i: (bi, n - 1 - i, 0))
    tri = jnp.asarray(np.stack([np.tril(np.ones((chunk, chunk), np.float32)),
                                np.triu(np.ones((chunk, chunk), np.float32))]), dtype=BF16)
    state_shape = (2, HEAD_PAIRS, 2 * RWKV_HEAD, LANES)
    state_spec = pl.BlockSpec((1,) + state_shape, lambda bi, i: (bi, 0, 0, 0, 0))
    args = [r, v, kk, lw[0], kd[0], ag[0], r, v, kk, lw[1], kd[1], ag[1], tri]
    in_specs = [fwd] * 6 + [bwd] * 6 + [pl.BlockSpec((2, chunk, chunk), lambda bi, i: (0, 0, 0))]
    if s0 is not None:
        args.append(s0)
        in_specs.append(state_spec)
    out_shape = [jax.ShapeDtypeStruct((b,) + state_shape, F32)]
    out_specs = [state_spec]
    if emit_y:
        out_shape = [jax.ShapeDtypeStruct((b, t, D_MODEL), F32)] * 2 + out_shape
        out_specs = [fwd, bwd] + out_specs
    res = pl.pallas_call(
        functools.partial(_wkv_kernel, chunk=chunk, emit_y=emit_y, has_init=s0 is not None),
        out_shape=tuple(out_shape),
        grid=(b, n),
        in_specs=in_specs,
        out_specs=tuple(out_specs),
        scratch_shapes=[pltpu.VMEM(state_shape, F32),
                        pltpu.VMEM((2, HEAD_PAIRS, 4 * chunk, LANES), BF16),
                        pltpu.VMEM((2, HEAD_PAIRS, 4 * chunk, LANES), BF16),
                        pltpu.VMEM((2, HEAD_PAIRS, 2 * chunk, LANES), BF16),
                        pltpu.VMEM((2, HEAD_PAIRS, 1, LANES), F32)],
        compiler_params=_cparams("parallel", "arbitrary"),
        name="l1_wkv_scan",
    )(*args)
    return res


def _readout_kernel(x_ref, gt_ref, yf_ref, yb_ref, bonus_ref, g_ref, gng_ref, gnb_ref, seg_ref, wo_ref, o_ref):
    seg = seg_ref[...]
    y = yf_ref[0] + yb_ref[0]
    mean = _head_sum(y, seg) * (1.0 / RWKV_HEAD)
    c = y - mean
    var = _head_sum(c * c, seg) * (1.0 / RWKV_HEAD)
    yn = c * lax.rsqrt(var + GN_EPS) * gng_ref[...] + gnb_ref[...]
    out = (yn + bonus_ref[0]) * g_ref[0].astype(F32)
    o_ref[0] = x_ref[0] + gt_ref[0] * _dot(out.astype(BF16), wo_ref[...])


def _readout(x, gt, y_f, y_b, bonus, g, gn_g, gn_b, w_o):
    b, t, _ = x.shape
    tm = min(256, t)
    tok = pl.BlockSpec((1, tm, D_MODEL), lambda bi, i: (bi, i, 0))
    row = pl.BlockSpec((1, D_MODEL), lambda bi, i: (0, 0))
    return pl.pallas_call(
        _readout_kernel,
        out_shape=jax.ShapeDtypeStruct((b, t, D_MODEL), F32),
        grid=(b, t // tm),
        in_specs=[tok, pl.BlockSpec((1, 1, D_MODEL), lambda bi, i: (bi, 0, 0)), tok, tok, tok, tok, row, row,
                  pl.BlockSpec((LANES, LANES), lambda bi, i: (0, 0)),
                  pl.BlockSpec((D_MODEL, D_MODEL), lambda bi, i: (0, 0))],
        out_specs=tok,
        compiler_params=_cparams("parallel", "arbitrary"),
        name="l1_readout",
    )(x, gt, y_f, y_b, bonus, g, gn_g.reshape(1, D_MODEL), gn_b.reshape(1, D_MODEL), _seg_ones(LANES), w_o)


def _modulation(c, c_ctx, w_mod, b_mod):
    b = c.shape[0]
    cond = jnp.zeros((SUBLANES, D_MODEL), F32).at[:b].set(c).at[b].set(c_ctx)
    m = _ada(cond, w_mod, b_mod)
    lat = [m[:b, j * D_MODEL:(j + 1) * D_MODEL].reshape(b, 1, D_MODEL) for j in range(6)]
    ctx = [jnp.broadcast_to(m[b, j * D_MODEL:(j + 1) * D_MODEL].reshape(1, 1, D_MODEL), (b, 1, D_MODEL))
           for j in range(6)]
    return lat, ctx


def kernel(x, c, ctx, c_ctx, l0_w_mod, l0_b_mod, l0_norm_mix, l0_norm_ffn, l0_w_in, l0_conv_w, l0_conv_b, l0_lru_wi, l0_lru_bi, l0_lru_wr, l0_lru_br, l0_lru_lam, l0_q_gain, l0_k_gain, l0_w_out, l0_router, l0_we_gate, l0_we_up, l0_we_down, l1_w_mod, l1_b_mod, l1_norm_mix, l1_norm_ffn, l1_mu, l1_w_r, l1_w_k, l1_w_v, l1_w0, l1_w1, l1_w2, l1_a0, l1_a1, l1_a2, l1_g1, l1_g2, l1_k_k, l1_k_a, l1_r_k, l1_gn_g, l1_gn_b, l1_w_o, l1_router, l1_we_gate, l1_we_up, l1_we_down):
    t = x.shape[1]

    (sh1, sc1, gt1, sh2, sc2, gt2), (csh1, csc1, cgt1, csh2, csc2, cgt2) = _modulation(c, c_ctx, l0_w_mod, l0_b_mod)
    w_in = l0_w_in.astype(BF16)
    cos_t, sin_t = _rope_tables(t)
    xa_l, ga_l, q_l, k_l, v_l = _proj0(x, sc1, sh1, l0_norm_mix, w_in, l0_q_gain, l0_k_gain, cos_t, sin_t, True)
    lc = ctx.shape[1]
    xa_c, ga_c, q_c, k_c, v_c = _proj0(ctx, csc1, csh1, l0_norm_mix, w_in, l0_q_gain, l0_k_gain,
                                       cos_t[:lc], sin_t[:lc], False)
    ya_c, ya_l = _rglru(xa_c, ga_c, xa_l, ga_l, l0_conv_w, l0_conv_b, l0_lru_wi, l0_lru_bi, l0_lru_wr,
                        l0_lru_br, l0_lru_lam)
    seg_l = _kv_layout(k_l, v_l)
    seg_c = _kv_layout(k_c, v_c)
    yb_l = _attention(q_l, [seg_l, seg_c])
    yb_c = _attention(q_c, [seg_c])
    w_out = l0_w_out.astype(BF16)
    w_oa, w_ob = w_out[:LRU_WIDTH], w_out[LRU_WIDTH:]
    x = _residual_matmul(x, gt1, [ya_l, yb_l], [w_oa, w_ob])
    ctx = _residual_matmul(ctx, cgt1, [ya_c, yb_c], [w_oa, w_ob])
    x = _moe(x, sc2, sh2, gt2, l0_norm_ffn, l0_router, l0_we_gate, l0_we_up, l0_we_down)
    ctx = _moe(ctx, csc2, csh2, cgt2, l0_norm_ffn, l0_router, l0_we_gate, l0_we_up, l0_we_down)

    (sh1, sc1, gt1, sh2, sc2, gt2), (csh1, csc1, _, _, _, _) = _modulation(c, c_ctx, l1_w_mod, l1_b_mod)
    p = _rwkv_params(l1_norm_mix, l1_mu, l1_w_r, l1_w_k, l1_w_v, l1_w0, l1_w1, l1_w2, l1_a0, l1_a1, l1_a2,
                     l1_g1, l1_g2, l1_k_k, l1_k_a, l1_r_k)
    r_c, v_c, _, kk_c, _, lw0_c, lw1_c, kd0_c, kd1_c, ag0_c, ag1_c = _rwkv_prep(ctx, csc1, csh1, l1_norm_mix, p)
    r_l, v_l, g_l, kk_l, bonus_l, lw0, lw1, kd0, kd1, ag0, ag1 = _rwkv_prep(x, sc1, sh1, l1_norm_mix, p)
    (s_ctx,) = _wkv(r_c, v_c, kk_c, (lw0_c, lw1_c), (kd0_c, kd1_c), (ag0_c, ag1_c), None, False)
    y_f, y_b, _ = _wkv(r_l, v_l, kk_l, (lw0, lw1), (kd0, kd1), (ag0, ag1), s_ctx, True)
    x = _readout(x, gt1, y_f, y_b, bonus_l, g_l, l1_gn_g, l1_gn_b, l1_w_o.astype(BF16))
    x = _moe(x, sc2, sh2, gt2, l1_norm_ffn, l1_router, l1_we_gate, l1_we_up, l1_we_down)
    return x
```

```python
import functools

import jax
import jax.numpy as jnp
import numpy as np
from jax import lax
from jax.experimental import pallas as pl
from jax.experimental.pallas import tpu as pltpu

F32 = jnp.float32
BF16 = jnp.bfloat16
I32 = jnp.int32

D_MODEL = 1024
LANES = 128
SUBLANES = 8
GRID_W = 64
NORM_EPS = 1e-6
LRU_WIDTH = 512
LRU_BLOCKS = 8
LRU_BLOCK = LRU_WIDTH // LRU_BLOCKS
LRU_C = 8.0
CONV_W = 4
N_Q_HEADS = 8
N_KV_HEADS = 2
HEAD_DIM = 64
Q_GROUP = N_Q_HEADS // N_KV_HEADS
ROPE_BASE = 10000.0
ATTN_SCALE = HEAD_DIM ** -0.5
Q_PRESCALE = ATTN_SCALE * float(np.log2(np.e))
ATTN_KV_CHUNK = 1024
ATTN_Q_TILE = 128
ATTN_LOOKAHEAD = 2
ATTN_VT_ROWS = HEAD_DIM + 16
Q_WIDTH = N_Q_HEADS * HEAD_DIM
KV_WIDTH = N_KV_HEADS * HEAD_DIM
IN_WIDTH = 2 * LRU_WIDTH + Q_WIDTH + 2 * KV_WIDTH
RWKV_HEAD = 64
RWKV_HEADS = D_MODEL // RWKV_HEAD
HEAD_PAIRS = D_MODEL // LANES
DECAY_LORA = 64
AAA_LORA = 64
GATE_LORA = 160
GATE_LORA_PAD = 256
GN_EPS = 64e-5
N_EXPERTS = 16
EXPERT_FF = 2048
CAPACITY_FACTOR = 2
ROUTE_PAD = LANES
VMEM_LIMIT = 60 * 1024 * 1024
MOE_TILE = 256
MOE_ALIGN_LOG2 = 4
MOE_WIN = 64 + (1 << MOE_ALIGN_LOG2)
MOE_SLACK = 128
MOE_COL_SPLIT = 3
MOE_FFN_ROWS = 1024
MOE_FFN_SLICE = 512


def _cparams(*sem):
    return pltpu.CompilerParams(dimension_semantics=sem, vmem_limit_bytes=VMEM_LIMIT)


def _dot(a, b):
    return jnp.dot(a, b, preferred_element_type=F32)


def _dot_nt(a, b):
    return lax.dot_general(a, b, (((1,), (1,)), ((), ())), preferred_element_type=F32)


def _dot_tn(a, b):
    return lax.dot_general(a, b, (((0,), (0,)), ((), ())), preferred_element_type=F32)


def _split2(x):
    hi = x.astype(BF16)
    lo = (x - hi.astype(F32)).astype(BF16)
    return hi, lo


def _split3(x):
    hi = x.astype(BF16)
    r = x - hi.astype(F32)
    mid = r.astype(BF16)
    lo = (r - mid.astype(F32)).astype(BF16)
    return hi, mid, lo


def _dot_f32(a, b, pieces=3):
    split = _split3 if pieces == 3 else _split2
    ap = split(a)
    bp = split(b)
    out = None
    for i in range(pieces):
        for j in range(pieces - i):
            d = _dot(ap[i], bp[j])
            out = d if out is None else out + d
    return out


def _softplus(x):
    return jnp.maximum(x, 0.0) + jnp.log1p(jnp.exp(-jnp.abs(x)))


def _gelu_tanh(x):
    c = np.float32(np.sqrt(2.0 / np.pi))
    return 0.5 * x * (1.0 + jnp.tanh(c * (x + 0.044715 * (x * x * x))))


def _norm_mod(x, gain, sc, sh):
    ms = jnp.mean(x * x, axis=-1, keepdims=True)
    y = x * lax.rsqrt(ms + NORM_EPS)
    return (y * gain) * (1.0 + sc) + sh


def _seg_ones(width):
    i = np.arange(width)[:, None] // HEAD_DIM
    j = np.arange(width)[None, :] // HEAD_DIM
    return jnp.asarray(i == j, dtype=BF16)


def _head_sum(x, seg, pieces=2):
    outs = []
    for c in range(x.shape[1] // LANES):
        xc = x[:, c * LANES:(c + 1) * LANES]
        if pieces == 1:
            outs.append(_dot(xc.astype(BF16), seg))
        else:
            hi, lo = _split2(xc)
            outs.append(_dot(hi, seg) + _dot(lo, seg))
    return outs[0] if len(outs) == 1 else jnp.concatenate(outs, axis=1)


def _ada_kernel(c_ref, w_ref, b_ref, o_ref):
    s = c_ref[...]
    s = s * jax.nn.sigmoid(s)
    o_ref[...] = _dot_f32(s, w_ref[...]) + b_ref[...]


def _ada(cond8, w_mod, b_mod):
    n = w_mod.shape[1]
    tn = 1536
    return pl.pallas_call(
        _ada_kernel,
        out_shape=jax.ShapeDtypeStruct((SUBLANES, n), F32),
        grid=(n // tn,),
        in_specs=[pl.BlockSpec((SUBLANES, D_MODEL), lambda j: (0, 0)),
                  pl.BlockSpec((D_MODEL, tn), lambda j: (0, j)),
                  pl.BlockSpec((1, tn), lambda j: (0, j))],
        out_specs=pl.BlockSpec((SUBLANES, tn), lambda j: (0, j)),
        compiler_params=_cparams("arbitrary"),
        name="ada_params",
    )(cond8, w_mod, b_mod.reshape(1, n))


def _swap_pairs(x):
    lane = lax.broadcasted_iota(I32, x.shape, 1)
    nxt = pltpu.roll(x, LANES - 1, 1)
    prv = pltpu.roll(x, 1, 1)
    return jnp.where(lane % 2 == 0, nxt, prv)


def _proj0_kernel(x_ref, sc_ref, sh_ref, gain_ref, w_ref, qg_ref, kg_ref, seg_ref, cos_ref, sin_ref,
                  xa_ref, ga_ref, q_ref, k_ref, v_ref, *, rope):
    h = _norm_mod(x_ref[0], gain_ref[...], sc_ref[0], sh_ref[0])
    res = _dot(h.astype(BF16), w_ref[...])
    xa_ref[0] = res[:, :LRU_WIDTH]
    ga_ref[0] = res[:, LRU_WIDTH:2 * LRU_WIDTH]
    q0 = 2 * LRU_WIDTH
    seg = seg_ref[...]

    def head_norm_rope(z, gain):
        ms = _head_sum(z * z, seg) * (1.0 / HEAD_DIM)
        zn = z * lax.rsqrt(ms + NORM_EPS) * gain
        if not rope:
            return zn
        c = cos_ref[...]
        s = sin_ref[...]
        outs = []
        for t in range(zn.shape[1] // LANES):
            zt = zn[:, t * LANES:(t + 1) * LANES]
            outs.append(zt * c + _swap_pairs(zt) * s)
        return outs[0] if len(outs) == 1 else jnp.concatenate(outs, axis=1)

    q = head_norm_rope(res[:, q0:q0 + Q_WIDTH], qg_ref[...])
    q_ref[0] = (q * Q_PRESCALE).astype(BF16)
    k = head_norm_rope(res[:, q0 + Q_WIDTH:q0 + Q_WIDTH + KV_WIDTH], kg_ref[...])
    k_ref[0] = k.astype(BF16)
    v_ref[0] = res[:, q0 + Q_WIDTH + KV_WIDTH:].astype(BF16)


def _proj0(x, sc, sh, gain, w_in, q_gain, k_gain, cos_t, sin_t, rope):
    b, t, _ = x.shape
    tm = min(512, t)
    qg = jnp.tile(q_gain, N_Q_HEADS).reshape(1, Q_WIDTH)
    kg = jnp.tile(k_gain, N_KV_HEADS).reshape(1, KV_WIDTH)
    seg = _seg_ones(LANES)
    const = lambda shape: pl.BlockSpec(shape, lambda bi, i: (0,) * len(shape))
    tok = lambda w: pl.BlockSpec((1, tm, w), lambda bi, i: (bi, i, 0))
    per_b = pl.BlockSpec((1, 1, D_MODEL), lambda bi, i: (bi, 0, 0))
    return pl.pallas_call(
        functools.partial(_proj0_kernel, rope=rope),
        out_shape=(jax.ShapeDtypeStruct((b, t, LRU_WIDTH), F32),
                   jax.ShapeDtypeStruct((b, t, LRU_WIDTH), F32),
                   jax.ShapeDtypeStruct((b, t, Q_WIDTH), BF16),
                   jax.ShapeDtypeStruct((b, t, KV_WIDTH), BF16),
                   jax.ShapeDtypeStruct((b, t, KV_WIDTH), BF16)),
        grid=(b, t // tm),
        in_specs=[tok(D_MODEL), per_b, per_b, const((1, D_MODEL)), const((D_MODEL, IN_WIDTH)),
                  const((1, Q_WIDTH)), const((1, KV_WIDTH)), const((LANES, LANES)),
                  pl.BlockSpec((tm, LANES), lambda bi, i: (i, 0)),
                  pl.BlockSpec((tm, LANES), lambda bi, i: (i, 0))],
        out_specs=(tok(LRU_WIDTH), tok(LRU_WIDTH), tok(Q_WIDTH), tok(KV_WIDTH), tok(KV_WIDTH)),
        compiler_params=_cparams("parallel", "arbitrary"),
        name="l0_in_proj",
    )(x, sc, sh, gain.reshape(1, D_MODEL), w_in, qg, kg, seg, cos_t, sin_t)


def _rope_tables(t):
    n_rows = t // GRID_W
    row = jnp.repeat(jnp.arange(n_rows, dtype=F32), GRID_W)
    col = jnp.tile(jnp.arange(GRID_W, dtype=F32), n_rows)
    axis_dim = HEAD_DIM // 2
    inv_freq = ROPE_BASE ** (-jnp.arange(0, axis_dim, 2, dtype=F32) / axis_dim)
    ang = jnp.concatenate([row[:, None] * inv_freq, col[:, None] * inv_freq], axis=-1)
    cos = jnp.repeat(jnp.cos(ang), 2, axis=-1)
    sin = jnp.repeat(jnp.sin(ang), 2, axis=-1)
    sign = jnp.tile(jnp.asarray([-1.0, 1.0], F32), HEAD_DIM // 2)
    return jnp.tile(cos, (1, 2)), jnp.tile(sin * sign, (1, 2))


def _scan_rows(a, b, h0, rev):
    n = a.shape[0]
    groups = n // SUBLANES
    a = a.reshape(groups, SUBLANES, LANES)
    b = b.reshape(groups, SUBLANES, LANES)
    row = lax.broadcasted_iota(I32, a.shape, 1)
    for d in (1, 2, 4):
        shift = SUBLANES - d if rev else d
        m = row < SUBLANES - d if rev else row >= d
        a_s = pltpu.roll(a, shift, 1)
        b_s = pltpu.roll(b, shift, 1)
        b = jnp.where(m, a * b_s + b, b)
        a = jnp.where(m, a * a_s, a)
    outs = [None] * groups
    h = h0
    for g in (range(groups - 1, -1, -1) if rev else range(groups)):
        hg = a[g] * h + b[g]
        outs[g] = hg
        h = hg[0:1] if rev else hg[SUBLANES - 1:SUBLANES]
    return jnp.concatenate(outs, axis=0), h


def _rglru_kernel(xc_ref, gc_ref, xl_ref, gl_ref, cw_ref, cb_ref, wg_ref, bg_ref, lam_ref,
                  yc_ref, yl_ref, rec_c, rec_l, *, tt_c, tt_l):
    cw = cw_ref[...]
    cb = cb_ref[...]

    def coeffs(x_ref, t0, tt, d):
        n = x_ref.shape[1]
        main = x_ref[0, pl.ds(t0, tt), :]
        prev = x_ref[0, pl.ds(pl.multiple_of(jnp.maximum(t0 - SUBLANES, 0), SUBLANES), SUBLANES), :]
        prev = jnp.where(t0 > 0, prev, 0.0)
        nxt = x_ref[0, pl.ds(pl.multiple_of(jnp.minimum(t0 + tt, n - SUBLANES), SUBLANES), SUBLANES), :]
        nxt = jnp.where(t0 + tt < n, nxt, 0.0)
        xe = jnp.concatenate([prev, main, nxt], axis=0)
        o = SUBLANES - CONV_W // 2
        xc = cb
        for j in range(CONV_W):
            xc = xc + cw[j:j + 1] * xe[o + j:o + j + tt]
        g = _dot(xc.astype(BF16), wg_ref[d, 0]) + bg_ref[d, 0]
        i_gate = jax.nn.sigmoid(g[:, :LANES])
        r_gate = jax.nn.sigmoid(g[:, LANES:])
        log_a = LRU_C * r_gate * (-_softplus(-lam_ref[d, 0]))
        a = jnp.exp(log_a)
        bco = jnp.sqrt(-jnp.tanh(log_a) * (a * a + 1.0)) * (i_gate * xc)
        return a, bco

    def sweep(x_ref, tt, d, rev, h, emit):
        nch = x_ref.shape[1] // tt

        def body(i, h):
            ci = nch - 1 - i if rev else i
            t0 = pl.multiple_of(ci * tt, tt)
            a, bco = coeffs(x_ref, t0, tt, d)
            hs, h = _scan_rows(a, bco, h, rev)
            emit(t0, tt, hs)
            return h

        return lax.fori_loop(0, nch, body, h)

    def store_rec(rec):
        def emit(t0, tt, hs):
            rec[pl.ds(t0, tt), :] = hs
        return emit

    def store_out(rec, g_ref, y_ref):
        def emit(t0, tt, hs):
            tot = rec[pl.ds(t0, tt), :] + hs
            y_ref[0, pl.ds(t0, tt), :] = (tot * _gelu_tanh(g_ref[0, pl.ds(t0, tt), :])).astype(y_ref.dtype)
        return emit

    zero = jnp.zeros((1, LANES), F32)
    h = sweep(xc_ref, tt_c, 0, False, zero, store_rec(rec_c))
    sweep(xl_ref, tt_l, 0, False, h, store_rec(rec_l))
    h = sweep(xc_ref, tt_c, 1, True, zero, store_out(rec_c, gc_ref, yc_ref))
    sweep(xl_ref, tt_l, 1, True, h, store_out(rec_l, gl_ref, yl_ref))


def _rglru(xa_c, ga_c, xa_l, ga_l, conv_w, conv_b, lru_wi, lru_bi, lru_wr, lru_br, lru_lam):
    b, lc, _ = xa_c.shape
    t = xa_l.shape[1]
    nt = LRU_WIDTH // LANES
    per_tile = LANES // LRU_BLOCK

    def block_diag(w):
        w = w.reshape(2, nt, per_tile, LRU_BLOCK, LRU_BLOCK)
        eye = jnp.eye(per_tile, dtype=w.dtype)
        return jnp.einsum('dtpij,pq->dtpiqj', w, eye).reshape(2, nt, LANES, LANES)

    wg = jnp.concatenate([block_diag(lru_wi), block_diag(lru_wr)], axis=-1).astype(BF16)
    bg = jnp.concatenate([lru_bi.reshape(2, nt, 1, LANES), lru_br.reshape(2, nt, 1, LANES)], axis=-1)
    lam = lru_lam.reshape(2, nt, 1, LANES)
    tt_c = min(256, lc)
    tt_l = min(256, t)
    seq = lambda n: pl.BlockSpec((1, n, LANES), lambda bi, j: (bi, 0, j))
    return pl.pallas_call(
        functools.partial(_rglru_kernel, tt_c=tt_c, tt_l=tt_l),
        out_shape=(jax.ShapeDtypeStruct((b, lc, LRU_WIDTH), BF16),
                   jax.ShapeDtypeStruct((b, t, LRU_WIDTH), BF16)),
        grid=(b, nt),
        in_specs=[seq(lc), seq(lc), seq(t), seq(t),
                  pl.BlockSpec((CONV_W, LANES), lambda bi, j: (0, j)),
                  pl.BlockSpec((1, LANES), lambda bi, j: (0, j)),
                  pl.BlockSpec((2, 1, LANES, 2 * LANES), lambda bi, j: (0, j, 0, 0)),
                  pl.BlockSpec((2, 1, 1, 2 * LANES), lambda bi, j: (0, j, 0, 0)),
                  pl.BlockSpec((2, 1, 1, LANES), lambda bi, j: (0, j, 0, 0))],
        out_specs=(seq(lc), seq(t)),
        scratch_shapes=[pltpu.VMEM((lc, LANES), F32), pltpu.VMEM((t, LANES), F32)],
        compiler_params=_cparams("parallel", "arbitrary"),
        name="l0_rglru",
    )(xa_c, ga_c, xa_l, ga_l, conv_w, conv_b.reshape(1, LRU_WIDTH), wg, bg, lam)


def _attn_kernel(q_ref, *refs, n_seg, kv_chunk):
    kv = refs[:2 * n_seg]
    o_ref = refs[2 * n_seg]
    g = pl.program_id(1)
    q = q_ref[0]
    tq = q.shape[0]
    lane = lax.broadcasted_iota(I32, (1, KV_WIDTH), 1)
    mine = (lane >= g * HEAD_DIM) & (lane < (g + 1) * HEAD_DIM)
    rows = []
    for h in range(Q_GROUP):
        qh = q[:, h * HEAD_DIM:(h + 1) * HEAD_DIM]
        both = jnp.concatenate([qh] * N_KV_HEADS, axis=1)
        rows.append(jnp.where(mine, both, jnp.zeros_like(both)))
    qpad = jnp.concatenate(rows, axis=0)
    nq = Q_GROUP * tq
    m = jnp.full((1, nq), -jnp.inf, F32)
    acc = jnp.zeros((kv[1].shape[2], nq), F32)
    chunks = []
    for i in range(n_seg):
        tk = kv[2 * i].shape[1]
        ck = min(kv_chunk, tk)
        chunks += [(kv[2 * i], kv[2 * i + 1], c * ck, ck) for c in range(tk // ck)]
    scores = lambda ch: _dot_nt(ch[0][0, ch[2]:ch[2] + ch[3], :], qpad)
    ready = [scores(ch) for ch in chunks[:ATTN_LOOKAHEAD]]
    for j, (_, vt_ref, c0, ck) in enumerate(chunks):
        s = ready.pop(0)
        if j + ATTN_LOOKAHEAD < len(chunks):
            ready.append(scores(chunks[j + ATTN_LOOKAHEAD]))
        m_new = jnp.maximum(m, s.max(axis=0, keepdims=True))
        alpha = jnp.exp2(m - m_new)
        p = jnp.exp2(s - m_new)
        acc = alpha * acc + _dot(vt_ref[0, 0, :, c0:c0 + ck], p.astype(BF16))
        m = m_new
    out_t = acc[:HEAD_DIM] / acc[HEAD_DIM:HEAD_DIM + 1]
    outs = [out_t[:, h * tq:(h + 1) * tq].T for h in range(Q_GROUP)]
    o_ref[0] = jnp.concatenate(outs, axis=1).astype(o_ref.dtype)


def _attention(q, segs):
    b, t, _ = q.shape
    tq = min(ATTN_Q_TILE, t)
    gw = Q_GROUP * HEAD_DIM
    in_specs = [pl.BlockSpec((1, tq, gw), lambda bi, g, i: (bi, i, g))]
    args = [q]
    for k, vt in segs:
        tk = k.shape[1]
        in_specs.append(pl.BlockSpec((1, tk, KV_WIDTH), lambda bi, g, i: (bi, 0, 0)))
        in_specs.append(pl.BlockSpec((1, 1, ATTN_VT_ROWS, tk), lambda bi, g, i: (bi, g, 0, 0)))
        args += [k, vt]
    return pl.pallas_call(
        functools.partial(_attn_kernel, n_seg=len(segs), kv_chunk=ATTN_KV_CHUNK),
        out_shape=jax.ShapeDtypeStruct((b, t, Q_WIDTH), BF16),
        grid=(b, N_KV_HEADS, t // tq),
        in_specs=in_specs,
        out_specs=pl.BlockSpec((1, tq, gw), lambda bi, g, i: (bi, i, g)),
        compiler_params=_cparams("parallel", "parallel", "arbitrary"),
        name="l0_attention",
    )(*args)


def _kv_layout(k, v):
    b, t, _ = k.shape
    vt = v.reshape(b, t, N_KV_HEADS, HEAD_DIM).transpose(0, 2, 3, 1)
    extra = jnp.zeros((b, N_KV_HEADS, ATTN_VT_ROWS - HEAD_DIM, t), v.dtype).at[:, :, 0].set(1)
    return k, jnp.concatenate([vt, extra], axis=2)


def _resmm_kernel(x_ref, g_ref, *refs, n):
    acc = None
    for i in range(n):
        d = _dot(refs[i][0], refs[n + i][...])
        acc = d if acc is None else acc + d
    o_ref = refs[2 * n]
    o_ref[0] = x_ref[0] + g_ref[0] * acc


def _residual_matmul(x, gate, acts, weights):
    b, t, _ = x.shape
    tm = min(512, t)
    n = len(acts)
    tok = lambda w: pl.BlockSpec((1, tm, w), lambda bi, i: (bi, i, 0))
    in_specs = [tok(D_MODEL), pl.BlockSpec((1, 1, D_MODEL), lambda bi, i: (bi, 0, 0))]
    in_specs += [tok(a.shape[-1]) for a in acts]
    in_specs += [pl.BlockSpec(w.shape, lambda bi, i: (0, 0)) for w in weights]
    return pl.pallas_call(
        functools.partial(_resmm_kernel, n=n),
        out_shape=jax.ShapeDtypeStruct((b, t, D_MODEL), F32),
        grid=(b, t // tm),
        in_specs=in_specs,
        out_specs=tok(D_MODEL),
        compiler_params=_cparams("parallel", "arbitrary"),
        name="residual_proj",
    )(x, gate, *acts, *weights)


def _router_kernel(x_ref, sc_ref, sh_ref, gain_ref, wr_ref, h_ref, aff_ref):
    h = _norm_mod(x_ref[0], gain_ref[...], sc_ref[0], sh_ref[0])
    tm = h.shape[0]
    logits = _dot_f32(h, wr_ref[...], 2)
    lane = lax.broadcasted_iota(I32, (tm, ROUTE_PAD), 1)
    z = jnp.where(lane < N_EXPERTS, logits, -jnp.inf)
    e = jnp.exp(z - z.max(axis=1, keepdims=True))
    aff = e / e.sum(axis=1, keepdims=True)
    h_ref[0, :, :D_MODEL] = h.astype(BF16)
    hi = aff.astype(BF16).astype(F32)
    h_ref[0, :, D_MODEL:] = (hi + pltpu.roll(aff - hi, N_EXPERTS, 1)).astype(BF16)
    aff_ref[0] = aff.T[:N_EXPERTS]


def _router(x, sc, sh, gain, w_router):
    b, t, _ = x.shape
    tm = min(512, t)
    wr = jnp.pad(w_router, ((0, 0), (0, ROUTE_PAD - N_EXPERTS)))
    tok = lambda w: pl.BlockSpec((1, tm, w), lambda bi, i: (bi, i, 0))
    per_b = pl.BlockSpec((1, 1, D_MODEL), lambda bi, i: (bi, 0, 0))
    return pl.pallas_call(
        _router_kernel,
        out_shape=(jax.ShapeDtypeStruct((b, t, D_MODEL + ROUTE_PAD), BF16),
                   jax.ShapeDtypeStruct((b, N_EXPERTS, t), F32)),
        grid=(b, t // tm),
        in_specs=[tok(D_MODEL), per_b, per_b,
                  pl.BlockSpec((1, D_MODEL), lambda bi, i: (0, 0)),
                  pl.BlockSpec((D_MODEL, ROUTE_PAD), lambda bi, i: (0, 0))],
        out_specs=(tok(D_MODEL + ROUTE_PAD), pl.BlockSpec((1, N_EXPERTS, tm), lambda bi, i: (bi, 0, i))),
        compiler_params=_cparams("parallel", "arbitrary"),
        name="moe_router",
    )(x, sc, sh, gain.reshape(1, D_MODEL), wr)


def _cumsum_lanes(x01, tri):
    outs = []
    off = jnp.zeros((x01.shape[0], 1), F32)
    for j in range(x01.shape[1] // LANES):
        cj = _dot(x01[:, j * LANES:(j + 1) * LANES].astype(BF16), tri) + off
        outs.append(cj)
        off = cj[:, LANES - 1:LANES]
    return outs[0] if len(outs) == 1 else jnp.concatenate(outs, axis=1)


def _select_kernel(aff_ref, tri_ref, tile_ref, pos_ref, tab_ref, *, cap):
    aff = aff_ref[0]
    n = aff.shape[1]
    keys = pltpu.bitcast(aff, I32)

    def bit_step(i, tau):
        cand = tau | jnp.left_shift(jnp.int32(1), 30 - i)
        cnt = jnp.sum((keys >= cand).astype(I32), axis=1, keepdims=True)
        return jnp.where(cnt >= cap, cand, tau)

    tau = lax.fori_loop(0, 31, bit_step, jnp.zeros((N_EXPERTS, 1), I32))
    gt = keys > tau
    eq = keys == tau
    need = (cap - jnp.sum(gt.astype(I32), axis=1, keepdims=True)).astype(F32)
    tri = tri_ref[...]
    c_eq = _cumsum_lanes(eq.astype(F32), tri)
    sel = gt | (eq & (c_eq <= need))
    c_sel = _cumsum_lanes(sel.astype(F32), tri)
    pos_ref[0] = jnp.where(sel, c_sel - 1.0, -1.0)
    tab_ref[0] = _dot(sel.astype(BF16), tile_ref[...]).astype(I32)


def _select(aff_t, cap):
    b, _, n = aff_t.shape
    assert n % MOE_TILE == 0 and n // MOE_TILE <= LANES // 2
    tri = jnp.asarray(np.triu(np.ones((LANES, LANES), np.float32)), dtype=BF16)
    tok = np.arange(n)[:, None]
    j = np.arange(LANES // 2)[None, :]
    tile_tab = jnp.asarray(np.concatenate([tok < j * MOE_TILE, tok // MOE_TILE == j], axis=1), dtype=BF16)
    return pl.pallas_call(
        functools.partial(_select_kernel, cap=cap),
        out_shape=(jax.ShapeDtypeStruct((b, N_EXPERTS, n), F32),
                   jax.ShapeDtypeStruct((b, N_EXPERTS, LANES), I32)),
        grid=(b,),
        in_specs=[pl.BlockSpec((1, N_EXPERTS, n), lambda bi: (bi, 0, 0)),
                  pl.BlockSpec((LANES, LANES), lambda bi: (0, 0)),
                  pl.BlockSpec((n, LANES), lambda bi: (0, 0))],
        out_specs=(pl.BlockSpec((1, N_EXPERTS, n), lambda bi: (bi, 0, 0)),
                   pl.BlockSpec((1, N_EXPERTS, LANES), lambda bi: (bi, 0, 0))),
        compiler_params=_cparams("parallel"),
        name="moe_select",
    )(aff_t, tri, tile_tab)


def _tile_windows(tab_ref, pos_ref, j, win):
    slot = lax.broadcasted_iota(I32, (win, MOE_TILE), 0).astype(F32)
    out = []
    for e in range(N_EXPERTS):
        start = tab_ref[0, e, j]
        count = tab_ref[0, e, LANES // 2 + j]
        a0 = pl.multiple_of(lax.shift_left(lax.shift_right_logical(start, MOE_ALIGN_LOG2), MOE_ALIGN_LOG2),
                            1 << MOE_ALIGN_LOG2)
        p = pos_ref[0, e:e + 1, :]
        onehot = lambda first, p=p: (p == slot + jnp.asarray(first, F32)).astype(BF16)
        extra = jnp.maximum(start - a0 + count - 1, 0) // win
        out.append((a0, onehot, extra))
    return out


def _moe_gather_kernel(tab_ref, h_ref, pos_ref, o_ref, *, win):
    j = pl.program_id(2)

    @pl.when(j == 0)
    def _():
        o_ref[...] = jnp.zeros(o_ref.shape, o_ref.dtype)

    h = h_ref[0]
    wins = _tile_windows(tab_ref, pos_ref, j, win)
    x = _dot(jnp.concatenate([oh(a0) for a0, oh, _ in wins], axis=0), h).astype(BF16)
    for e, (a0, oh, extra) in enumerate(wins):
        o_ref[0, e, pl.ds(a0, win), :] = o_ref[0, e, pl.ds(a0, win), :] + x[e * win:(e + 1) * win]

        def more(k, c, e=e, a0=a0, oh=oh):
            ak = pl.multiple_of(a0 + (k + 1) * win, 1 << MOE_ALIGN_LOG2)
            o_ref[0, e, pl.ds(ak, win), :] = o_ref[0, e, pl.ds(ak, win), :] + _dot(oh(ak), h).astype(BF16)
            return c

        lax.fori_loop(0, extra, more, 0)


def _moe_gather(h_ext, pos, tab, cap):
    b, n, width = h_ext.shape
    cols = width // MOE_COL_SPLIT
    rows = cap + MOE_SLACK
    return pl.pallas_call(
        functools.partial(_moe_gather_kernel, win=MOE_WIN),
        out_shape=jax.ShapeDtypeStruct((b, N_EXPERTS, rows, width), BF16),
        grid=(b, MOE_COL_SPLIT, n // MOE_TILE),
        in_specs=[pl.BlockSpec((1, N_EXPERTS, LANES), lambda bi, c, j: (bi, 0, 0), memory_space=pltpu.SMEM),
                  pl.BlockSpec((1, MOE_TILE, cols), lambda bi, c, j: (bi, j, c)),
                  pl.BlockSpec((1, N_EXPERTS, MOE_TILE), lambda bi, c, j: (bi, 0, j))],
        out_specs=pl.BlockSpec((1, N_EXPERTS, rows, cols), lambda bi, c, j: (bi, 0, 0, c)),
        compiler_params=_cparams("parallel", "parallel", "arbitrary"),
        name="moe_gather",
    )(tab, h_ext, pos)


def _moe_ffn_kernel(x_ref, wg_ref, wu_ref, wd_ref, y_ref, acc_s, *, cap):
    e = pl.program_id(0)
    f = pl.program_id(2)
    nb = x_ref.shape[0]
    x = x_ref[:, 0].reshape(nb * cap, x_ref.shape[3])
    xb = x[:, :D_MODEL]

    @pl.when(f == 0)
    def _():
        acc_s[...] = jnp.zeros(acc_s.shape, F32)

    g = _dot(xb, wg_ref[0].astype(BF16))
    u = _dot(xb, wu_ref[0].astype(BF16))
    hid = (g * jax.nn.sigmoid(g)) * u
    acc_s[...] += _dot(hid.astype(BF16), wd_ref[0].astype(BF16))

    @pl.when(f == pl.num_programs(2) - 1)
    def _():
        lane = lax.broadcasted_iota(I32, (nb * cap, ROUTE_PAD), 1)
        mine = (lane == e) | (lane == e + N_EXPERTS)
        gate = jnp.sum(jnp.where(mine, x[:, D_MODEL:].astype(F32), 0.0), axis=1, keepdims=True)
        y_ref[:, 0, :cap] = (acc_s[...] * gate).astype(y_ref.dtype).reshape(nb, cap, D_MODEL)
        slack = y_ref.shape[2] - cap
        y_ref[:, 0, cap:] = jnp.zeros((nb, slack, D_MODEL), y_ref.dtype)


def _moe_ffn(xe, wg, wu, wd, cap):
    b, _, rows, width = xe.shape
    nb = b if b * cap <= MOE_FFN_ROWS else 1
    fc = MOE_FFN_SLICE
    return pl.pallas_call(
        functools.partial(_moe_ffn_kernel, cap=cap),
        out_shape=jax.ShapeDtypeStruct((b, N_EXPERTS, rows, D_MODEL), BF16),
        grid=(N_EXPERTS, b // nb, EXPERT_FF // fc),
        in_specs=[pl.BlockSpec((nb, 1, cap, width), lambda e, bi, f: (bi, e, 0, 0)),
                  pl.BlockSpec((1, D_MODEL, fc), lambda e, bi, f: (e, 0, f)),
                  pl.BlockSpec((1, D_MODEL, fc), lambda e, bi, f: (e, 0, f)),
                  pl.BlockSpec((1, fc, D_MODEL), lambda e, bi, f: (e, f, 0))],
        out_specs=pl.BlockSpec((nb, 1, rows, D_MODEL), lambda e, bi, f: (bi, e, 0, 0)),
        scratch_shapes=[pltpu.VMEM((nb * cap, D_MODEL), F32)],
        compiler_params=_cparams("arbitrary", "arbitrary", "arbitrary"),
        name="moe_experts",
    )(xe, wg, wu, wd)


def _moe_combine_kernel(tab_ref, x_ref, gt_ref, pos_ref, y_ref, o_ref, acc_s, *, win):
    j = pl.program_id(2)
    wins = _tile_windows(tab_ref, pos_ref, j, win)
    onehots = jnp.concatenate([oh(a0) for a0, oh, _ in wins], axis=0)
    rows = jnp.concatenate([y_ref[0, e, pl.ds(a0, win), :] for e, (a0, _, _) in enumerate(wins)], axis=0)
    acc_s[...] = _dot_tn(onehots, rows)
    for e, (a0, oh, extra) in enumerate(wins):
        def more(k, c, e=e, a0=a0, oh=oh):
            ak = pl.multiple_of(a0 + (k + 1) * win, 1 << MOE_ALIGN_LOG2)
            acc_s[...] += _dot_tn(oh(ak), y_ref[0, e, pl.ds(ak, win), :])
            return c

        lax.fori_loop(0, extra, more, 0)
    o_ref[0] = x_ref[0] + gt_ref[0] * acc_s[...]


def _moe_combine(x, gt, pos, tab, y):
    b, n, _ = x.shape
    rows = y.shape[2]
    cols = D_MODEL // 2
    tile = lambda: pl.BlockSpec((1, MOE_TILE, cols), lambda bi, c, j: (bi, j, c))
    return pl.pallas_call(
        functools.partial(_moe_combine_kernel, win=MOE_WIN),
        out_shape=jax.ShapeDtypeStruct((b, n, D_MODEL), F32),
        grid=(b, 2, n // MOE_TILE),
        in_specs=[pl.BlockSpec((1, N_EXPERTS, LANES), lambda bi, c, j: (bi, 0, 0), memory_space=pltpu.SMEM),
                  tile(),
                  pl.BlockSpec((1, 1, cols), lambda bi, c, j: (bi, 0, c)),
                  pl.BlockSpec((1, N_EXPERTS, MOE_TILE), lambda bi, c, j: (bi, 0, j)),
                  pl.BlockSpec((1, N_EXPERTS, rows, cols), lambda bi, c, j: (bi, 0, 0, c))],
        out_specs=tile(),
        scratch_shapes=[pltpu.VMEM((MOE_TILE, cols), F32)],
        compiler_params=_cparams("parallel", "parallel", "arbitrary"),
        name="moe_combine",
    )(tab, x, gt, pos, y)


def _moe(x, sc, sh, gt, gain, w_router, wg, wu, wd):
    n = x.shape[1]
    cap = max(1, CAPACITY_FACTOR * n // N_EXPERTS)
    h_ext, aff_t = _router(x, sc, sh, gain, w_router)
    pos, tab = _select(aff_t, cap)
    y = _moe_ffn(_moe_gather(h_ext, pos, tab, cap), wg, wu, wd, cap)
    return _moe_combine(x, gt, pos, tab, y)


def _rwkv_prep_kernel(xm_ref, xp_ref, xn_ref, sc_ref, sh_ref, gain_ref, mu_ref, wr_ref, wk_ref, wv_ref,
                      w1_ref, w2_ref, a1_ref, a2_ref, g1_ref, g2_ref, w0_ref, a0_ref, kks_ref, ka_ref, rk_ref,
                      seg_ref, r_ref, v_ref, g_ref, kk_ref, bonus_ref, lw0_ref, lw1_ref, kd0_ref, kd1_ref,
                      ag0_ref, ag1_ref):
    i = pl.program_id(1)
    last = pl.num_programs(1) - 1
    tm = xm_ref.shape[1]
    gain = gain_ref[...]
    sc = sc_ref[0]
    sh = sh_ref[0]
    xe = jnp.concatenate([xp_ref[0], xm_ref[0], xn_ref[0]], axis=0)
    he = _norm_mod(xe, gain, sc, sh)
    row = lax.broadcasted_iota(I32, (tm, 1), 0)
    h = he[SUBLANES:SUBLANES + tm]
    hm1 = jnp.where((row == 0) & (i == 0), 0.0, he[SUBLANES - 1:SUBLANES - 1 + tm])
    hp1 = jnp.where((row == tm - 1) & (i == last), 0.0, he[SUBLANES + 1:SUBLANES + 1 + tm])
    xx = 0.5 * (hm1 + hp1) - h
    mu = mu_ref[...]
    mix = lambda j: (h + xx * mu[j:j + 1]).astype(BF16)
    r = _dot(mix(0), wr_ref[...])
    k = _dot(mix(2), wk_ref[...])
    v = _dot(mix(3), wv_ref[...])
    tw = jnp.tanh(_dot(mix(1), w1_ref[...])).astype(BF16)
    la = _dot(mix(4), a1_ref[...]).astype(BF16)
    g = _dot(jax.nn.sigmoid(_dot(mix(5), g1_ref[...])).astype(BF16), g2_ref[...])
    seg = seg_ref[...]
    kkv = k * kks_ref[...]
    kk = kkv * lax.rsqrt(jnp.maximum(_head_sum(kkv * kkv, seg, 1), 1e-24))
    r_ref[0] = r.astype(r_ref.dtype)
    v_ref[0] = v.astype(v_ref.dtype)
    g_ref[0] = g.astype(g_ref.dtype)
    kk_ref[0] = kk.astype(kk_ref.dtype)
    bonus = None
    for d, (lw_ref, kd_ref, ag_ref) in enumerate(((lw0_ref, kd0_ref, ag0_ref), (lw1_ref, kd1_ref, ag1_ref))):
        w_pre = w0_ref[d] + _dot(tw, w2_ref[d])
        lw_ref[0] = jax.nn.sigmoid(w_pre) * np.float32(-np.exp(-0.5))
        a = jax.nn.sigmoid(a0_ref[d] + _dot(la, a2_ref[d]))
        kd = k * (1.0 + (a - 1.0) * ka_ref[...])
        kd_ref[0] = kd.astype(kd_ref.dtype)
        ag_ref[0] = a.astype(ag_ref.dtype)
        bd = _head_sum(r * kd * rk_ref[...], seg, 1) * v
        bonus = bd if bonus is None else bonus + bd
    bonus_ref[0] = bonus


def _rwkv_prep(x, sc, sh, gain, p):
    b, t, _ = x.shape
    tm = min(256, t)
    nb8 = tm // SUBLANES
    tok = lambda: pl.BlockSpec((1, tm, D_MODEL), lambda bi, i: (bi, i, 0))
    prev = pl.BlockSpec((1, SUBLANES, D_MODEL), lambda bi, i: (bi, jnp.maximum(i * nb8 - 1, 0), 0))
    nxt = pl.BlockSpec((1, SUBLANES, D_MODEL),
                       lambda bi, i: (bi, jnp.minimum((i + 1) * nb8, t // SUBLANES - 1), 0))
    per_b = pl.BlockSpec((1, 1, D_MODEL), lambda bi, i: (bi, 0, 0))
    const = lambda a: pl.BlockSpec(a.shape, lambda bi, i: (0,) * a.ndim)
    consts = [p['gain'], p['mu'], p['w_r'], p['w_k'], p['w_v'], p['w1'], p['w2'], p['a1'], p['a2'], p['g1'],
              p['g2'], p['w0'], p['a0'], p['kks'], p['ka'], p['rk'], p['seg']]
    bf = jax.ShapeDtypeStruct((b, t, D_MODEL), BF16)
    f32 = jax.ShapeDtypeStruct((b, t, D_MODEL), F32)
    return pl.pallas_call(
        _rwkv_prep_kernel,
        out_shape=(bf, bf, bf, bf, f32, f32, f32, bf, bf, bf, bf),
        grid=(b, t // tm),
        in_specs=[tok(), prev, nxt, per_b, per_b] + [const(a) for a in consts],
        out_specs=tuple(tok() for _ in range(11)),
        compiler_params=_cparams("parallel", "arbitrary"),
        name="l1_rwkv_prep",
    )(x, x, x, sc, sh, *consts)


def _rwkv_params(gain, mu, w_r, w_k, w_v, w0, w1, w2, a0, a1, a2, g1, g2, k_k, k_a, r_k):
    def pad_dir(w):
        z = jnp.zeros_like(w[0])
        return jnp.stack([jnp.concatenate([w[0], z], axis=0), jnp.concatenate([z, w[1]], axis=0)])

    row = lambda a: a.reshape(1, D_MODEL)
    return dict(
        gain=row(gain), mu=mu, w_r=w_r.astype(BF16), w_k=w_k.astype(BF16), w_v=w_v.astype(BF16),
        w1=jnp.concatenate([w1[0], w1[1]], axis=1).astype(BF16), w2=pad_dir(w2).astype(BF16),
        a1=jnp.concatenate([a1[0], a1[1]], axis=1).astype(BF16), a2=pad_dir(a2).astype(BF16),
        g1=jnp.pad(g1, ((0, 0), (0, GATE_LORA_PAD - GATE_LORA))).astype(BF16),
        g2=jnp.pad(g2, ((0, GATE_LORA_PAD - GATE_LORA), (0, 0))).astype(BF16),
        w0=w0.reshape(2, 1, D_MODEL), a0=a0.reshape(2, 1, D_MODEL),
        kks=row(k_k), ka=row(k_a), rk=row(r_k), seg=_seg_ones(LANES))


SOLVE_BASE = 8


def _solve_unit_tri(nmat, x, top):
    n = nmat.shape[0]
    rowi = lax.broadcasted_iota(I32, (n, n), 0)
    coli = lax.broadcasted_iota(I32, (n, n), 1)

    def same_block(shift):
        return (rowi >> shift) == (coli >> shift)

    k = int(np.log2(SOLVE_BASE))
    m = jnp.where(same_block(k), nmat, 0.0).astype(BF16)
    tinv = jnp.where(rowi == coli, 1.0, 0.0) + m.astype(F32)
    for _ in range(k - 1):
        m = _dot(m, m).astype(BF16)
        yield None
        tinv = tinv + _dot(m, tinv.astype(BF16))
        yield None
    while (1 << k) < top:
        off = jnp.where(same_block(k + 1) & jnp.logical_not(same_block(k)), nmat, 0.0).astype(BF16)
        tb = tinv.astype(BF16)
        half = _dot(tb, off).astype(BF16)
        yield None
        tinv = tinv + _dot(half, tb)
        yield None
        k += 1
    yield _dot(tinv.astype(BF16), x.astype(BF16))


def _wkv_stage(r_ref, v_ref, kk_ref, lw_ref, kd_ref, ag_ref, tri, ws, zs, vs, gts, d, rev, chunk):
    lw = lw_ref[0]
    p3 = _split3(lw)
    cum = _dot(tri, p3[0]) + _dot(tri, p3[1]) + _dot(tri, p3[2])
    g_in = jnp.exp(cum)
    g_ex = jnp.exp(cum - lw)
    g_inv = jnp.exp(-cum)
    kk = kk_ref[0].astype(F32)
    a_t = -(kk * g_ex)
    r_t = r_ref[0].astype(F32) * g_in
    b_t = kk * ag_ref[0].astype(F32) * g_inv
    k_t = kd_ref[0].astype(F32) * g_inv
    g_tot = jnp.exp(cum[0:1] if rev else cum[chunk - 1:chunk])
    first = (lax.broadcasted_iota(I32, (1, D_MODEL), 1) % LANES) < RWKV_HEAD
    stacked = lambda x: (jnp.where(first, x, jnp.zeros_like(x)), jnp.where(first, jnp.zeros_like(x), x))
    a2, r2, b2, k2 = (stacked(x.astype(BF16)) for x in (a_t, r_t, b_t, k_t))
    v2 = stacked(v_ref[0])
    c2 = 2 * chunk
    for p in range(HEAD_PAIRS):
        sl = slice(p * LANES, (p + 1) * LANES)
        for h in range(2):
            ws[d, p, h * chunk:(h + 1) * chunk] = a2[h][:, sl]
            ws[d, p, c2 + h * chunk:c2 + (h + 1) * chunk] = r2[h][:, sl]
            zs[d, p, h * chunk:(h + 1) * chunk] = b2[h][:, sl]
            zs[d, p, c2 + h * chunk:c2 + (h + 1) * chunk] = k2[h][:, sl]
            vs[d, p, h * chunk:(h + 1) * chunk] = v2[h][:, sl]
        gts[d, p] = g_tot[:, sl]


def _wkv_chain(s_ref, y_ref, ws, zs, vs, gts, d, p, rev, chunk):
    n = 2 * chunk
    w = ws[d, p]
    z = zs[d, p]
    v = vs[d, p]
    s_old = s_ref[d, p]
    pm = _dot_nt(w, z)
    yield
    wst = _dot_nt(w, s_old.astype(BF16))
    yield
    ti = lax.broadcasted_iota(I32, (n, n), 0) & (chunk - 1)
    tj = lax.broadcasted_iota(I32, (n, n), 1) & (chunk - 1)
    strict = ti < tj if rev else ti > tj
    incl = ti <= tj if rev else ti >= tj
    nmat = jnp.where(strict, pm[:n, :n], 0.0)
    ak = jnp.where(strict, pm[:n, n:], 0.0).astype(BF16)
    rbk = jnp.concatenate([jnp.where(incl, pm[n:, :n], 0.0), jnp.where(incl, pm[n:, n:], 0.0)], axis=1).astype(BF16)
    x = wst[:n] + _dot(ak, v)
    yield
    u = None
    for u in _solve_unit_tri(nmat, x, chunk):
        yield
    uv = jnp.concatenate([u.astype(BF16), v], axis=0)
    y = wst[n:] + _dot(rbk, uv)
    yield
    if y_ref is not None:
        y_ref[0, :, p * LANES:(p + 1) * LANES] = y[:chunk] + y[chunk:]
    ds = _dot_tn(uv, z)
    yield
    s_ref[d, p] = (s_old + ds) * gts[d, p]


def _wkv_kernel(*refs, chunk, emit_y, has_init):
    ins = refs[:12]
    tri_ref = refs[12]
    pos = 13
    if has_init:
        s0_ref = refs[pos]
        pos += 1
    yf_ref = yb_ref = None
    if emit_y:
        yf_ref, yb_ref = refs[pos:pos + 2]
        pos += 2
    sfin_ref = refs[pos]
    s_ref, ws, zs, vs, gts = refs[pos + 1:]
    i = pl.program_id(1)

    @pl.when(i == 0)
    def _():
        if has_init:
            s_ref[...] = s0_ref[0]
        else:
            s_ref[...] = jnp.zeros(s_ref.shape, F32)

    for d, rev in ((0, False), (1, True)):
        _wkv_stage(*ins[6 * d:6 * d + 6], tri_ref[d], ws, zs, vs, gts, d, rev, chunk)

    chains = [_wkv_chain(s_ref, (yf_ref, yb_ref)[d], ws, zs, vs, gts, d, p, bool(d), chunk)
              for p in range(HEAD_PAIRS) for d in range(2)]
    while chains:
        alive = []
        for ch in chains:
            try:
                next(ch)
                alive.append(ch)
            except StopIteration:
                pass
        chains = alive

    @pl.when(i == pl.num_programs(1) - 1)
    def _():
        sfin_ref[0] = s_ref[...]


def _wkv(r, v, kk, lw, kd, ag, s0, emit_y, chunk=64):
    b, t, _ = r.shape
    n = t // chunk
    fwd = pl.BlockSpec((1, chunk, D_MODEL), lambda bi, i: (bi, i, 0))
    bwd = pl.BlockSpec((1, chunk, D_MODEL), lambda bi, i: (bi, n - 1 - i, 0))
    tri = jnp.asarray(np.stack([np.tril(np.ones((chunk, chunk), np.float32)),
                                np.triu(np.ones((chunk, chunk), np.float32))]), dtype=BF16)
    state_shape = (2, HEAD_PAIRS, 2 * RWKV_HEAD, LANES)
    state_spec = pl.BlockSpec((1,) + state_shape, lambda bi, i: (bi, 0, 0, 0, 0))
    args = [r, v, kk, lw[0], kd[0], ag[0], r, v, kk, lw[1], kd[1], ag[1], tri]
    in_specs = [fwd] * 6 + [bwd] * 6 + [pl.BlockSpec((2, chunk, chunk), lambda bi, i: (0, 0, 0))]
    if s0 is not None:
        args.append(s0)
        in_specs.append(state_spec)
    out_shape = [jax.ShapeDtypeStruct((b,) + state_shape, F32)]
    out_specs = [state_spec]
    if emit_y:
        out_shape = [jax.ShapeDtypeStruct((b, t, D_MODEL), F32)] * 2 + out_shape
        out_specs = [fwd, bwd] + out_specs
    res = pl.pallas_call(
        functools.partial(_wkv_kernel, chunk=chunk, emit_y=emit_y, has_init=s0 is not None),
        out_shape=tuple(out_shape),
        grid=(b, n),
        in_specs=in_specs,
        out_specs=tuple(out_specs),
        scratch_shapes=[pltpu.VMEM(state_shape, F32),
                        pltpu.VMEM((2, HEAD_PAIRS, 4 * chunk, LANES), BF16),
                        pltpu.VMEM((2, HEAD_PAIRS, 4 * chunk, LANES), BF16),
                        pltpu.VMEM((2, HEAD_PAIRS, 2 * chunk, LANES), BF16),
                        pltpu.VMEM((2, HEAD_PAIRS, 1, LANES), F32)],
        compiler_params=_cparams("parallel", "arbitrary"),
        name="l1_wkv_scan",
    )(*args)
    return res


def _readout_kernel(x_ref, gt_ref, yf_ref, yb_ref, bonus_ref, g_ref, gng_ref, gnb_ref, seg_ref, wo_ref, o_ref):
    seg = seg_ref[...]
    y = yf_ref[0] + yb_ref[0]
    mean = _head_sum(y, seg) * (1.0 / RWKV_HEAD)
    c = y - mean
    var = _head_sum(c * c, seg, 1) * (1.0 / RWKV_HEAD)
    yn = c * lax.rsqrt(var + GN_EPS) * gng_ref[...] + gnb_ref[...]
    out = (yn + bonus_ref[0]) * g_ref[0].astype(F32)
    o_ref[0] = x_ref[0] + gt_ref[0] * _dot(out.astype(BF16), wo_ref[...])


def _readout(x, gt, y_f, y_b, bonus, g, gn_g, gn_b, w_o):
    b, t, _ = x.shape
    tm = min(256, t)
    tok = pl.BlockSpec((1, tm, D_MODEL), lambda bi, i: (bi, i, 0))
    row = pl.BlockSpec((1, D_MODEL), lambda bi, i: (0, 0))
    return pl.pallas_call(
        _readout_kernel,
        out_shape=jax.ShapeDtypeStruct((b, t, D_MODEL), F32),
        grid=(b, t // tm),
        in_specs=[tok, pl.BlockSpec((1, 1, D_MODEL), lambda bi, i: (bi, 0, 0)), tok, tok, tok, tok, row, row,
                  pl.BlockSpec((LANES, LANES), lambda bi, i: (0, 0)),
                  pl.BlockSpec((D_MODEL, D_MODEL), lambda bi, i: (0, 0))],
        out_specs=tok,
        compiler_params=_cparams("parallel", "arbitrary"),
        name="l1_readout",
    )(x, gt, y_f, y_b, bonus, g, gn_g.reshape(1, D_MODEL), gn_b.reshape(1, D_MODEL), _seg_ones(LANES), w_o)


def _modulation(c, c_ctx, w_mod, b_mod):
    b = c.shape[0]
    cond = jnp.zeros((SUBLANES, D_MODEL), F32).at[:b].set(c).at[b].set(c_ctx)
    m = _ada(cond, w_mod, b_mod)
    lat = [m[:b, j * D_MODEL:(j + 1) * D_MODEL].reshape(b, 1, D_MODEL) for j in range(6)]
    ctx = [jnp.broadcast_to(m[b, j * D_MODEL:(j + 1) * D_MODEL].reshape(1, 1, D_MODEL), (b, 1, D_MODEL))
           for j in range(6)]
    return lat, ctx


def kernel(x, c, ctx, c_ctx, l0_w_mod, l0_b_mod, l0_norm_mix, l0_norm_ffn, l0_w_in, l0_conv_w, l0_conv_b, l0_lru_wi, l0_lru_bi, l0_lru_wr, l0_lru_br, l0_lru_lam, l0_q_gain, l0_k_gain, l0_w_out, l0_router, l0_we_gate, l0_we_up, l0_we_down, l1_w_mod, l1_b_mod, l1_norm_mix, l1_norm_ffn, l1_mu, l1_w_r, l1_w_k, l1_w_v, l1_w0, l1_w1, l1_w2, l1_a0, l1_a1, l1_a2, l1_g1, l1_g2, l1_k_k, l1_k_a, l1_r_k, l1_gn_g, l1_gn_b, l1_w_o, l1_router, l1_we_gate, l1_we_up, l1_we_down):
    t = x.shape[1]

    (sh1, sc1, gt1, sh2, sc2, gt2), (csh1, csc1, cgt1, csh2, csc2, cgt2) = _modulation(c, c_ctx, l0_w_mod, l0_b_mod)
    w_in = l0_w_in.astype(BF16)
    cos_t, sin_t = _rope_tables(t)
    xa_l, ga_l, q_l, k_l, v_l = _proj0(x, sc1, sh1, l0_norm_mix, w_in, l0_q_gain, l0_k_gain, cos_t, sin_t, True)
    lc = ctx.shape[1]
    xa_c, ga_c, q_c, k_c, v_c = _proj0(ctx, csc1, csh1, l0_norm_mix, w_in, l0_q_gain, l0_k_gain,
                                       cos_t[:lc], sin_t[:lc], False)
    ya_c, ya_l = _rglru(xa_c, ga_c, xa_l, ga_l, l0_conv_w, l0_conv_b, l0_lru_wi, l0_lru_bi, l0_lru_wr,
                        l0_lru_br, l0_lru_lam)
    seg_l = _kv_layout(k_l, v_l)
    seg_c = _kv_layout(k_c, v_c)
    yb_l = _attention(q_l, [seg_l, seg_c])
    yb_c = _attention(q_c, [seg_c])
    w_out = l0_w_out.astype(BF16)
    w_oa, w_ob = w_out[:LRU_WIDTH], w_out[LRU_WIDTH:]
    x = _residual_matmul(x, gt1, [ya_l, yb_l], [w_oa, w_ob])
    ctx = _residual_matmul(ctx, cgt1, [ya_c, yb_c], [w_oa, w_ob])
    x = _moe(x, sc2, sh2, gt2, l0_norm_ffn, l0_router, l0_we_gate, l0_we_up, l0_we_down)
    ctx = _moe(ctx, csc2, csh2, cgt2, l0_norm_ffn, l0_router, l0_we_gate, l0_we_up, l0_we_down)

    (sh1, sc1, gt1, sh2, sc2, gt2), (csh1, csc1, _, _, _, _) = _modulation(c, c_ctx, l1_w_mod, l1_b_mod)
    p = _rwkv_params(l1_norm_mix, l1_mu, l1_w_r, l1_w_k, l1_w_v, l1_w0, l1_w1, l1_w2, l1_a0, l1_a1, l1_a2,
                     l1_g1, l1_g2, l1_k_k, l1_k_a, l1_r_k)
    r_c, v_c, _, kk_c, _, lw0_c, lw1_c, kd0_c, kd1_c, ag0_c, ag1_c = _rwkv_prep(ctx, csc1, csh1, l1_norm_mix, p)
    r_l, v_l, g_l, kk_l, bonus_l, lw0, lw1, kd0, kd1, ag0, ag1 = _rwkv_prep(x, sc1, sh1, l1_norm_mix, p)
    (s_ctx,) = _wkv(r_c, v_c, kk_c, (lw0_c, lw1_c), (kd0_c, kd1_c), (ag0_c, ag1_c), None, False)
    y_f, y_b, _ = _wkv(r_l, v_l, kk_l, (lw0, lw1), (kd0, kd1), (ag0, ag1), s_ctx, True)
    x = _readout(x, gt1, y_f, y_b, bonus_l, g_l, l1_gn_g, l1_gn_b, l1_w_o.astype(BF16))
    x = _moe(x, sc2, sh2, gt2, l1_norm_ffn, l1_router, l1_we_gate, l1_we_up, l1_we_down)
    return x
```

```python
import functools

import jax
import jax.numpy as jnp
import numpy as np
from jax import lax
from jax.experimental import pallas as pl
from jax.experimental.pallas import tpu as pltpu

F32 = jnp.float32
BF16 = jnp.bfloat16
I32 = jnp.int32

D_MODEL = 1024
LANES = 128
SUBLANES = 8
GRID_W = 64
NORM_EPS = 1e-6
LRU_WIDTH = 512
LRU_BLOCKS = 8
LRU_BLOCK = LRU_WIDTH // LRU_BLOCKS
LRU_C = 8.0
CONV_W = 4
N_Q_HEADS = 8
N_KV_HEADS = 2
HEAD_DIM = 64
Q_GROUP = N_Q_HEADS // N_KV_HEADS
ROPE_BASE = 10000.0
ATTN_SCALE = HEAD_DIM ** -0.5
Q_PRESCALE = ATTN_SCALE * float(np.log2(np.e))
ATTN_KV_CHUNK = 1024
ATTN_Q_TILE = 128
ATTN_LOOKAHEAD = 2
ATTN_VT_ROWS = HEAD_DIM + 16
Q_WIDTH = N_Q_HEADS * HEAD_DIM
KV_WIDTH = N_KV_HEADS * HEAD_DIM
IN_WIDTH = 2 * LRU_WIDTH + Q_WIDTH + 2 * KV_WIDTH
RWKV_HEAD = 64
RWKV_HEADS = D_MODEL // RWKV_HEAD
HEAD_PAIRS = D_MODEL // LANES
DECAY_LORA = 64
AAA_LORA = 64
GATE_LORA = 160
GATE_LORA_PAD = 256
GN_EPS = 64e-5
N_EXPERTS = 16
EXPERT_FF = 2048
CAPACITY_FACTOR = 2
ROUTE_PAD = LANES
VMEM_LIMIT = 60 * 1024 * 1024
MOE_TILE = 256
MOE_ALIGN_LOG2 = 4
MOE_WIN = 48 + (1 << MOE_ALIGN_LOG2)
MOE_SLACK = 128
MOE_COL_SPLIT = 3
MOE_FFN_ROWS = 1024
MOE_FFN_SLICE = 512


def _cparams(*sem):
    return pltpu.CompilerParams(dimension_semantics=sem, vmem_limit_bytes=VMEM_LIMIT)


def _dot(a, b):
    return jnp.dot(a, b, preferred_element_type=F32)


def _dot_nt(a, b):
    return lax.dot_general(a, b, (((1,), (1,)), ((), ())), preferred_element_type=F32)


def _dot_tn(a, b):
    return lax.dot_general(a, b, (((0,), (0,)), ((), ())), preferred_element_type=F32)


def _split2(x):
    hi = x.astype(BF16)
    lo = (x - hi.astype(F32)).astype(BF16)
    return hi, lo


def _split3(x):
    hi = x.astype(BF16)
    r = x - hi.astype(F32)
    mid = r.astype(BF16)
    lo = (r - mid.astype(F32)).astype(BF16)
    return hi, mid, lo


def _dot_f32(a, b, pieces=3):
    split = _split3 if pieces == 3 else _split2
    ap = split(a)
    bp = split(b)
    out = None
    for i in range(pieces):
        for j in range(pieces - i):
            d = _dot(ap[i], bp[j])
            out = d if out is None else out + d
    return out


def _softplus(x):
    return jnp.maximum(x, 0.0) + jnp.log1p(jnp.exp(-jnp.abs(x)))


def _gelu_tanh(x):
    c = np.float32(np.sqrt(2.0 / np.pi))
    return 0.5 * x * (1.0 + jnp.tanh(c * (x + 0.044715 * (x * x * x))))


def _norm_mod(x, gain, sc, sh):
    ms = jnp.mean(x * x, axis=-1, keepdims=True)
    y = x * lax.rsqrt(ms + NORM_EPS)
    return (y * gain) * (1.0 + sc) + sh


def _seg_ones(width):
    i = np.arange(width)[:, None] // HEAD_DIM
    j = np.arange(width)[None, :] // HEAD_DIM
    return jnp.asarray(i == j, dtype=BF16)


def _head_sum(x, seg, pieces=2):
    outs = []
    for c in range(x.shape[1] // LANES):
        xc = x[:, c * LANES:(c + 1) * LANES]
        if pieces == 1:
            outs.append(_dot(xc.astype(BF16), seg))
        else:
            hi, lo = _split2(xc)
            outs.append(_dot(hi, seg) + _dot(lo, seg))
    return outs[0] if len(outs) == 1 else jnp.concatenate(outs, axis=1)


def _ada_kernel(c_ref, w_ref, b_ref, o_ref):
    s = c_ref[...]
    s = s * jax.nn.sigmoid(s)
    o_ref[...] = _dot_f32(s, w_ref[...]) + b_ref[...]


def _ada(cond8, w_mod, b_mod):
    n = w_mod.shape[1]
    tn = 1536
    return pl.pallas_call(
        _ada_kernel,
        out_shape=jax.ShapeDtypeStruct((SUBLANES, n), F32),
        grid=(n // tn,),
        in_specs=[pl.BlockSpec((SUBLANES, D_MODEL), lambda j: (0, 0)),
                  pl.BlockSpec((D_MODEL, tn), lambda j: (0, j)),
                  pl.BlockSpec((1, tn), lambda j: (0, j))],
        out_specs=pl.BlockSpec((SUBLANES, tn), lambda j: (0, j)),
        compiler_params=_cparams("arbitrary"),
        name="ada_params",
    )(cond8, w_mod, b_mod.reshape(1, n))


def _swap_pairs(x):
    lane = lax.broadcasted_iota(I32, x.shape, 1)
    nxt = pltpu.roll(x, LANES - 1, 1)
    prv = pltpu.roll(x, 1, 1)
    return jnp.where(lane % 2 == 0, nxt, prv)


def _proj0_kernel(x_ref, sc_ref, sh_ref, gain_ref, w_ref, qg_ref, kg_ref, seg_ref, cos_ref, sin_ref,
                  xa_ref, ga_ref, q_ref, k_ref, v_ref, *, rope):
    h = _norm_mod(x_ref[0], gain_ref[...], sc_ref[0], sh_ref[0])
    res = _dot(h.astype(BF16), w_ref[...])
    xa_ref[0] = res[:, :LRU_WIDTH]
    ga_ref[0] = res[:, LRU_WIDTH:2 * LRU_WIDTH]
    q0 = 2 * LRU_WIDTH
    seg = seg_ref[...]

    def head_norm_rope(z, gain):
        ms = _head_sum(z * z, seg) * (1.0 / HEAD_DIM)
        zn = z * lax.rsqrt(ms + NORM_EPS) * gain
        if not rope:
            return zn
        c = cos_ref[...]
        s = sin_ref[...]
        outs = []
        for t in range(zn.shape[1] // LANES):
            zt = zn[:, t * LANES:(t + 1) * LANES]
            outs.append(zt * c + _swap_pairs(zt) * s)
        return outs[0] if len(outs) == 1 else jnp.concatenate(outs, axis=1)

    q = head_norm_rope(res[:, q0:q0 + Q_WIDTH], qg_ref[...])
    q_ref[0] = (q * Q_PRESCALE).astype(BF16)
    k = head_norm_rope(res[:, q0 + Q_WIDTH:q0 + Q_WIDTH + KV_WIDTH], kg_ref[...])
    k_ref[0] = k.astype(BF16)
    v_ref[0] = res[:, q0 + Q_WIDTH + KV_WIDTH:].astype(BF16)


def _proj0(x, sc, sh, gain, w_in, q_gain, k_gain, cos_t, sin_t, rope):
    b, t, _ = x.shape
    tm = min(512, t)
    qg = jnp.tile(q_gain, N_Q_HEADS).reshape(1, Q_WIDTH)
    kg = jnp.tile(k_gain, N_KV_HEADS).reshape(1, KV_WIDTH)
    seg = _seg_ones(LANES)
    const = lambda shape: pl.BlockSpec(shape, lambda bi, i: (0,) * len(shape))
    tok = lambda w: pl.BlockSpec((1, tm, w), lambda bi, i: (bi, i, 0))
    per_b = pl.BlockSpec((1, 1, D_MODEL), lambda bi, i: (bi, 0, 0))
    return pl.pallas_call(
        functools.partial(_proj0_kernel, rope=rope),
        out_shape=(jax.ShapeDtypeStruct((b, t, LRU_WIDTH), F32),
                   jax.ShapeDtypeStruct((b, t, LRU_WIDTH), F32),
                   jax.ShapeDtypeStruct((b, t, Q_WIDTH), BF16),
                   jax.ShapeDtypeStruct((b, t, KV_WIDTH), BF16),
                   jax.ShapeDtypeStruct((b, t, KV_WIDTH), BF16)),
        grid=(b, t // tm),
        in_specs=[tok(D_MODEL), per_b, per_b, const((1, D_MODEL)), const((D_MODEL, IN_WIDTH)),
                  const((1, Q_WIDTH)), const((1, KV_WIDTH)), const((LANES, LANES)),
                  pl.BlockSpec((tm, LANES), lambda bi, i: (i, 0)),
                  pl.BlockSpec((tm, LANES), lambda bi, i: (i, 0))],
        out_specs=(tok(LRU_WIDTH), tok(LRU_WIDTH), tok(Q_WIDTH), tok(KV_WIDTH), tok(KV_WIDTH)),
        compiler_params=_cparams("parallel", "arbitrary"),
        name="l0_in_proj",
    )(x, sc, sh, gain.reshape(1, D_MODEL), w_in, qg, kg, seg, cos_t, sin_t)


def _rope_tables(t):
    n_rows = t // GRID_W
    row = jnp.repeat(jnp.arange(n_rows, dtype=F32), GRID_W)
    col = jnp.tile(jnp.arange(GRID_W, dtype=F32), n_rows)
    axis_dim = HEAD_DIM // 2
    inv_freq = ROPE_BASE ** (-jnp.arange(0, axis_dim, 2, dtype=F32) / axis_dim)
    ang = jnp.concatenate([row[:, None] * inv_freq, col[:, None] * inv_freq], axis=-1)
    cos = jnp.repeat(jnp.cos(ang), 2, axis=-1)
    sin = jnp.repeat(jnp.sin(ang), 2, axis=-1)
    sign = jnp.tile(jnp.asarray([-1.0, 1.0], F32), HEAD_DIM // 2)
    return jnp.tile(cos, (1, 2)), jnp.tile(sin * sign, (1, 2))


def _scan_rows(a, b, h0, rev):
    n = a.shape[0]
    groups = n // SUBLANES
    a = a.reshape(groups, SUBLANES, LANES)
    b = b.reshape(groups, SUBLANES, LANES)
    row = lax.broadcasted_iota(I32, a.shape, 1)
    for d in (1, 2, 4):
        shift = SUBLANES - d if rev else d
        m = row < SUBLANES - d if rev else row >= d
        a_s = pltpu.roll(a, shift, 1)
        b_s = pltpu.roll(b, shift, 1)
        b = jnp.where(m, a * b_s + b, b)
        a = jnp.where(m, a * a_s, a)
    outs = [None] * groups
    h = h0
    for g in (range(groups - 1, -1, -1) if rev else range(groups)):
        hg = a[g] * h + b[g]
        outs[g] = hg
        h = hg[0:1] if rev else hg[SUBLANES - 1:SUBLANES]
    return jnp.concatenate(outs, axis=0), h


def _rglru_kernel(xc_ref, gc_ref, xl_ref, gl_ref, cw_ref, cb_ref, wg_ref, bg_ref, lam_ref,
                  yc_ref, yl_ref, rec_c, rec_l, *, tt_c, tt_l):
    cw = cw_ref[...]
    cb = cb_ref[...]

    def coeffs(x_ref, t0, tt, d):
        n = x_ref.shape[1]
        main = x_ref[0, pl.ds(t0, tt), :]
        prev = x_ref[0, pl.ds(pl.multiple_of(jnp.maximum(t0 - SUBLANES, 0), SUBLANES), SUBLANES), :]
        prev = jnp.where(t0 > 0, prev, 0.0)
        nxt = x_ref[0, pl.ds(pl.multiple_of(jnp.minimum(t0 + tt, n - SUBLANES), SUBLANES), SUBLANES), :]
        nxt = jnp.where(t0 + tt < n, nxt, 0.0)
        xe = jnp.concatenate([prev, main, nxt], axis=0)
        o = SUBLANES - CONV_W // 2
        xc = cb
        for j in range(CONV_W):
            xc = xc + cw[j:j + 1] * xe[o + j:o + j + tt]
        g = _dot(xc.astype(BF16), wg_ref[d, 0]) + bg_ref[d, 0]
        i_gate = jax.nn.sigmoid(g[:, :LANES])
        r_gate = jax.nn.sigmoid(g[:, LANES:])
        log_a = LRU_C * r_gate * (-_softplus(-lam_ref[d, 0]))
        a = jnp.exp(log_a)
        bco = jnp.sqrt(-jnp.tanh(log_a) * (a * a + 1.0)) * (i_gate * xc)
        return a, bco

    def sweep(x_ref, tt, d, rev, h, emit):
        nch = x_ref.shape[1] // tt

        def body(i, h):
            ci = nch - 1 - i if rev else i
            t0 = pl.multiple_of(ci * tt, tt)
            a, bco = coeffs(x_ref, t0, tt, d)
            hs, h = _scan_rows(a, bco, h, rev)
            emit(t0, tt, hs)
            return h

        return lax.fori_loop(0, nch, body, h)

    def store_rec(rec):
        def emit(t0, tt, hs):
            rec[pl.ds(t0, tt), :] = hs
        return emit

    def store_out(rec, g_ref, y_ref):
        def emit(t0, tt, hs):
            tot = rec[pl.ds(t0, tt), :] + hs
            y_ref[0, pl.ds(t0, tt), :] = (tot * _gelu_tanh(g_ref[0, pl.ds(t0, tt), :])).astype(y_ref.dtype)
        return emit

    zero = jnp.zeros((1, LANES), F32)
    h = sweep(xc_ref, tt_c, 0, False, zero, store_rec(rec_c))
    sweep(xl_ref, tt_l, 0, False, h, store_rec(rec_l))
    h = sweep(xc_ref, tt_c, 1, True, zero, store_out(rec_c, gc_ref, yc_ref))
    sweep(xl_ref, tt_l, 1, True, h, store_out(rec_l, gl_ref, yl_ref))


def _rglru(xa_c, ga_c, xa_l, ga_l, conv_w, conv_b, lru_wi, lru_bi, lru_wr, lru_br, lru_lam):
    b, lc, _ = xa_c.shape
    t = xa_l.shape[1]
    nt = LRU_WIDTH // LANES
    per_tile = LANES // LRU_BLOCK

    def block_diag(w):
        w = w.reshape(2, nt, per_tile, LRU_BLOCK, LRU_BLOCK)
        eye = jnp.eye(per_tile, dtype=w.dtype)
        return jnp.einsum('dtpij,pq->dtpiqj', w, eye).reshape(2, nt, LANES, LANES)

    wg = jnp.concatenate([block_diag(lru_wi), block_diag(lru_wr)], axis=-1).astype(BF16)
    bg = jnp.concatenate([lru_bi.reshape(2, nt, 1, LANES), lru_br.reshape(2, nt, 1, LANES)], axis=-1)
    lam = lru_lam.reshape(2, nt, 1, LANES)
    tt_c = min(256, lc)
    tt_l = min(256, t)
    seq = lambda n: pl.BlockSpec((1, n, LANES), lambda bi, j: (bi, 0, j))
    return pl.pallas_call(
        functools.partial(_rglru_kernel, tt_c=tt_c, tt_l=tt_l),
        out_shape=(jax.ShapeDtypeStruct((b, lc, LRU_WIDTH), BF16),
                   jax.ShapeDtypeStruct((b, t, LRU_WIDTH), BF16)),
        grid=(b, nt),
        in_specs=[seq(lc), seq(lc), seq(t), seq(t),
                  pl.BlockSpec((CONV_W, LANES), lambda bi, j: (0, j)),
                  pl.BlockSpec((1, LANES), lambda bi, j: (0, j)),
                  pl.BlockSpec((2, 1, LANES, 2 * LANES), lambda bi, j: (0, j, 0, 0)),
                  pl.BlockSpec((2, 1, 1, 2 * LANES), lambda bi, j: (0, j, 0, 0)),
                  pl.BlockSpec((2, 1, 1, LANES), lambda bi, j: (0, j, 0, 0))],
        out_specs=(seq(lc), seq(t)),
        scratch_shapes=[pltpu.VMEM((lc, LANES), F32), pltpu.VMEM((t, LANES), F32)],
        compiler_params=_cparams("parallel", "arbitrary"),
        name="l0_rglru",
    )(xa_c, ga_c, xa_l, ga_l, conv_w, conv_b.reshape(1, LRU_WIDTH), wg, bg, lam)


def _attn_kernel(q_ref, *refs, n_seg, kv_chunk):
    kv = refs[:2 * n_seg]
    o_ref = refs[2 * n_seg]
    g = pl.program_id(1)
    q = q_ref[0]
    tq = q.shape[0]
    lane = lax.broadcasted_iota(I32, (1, KV_WIDTH), 1)
    mine = (lane >= g * HEAD_DIM) & (lane < (g + 1) * HEAD_DIM)
    rows = []
    for h in range(Q_GROUP):
        qh = q[:, h * HEAD_DIM:(h + 1) * HEAD_DIM]
        both = jnp.concatenate([qh] * N_KV_HEADS, axis=1)
        rows.append(jnp.where(mine, both, jnp.zeros_like(both)))
    qpad = jnp.concatenate(rows, axis=0)
    nq = Q_GROUP * tq
    m = jnp.full((1, nq), -jnp.inf, F32)
    acc = jnp.zeros((kv[1].shape[2], nq), F32)
    chunks = []
    for i in range(n_seg):
        tk = kv[2 * i].shape[1]
        ck = min(kv_chunk, tk)
        chunks += [(kv[2 * i], kv[2 * i + 1], c * ck, ck) for c in range(tk // ck)]
    scores = lambda ch: _dot_nt(ch[0][0, ch[2]:ch[2] + ch[3], :], qpad)
    ready = [scores(ch) for ch in chunks[:ATTN_LOOKAHEAD]]
    for j, (_, vt_ref, c0, ck) in enumerate(chunks):
        s = ready.pop(0)
        if j + ATTN_LOOKAHEAD < len(chunks):
            ready.append(scores(chunks[j + ATTN_LOOKAHEAD]))
        m_new = jnp.maximum(m, s.max(axis=0, keepdims=True))
        alpha = jnp.exp2(m - m_new)
        p = jnp.exp2(s - m_new)
        acc = alpha * acc + _dot(vt_ref[0, 0, :, c0:c0 + ck], p.astype(BF16))
        m = m_new
    out_t = acc[:HEAD_DIM] / acc[HEAD_DIM:HEAD_DIM + 1]
    outs = [out_t[:, h * tq:(h + 1) * tq].T for h in range(Q_GROUP)]
    o_ref[0] = jnp.concatenate(outs, axis=1).astype(o_ref.dtype)


def _attention(q, segs):
    b, t, _ = q.shape
    tq = min(ATTN_Q_TILE, t)
    gw = Q_GROUP * HEAD_DIM
    in_specs = [pl.BlockSpec((1, tq, gw), lambda bi, g, i: (bi, i, g))]
    args = [q]
    for k, vt in segs:
        tk = k.shape[1]
        in_specs.append(pl.BlockSpec((1, tk, KV_WIDTH), lambda bi, g, i: (bi, 0, 0)))
        in_specs.append(pl.BlockSpec((1, 1, ATTN_VT_ROWS, tk), lambda bi, g, i: (bi, g, 0, 0)))
        args += [k, vt]
    return pl.pallas_call(
        functools.partial(_attn_kernel, n_seg=len(segs), kv_chunk=ATTN_KV_CHUNK),
        out_shape=jax.ShapeDtypeStruct((b, t, Q_WIDTH), BF16),
        grid=(b, N_KV_HEADS, t // tq),
        in_specs=in_specs,
        out_specs=pl.BlockSpec((1, tq, gw), lambda bi, g, i: (bi, i, g)),
        compiler_params=_cparams("parallel", "parallel", "arbitrary"),
        name="l0_attention",
    )(*args)


def _kv_layout(k, v):
    b, t, _ = k.shape
    vt = v.reshape(b, t, N_KV_HEADS, HEAD_DIM).transpose(0, 2, 3, 1)
    extra = jnp.zeros((b, N_KV_HEADS, ATTN_VT_ROWS - HEAD_DIM, t), v.dtype).at[:, :, 0].set(1)
    return k, jnp.concatenate([vt, extra], axis=2)


def _resmm_kernel(x_ref, g_ref, *refs, n):
    acc = None
    for i in range(n):
        d = _dot(refs[i][0], refs[n + i][...])
        acc = d if acc is None else acc + d
    o_ref = refs[2 * n]
    o_ref[0] = x_ref[0] + g_ref[0] * acc


def _residual_matmul(x, gate, acts, weights):
    b, t, _ = x.shape
    tm = min(512, t)
    n = len(acts)
    tok = lambda w: pl.BlockSpec((1, tm, w), lambda bi, i: (bi, i, 0))
    in_specs = [tok(D_MODEL), pl.BlockSpec((1, 1, D_MODEL), lambda bi, i: (bi, 0, 0))]
    in_specs += [tok(a.shape[-1]) for a in acts]
    in_specs += [pl.BlockSpec(w.shape, lambda bi, i: (0, 0)) for w in weights]
    return pl.pallas_call(
        functools.partial(_resmm_kernel, n=n),
        out_shape=jax.ShapeDtypeStruct((b, t, D_MODEL), F32),
        grid=(b, t // tm),
        in_specs=in_specs,
        out_specs=tok(D_MODEL),
        compiler_params=_cparams("parallel", "arbitrary"),
        name="residual_proj",
    )(x, gate, *acts, *weights)


def _router_kernel(x_ref, sc_ref, sh_ref, gain_ref, wr_ref, h_ref, aff_ref):
    h = _norm_mod(x_ref[0], gain_ref[...], sc_ref[0], sh_ref[0])
    tm = h.shape[0]
    logits = _dot_f32(h, wr_ref[...], 2)
    lane = lax.broadcasted_iota(I32, (tm, ROUTE_PAD), 1)
    z = jnp.where(lane < N_EXPERTS, logits, -jnp.inf)
    e = jnp.exp(z - z.max(axis=1, keepdims=True))
    aff = e / e.sum(axis=1, keepdims=True)
    h_ref[0, :, :D_MODEL] = h.astype(BF16)
    hi = aff.astype(BF16).astype(F32)
    h_ref[0, :, D_MODEL:] = (hi + pltpu.roll(aff - hi, N_EXPERTS, 1)).astype(BF16)
    aff_ref[0] = aff.T[:N_EXPERTS]


def _router(x, sc, sh, gain, w_router):
    b, t, _ = x.shape
    tm = min(512, t)
    wr = jnp.pad(w_router, ((0, 0), (0, ROUTE_PAD - N_EXPERTS)))
    tok = lambda w: pl.BlockSpec((1, tm, w), lambda bi, i: (bi, i, 0))
    per_b = pl.BlockSpec((1, 1, D_MODEL), lambda bi, i: (bi, 0, 0))
    return pl.pallas_call(
        _router_kernel,
        out_shape=(jax.ShapeDtypeStruct((b, t, D_MODEL + ROUTE_PAD), BF16),
                   jax.ShapeDtypeStruct((b, N_EXPERTS, t), F32)),
        grid=(b, t // tm),
        in_specs=[tok(D_MODEL), per_b, per_b,
                  pl.BlockSpec((1, D_MODEL), lambda bi, i: (0, 0)),
                  pl.BlockSpec((D_MODEL, ROUTE_PAD), lambda bi, i: (0, 0))],
        out_specs=(tok(D_MODEL + ROUTE_PAD), pl.BlockSpec((1, N_EXPERTS, tm), lambda bi, i: (bi, 0, i))),
        compiler_params=_cparams("parallel", "arbitrary"),
        name="moe_router",
    )(x, sc, sh, gain.reshape(1, D_MODEL), wr)


def _cumsum_lanes(x01, tri):
    outs = []
    off = jnp.zeros((x01.shape[0], 1), F32)
    for j in range(x01.shape[1] // LANES):
        cj = _dot(x01[:, j * LANES:(j + 1) * LANES].astype(BF16), tri) + off
        outs.append(cj)
        off = cj[:, LANES - 1:LANES]
    return outs[0] if len(outs) == 1 else jnp.concatenate(outs, axis=1)


def _select_kernel(aff_ref, tri_ref, tile_ref, pos_ref, tab_ref, *, cap):
    aff = aff_ref[0]
    n = aff.shape[1]
    keys = pltpu.bitcast(aff, I32)

    def bit_step(i, tau):
        cand = tau | jnp.left_shift(jnp.int32(1), 30 - i)
        cnt = jnp.sum((keys >= cand).astype(I32), axis=1, keepdims=True)
        return jnp.where(cnt >= cap, cand, tau)

    tau = lax.fori_loop(0, 31, bit_step, jnp.zeros((N_EXPERTS, 1), I32))
    gt = keys > tau
    eq = keys == tau
    need = (cap - jnp.sum(gt.astype(I32), axis=1, keepdims=True)).astype(F32)
    tri = tri_ref[...]
    c_eq = _cumsum_lanes(eq.astype(F32), tri)
    sel = gt | (eq & (c_eq <= need))
    c_sel = _cumsum_lanes(sel.astype(F32), tri)
    pos_ref[0] = jnp.where(sel, c_sel - 1.0, -1.0)
    tab_ref[0] = _dot(sel.astype(BF16), tile_ref[...]).astype(I32)


def _select(aff_t, cap):
    b, _, n = aff_t.shape
    assert n % MOE_TILE == 0 and n // MOE_TILE <= LANES // 2
    tri = jnp.asarray(np.triu(np.ones((LANES, LANES), np.float32)), dtype=BF16)
    tok = np.arange(n)[:, None]
    j = np.arange(LANES // 2)[None, :]
    tile_tab = jnp.asarray(np.concatenate([tok < j * MOE_TILE, tok // MOE_TILE == j], axis=1), dtype=BF16)
    return pl.pallas_call(
        functools.partial(_select_kernel, cap=cap),
        out_shape=(jax.ShapeDtypeStruct((b, N_EXPERTS, n), F32),
                   jax.ShapeDtypeStruct((b, N_EXPERTS, LANES), I32)),
        grid=(b,),
        in_specs=[pl.BlockSpec((1, N_EXPERTS, n), lambda bi: (bi, 0, 0)),
                  pl.BlockSpec((LANES, LANES), lambda bi: (0, 0)),
                  pl.BlockSpec((n, LANES), lambda bi: (0, 0))],
        out_specs=(pl.BlockSpec((1, N_EXPERTS, n), lambda bi: (bi, 0, 0)),
                   pl.BlockSpec((1, N_EXPERTS, LANES), lambda bi: (bi, 0, 0))),
        compiler_params=_cparams("parallel"),
        name="moe_select",
    )(aff_t, tri, tile_tab)


def _tile_windows(tab_ref, pos_ref, j, win):
    slot = lax.broadcasted_iota(I32, (win, MOE_TILE), 0).astype(F32)
    out = []
    for e in range(N_EXPERTS):
        start = tab_ref[0, e, j]
        count = tab_ref[0, e, LANES // 2 + j]
        a0 = pl.multiple_of(lax.shift_left(lax.shift_right_logical(start, MOE_ALIGN_LOG2), MOE_ALIGN_LOG2),
                            1 << MOE_ALIGN_LOG2)
        p = pos_ref[0, e:e + 1, :]
        onehot = lambda first, p=p: (p == slot + jnp.asarray(first, F32)).astype(BF16)
        extra = jnp.maximum(start - a0 + count - 1, 0) // win
        out.append((a0, onehot, extra))
    return out


def _moe_gather_kernel(tab_ref, h_ref, pos_ref, o_ref, *, win):
    j = pl.program_id(2)

    @pl.when(j == 0)
    def _():
        o_ref[...] = jnp.zeros(o_ref.shape, o_ref.dtype)

    h = h_ref[0]
    wins = _tile_windows(tab_ref, pos_ref, j, win)
    x = _dot(jnp.concatenate([oh(a0) for a0, oh, _ in wins], axis=0), h).astype(BF16)
    for e, (a0, oh, extra) in enumerate(wins):
        o_ref[0, e, pl.ds(a0, win), :] = o_ref[0, e, pl.ds(a0, win), :] + x[e * win:(e + 1) * win]

        def more(k, c, e=e, a0=a0, oh=oh):
            ak = pl.multiple_of(a0 + (k + 1) * win, 1 << MOE_ALIGN_LOG2)
            o_ref[0, e, pl.ds(ak, win), :] = o_ref[0, e, pl.ds(ak, win), :] + _dot(oh(ak), h).astype(BF16)
            return c

        lax.fori_loop(0, extra, more, 0)


def _moe_gather(h_ext, pos, tab, cap):
    b, n, width = h_ext.shape
    cols = width // MOE_COL_SPLIT
    rows = cap + MOE_SLACK
    return pl.pallas_call(
        functools.partial(_moe_gather_kernel, win=MOE_WIN),
        out_shape=jax.ShapeDtypeStruct((b, N_EXPERTS, rows, width), BF16),
        grid=(b, MOE_COL_SPLIT, n // MOE_TILE),
        in_specs=[pl.BlockSpec((1, N_EXPERTS, LANES), lambda bi, c, j: (bi, 0, 0), memory_space=pltpu.SMEM),
                  pl.BlockSpec((1, MOE_TILE, cols), lambda bi, c, j: (bi, j, c)),
                  pl.BlockSpec((1, N_EXPERTS, MOE_TILE), lambda bi, c, j: (bi, 0, j))],
        out_specs=pl.BlockSpec((1, N_EXPERTS, rows, cols), lambda bi, c, j: (bi, 0, 0, c)),
        compiler_params=_cparams("parallel", "parallel", "arbitrary"),
        name="moe_gather",
    )(tab, h_ext, pos)


def _moe_ffn_kernel(x_ref, wg_ref, wu_ref, wd_ref, y_ref, acc_s, *, cap):
    e = pl.program_id(0)
    f = pl.program_id(2)
    nb = x_ref.shape[0]
    x = x_ref[:, 0].reshape(nb * cap, x_ref.shape[3])
    xb = x[:, :D_MODEL]

    @pl.when(f == 0)
    def _():
        acc_s[...] = jnp.zeros(acc_s.shape, F32)

    g = _dot(xb, wg_ref[0].astype(BF16))
    u = _dot(xb, wu_ref[0].astype(BF16))
    hid = (g * jax.nn.sigmoid(g)) * u
    acc_s[...] += _dot(hid.astype(BF16), wd_ref[0].astype(BF16))

    @pl.when(f == pl.num_programs(2) - 1)
    def _():
        lane = lax.broadcasted_iota(I32, (nb * cap, ROUTE_PAD), 1)
        mine = (lane == e) | (lane == e + N_EXPERTS)
        gate = jnp.sum(jnp.where(mine, x[:, D_MODEL:].astype(F32), 0.0), axis=1, keepdims=True)
        y_ref[:, 0, :cap] = (acc_s[...] * gate).astype(y_ref.dtype).reshape(nb, cap, D_MODEL)
        slack = y_ref.shape[2] - cap
        y_ref[:, 0, cap:] = jnp.zeros((nb, slack, D_MODEL), y_ref.dtype)


def _moe_ffn(xe, wg, wu, wd, cap):
    b, _, rows, width = xe.shape
    nb = b if b * cap <= MOE_FFN_ROWS else 1
    fc = MOE_FFN_SLICE
    return pl.pallas_call(
        functools.partial(_moe_ffn_kernel, cap=cap),
        out_shape=jax.ShapeDtypeStruct((b, N_EXPERTS, rows, D_MODEL), BF16),
        grid=(N_EXPERTS, b // nb, EXPERT_FF // fc),
        in_specs=[pl.BlockSpec((nb, 1, cap, width), lambda e, bi, f: (bi, e, 0, 0)),
                  pl.BlockSpec((1, D_MODEL, fc), lambda e, bi, f: (e, 0, f)),
                  pl.BlockSpec((1, D_MODEL, fc), lambda e, bi, f: (e, 0, f)),
                  pl.BlockSpec((1, fc, D_MODEL), lambda e, bi, f: (e, f, 0))],
        out_specs=pl.BlockSpec((nb, 1, rows, D_MODEL), lambda e, bi, f: (bi, e, 0, 0)),
        scratch_shapes=[pltpu.VMEM((nb * cap, D_MODEL), F32)],
        compiler_params=_cparams("arbitrary", "arbitrary", "arbitrary"),
        name="moe_experts",
    )(xe, wg, wu, wd)


def _moe_combine_kernel(tab_ref, x_ref, gt_ref, pos_ref, y_ref, o_ref, acc_s, *, win):
    j = pl.program_id(2)
    wins = _tile_windows(tab_ref, pos_ref, j, win)
    onehots = jnp.concatenate([oh(a0) for a0, oh, _ in wins], axis=0)
    rows = jnp.concatenate([y_ref[0, e, pl.ds(a0, win), :] for e, (a0, _, _) in enumerate(wins)], axis=0)
    acc_s[...] = _dot_tn(onehots, rows)
    for e, (a0, oh, extra) in enumerate(wins):
        def more(k, c, e=e, a0=a0, oh=oh):
            ak = pl.multiple_of(a0 + (k + 1) * win, 1 << MOE_ALIGN_LOG2)
            acc_s[...] += _dot_tn(oh(ak), y_ref[0, e, pl.ds(ak, win), :])
            return c

        lax.fori_loop(0, extra, more, 0)
    o_ref[0] = x_ref[0] + gt_ref[0] * acc_s[...]


def _moe_combine(x, gt, pos, tab, y):
    b, n, _ = x.shape
    rows = y.shape[2]
    cols = D_MODEL // 2
    tile = lambda: pl.BlockSpec((1, MOE_TILE, cols), lambda bi, c, j: (bi, j, c))
    return pl.pallas_call(
        functools.partial(_moe_combine_kernel, win=MOE_WIN),
        out_shape=jax.ShapeDtypeStruct((b, n, D_MODEL), F32),
        grid=(b, 2, n // MOE_TILE),
        in_specs=[pl.BlockSpec((1, N_EXPERTS, LANES), lambda bi, c, j: (bi, 0, 0), memory_space=pltpu.SMEM),
                  tile(),
                  pl.BlockSpec((1, 1, cols), lambda bi, c, j: (bi, 0, c)),
                  pl.BlockSpec((1, N_EXPERTS, MOE_TILE), lambda bi, c, j: (bi, 0, j)),
                  pl.BlockSpec((1, N_EXPERTS, rows, cols), lambda bi, c, j: (bi, 0, 0, c))],
        out_specs=tile(),
        scratch_shapes=[pltpu.VMEM((MOE_TILE, cols), F32)],
        compiler_params=_cparams("parallel", "parallel", "arbitrary"),
        name="moe_combine",
    )(tab, x, gt, pos, y)


def _moe(x, sc, sh, gt, gain, w_router, wg, wu, wd):
    n = x.shape[1]
    cap = max(1, CAPACITY_FACTOR * n // N_EXPERTS)
    h_ext, aff_t = _router(x, sc, sh, gain, w_router)
    pos, tab = _select(aff_t, cap)
    y = _moe_ffn(_moe_gather(h_ext, pos, tab, cap), wg, wu, wd, cap)
    return _moe_combine(x, gt, pos, tab, y)


def _rwkv_prep_kernel(xm_ref, xp_ref, xn_ref, sc_ref, sh_ref, gain_ref, mu_ref, wr_ref, wk_ref, wv_ref,
                      w1_ref, w2_ref, a1_ref, a2_ref, g1_ref, g2_ref, w0_ref, a0_ref, kks_ref, ka_ref, rk_ref,
                      seg_ref, r_ref, v_ref, g_ref, kk_ref, bonus_ref, lw0_ref, lw1_ref, kd0_ref, kd1_ref,
                      ag0_ref, ag1_ref):
    i = pl.program_id(1)
    last = pl.num_programs(1) - 1
    tm = xm_ref.shape[1]
    gain = gain_ref[...]
    sc = sc_ref[0]
    sh = sh_ref[0]
    xe = jnp.concatenate([xp_ref[0], xm_ref[0], xn_ref[0]], axis=0)
    he = _norm_mod(xe, gain, sc, sh)
    row = lax.broadcasted_iota(I32, (tm, 1), 0)
    h = he[SUBLANES:SUBLANES + tm]
    hm1 = jnp.where((row == 0) & (i == 0), 0.0, he[SUBLANES - 1:SUBLANES - 1 + tm])
    hp1 = jnp.where((row == tm - 1) & (i == last), 0.0, he[SUBLANES + 1:SUBLANES + 1 + tm])
    xx = 0.5 * (hm1 + hp1) - h
    mu = mu_ref[...]
    mix = lambda j: (h + xx * mu[j:j + 1]).astype(BF16)
    r = _dot(mix(0), wr_ref[...])
    k = _dot(mix(2), wk_ref[...])
    v = _dot(mix(3), wv_ref[...])
    tw = jnp.tanh(_dot(mix(1), w1_ref[...])).astype(BF16)
    la = _dot(mix(4), a1_ref[...]).astype(BF16)
    g = _dot(jax.nn.sigmoid(_dot(mix(5), g1_ref[...])).astype(BF16), g2_ref[...])
    seg = seg_ref[...]
    kkv = k * kks_ref[...]
    kk = kkv * lax.rsqrt(jnp.maximum(_head_sum(kkv * kkv, seg, 1), 1e-24))
    r_ref[0] = r.astype(r_ref.dtype)
    v_ref[0] = v.astype(v_ref.dtype)
    g_ref[0] = g.astype(g_ref.dtype)
    kk_ref[0] = kk.astype(kk_ref.dtype)
    bonus = None
    for d, (lw_ref, kd_ref, ag_ref) in enumerate(((lw0_ref, kd0_ref, ag0_ref), (lw1_ref, kd1_ref, ag1_ref))):
        w_pre = w0_ref[d] + _dot(tw, w2_ref[d])
        lw_ref[0] = jax.nn.sigmoid(w_pre) * np.float32(-np.exp(-0.5))
        a = jax.nn.sigmoid(a0_ref[d] + _dot(la, a2_ref[d]))
        kd = k * (1.0 + (a - 1.0) * ka_ref[...])
        kd_ref[0] = kd.astype(kd_ref.dtype)
        ag_ref[0] = a.astype(ag_ref.dtype)
        bd = _head_sum(r * kd * rk_ref[...], seg, 1) * v
        bonus = bd if bonus is None else bonus + bd
    bonus_ref[0] = bonus


def _rwkv_prep(x, sc, sh, gain, p):
    b, t, _ = x.shape
    tm = min(256, t)
    nb8 = tm // SUBLANES
    tok = lambda: pl.BlockSpec((1, tm, D_MODEL), lambda bi, i: (bi, i, 0))
    prev = pl.BlockSpec((1, SUBLANES, D_MODEL), lambda bi, i: (bi, jnp.maximum(i * nb8 - 1, 0), 0))
    nxt = pl.BlockSpec((1, SUBLANES, D_MODEL),
                       lambda bi, i: (bi, jnp.minimum((i + 1) * nb8, t // SUBLANES - 1), 0))
    per_b = pl.BlockSpec((1, 1, D_MODEL), lambda bi, i: (bi, 0, 0))
    const = lambda a: pl.BlockSpec(a.shape, lambda bi, i: (0,) * a.ndim)
    consts = [p['gain'], p['mu'], p['w_r'], p['w_k'], p['w_v'], p['w1'], p['w2'], p['a1'], p['a2'], p['g1'],
              p['g2'], p['w0'], p['a0'], p['kks'], p['ka'], p['rk'], p['seg']]
    bf = jax.ShapeDtypeStruct((b, t, D_MODEL), BF16)
    f32 = jax.ShapeDtypeStruct((b, t, D_MODEL), F32)
    return pl.pallas_call(
        _rwkv_prep_kernel,
        out_shape=(bf, bf, bf, bf, f32, f32, f32, bf, bf, bf, bf),
        grid=(b, t // tm),
        in_specs=[tok(), prev, nxt, per_b, per_b] + [const(a) for a in consts],
        out_specs=tuple(tok() for _ in range(11)),
        compiler_params=_cparams("parallel", "arbitrary"),
        name="l1_rwkv_prep",
    )(x, x, x, sc, sh, *consts)


def _rwkv_params(gain, mu, w_r, w_k, w_v, w0, w1, w2, a0, a1, a2, g1, g2, k_k, k_a, r_k):
    def pad_dir(w):
        z = jnp.zeros_like(w[0])
        return jnp.stack([jnp.concatenate([w[0], z], axis=0), jnp.concatenate([z, w[1]], axis=0)])

    row = lambda a: a.reshape(1, D_MODEL)
    return dict(
        gain=row(gain), mu=mu, w_r=w_r.astype(BF16), w_k=w_k.astype(BF16), w_v=w_v.astype(BF16),
        w1=jnp.concatenate([w1[0], w1[1]], axis=1).astype(BF16), w2=pad_dir(w2).astype(BF16),
        a1=jnp.concatenate([a1[0], a1[1]], axis=1).astype(BF16), a2=pad_dir(a2).astype(BF16),
        g1=jnp.pad(g1, ((0, 0), (0, GATE_LORA_PAD - GATE_LORA))).astype(BF16),
        g2=jnp.pad(g2, ((0, GATE_LORA_PAD - GATE_LORA), (0, 0))).astype(BF16),
        w0=w0.reshape(2, 1, D_MODEL), a0=a0.reshape(2, 1, D_MODEL),
        kks=row(k_k), ka=row(k_a), rk=row(r_k), seg=_seg_ones(LANES))


SOLVE_BASE = 8


def _solve_masks(n, top):
    rowi = lax.broadcasted_iota(I32, (n, n), 0)
    coli = lax.broadcasted_iota(I32, (n, n), 1)
    same = lambda shift: (rowi >> shift) == (coli >> shift)
    k = int(np.log2(SOLVE_BASE))
    levels = range(k, int(np.log2(top)))
    return (jnp.where(rowi == coli, 1.0, 0.0), same(k),
            [same(j + 1) & jnp.logical_not(same(j)) for j in levels])


def _solve_unit_tri(nmat, x, masks):
    eye, base, joins = masks
    m = jnp.where(base, nmat, 0.0).astype(BF16)
    tinv = eye + m.astype(F32)
    for _ in range(int(np.log2(SOLVE_BASE)) - 1):
        m = _dot(m, m).astype(BF16)
        yield None
        tinv = tinv + _dot(m, tinv.astype(BF16))
        yield None
    for join in joins:
        off = jnp.where(join, nmat, 0.0).astype(BF16)
        tb = tinv.astype(BF16)
        half = _dot(tb, off).astype(BF16)
        yield None
        tinv = tinv + _dot(half, tb)
        yield None
    yield _dot(tinv.astype(BF16), x.astype(BF16))


def _wkv_stage(r_ref, v_ref, kk_ref, lw_ref, kd_ref, ag_ref, tri, ws, zs, vs, gts, d, rev, chunk):
    lw = lw_ref[0]
    p3 = _split3(lw)
    cum = _dot(tri, p3[0]) + _dot(tri, p3[1]) + _dot(tri, p3[2])
    g_in = jnp.exp(cum)
    g_ex = jnp.exp(cum - lw)
    g_inv = jnp.exp(-cum)
    kk = kk_ref[0].astype(F32)
    a_t = -(kk * g_ex)
    r_t = r_ref[0].astype(F32) * g_in
    b_t = kk * ag_ref[0].astype(F32) * g_inv
    k_t = kd_ref[0].astype(F32) * g_inv
    g_tot = jnp.exp(cum[0:1] if rev else cum[chunk - 1:chunk])
    first = (lax.broadcasted_iota(I32, (1, D_MODEL), 1) % LANES) < RWKV_HEAD
    stacked = lambda x: (jnp.where(first, x, jnp.zeros_like(x)), jnp.where(first, jnp.zeros_like(x), x))
    a2, r2, b2, k2 = (stacked(x.astype(BF16)) for x in (a_t, r_t, b_t, k_t))
    v2 = stacked(v_ref[0])
    c2 = 2 * chunk
    for p in range(HEAD_PAIRS):
        sl = slice(p * LANES, (p + 1) * LANES)
        for h in range(2):
            ws[d, p, h * chunk:(h + 1) * chunk] = a2[h][:, sl]
            ws[d, p, c2 + h * chunk:c2 + (h + 1) * chunk] = r2[h][:, sl]
            zs[d, p, h * chunk:(h + 1) * chunk] = b2[h][:, sl]
            zs[d, p, c2 + h * chunk:c2 + (h + 1) * chunk] = k2[h][:, sl]
            vs[d, p, h * chunk:(h + 1) * chunk] = v2[h][:, sl]
        gts[d, p] = g_tot[:, sl]


def _wkv_masks(chunk, rev):
    n = 2 * chunk
    ti = lax.broadcasted_iota(I32, (n, 2 * n), 0) & (chunk - 1)
    tj = lax.broadcasted_iota(I32, (n, 2 * n), 1) & (chunk - 1)
    strict = ti < tj if rev else ti > tj
    incl = ti <= tj if rev else ti >= tj
    return strict, incl, _solve_masks(n, chunk)


def _wkv_chain(s_ref, y_ref, ws, zs, vs, gts, d, p, masks, chunk):
    n = 2 * chunk
    w = ws[d, p]
    z = zs[d, p]
    v = vs[d, p]
    s_old = s_ref[d, p]
    pm = _dot_nt(w, z)
    yield
    wst = _dot_nt(w, s_old.astype(BF16))
    yield
    strict, incl, solve_masks = masks
    top = jnp.where(strict, pm[:n], 0.0)
    nmat = top[:, :n]
    ak = top[:, n:].astype(BF16)
    rbk = jnp.where(incl, pm[n:], 0.0).astype(BF16)
    x = wst[:n] + _dot(ak, v)
    yield
    u = None
    for u in _solve_unit_tri(nmat, x, solve_masks):
        yield
    uv = jnp.concatenate([u.astype(BF16), v], axis=0)
    y = wst[n:] + _dot(rbk, uv)
    yield
    if y_ref is not None:
        y_ref[0, :, p * LANES:(p + 1) * LANES] = y[:chunk] + y[chunk:]
    ds = _dot_tn(uv, z)
    yield
    s_ref[d, p] = (s_old + ds) * gts[d, p]


def _wkv_kernel(*refs, chunk, emit_y, has_init):
    ins = refs[:12]
    tri_ref = refs[12]
    pos = 13
    if has_init:
        s0_ref = refs[pos]
        pos += 1
    yf_ref = yb_ref = None
    if emit_y:
        yf_ref, yb_ref = refs[pos:pos + 2]
        pos += 2
    sfin_ref = refs[pos]
    s_ref, ws, zs, vs, gts = refs[pos + 1:]
    i = pl.program_id(1)

    @pl.when(i == 0)
    def _():
        if has_init:
            s_ref[...] = s0_ref[0]
        else:
            s_ref[...] = jnp.zeros(s_ref.shape, F32)

    for d, rev in ((0, False), (1, True)):
        _wkv_stage(*ins[6 * d:6 * d + 6], tri_ref[d], ws, zs, vs, gts, d, rev, chunk)

    masks = [_wkv_masks(chunk, False), _wkv_masks(chunk, True)]
    chains = [_wkv_chain(s_ref, (yf_ref, yb_ref)[d], ws, zs, vs, gts, d, p, masks[d], chunk)
              for p in range(HEAD_PAIRS) for d in range(2)]
    while chains:
        alive = []
        for ch in chains:
            try:
                next(ch)
                alive.append(ch)
            except StopIteration:
                pass
        chains = alive

    @pl.when(i == pl.num_programs(1) - 1)
    def _():
        sfin_ref[0] = s_ref[...]


def _wkv(r, v, kk, lw, kd, ag, s0, emit_y, chunk=64):
    b, t, _ = r.shape
    n = t // chunk
    fwd = pl.BlockSpec((1, chunk, D_MODEL), lambda bi, i: (bi, i, 0))
    bwd = pl.BlockSpec((1, chunk, D_MODEL), lambda bi, i: (bi, n - 1 - i, 0))
    tri = jnp.asarray(np.stack([np.tril(np.ones((chunk, chunk), np.float32)),
                                np.triu(np.ones((chunk, chunk), np.float32))]), dtype=BF16)
    state_shape = (2, HEAD_PAIRS, 2 * RWKV_HEAD, LANES)
    state_spec = pl.BlockSpec((1,) + state_shape, lambda bi, i: (bi, 0, 0, 0, 0))
    args = [r, v, kk, lw[0], kd[0], ag[0], r, v, kk, lw[1], kd[1], ag[1], tri]
    in_specs = [fwd] * 6 + [bwd] * 6 + [pl.BlockSpec((2, chunk, chunk), lambda bi, i: (0, 0, 0))]
    if s0 is not None:
        args.append(s0)
        in_specs.append(state_spec)
    out_shape = [jax.ShapeDtypeStruct((b,) + state_shape, F32)]
    out_specs = [state_spec]
    if emit_y:
        out_shape = [jax.ShapeDtypeStruct((b, t, D_MODEL), F32)] * 2 + out_shape
        out_specs = [fwd, bwd] + out_specs
    res = pl.pallas_call(
        functools.partial(_wkv_kernel, chunk=chunk, emit_y=emit_y, has_init=s0 is not None),
        out_shape=tuple(out_shape),
        grid=(b, n),
        in_specs=in_specs,
        out_specs=tuple(out_specs),
        scratch_shapes=[pltpu.VMEM(state_shape, F32),
                        pltpu.VMEM((2, HEAD_PAIRS, 4 * chunk, LANES), BF16),
                        pltpu.VMEM((2, HEAD_PAIRS, 4 * chunk, LANES), BF16),
                        pltpu.VMEM((2, HEAD_PAIRS, 2 * chunk, LANES), BF16),
                        pltpu.VMEM((2, HEAD_PAIRS, 1, LANES), F32)],
        compiler_params=_cparams("parallel", "arbitrary"),
        name="l1_wkv_scan",
    )(*args)
    return res


def _readout_kernel(x_ref, gt_ref, yf_ref, yb_ref, bonus_ref, g_ref, gng_ref, gnb_ref, seg_ref, wo_ref, o_ref):
    seg = seg_ref[...]
    y = yf_ref[0] + yb_ref[0]
    mean = _head_sum(y, seg) * (1.0 / RWKV_HEAD)
    c = y - mean
    var = _head_sum(c * c, seg, 1) * (1.0 / RWKV_HEAD)
    yn = c * lax.rsqrt(var + GN_EPS) * gng_ref[...] + gnb_ref[...]
    out = (yn + bonus_ref[0]) * g_ref[0].astype(F32)
    o_ref[0] = x_ref[0] + gt_ref[0] * _dot(out.astype(BF16), wo_ref[...])


def _readout(x, gt, y_f, y_b, bonus, g, gn_g, gn_b, w_o):
    b, t, _ = x.shape
    tm = min(256, t)
    tok = pl.BlockSpec((1, tm, D_MODEL), lambda bi, i: (bi, i, 0))
    row = pl.BlockSpec((1, D_MODEL), lambda bi, i: (0, 0))
    return pl.pallas_call(
        _readout_kernel,
        out_shape=jax.ShapeDtypeStruct((b, t, D_MODEL), F32),
        grid=(b, t // tm),
        in_specs=[tok, pl.BlockSpec((1, 1, D_MODEL), lambda bi, i: (bi, 0, 0)), tok, tok, tok, tok, row, row,
                  pl.BlockSpec((LANES, LANES), lambda bi, i: (0, 0)),
                  pl.BlockSpec((D_MODEL, D_MODEL), lambda bi, i: (0, 0))],
        out_specs=tok,
        compiler_params=_cparams("parallel", "arbitrary"),
        name="l1_readout",
    )(x, gt, y_f, y_b, bonus, g, gn_g.reshape(1, D_MODEL), gn_b.reshape(1, D_MODEL), _seg_ones(LANES), w_o)


def _modulation(c, c_ctx, w_mod, b_mod):
    b = c.shape[0]
    cond = jnp.zeros((SUBLANES, D_MODEL), F32).at[:b].set(c).at[b].set(c_ctx)
    m = _ada(cond, w_mod, b_mod)
    lat = [m[:b, j * D_MODEL:(j + 1) * D_MODEL].reshape(b, 1, D_MODEL) for j in range(6)]
    ctx = [jnp.broadcast_to(m[b, j * D_MODEL:(j + 1) * D_MODEL].reshape(1, 1, D_MODEL), (b, 1, D_MODEL))
           for j in range(6)]
    return lat, ctx


def kernel(x, c, ctx, c_ctx, l0_w_mod, l0_b_mod, l0_norm_mix, l0_norm_ffn, l0_w_in, l0_conv_w, l0_conv_b, l0_lru_wi, l0_lru_bi, l0_lru_wr, l0_lru_br, l0_lru_lam, l0_q_gain, l0_k_gain, l0_w_out, l0_router, l0_we_gate, l0_we_up, l0_we_down, l1_w_mod, l1_b_mod, l1_norm_mix, l1_norm_ffn, l1_mu, l1_w_r, l1_w_k, l1_w_v, l1_w0, l1_w1, l1_w2, l1_a0, l1_a1, l1_a2, l1_g1, l1_g2, l1_k_k, l1_k_a, l1_r_k, l1_gn_g, l1_gn_b, l1_w_o, l1_router, l1_we_gate, l1_we_up, l1_we_down):
    t = x.shape[1]

    (sh1, sc1, gt1, sh2, sc2, gt2), (csh1, csc1, cgt1, csh2, csc2, cgt2) = _modulation(c, c_ctx, l0_w_mod, l0_b_mod)
    w_in = l0_w_in.astype(BF16)
    cos_t, sin_t = _rope_tables(t)
    xa_l, ga_l, q_l, k_l, v_l = _proj0(x, sc1, sh1, l0_norm_mix, w_in, l0_q_gain, l0_k_gain, cos_t, sin_t, True)
    lc = ctx.shape[1]
    xa_c, ga_c, q_c, k_c, v_c = _proj0(ctx, csc1, csh1, l0_norm_mix, w_in, l0_q_gain, l0_k_gain,
                                       cos_t[:lc], sin_t[:lc], False)
    ya_c, ya_l = _rglru(xa_c, ga_c, xa_l, ga_l, l0_conv_w, l0_conv_b, l0_lru_wi, l0_lru_bi, l0_lru_wr,
                        l0_lru_br, l0_lru_lam)
    seg_l = _kv_layout(k_l, v_l)
    seg_c = _kv_layout(k_c, v_c)
    yb_l = _attention(q_l, [seg_l, seg_c])
    yb_c = _attention(q_c, [seg_c])
    w_out = l0_w_out.astype(BF16)
    w_oa, w_ob = w_out[:LRU_WIDTH], w_out[LRU_WIDTH:]
    x = _residual_matmul(x, gt1, [ya_l, yb_l], [w_oa, w_ob])
    ctx = _residual_matmul(ctx, cgt1, [ya_c, yb_c], [w_oa, w_ob])
    x = _moe(x, sc2, sh2, gt2, l0_norm_ffn, l0_router, l0_we_gate, l0_we_up, l0_we_down)
    ctx = _moe(ctx, csc2, csh2, cgt2, l0_norm_ffn, l0_router, l0_we_gate, l0_we_up, l0_we_down)

    (sh1, sc1, gt1, sh2, sc2, gt2), (csh1, csc1, _, _, _, _) = _modulation(c, c_ctx, l1_w_mod, l1_b_mod)
    p = _rwkv_params(l1_norm_mix, l1_mu, l1_w_r, l1_w_k, l1_w_v, l1_w0, l1_w1, l1_w2, l1_a0, l1_a1, l1_a2,
                     l1_g1, l1_g2, l1_k_k, l1_k_a, l1_r_k)
    r_c, v_c, _, kk_c, _, lw0_c, lw1_c, kd0_c, kd1_c, ag0_c, ag1_c = _rwkv_prep(ctx, csc1, csh1, l1_norm_mix, p)
    r_l, v_l, g_l, kk_l, bonus_l, lw0, lw1, kd0, kd1, ag0, ag1 = _rwkv_prep(x, sc1, sh1, l1_norm_mix, p)
    (s_ctx,) = _wkv(r_c, v_c, kk_c, (lw0_c, lw1_c), (kd0_c, kd1_c), (ag0_c, ag1_c), None, False)
    y_f, y_b, _ = _wkv(r_l, v_l, kk_l, (lw0, lw1), (kd0, kd1), (ag0, ag1), s_ctx, True)
    x = _readout(x, gt1, y_f, y_b, bonus_l, g_l, l1_gn_g, l1_gn_b, l1_w_o.astype(BF16))
    x = _moe(x, sc2, sh2, gt2, l1_norm_ffn, l1_router, l1_we_gate, l1_we_up, l1_we_down)
    return x
```

```python
import functools

import jax
import jax.numpy as jnp
import numpy as np
from jax import lax
from jax.experimental import pallas as pl
from jax.experimental.pallas import tpu as pltpu

F32 = jnp.float32
BF16 = jnp.bfloat16
I32 = jnp.int32

D_MODEL = 1024
LANES = 128
SUBLANES = 8
GRID_W = 64
NORM_EPS = 1e-6
LRU_WIDTH = 512
LRU_BLOCKS = 8
LRU_BLOCK = LRU_WIDTH // LRU_BLOCKS
LRU_C = 8.0
CONV_W = 4
N_Q_HEADS = 8
N_KV_HEADS = 2
HEAD_DIM = 64
Q_GROUP = N_Q_HEADS // N_KV_HEADS
ROPE_BASE = 10000.0
ATTN_SCALE = HEAD_DIM ** -0.5
Q_PRESCALE = ATTN_SCALE * float(np.log2(np.e))
ATTN_KV_CHUNK = 1024
ATTN_Q_TILE = 128
ATTN_LOOKAHEAD = 2
ATTN_VT_ROWS = HEAD_DIM + 16
Q_WIDTH = N_Q_HEADS * HEAD_DIM
KV_WIDTH = N_KV_HEADS * HEAD_DIM
IN_WIDTH = 2 * LRU_WIDTH + Q_WIDTH + 2 * KV_WIDTH
RWKV_HEAD = 64
RWKV_HEADS = D_MODEL // RWKV_HEAD
HEAD_PAIRS = D_MODEL // LANES
DECAY_LORA = 64
AAA_LORA = 64
GATE_LORA = 160
GATE_LORA_PAD = 256
GN_EPS = 64e-5
N_EXPERTS = 16
EXPERT_FF = 2048
CAPACITY_FACTOR = 2
ROUTE_PAD = LANES
VMEM_LIMIT = 60 * 1024 * 1024
MOE_TILE = 256
MOE_ALIGN_LOG2 = 4
MOE_WIN = 48 + (1 << MOE_ALIGN_LOG2)
MOE_SLACK = 128
MOE_COL_SPLIT = 3
MOE_FFN_ROWS = 1024
MOE_FFN_SLICE = 512


def _cparams(*sem):
    return pltpu.CompilerParams(dimension_semantics=sem, vmem_limit_bytes=VMEM_LIMIT)


def _dot(a, b):
    return jnp.dot(a, b, preferred_element_type=F32)


def _dot_nt(a, b):
    return lax.dot_general(a, b, (((1,), (1,)), ((), ())), preferred_element_type=F32)


def _dot_tn(a, b):
    return lax.dot_general(a, b, (((0,), (0,)), ((), ())), preferred_element_type=F32)


def _split2(x):
    hi = x.astype(BF16)
    lo = (x - hi.astype(F32)).astype(BF16)
    return hi, lo


def _split3(x):
    hi = x.astype(BF16)
    r = x - hi.astype(F32)
    mid = r.astype(BF16)
    lo = (r - mid.astype(F32)).astype(BF16)
    return hi, mid, lo


def _dot_f32(a, b, pieces=3):
    split = _split3 if pieces == 3 else _split2
    ap = split(a)
    bp = split(b)
    out = None
    for i in range(pieces):
        for j in range(pieces - i):
            d = _dot(ap[i], bp[j])
            out = d if out is None else out + d
    return out


def _softplus(x):
    return jnp.maximum(x, 0.0) + jnp.log1p(jnp.exp(-jnp.abs(x)))


def _gelu_tanh(x):
    c = np.float32(np.sqrt(2.0 / np.pi))
    return 0.5 * x * (1.0 + jnp.tanh(c * (x + 0.044715 * (x * x * x))))


def _norm_mod(x, gain, sc, sh):
    ms = jnp.mean(x * x, axis=-1, keepdims=True)
    y = x * lax.rsqrt(ms + NORM_EPS)
    return (y * gain) * (1.0 + sc) + sh


def _seg_ones(width):
    i = np.arange(width)[:, None] // HEAD_DIM
    j = np.arange(width)[None, :] // HEAD_DIM
    return jnp.asarray(i == j, dtype=BF16)


def _head_sum(x, seg, pieces=2):
    outs = []
    for c in range(x.shape[1] // LANES):
        xc = x[:, c * LANES:(c + 1) * LANES]
        if pieces == 1:
            outs.append(_dot(xc.astype(BF16), seg))
        else:
            hi, lo = _split2(xc)
            outs.append(_dot(hi, seg) + _dot(lo, seg))
    return outs[0] if len(outs) == 1 else jnp.concatenate(outs, axis=1)


def _ada_kernel(c_ref, w_ref, b_ref, o_ref):
    s = c_ref[...]
    s = s * jax.nn.sigmoid(s)
    o_ref[...] = _dot_f32(s, w_ref[...]) + b_ref[...]


def _ada(cond8, w_mod, b_mod):
    n = w_mod.shape[1]
    tn = 1536
    return pl.pallas_call(
        _ada_kernel,
        out_shape=jax.ShapeDtypeStruct((SUBLANES, n), F32),
        grid=(n // tn,),
        in_specs=[pl.BlockSpec((SUBLANES, D_MODEL), lambda j: (0, 0)),
                  pl.BlockSpec((D_MODEL, tn), lambda j: (0, j)),
                  pl.BlockSpec((1, tn), lambda j: (0, j))],
        out_specs=pl.BlockSpec((SUBLANES, tn), lambda j: (0, j)),
        compiler_params=_cparams("arbitrary"),
        name="ada_params",
    )(cond8, w_mod, b_mod.reshape(1, n))


def _swap_pairs(x):
    lane = lax.broadcasted_iota(I32, x.shape, 1)
    nxt = pltpu.roll(x, LANES - 1, 1)
    prv = pltpu.roll(x, 1, 1)
    return jnp.where(lane % 2 == 0, nxt, prv)


def _proj0_kernel(x_ref, sc_ref, sh_ref, gain_ref, w_ref, qg_ref, kg_ref, seg_ref, cos_ref, sin_ref,
                  xa_ref, ga_ref, q_ref, k_ref, v_ref, *, rope):
    h = _norm_mod(x_ref[0], gain_ref[...], sc_ref[0], sh_ref[0])
    res = _dot(h.astype(BF16), w_ref[...])
    xa_ref[0] = res[:, :LRU_WIDTH]
    ga_ref[0] = res[:, LRU_WIDTH:2 * LRU_WIDTH]
    q0 = 2 * LRU_WIDTH
    seg = seg_ref[...]

    def head_norm_rope(z, gain):
        ms = _head_sum(z * z, seg) * (1.0 / HEAD_DIM)
        zn = z * lax.rsqrt(ms + NORM_EPS) * gain
        if not rope:
            return zn
        c = cos_ref[...]
        s = sin_ref[...]
        outs = []
        for t in range(zn.shape[1] // LANES):
            zt = zn[:, t * LANES:(t + 1) * LANES]
            outs.append(zt * c + _swap_pairs(zt) * s)
        return outs[0] if len(outs) == 1 else jnp.concatenate(outs, axis=1)

    q = head_norm_rope(res[:, q0:q0 + Q_WIDTH], qg_ref[...])
    q_ref[0] = (q * Q_PRESCALE).astype(BF16)
    k = head_norm_rope(res[:, q0 + Q_WIDTH:q0 + Q_WIDTH + KV_WIDTH], kg_ref[...])
    k_ref[0] = k.astype(BF16)
    v_ref[0] = res[:, q0 + Q_WIDTH + KV_WIDTH:].astype(BF16)


def _proj0(x, sc, sh, gain, w_in, q_gain, k_gain, cos_t, sin_t, rope):
    b, t, _ = x.shape
    tm = min(512, t)
    qg = jnp.tile(q_gain, N_Q_HEADS).reshape(1, Q_WIDTH)
    kg = jnp.tile(k_gain, N_KV_HEADS).reshape(1, KV_WIDTH)
    seg = _seg_ones(LANES)
    const = lambda shape: pl.BlockSpec(shape, lambda bi, i: (0,) * len(shape))
    tok = lambda w: pl.BlockSpec((1, tm, w), lambda bi, i: (bi, i, 0))
    per_b = pl.BlockSpec((1, 1, D_MODEL), lambda bi, i: (bi, 0, 0))
    return pl.pallas_call(
        functools.partial(_proj0_kernel, rope=rope),
        out_shape=(jax.ShapeDtypeStruct((b, t, LRU_WIDTH), F32),
                   jax.ShapeDtypeStruct((b, t, LRU_WIDTH), F32),
                   jax.ShapeDtypeStruct((b, t, Q_WIDTH), BF16),
                   jax.ShapeDtypeStruct((b, t, KV_WIDTH), BF16),
                   jax.ShapeDtypeStruct((b, t, KV_WIDTH), BF16)),
        grid=(b, t // tm),
        in_specs=[tok(D_MODEL), per_b, per_b, const((1, D_MODEL)), const((D_MODEL, IN_WIDTH)),
                  const((1, Q_WIDTH)), const((1, KV_WIDTH)), const((LANES, LANES)),
                  pl.BlockSpec((tm, LANES), lambda bi, i: (i, 0)),
                  pl.BlockSpec((tm, LANES), lambda bi, i: (i, 0))],
        out_specs=(tok(LRU_WIDTH), tok(LRU_WIDTH), tok(Q_WIDTH), tok(KV_WIDTH), tok(KV_WIDTH)),
        compiler_params=_cparams("parallel", "arbitrary"),
        name="l0_in_proj",
    )(x, sc, sh, gain.reshape(1, D_MODEL), w_in, qg, kg, seg, cos_t, sin_t)


def _rope_tables(t):
    n_rows = t // GRID_W
    row = jnp.repeat(jnp.arange(n_rows, dtype=F32), GRID_W)
    col = jnp.tile(jnp.arange(GRID_W, dtype=F32), n_rows)
    axis_dim = HEAD_DIM // 2
    inv_freq = ROPE_BASE ** (-jnp.arange(0, axis_dim, 2, dtype=F32) / axis_dim)
    ang = jnp.concatenate([row[:, None] * inv_freq, col[:, None] * inv_freq], axis=-1)
    cos = jnp.repeat(jnp.cos(ang), 2, axis=-1)
    sin = jnp.repeat(jnp.sin(ang), 2, axis=-1)
    sign = jnp.tile(jnp.asarray([-1.0, 1.0], F32), HEAD_DIM // 2)
    return jnp.tile(cos, (1, 2)), jnp.tile(sin * sign, (1, 2))


def _scan_rows(a, b, h0, rev):
    n = a.shape[0]
    groups = n // SUBLANES
    a = a.reshape(groups, SUBLANES, LANES)
    b = b.reshape(groups, SUBLANES, LANES)
    row = lax.broadcasted_iota(I32, a.shape, 1)
    for d in (1, 2, 4):
        shift = SUBLANES - d if rev else d
        m = row < SUBLANES - d if rev else row >= d
        a_s = pltpu.roll(a, shift, 1)
        b_s = pltpu.roll(b, shift, 1)
        b = jnp.where(m, a * b_s + b, b)
        a = jnp.where(m, a * a_s, a)
    outs = [None] * groups
    h = h0
    for g in (range(groups - 1, -1, -1) if rev else range(groups)):
        hg = a[g] * h + b[g]
        outs[g] = hg
        h = hg[0:1] if rev else hg[SUBLANES - 1:SUBLANES]
    return jnp.concatenate(outs, axis=0), h


def _rglru_kernel(xc_ref, gc_ref, xl_ref, gl_ref, cw_ref, cb_ref, wg_ref, bg_ref, lam_ref,
                  yc_ref, yl_ref, rec_c, rec_l, *, tt_c, tt_l):
    cw = cw_ref[...]
    cb = cb_ref[...]

    def coeffs(x_ref, t0, tt, d):
        n = x_ref.shape[1]
        main = x_ref[0, pl.ds(t0, tt), :]
        prev = x_ref[0, pl.ds(pl.multiple_of(jnp.maximum(t0 - SUBLANES, 0), SUBLANES), SUBLANES), :]
        prev = jnp.where(t0 > 0, prev, 0.0)
        nxt = x_ref[0, pl.ds(pl.multiple_of(jnp.minimum(t0 + tt, n - SUBLANES), SUBLANES), SUBLANES), :]
        nxt = jnp.where(t0 + tt < n, nxt, 0.0)
        xe = jnp.concatenate([prev, main, nxt], axis=0)
        o = SUBLANES - CONV_W // 2
        xc = cb
        for j in range(CONV_W):
            xc = xc + cw[j:j + 1] * xe[o + j:o + j + tt]
        g = _dot(xc.astype(BF16), wg_ref[d, 0]) + bg_ref[d, 0]
        i_gate = jax.nn.sigmoid(g[:, :LANES])
        r_gate = jax.nn.sigmoid(g[:, LANES:])
        log_a = LRU_C * r_gate * (-_softplus(-lam_ref[d, 0]))
        a = jnp.exp(log_a)
        bco = jnp.sqrt(-jnp.tanh(log_a) * (a * a + 1.0)) * (i_gate * xc)
        return a, bco

    def sweep(x_ref, tt, d, rev, h, emit):
        nch = x_ref.shape[1] // tt

        def body(i, h):
            ci = nch - 1 - i if rev else i
            t0 = pl.multiple_of(ci * tt, tt)
            a, bco = coeffs(x_ref, t0, tt, d)
            hs, h = _scan_rows(a, bco, h, rev)
            emit(t0, tt, hs)
            return h

        return lax.fori_loop(0, nch, body, h)

    def store_rec(rec):
        def emit(t0, tt, hs):
            rec[pl.ds(t0, tt), :] = hs
        return emit

    def store_out(rec, g_ref, y_ref):
        def emit(t0, tt, hs):
            tot = rec[pl.ds(t0, tt), :] + hs
            y_ref[0, pl.ds(t0, tt), :] = (tot * _gelu_tanh(g_ref[0, pl.ds(t0, tt), :])).astype(y_ref.dtype)
        return emit

    zero = jnp.zeros((1, LANES), F32)
    h = sweep(xc_ref, tt_c, 0, False, zero, store_rec(rec_c))
    sweep(xl_ref, tt_l, 0, False, h, store_rec(rec_l))
    h = sweep(xc_ref, tt_c, 1, True, zero, store_out(rec_c, gc_ref, yc_ref))
    sweep(xl_ref, tt_l, 1, True, h, store_out(rec_l, gl_ref, yl_ref))


def _rglru(xa_c, ga_c, xa_l, ga_l, conv_w, conv_b, lru_wi, lru_bi, lru_wr, lru_br, lru_lam):
    b, lc, _ = xa_c.shape
    t = xa_l.shape[1]
    nt = LRU_WIDTH // LANES
    per_tile = LANES // LRU_BLOCK

    def block_diag(w):
        w = w.reshape(2, nt, per_tile, LRU_BLOCK, LRU_BLOCK)
        eye = jnp.eye(per_tile, dtype=w.dtype)
        return jnp.einsum('dtpij,pq->dtpiqj', w, eye).reshape(2, nt, LANES, LANES)

    wg = jnp.concatenate([block_diag(lru_wi), block_diag(lru_wr)], axis=-1).astype(BF16)
    bg = jnp.concatenate([lru_bi.reshape(2, nt, 1, LANES), lru_br.reshape(2, nt, 1, LANES)], axis=-1)
    lam = lru_lam.reshape(2, nt, 1, LANES)
    tt_c = min(256, lc)
    tt_l = min(256, t)
    seq = lambda n: pl.BlockSpec((1, n, LANES), lambda bi, j: (bi, 0, j))
    return pl.pallas_call(
        functools.partial(_rglru_kernel, tt_c=tt_c, tt_l=tt_l),
        out_shape=(jax.ShapeDtypeStruct((b, lc, LRU_WIDTH), BF16),
                   jax.ShapeDtypeStruct((b, t, LRU_WIDTH), BF16)),
        grid=(b, nt),
        in_specs=[seq(lc), seq(lc), seq(t), seq(t),
                  pl.BlockSpec((CONV_W, LANES), lambda bi, j: (0, j)),
                  pl.BlockSpec((1, LANES), lambda bi, j: (0, j)),
                  pl.BlockSpec((2, 1, LANES, 2 * LANES), lambda bi, j: (0, j, 0, 0)),
                  pl.BlockSpec((2, 1, 1, 2 * LANES), lambda bi, j: (0, j, 0, 0)),
                  pl.BlockSpec((2, 1, 1, LANES), lambda bi, j: (0, j, 0, 0))],
        out_specs=(seq(lc), seq(t)),
        scratch_shapes=[pltpu.VMEM((lc, LANES), F32), pltpu.VMEM((t, LANES), F32)],
        compiler_params=_cparams("parallel", "arbitrary"),
        name="l0_rglru",
    )(xa_c, ga_c, xa_l, ga_l, conv_w, conv_b.reshape(1, LRU_WIDTH), wg, bg, lam)


def _attn_kernel(q_ref, *refs, n_seg, kv_chunk):
    kv = refs[:2 * n_seg]
    o_ref = refs[2 * n_seg]
    g = pl.program_id(1)
    q = q_ref[0]
    tq = q.shape[0]
    lane = lax.broadcasted_iota(I32, (1, KV_WIDTH), 1)
    mine = (lane >= g * HEAD_DIM) & (lane < (g + 1) * HEAD_DIM)
    rows = []
    for h in range(Q_GROUP):
        qh = q[:, h * HEAD_DIM:(h + 1) * HEAD_DIM]
        both = jnp.concatenate([qh] * N_KV_HEADS, axis=1)
        rows.append(jnp.where(mine, both, jnp.zeros_like(both)))
    qpad = jnp.concatenate(rows, axis=0)
    nq = Q_GROUP * tq
    m = jnp.full((1, nq), -jnp.inf, F32)
    acc = jnp.zeros((kv[1].shape[2], nq), F32)
    chunks = []
    for i in range(n_seg):
        tk = kv[2 * i].shape[1]
        ck = min(kv_chunk, tk)
        chunks += [(kv[2 * i], kv[2 * i + 1], c * ck, ck) for c in range(tk // ck)]
    scores = lambda ch: _dot_nt(ch[0][0, ch[2]:ch[2] + ch[3], :], qpad)
    ready = [scores(ch) for ch in chunks[:ATTN_LOOKAHEAD]]
    for j, (_, vt_ref, c0, ck) in enumerate(chunks):
        s = ready.pop(0)
        if j + ATTN_LOOKAHEAD < len(chunks):
            ready.append(scores(chunks[j + ATTN_LOOKAHEAD]))
        m_new = jnp.maximum(m, s.max(axis=0, keepdims=True))
        alpha = jnp.exp2(m - m_new)
        p = jnp.exp2(s - m_new)
        acc = alpha * acc + _dot(vt_ref[0, 0, :, c0:c0 + ck], p.astype(BF16))
        m = m_new
    out_t = acc[:HEAD_DIM] / acc[HEAD_DIM:HEAD_DIM + 1]
    outs = [out_t[:, h * tq:(h + 1) * tq].T for h in range(Q_GROUP)]
    o_ref[0] = jnp.concatenate(outs, axis=1).astype(o_ref.dtype)


def _attention(q, segs):
    b, t, _ = q.shape
    tq = min(ATTN_Q_TILE, t)
    gw = Q_GROUP * HEAD_DIM
    in_specs = [pl.BlockSpec((1, tq, gw), lambda bi, g, i: (bi, i, g))]
    args = [q]
    for k, vt in segs:
        tk = k.shape[1]
        in_specs.append(pl.BlockSpec((1, tk, KV_WIDTH), lambda bi, g, i: (bi, 0, 0)))
        in_specs.append(pl.BlockSpec((1, 1, ATTN_VT_ROWS, tk), lambda bi, g, i: (bi, g, 0, 0)))
        args += [k, vt]
    return pl.pallas_call(
        functools.partial(_attn_kernel, n_seg=len(segs), kv_chunk=ATTN_KV_CHUNK),
        out_shape=jax.ShapeDtypeStruct((b, t, Q_WIDTH), BF16),
        grid=(b, N_KV_HEADS, t // tq),
        in_specs=in_specs,
        out_specs=pl.BlockSpec((1, tq, gw), lambda bi, g, i: (bi, i, g)),
        compiler_params=_cparams("parallel", "parallel", "arbitrary"),
        name="l0_attention",
    )(*args)


def _kv_layout(k, v):
    b, t, _ = k.shape
    vt = v.reshape(b, t, N_KV_HEADS, HEAD_DIM).transpose(0, 2, 3, 1)
    extra = jnp.zeros((b, N_KV_HEADS, ATTN_VT_ROWS - HEAD_DIM, t), v.dtype).at[:, :, 0].set(1)
    return k, jnp.concatenate([vt, extra], axis=2)


def _resmm_kernel(x_ref, g_ref, *refs, n):
    acc = None
    for i in range(n):
        d = _dot(refs[i][0], refs[n + i][...])
        acc = d if acc is None else acc + d
    o_ref = refs[2 * n]
    o_ref[0] = x_ref[0] + g_ref[0] * acc


def _residual_matmul(x, gate, acts, weights):
    b, t, _ = x.shape
    tm = min(512, t)
    n = len(acts)
    tok = lambda w: pl.BlockSpec((1, tm, w), lambda bi, i: (bi, i, 0))
    in_specs = [tok(D_MODEL), pl.BlockSpec((1, 1, D_MODEL), lambda bi, i: (bi, 0, 0))]
    in_specs += [tok(a.shape[-1]) for a in acts]
    in_specs += [pl.BlockSpec(w.shape, lambda bi, i: (0, 0)) for w in weights]
    return pl.pallas_call(
        functools.partial(_resmm_kernel, n=n),
        out_shape=jax.ShapeDtypeStruct((b, t, D_MODEL), F32),
        grid=(b, t // tm),
        in_specs=in_specs,
        out_specs=tok(D_MODEL),
        compiler_params=_cparams("parallel", "arbitrary"),
        name="residual_proj",
    )(x, gate, *acts, *weights)


def _router_kernel(x_ref, sc_ref, sh_ref, gain_ref, wr_ref, h_ref, aff_ref):
    h = _norm_mod(x_ref[0], gain_ref[...], sc_ref[0], sh_ref[0])
    tm = h.shape[0]
    logits = _dot_f32(h, wr_ref[...], 2)
    lane = lax.broadcasted_iota(I32, (tm, ROUTE_PAD), 1)
    z = jnp.where(lane < N_EXPERTS, logits, -jnp.inf)
    e = jnp.exp(z - z.max(axis=1, keepdims=True))
    aff = e / e.sum(axis=1, keepdims=True)
    h_ref[0, :, :D_MODEL] = h.astype(BF16)
    hi = aff.astype(BF16).astype(F32)
    h_ref[0, :, D_MODEL:] = (hi + pltpu.roll(aff - hi, N_EXPERTS, 1)).astype(BF16)
    aff_ref[0] = aff.T[:N_EXPERTS]


def _router(x, sc, sh, gain, w_router):
    b, t, _ = x.shape
    tm = min(512, t)
    wr = jnp.pad(w_router, ((0, 0), (0, ROUTE_PAD - N_EXPERTS)))
    tok = lambda w: pl.BlockSpec((1, tm, w), lambda bi, i: (bi, i, 0))
    per_b = pl.BlockSpec((1, 1, D_MODEL), lambda bi, i: (bi, 0, 0))
    return pl.pallas_call(
        _router_kernel,
        out_shape=(jax.ShapeDtypeStruct((b, t, D_MODEL + ROUTE_PAD), BF16),
                   jax.ShapeDtypeStruct((b, N_EXPERTS, t), F32)),
        grid=(b, t // tm),
        in_specs=[tok(D_MODEL), per_b, per_b,
                  pl.BlockSpec((1, D_MODEL), lambda bi, i: (0, 0)),
                  pl.BlockSpec((D_MODEL, ROUTE_PAD), lambda bi, i: (0, 0))],
        out_specs=(tok(D_MODEL + ROUTE_PAD), pl.BlockSpec((1, N_EXPERTS, tm), lambda bi, i: (bi, 0, i))),
        compiler_params=_cparams("parallel", "arbitrary"),
        name="moe_router",
    )(x, sc, sh, gain.reshape(1, D_MODEL), wr)


def _cumsum_lanes(x01, tri):
    outs = []
    off = jnp.zeros((x01.shape[0], 1), F32)
    for j in range(x01.shape[1] // LANES):
        cj = _dot(x01[:, j * LANES:(j + 1) * LANES].astype(BF16), tri) + off
        outs.append(cj)
        off = cj[:, LANES - 1:LANES]
    return outs[0] if len(outs) == 1 else jnp.concatenate(outs, axis=1)


def _select_kernel(aff_ref, tri_ref, tile_ref, pos_ref, tab_ref, *, cap):
    aff = aff_ref[0]
    n = aff.shape[1]
    keys = pltpu.bitcast(aff, I32)

    def bit_step(i, tau):
        cand = tau | jnp.left_shift(jnp.int32(1), 30 - i)
        cnt = jnp.sum((keys >= cand).astype(I32), axis=1, keepdims=True)
        return jnp.where(cnt >= cap, cand, tau)

    tau = lax.fori_loop(0, 31, bit_step, jnp.zeros((N_EXPERTS, 1), I32))
    gt = keys > tau
    eq = keys == tau
    need = (cap - jnp.sum(gt.astype(I32), axis=1, keepdims=True)).astype(F32)
    tri = tri_ref[...]
    c_eq = _cumsum_lanes(eq.astype(F32), tri)
    sel = gt | (eq & (c_eq <= need))
    c_sel = _cumsum_lanes(sel.astype(F32), tri)
    pos_ref[0] = jnp.where(sel, c_sel - 1.0, -1.0)
    tab_ref[0] = _dot(sel.astype(BF16), tile_ref[...]).astype(I32)


def _select(aff_t, cap):
    b, _, n = aff_t.shape
    assert n % MOE_TILE == 0 and n // MOE_TILE <= LANES // 2
    tri = jnp.asarray(np.triu(np.ones((LANES, LANES), np.float32)), dtype=BF16)
    tok = np.arange(n)[:, None]
    j = np.arange(LANES // 2)[None, :]
    tile_tab = jnp.asarray(np.concatenate([tok < j * MOE_TILE, tok // MOE_TILE == j], axis=1), dtype=BF16)
    return pl.pallas_call(
        functools.partial(_select_kernel, cap=cap),
        out_shape=(jax.ShapeDtypeStruct((b, N_EXPERTS, n), F32),
                   jax.ShapeDtypeStruct((b, N_EXPERTS, LANES), I32)),
        grid=(b,),
        in_specs=[pl.BlockSpec((1, N_EXPERTS, n), lambda bi: (bi, 0, 0)),
                  pl.BlockSpec((LANES, LANES), lambda bi: (0, 0)),
                  pl.BlockSpec((n, LANES), lambda bi: (0, 0))],
        out_specs=(pl.BlockSpec((1, N_EXPERTS, n), lambda bi: (bi, 0, 0)),
                   pl.BlockSpec((1, N_EXPERTS, LANES), lambda bi: (bi, 0, 0))),
        compiler_params=_cparams("parallel"),
        name="moe_select",
    )(aff_t, tri, tile_tab)


def _tile_windows(tab_ref, pos_ref, j, win):
    slot = lax.broadcasted_iota(I32, (win, MOE_TILE), 0).astype(F32)
    out = []
    for e in range(N_EXPERTS):
        start = tab_ref[0, e, j]
        count = tab_ref[0, e, LANES // 2 + j]
        a0 = pl.multiple_of(lax.shift_left(lax.shift_right_logical(start, MOE_ALIGN_LOG2), MOE_ALIGN_LOG2),
                            1 << MOE_ALIGN_LOG2)
        p = pos_ref[0, e:e + 1, :]
        onehot = lambda first, p=p: (p == slot + jnp.asarray(first, F32)).astype(BF16)
        extra = jnp.maximum(start - a0 + count - 1, 0) // win
        out.append((a0, onehot, extra))
    return out


def _moe_gather_kernel(tab_ref, h_ref, pos_ref, o_ref, *, win):
    j = pl.program_id(2)

    @pl.when(j == 0)
    def _():
        o_ref[...] = jnp.zeros(o_ref.shape, o_ref.dtype)

    h = h_ref[0]
    wins = _tile_windows(tab_ref, pos_ref, j, win)
    x = _dot(jnp.concatenate([oh(a0) for a0, oh, _ in wins], axis=0), h).astype(BF16)
    for e, (a0, oh, extra) in enumerate(wins):
        o_ref[0, e, pl.ds(a0, win), :] = o_ref[0, e, pl.ds(a0, win), :] + x[e * win:(e + 1) * win]

        def more(k, c, e=e, a0=a0, oh=oh):
            ak = pl.multiple_of(a0 + (k + 1) * win, 1 << MOE_ALIGN_LOG2)
            o_ref[0, e, pl.ds(ak, win), :] = o_ref[0, e, pl.ds(ak, win), :] + _dot(oh(ak), h).astype(BF16)
            return c

        lax.fori_loop(0, extra, more, 0)


def _moe_gather(h_ext, pos, tab, cap):
    b, n, width = h_ext.shape
    cols = width // MOE_COL_SPLIT
    rows = cap + MOE_SLACK
    return pl.pallas_call(
        functools.partial(_moe_gather_kernel, win=MOE_WIN),
        out_shape=jax.ShapeDtypeStruct((b, N_EXPERTS, rows, width), BF16),
        grid=(b, MOE_COL_SPLIT, n // MOE_TILE),
        in_specs=[pl.BlockSpec((1, N_EXPERTS, LANES), lambda bi, c, j: (bi, 0, 0), memory_space=pltpu.SMEM),
                  pl.BlockSpec((1, MOE_TILE, cols), lambda bi, c, j: (bi, j, c)),
                  pl.BlockSpec((1, N_EXPERTS, MOE_TILE), lambda bi, c, j: (bi, 0, j))],
        out_specs=pl.BlockSpec((1, N_EXPERTS, rows, cols), lambda bi, c, j: (bi, 0, 0, c)),
        compiler_params=_cparams("parallel", "parallel", "arbitrary"),
        name="moe_gather",
    )(tab, h_ext, pos)


def _moe_ffn_kernel(x_ref, wg_ref, wu_ref, wd_ref, y_ref, acc_s, *, cap):
    e = pl.program_id(0)
    f = pl.program_id(2)
    nb = x_ref.shape[0]
    x = x_ref[:, 0].reshape(nb * cap, x_ref.shape[3])
    xb = x[:, :D_MODEL]

    @pl.when(f == 0)
    def _():
        acc_s[...] = jnp.zeros(acc_s.shape, F32)

    g = _dot(xb, wg_ref[0].astype(BF16))
    u = _dot(xb, wu_ref[0].astype(BF16))
    hid = (g * jax.nn.sigmoid(g)) * u
    acc_s[...] += _dot(hid.astype(BF16), wd_ref[0].astype(BF16))

    @pl.when(f == pl.num_programs(2) - 1)
    def _():
        lane = lax.broadcasted_iota(I32, (nb * cap, ROUTE_PAD), 1)
        mine = (lane == e) | (lane == e + N_EXPERTS)
        gate = jnp.sum(jnp.where(mine, x[:, D_MODEL:].astype(F32), 0.0), axis=1, keepdims=True)
        y_ref[:, 0, :cap] = (acc_s[...] * gate).astype(y_ref.dtype).reshape(nb, cap, D_MODEL)
        slack = y_ref.shape[2] - cap
        y_ref[:, 0, cap:] = jnp.zeros((nb, slack, D_MODEL), y_ref.dtype)


def _moe_ffn(xe, wg, wu, wd, cap):
    b, _, rows, width = xe.shape
    nb = b if b * cap <= MOE_FFN_ROWS else 1
    fc = MOE_FFN_SLICE
    return pl.pallas_call(
        functools.partial(_moe_ffn_kernel, cap=cap),
        out_shape=jax.ShapeDtypeStruct((b, N_EXPERTS, rows, D_MODEL), BF16),
        grid=(N_EXPERTS, b // nb, EXPERT_FF // fc),
        in_specs=[pl.BlockSpec((nb, 1, cap, width), lambda e, bi, f: (bi, e, 0, 0)),
                  pl.BlockSpec((1, D_MODEL, fc), lambda e, bi, f: (e, 0, f)),
                  pl.BlockSpec((1, D_MODEL, fc), lambda e, bi, f: (e, 0, f)),
                  pl.BlockSpec((1, fc, D_MODEL), lambda e, bi, f: (e, f, 0))],
        out_specs=pl.BlockSpec((nb, 1, rows, D_MODEL), lambda e, bi, f: (bi, e, 0, 0)),
        scratch_shapes=[pltpu.VMEM((nb * cap, D_MODEL), F32)],
        compiler_params=_cparams("arbitrary", "arbitrary", "arbitrary"),
        name="moe_experts",
    )(xe, wg, wu, wd)


def _moe_combine_kernel(tab_ref, x_ref, gt_ref, pos_ref, y_ref, o_ref, acc_s, *, win):
    j = pl.program_id(2)
    wins = _tile_windows(tab_ref, pos_ref, j, win)
    onehots = jnp.concatenate([oh(a0) for a0, oh, _ in wins], axis=0)
    rows = jnp.concatenate([y_ref[0, e, pl.ds(a0, win), :] for e, (a0, _, _) in enumerate(wins)], axis=0)
    acc_s[...] = _dot_tn(onehots, rows)
    for e, (a0, oh, extra) in enumerate(wins):
        def more(k, c, e=e, a0=a0, oh=oh):
            ak = pl.multiple_of(a0 + (k + 1) * win, 1 << MOE_ALIGN_LOG2)
            acc_s[...] += _dot_tn(oh(ak), y_ref[0, e, pl.ds(ak, win), :])
            return c

        lax.fori_loop(0, extra, more, 0)
    o_ref[0] = x_ref[0] + gt_ref[0] * acc_s[...]


def _moe_combine(x, gt, pos, tab, y):
    b, n, _ = x.shape
    rows = y.shape[2]
    cols = D_MODEL // 2
    tile = lambda: pl.BlockSpec((1, MOE_TILE, cols), lambda bi, c, j: (bi, j, c))
    return pl.pallas_call(
        functools.partial(_moe_combine_kernel, win=MOE_WIN),
        out_shape=jax.ShapeDtypeStruct((b, n, D_MODEL), F32),
        grid=(b, 2, n // MOE_TILE),
        in_specs=[pl.BlockSpec((1, N_EXPERTS, LANES), lambda bi, c, j: (bi, 0, 0), memory_space=pltpu.SMEM),
                  tile(),
                  pl.BlockSpec((1, 1, cols), lambda bi, c, j: (bi, 0, c)),
                  pl.BlockSpec((1, N_EXPERTS, MOE_TILE), lambda bi, c, j: (bi, 0, j)),
                  pl.BlockSpec((1, N_EXPERTS, rows, cols), lambda bi, c, j: (bi, 0, 0, c))],
        out_specs=tile(),
        scratch_shapes=[pltpu.VMEM((MOE_TILE, cols), F32)],
        compiler_params=_cparams("parallel", "parallel", "arbitrary"),
        name="moe_combine",
    )(tab, x, gt, pos, y)


def _moe(x, sc, sh, gt, gain, w_router, wg, wu, wd):
    n = x.shape[1]
    cap = max(1, CAPACITY_FACTOR * n // N_EXPERTS)
    h_ext, aff_t = _router(x, sc, sh, gain, w_router)
    pos, tab = _select(aff_t, cap)
    y = _moe_ffn(_moe_gather(h_ext, pos, tab, cap), wg, wu, wd, cap)
    return _moe_combine(x, gt, pos, tab, y)


def _rwkv_prep_kernel(xm_ref, xp_ref, xn_ref, sc_ref, sh_ref, gain_ref, mu_ref, wr_ref, wk_ref, wv_ref,
                      w1_ref, w2_ref, a1_ref, a2_ref, g1_ref, g2_ref, w0_ref, a0_ref, kks_ref, ka_ref, rk_ref,
                      seg_ref, r_ref, v_ref, g_ref, kk_ref, bonus_ref, lw0_ref, lw1_ref, kd0_ref, kd1_ref,
                      ag0_ref, ag1_ref):
    i = pl.program_id(1)
    last = pl.num_programs(1) - 1
    tm = xm_ref.shape[1]
    gain = gain_ref[...]
    sc = sc_ref[0]
    sh = sh_ref[0]
    xe = jnp.concatenate([xp_ref[0], xm_ref[0], xn_ref[0]], axis=0)
    he = _norm_mod(xe, gain, sc, sh)
    row = lax.broadcasted_iota(I32, (tm, 1), 0)
    h = he[SUBLANES:SUBLANES + tm]
    hm1 = jnp.where((row == 0) & (i == 0), 0.0, he[SUBLANES - 1:SUBLANES - 1 + tm])
    hp1 = jnp.where((row == tm - 1) & (i == last), 0.0, he[SUBLANES + 1:SUBLANES + 1 + tm])
    xx = 0.5 * (hm1 + hp1) - h
    mu = mu_ref[...]
    mix = lambda j: (h + xx * mu[j:j + 1]).astype(BF16)
    r = _dot(mix(0), wr_ref[...])
    k = _dot(mix(2), wk_ref[...])
    v = _dot(mix(3), wv_ref[...])
    tw = jnp.tanh(_dot(mix(1), w1_ref[...])).astype(BF16)
    la = _dot(mix(4), a1_ref[...]).astype(BF16)
    g = _dot(jax.nn.sigmoid(_dot(mix(5), g1_ref[...])).astype(BF16), g2_ref[...])
    seg = seg_ref[...]
    kkv = k * kks_ref[...]
    kk = kkv * lax.rsqrt(jnp.maximum(_head_sum(kkv * kkv, seg, 1), 1e-24))
    r_ref[0] = r.astype(r_ref.dtype)
    v_ref[0] = v.astype(v_ref.dtype)
    g_ref[0] = g.astype(g_ref.dtype)
    kk_ref[0] = kk.astype(kk_ref.dtype)
    bonus = None
    for d, (lw_ref, kd_ref, ag_ref) in enumerate(((lw0_ref, kd0_ref, ag0_ref), (lw1_ref, kd1_ref, ag1_ref))):
        w_pre = w0_ref[d] + _dot(tw, w2_ref[d])
        lw_ref[0] = jax.nn.sigmoid(w_pre) * np.float32(-np.exp(-0.5))
        a = jax.nn.sigmoid(a0_ref[d] + _dot(la, a2_ref[d]))
        kd = k * (1.0 + (a - 1.0) * ka_ref[...])
        kd_ref[0] = kd.astype(kd_ref.dtype)
        ag_ref[0] = a.astype(ag_ref.dtype)
        bd = _head_sum(r * kd * rk_ref[...], seg, 1) * v
        bonus = bd if bonus is None else bonus + bd
    bonus_ref[0] = bonus


def _rwkv_prep(x, sc, sh, gain, p):
    b, t, _ = x.shape
    tm = min(256, t)
    nb8 = tm // SUBLANES
    tok = lambda: pl.BlockSpec((1, tm, D_MODEL), lambda bi, i: (bi, i, 0))
    prev = pl.BlockSpec((1, SUBLANES, D_MODEL), lambda bi, i: (bi, jnp.maximum(i * nb8 - 1, 0), 0))
    nxt = pl.BlockSpec((1, SUBLANES, D_MODEL),
                       lambda bi, i: (bi, jnp.minimum((i + 1) * nb8, t // SUBLANES - 1), 0))
    per_b = pl.BlockSpec((1, 1, D_MODEL), lambda bi, i: (bi, 0, 0))
    const = lambda a: pl.BlockSpec(a.shape, lambda bi, i: (0,) * a.ndim)
    consts = [p['gain'], p['mu'], p['w_r'], p['w_k'], p['w_v'], p['w1'], p['w2'], p['a1'], p['a2'], p['g1'],
              p['g2'], p['w0'], p['a0'], p['kks'], p['ka'], p['rk'], p['seg']]
    bf = jax.ShapeDtypeStruct((b, t, D_MODEL), BF16)
    f32 = jax.ShapeDtypeStruct((b, t, D_MODEL), F32)
    return pl.pallas_call(
        _rwkv_prep_kernel,
        out_shape=(bf, bf, bf, bf, f32, f32, f32, bf, bf, bf, bf),
        grid=(b, t // tm),
        in_specs=[tok(), prev, nxt, per_b, per_b] + [const(a) for a in consts],
        out_specs=tuple(tok() for _ in range(11)),
        compiler_params=_cparams("parallel", "arbitrary"),
        name="l1_rwkv_prep",
    )(x, x, x, sc, sh, *consts)


def _rwkv_params(gain, mu, w_r, w_k, w_v, w0, w1, w2, a0, a1, a2, g1, g2, k_k, k_a, r_k):
    def pad_dir(w):
        z = jnp.zeros_like(w[0])
        return jnp.stack([jnp.concatenate([w[0], z], axis=0), jnp.concatenate([z, w[1]], axis=0)])

    row = lambda a: a.reshape(1, D_MODEL)
    return dict(
        gain=row(gain), mu=mu, w_r=w_r.astype(BF16), w_k=w_k.astype(BF16), w_v=w_v.astype(BF16),
        w1=jnp.concatenate([w1[0], w1[1]], axis=1).astype(BF16), w2=pad_dir(w2).astype(BF16),
        a1=jnp.concatenate([a1[0], a1[1]], axis=1).astype(BF16), a2=pad_dir(a2).astype(BF16),
        g1=jnp.pad(g1, ((0, 0), (0, GATE_LORA_PAD - GATE_LORA))).astype(BF16),
        g2=jnp.pad(g2, ((0, GATE_LORA_PAD - GATE_LORA), (0, 0))).astype(BF16),
        w0=w0.reshape(2, 1, D_MODEL), a0=a0.reshape(2, 1, D_MODEL),
        kks=row(k_k), ka=row(k_a), rk=row(r_k), seg=_seg_ones(LANES))


SOLVE_BASE = 8
WKV_SUB_CHUNKS = 4


def _solve_masks(n, top):
    rowi = lax.broadcasted_iota(I32, (n, n), 0)
    coli = lax.broadcasted_iota(I32, (n, n), 1)
    same = lambda shift: (rowi >> shift) == (coli >> shift)
    k = int(np.log2(SOLVE_BASE))
    levels = range(k, int(np.log2(top)))
    return (jnp.where(rowi == coli, 1.0, 0.0), same(k),
            [same(j + 1) & jnp.logical_not(same(j)) for j in levels])


def _solve_unit_tri(nmat, x, masks):
    eye, base, joins = masks
    m = jnp.where(base, nmat, 0.0).astype(BF16)
    tinv = eye + m.astype(F32)
    for _ in range(int(np.log2(SOLVE_BASE)) - 1):
        m = _dot(m, m).astype(BF16)
        yield None
        tinv = tinv + _dot(m, tinv.astype(BF16))
        yield None
    for join in joins:
        off = jnp.where(join, nmat, 0.0).astype(BF16)
        tb = tinv.astype(BF16)
        half = _dot(tb, off).astype(BF16)
        yield None
        tinv = tinv + _dot(half, tb)
        yield None
    yield _dot(tinv.astype(BF16), x.astype(BF16))


def _wkv_stage(r_ref, v_ref, kk_ref, lw_ref, kd_ref, ag_ref, tri, ws, zs, vs, gts, d, rev, chunk, rows, k):
    lw = lw_ref[0, rows]
    p3 = _split3(lw)
    cum = _dot(tri, p3[0]) + _dot(tri, p3[1]) + _dot(tri, p3[2])
    g_in = jnp.exp(cum)
    g_ex = jnp.exp(cum - lw)
    g_inv = jnp.exp(-cum)
    kk = kk_ref[0, rows].astype(F32)
    a_t = -(kk * g_ex)
    r_t = r_ref[0, rows].astype(F32) * g_in
    b_t = kk * ag_ref[0, rows].astype(F32) * g_inv
    k_t = kd_ref[0, rows].astype(F32) * g_inv
    g_tot = jnp.exp(cum[0:1] if rev else cum[chunk - 1:chunk])
    first = (lax.broadcasted_iota(I32, (1, D_MODEL), 1) % LANES) < RWKV_HEAD
    stacked = lambda x: (jnp.where(first, x, jnp.zeros_like(x)), jnp.where(first, jnp.zeros_like(x), x))
    a2, r2, b2, k2 = (stacked(x.astype(BF16)) for x in (a_t, r_t, b_t, k_t))
    v2 = stacked(v_ref[0, rows])
    c2 = 2 * chunk
    for p in range(HEAD_PAIRS):
        sl = slice(p * LANES, (p + 1) * LANES)
        for h in range(2):
            ws[k, d, p, h * chunk:(h + 1) * chunk] = a2[h][:, sl]
            ws[k, d, p, c2 + h * chunk:c2 + (h + 1) * chunk] = r2[h][:, sl]
            zs[k, d, p, h * chunk:(h + 1) * chunk] = b2[h][:, sl]
            zs[k, d, p, c2 + h * chunk:c2 + (h + 1) * chunk] = k2[h][:, sl]
            vs[k, d, p, h * chunk:(h + 1) * chunk] = v2[h][:, sl]
        gts[k, d, p] = g_tot[:, sl]


def _wkv_masks(chunk, rev):
    n = 2 * chunk
    ti = lax.broadcasted_iota(I32, (n, 2 * n), 0) & (chunk - 1)
    tj = lax.broadcasted_iota(I32, (n, 2 * n), 1) & (chunk - 1)
    strict = ti < tj if rev else ti > tj
    incl = ti <= tj if rev else ti >= tj
    return strict, incl, _solve_masks(n, chunk)


def _wkv_chain(s_ref, y_ref, ws, zs, vs, gts, d, p, masks, chunk, rows, k):
    n = 2 * chunk
    w = ws[k, d, p]
    z = zs[k, d, p]
    v = vs[k, d, p]
    s_old = s_ref[d, p]
    pm = _dot_nt(w, z)
    yield
    wst = _dot_nt(w, s_old.astype(BF16))
    yield
    strict, incl, solve_masks = masks
    top = jnp.where(strict, pm[:n], 0.0)
    nmat = top[:, :n]
    ak = top[:, n:].astype(BF16)
    rbk = jnp.where(incl, pm[n:], 0.0).astype(BF16)
    x = wst[:n] + _dot(ak, v)
    yield
    u = None
    for u in _solve_unit_tri(nmat, x, solve_masks):
        yield
    uv = jnp.concatenate([u.astype(BF16), v], axis=0)
    y = wst[n:] + _dot(rbk, uv)
    yield
    if y_ref is not None:
        y_ref[0, rows, p * LANES:(p + 1) * LANES] = y[:chunk] + y[chunk:]
    ds = _dot_tn(uv, z)
    yield
    s_ref[d, p] = (s_old + ds) * gts[k, d, p]


def _wkv_kernel(*refs, chunk, emit_y, has_init):
    ins = refs[:12]
    tri_ref = refs[12]
    pos = 13
    if has_init:
        s0_ref = refs[pos]
        pos += 1
    yf_ref = yb_ref = None
    if emit_y:
        yf_ref, yb_ref = refs[pos:pos + 2]
        pos += 2
    sfin_ref = refs[pos]
    s_ref, ws, zs, vs, gts = refs[pos + 1:]
    i = pl.program_id(1)

    @pl.when(i == 0)
    def _():
        if has_init:
            s_ref[...] = s0_ref[0]
        else:
            s_ref[...] = jnp.zeros(s_ref.shape, F32)

    n_sub = ins[0].shape[1] // chunk

    def rows_of(d, k):
        j = n_sub - 1 - k if d else k
        return slice(j * chunk, (j + 1) * chunk)

    def stage(k):
        for d, rev in ((0, False), (1, True)):
            _wkv_stage(*ins[6 * d:6 * d + 6], tri_ref[d], ws, zs, vs, gts, d, rev, chunk, rows_of(d, k), k)

    masks = [_wkv_masks(chunk, False), _wkv_masks(chunk, True)]
    stage(0)
    for k in range(n_sub):
        chains = [_wkv_chain(s_ref, (yf_ref, yb_ref)[d], ws, zs, vs, gts, d, p, masks[d], chunk, rows_of(d, k), k)
                  for p in range(HEAD_PAIRS) for d in range(2)]
        first_round = True
        while chains:
            alive = []
            for ch in chains:
                try:
                    next(ch)
                    alive.append(ch)
                except StopIteration:
                    pass
            chains = alive
            if first_round and k + 1 < n_sub:
                stage(k + 1)
            first_round = False

    @pl.when(i == pl.num_programs(1) - 1)
    def _():
        sfin_ref[0] = s_ref[...]


def _wkv(r, v, kk, lw, kd, ag, s0, emit_y, chunk=64):
    b, t, _ = r.shape
    n_sub = min(WKV_SUB_CHUNKS, t // chunk)
    n = t // (n_sub * chunk)
    fwd = pl.BlockSpec((1, n_sub * chunk, D_MODEL), lambda bi, i: (bi, i, 0))
    bwd = pl.BlockSpec((1, n_sub * chunk, D_MODEL), lambda bi, i: (bi, n - 1 - i, 0))
    tri = jnp.asarray(np.stack([np.tril(np.ones((chunk, chunk), np.float32)),
                                np.triu(np.ones((chunk, chunk), np.float32))]), dtype=BF16)
    state_shape = (2, HEAD_PAIRS, 2 * RWKV_HEAD, LANES)
    state_spec = pl.BlockSpec((1,) + state_shape, lambda bi, i: (bi, 0, 0, 0, 0))
    args = [r, v, kk, lw[0], kd[0], ag[0], r, v, kk, lw[1], kd[1], ag[1], tri]
    in_specs = [fwd] * 6 + [bwd] * 6 + [pl.BlockSpec((2, chunk, chunk), lambda bi, i: (0, 0, 0))]
    if s0 is not None:
        args.append(s0)
        in_specs.append(state_spec)
    out_shape = [jax.ShapeDtypeStruct((b,) + state_shape, F32)]
    out_specs = [state_spec]
    if emit_y:
        out_shape = [jax.ShapeDtypeStruct((b, t, D_MODEL), F32)] * 2 + out_shape
        out_specs = [fwd, bwd] + out_specs
    res = pl.pallas_call(
        functools.partial(_wkv_kernel, chunk=chunk, emit_y=emit_y, has_init=s0 is not None),
        out_shape=tuple(out_shape),
        grid=(b, n),
        in_specs=in_specs,
        out_specs=tuple(out_specs),
        scratch_shapes=[pltpu.VMEM(state_shape, F32),
                        pltpu.VMEM((n_sub, 2, HEAD_PAIRS, 4 * chunk, LANES), BF16),
                        pltpu.VMEM((n_sub, 2, HEAD_PAIRS, 4 * chunk, LANES), BF16),
                        pltpu.VMEM((n_sub, 2, HEAD_PAIRS, 2 * chunk, LANES), BF16),
                        pltpu.VMEM((n_sub, 2, HEAD_PAIRS, 1, LANES), F32)],
        compiler_params=_cparams("parallel", "arbitrary"),
        name="l1_wkv_scan",
    )(*args)
    return res


def _readout_kernel(x_ref, gt_ref, yf_ref, yb_ref, bonus_ref, g_ref, gng_ref, gnb_ref, seg_ref, wo_ref, o_ref):
    seg = seg_ref[...]
    y = yf_ref[0] + yb_ref[0]
    mean = _head_sum(y, seg) * (1.0 / RWKV_HEAD)
    c = y - mean
    var = _head_sum(c * c, seg, 1) * (1.0 / RWKV_HEAD)
    yn = c * lax.rsqrt(var + GN_EPS) * gng_ref[...] + gnb_ref[...]
    out = (yn + bonus_ref[0]) * g_ref[0].astype(F32)
    o_ref[0] = x_ref[0] + gt_ref[0] * _dot(out.astype(BF16), wo_ref[...])


def _readout(x, gt, y_f, y_b, bonus, g, gn_g, gn_b, w_o):
    b, t, _ = x.shape
    tm = min(256, t)
    tok = pl.BlockSpec((1, tm, D_MODEL), lambda bi, i: (bi, i, 0))
    row = pl.BlockSpec((1, D_MODEL), lambda bi, i: (0, 0))
    return pl.pallas_call(
        _readout_kernel,
        out_shape=jax.ShapeDtypeStruct((b, t, D_MODEL), F32),
        grid=(b, t // tm),
        in_specs=[tok, pl.BlockSpec((1, 1, D_MODEL), lambda bi, i: (bi, 0, 0)), tok, tok, tok, tok, row, row,
                  pl.BlockSpec((LANES, LANES), lambda bi, i: (0, 0)),
                  pl.BlockSpec((D_MODEL, D_MODEL), lambda bi, i: (0, 0))],
        out_specs=tok,
        compiler_params=_cparams("parallel", "arbitrary"),
        name="l1_readout",
    )(x, gt, y_f, y_b, bonus, g, gn_g.reshape(1, D_MODEL), gn_b.reshape(1, D_MODEL), _seg_ones(LANES), w_o)


def _modulation(c, c_ctx, w_mod, b_mod):
    b = c.shape[0]
    cond = jnp.zeros((SUBLANES, D_MODEL), F32).at[:b].set(c).at[b].set(c_ctx)
    m = _ada(cond, w_mod, b_mod)
    lat = [m[:b, j * D_MODEL:(j + 1) * D_MODEL].reshape(b, 1, D_MODEL) for j in range(6)]
    ctx = [jnp.broadcast_to(m[b, j * D_MODEL:(j + 1) * D_MODEL].reshape(1, 1, D_MODEL), (b, 1, D_MODEL))
           for j in range(6)]
    return lat, ctx


def kernel(x, c, ctx, c_ctx, l0_w_mod, l0_b_mod, l0_norm_mix, l0_norm_ffn, l0_w_in, l0_conv_w, l0_conv_b, l0_lru_wi, l0_lru_bi, l0_lru_wr, l0_lru_br, l0_lru_lam, l0_q_gain, l0_k_gain, l0_w_out, l0_router, l0_we_gate, l0_we_up, l0_we_down, l1_w_mod, l1_b_mod, l1_norm_mix, l1_norm_ffn, l1_mu, l1_w_r, l1_w_k, l1_w_v, l1_w0, l1_w1, l1_w2, l1_a0, l1_a1, l1_a2, l1_g1, l1_g2, l1_k_k, l1_k_a, l1_r_k, l1_gn_g, l1_gn_b, l1_w_o, l1_router, l1_we_gate, l1_we_up, l1_we_down):
    t = x.shape[1]

    (sh1, sc1, gt1, sh2, sc2, gt2), (csh1, csc1, cgt1, csh2, csc2, cgt2) = _modulation(c, c_ctx, l0_w_mod, l0_b_mod)
    w_in = l0_w_in.astype(BF16)
    cos_t, sin_t = _rope_tables(t)
    xa_l, ga_l, q_l, k_l, v_l = _proj0(x, sc1, sh1, l0_norm_mix, w_in, l0_q_gain, l0_k_gain, cos_t, sin_t, True)
    lc = ctx.shape[1]
    xa_c, ga_c, q_c, k_c, v_c = _proj0(ctx, csc1, csh1, l0_norm_mix, w_in, l0_q_gain, l0_k_gain,
                                       cos_t[:lc], sin_t[:lc], False)
    ya_c, ya_l = _rglru(xa_c, ga_c, xa_l, ga_l, l0_conv_w, l0_conv_b, l0_lru_wi, l0_lru_bi, l0_lru_wr,
                        l0_lru_br, l0_lru_lam)
    seg_l = _kv_layout(k_l, v_l)
    seg_c = _kv_layout(k_c, v_c)
    yb_l = _attention(q_l, [seg_l, seg_c])
    yb_c = _attention(q_c, [seg_c])
    w_out = l0_w_out.astype(BF16)
    w_oa, w_ob = w_out[:LRU_WIDTH], w_out[LRU_WIDTH:]
    x = _residual_matmul(x, gt1, [ya_l, yb_l], [w_oa, w_ob])
    ctx = _residual_matmul(ctx, cgt1, [ya_c, yb_c], [w_oa, w_ob])
    x = _moe(x, sc2, sh2, gt2, l0_norm_ffn, l0_router, l0_we_gate, l0_we_up, l0_we_down)
    ctx = _moe(ctx, csc2, csh2, cgt2, l0_norm_ffn, l0_router, l0_we_gate, l0_we_up, l0_we_down)

    (sh1, sc1, gt1, sh2, sc2, gt2), (csh1, csc1, _, _, _, _) = _modulation(c, c_ctx, l1_w_mod, l1_b_mod)
    p = _rwkv_params(l1_norm_mix, l1_mu, l1_w_r, l1_w_k, l1_w_v, l1_w0, l1_w1, l1_w2, l1_a0, l1_a1, l1_a2,
                     l1_g1, l1_g2, l1_k_k, l1_k_a, l1_r_k)
    r_c, v_c, _, kk_c, _, lw0_c, lw1_c, kd0_c, kd1_c, ag0_c, ag1_c = _rwkv_prep(ctx, csc1, csh1, l1_norm_mix, p)
    r_l, v_l, g_l, kk_l, bonus_l, lw0, lw1, kd0, kd1, ag0, ag1 = _rwkv_prep(x, sc1, sh1, l1_norm_mix, p)
    (s_ctx,) = _wkv(r_c, v_c, kk_c, (lw0_c, lw1_c), (kd0_c, kd1_c), (ag0_c, ag1_c), None, False)
    y_f, y_b, _ = _wkv(r_l, v_l, kk_l, (lw0, lw1), (kd0, kd1), (ag0, ag1), s_ctx, True)
    x = _readout(x, gt1, y_f, y_b, bonus_l, g_l, l1_gn_g, l1_gn_b, l1_w_o.astype(BF16))
    x = _moe(x, sc2, sh2, gt2, l1_norm_ffn, l1_router, l1_we_gate, l1_we_up, l1_we_down)
    return x
```

```python
import functools

import jax
import jax.numpy as jnp
import numpy as np
from jax import lax
from jax.experimental import pallas as pl
from jax.experimental.pallas import tpu as pltpu

F32 = jnp.float32
BF16 = jnp.bfloat16
I32 = jnp.int32

D_MODEL = 1024
LANES = 128
SUBLANES = 8
GRID_W = 64
NORM_EPS = 1e-6
LRU_WIDTH = 512
LRU_BLOCKS = 8
LRU_BLOCK = LRU_WIDTH // LRU_BLOCKS
LRU_C = 8.0
CONV_W = 4
N_Q_HEADS = 8
N_KV_HEADS = 2
HEAD_DIM = 64
Q_GROUP = N_Q_HEADS // N_KV_HEADS
ROPE_BASE = 10000.0
ATTN_SCALE = HEAD_DIM ** -0.5
Q_PRESCALE = ATTN_SCALE * float(np.log2(np.e))
ATTN_KV_CHUNK = 1024
ATTN_Q_TILE = 128
ATTN_LOOKAHEAD = 2
ATTN_VT_ROWS = HEAD_DIM + 16
Q_WIDTH = N_Q_HEADS * HEAD_DIM
KV_WIDTH = N_KV_HEADS * HEAD_DIM
IN_WIDTH = 2 * LRU_WIDTH + Q_WIDTH + 2 * KV_WIDTH
RWKV_HEAD = 64
RWKV_HEADS = D_MODEL // RWKV_HEAD
HEAD_PAIRS = D_MODEL // LANES
DECAY_LORA = 64
AAA_LORA = 64
GATE_LORA = 160
GATE_LORA_PAD = 256
GN_EPS = 64e-5
N_EXPERTS = 16
EXPERT_FF = 2048
CAPACITY_FACTOR = 2
ROUTE_PAD = LANES
VMEM_LIMIT = 60 * 1024 * 1024
MOE_TILE = 256
MOE_ALIGN_LOG2 = 4
MOE_WIN = 48 + (1 << MOE_ALIGN_LOG2)
MOE_SLACK = 128
MOE_COL_SPLIT = 3
MOE_FFN_ROWS = 1024
MOE_FFN_SLICE = 1024


def _cparams(*sem):
    return pltpu.CompilerParams(dimension_semantics=sem, vmem_limit_bytes=VMEM_LIMIT)


def _dot(a, b):
    return jnp.dot(a, b, preferred_element_type=F32)


def _dot_nt(a, b):
    return lax.dot_general(a, b, (((1,), (1,)), ((), ())), preferred_element_type=F32)


def _dot_tn(a, b):
    return lax.dot_general(a, b, (((0,), (0,)), ((), ())), preferred_element_type=F32)


def _split2(x):
    hi = x.astype(BF16)
    lo = (x - hi.astype(F32)).astype(BF16)
    return hi, lo


def _split3(x):
    hi = x.astype(BF16)
    r = x - hi.astype(F32)
    mid = r.astype(BF16)
    lo = (r - mid.astype(F32)).astype(BF16)
    return hi, mid, lo


def _dot_f32(a, b, pieces=3):
    split = _split3 if pieces == 3 else _split2
    ap = split(a)
    bp = split(b)
    out = None
    for i in range(pieces):
        for j in range(pieces - i):
            d = _dot(ap[i], bp[j])
            out = d if out is None else out + d
    return out


def _softplus(x):
    return jnp.maximum(x, 0.0) + jnp.log1p(jnp.exp(-jnp.abs(x)))


def _gelu_tanh(x):
    c = np.float32(np.sqrt(2.0 / np.pi))
    return 0.5 * x * (1.0 + jnp.tanh(c * (x + 0.044715 * (x * x * x))))


def _norm_mod(x, gain, sc, sh):
    ms = jnp.mean(x * x, axis=-1, keepdims=True)
    y = x * lax.rsqrt(ms + NORM_EPS)
    return (y * gain) * (1.0 + sc) + sh


def _seg_ones(width):
    i = np.arange(width)[:, None] // HEAD_DIM
    j = np.arange(width)[None, :] // HEAD_DIM
    return jnp.asarray(i == j, dtype=BF16)


def _head_sum(x, seg, pieces=2):
    outs = []
    for c in range(x.shape[1] // LANES):
        xc = x[:, c * LANES:(c + 1) * LANES]
        if pieces == 1:
            outs.append(_dot(xc.astype(BF16), seg))
        else:
            hi, lo = _split2(xc)
            outs.append(_dot(hi, seg) + _dot(lo, seg))
    return outs[0] if len(outs) == 1 else jnp.concatenate(outs, axis=1)


def _ada_kernel(c_ref, w_ref, b_ref, o_ref):
    s = c_ref[...]
    s = s * jax.nn.sigmoid(s)
    o_ref[...] = _dot_f32(s, w_ref[...]) + b_ref[...]


def _ada(cond8, w_mod, b_mod):
    n = w_mod.shape[1]
    tn = 1536
    return pl.pallas_call(
        _ada_kernel,
        out_shape=jax.ShapeDtypeStruct((SUBLANES, n), F32),
        grid=(n // tn,),
        in_specs=[pl.BlockSpec((SUBLANES, D_MODEL), lambda j: (0, 0)),
                  pl.BlockSpec((D_MODEL, tn), lambda j: (0, j)),
                  pl.BlockSpec((1, tn), lambda j: (0, j))],
        out_specs=pl.BlockSpec((SUBLANES, tn), lambda j: (0, j)),
        compiler_params=_cparams("arbitrary"),
        name="ada_params",
    )(cond8, w_mod, b_mod.reshape(1, n))


def _swap_pairs(x):
    lane = lax.broadcasted_iota(I32, x.shape, 1)
    nxt = pltpu.roll(x, LANES - 1, 1)
    prv = pltpu.roll(x, 1, 1)
    return jnp.where(lane % 2 == 0, nxt, prv)


def _proj0_kernel(x_ref, sc_ref, sh_ref, gain_ref, w_ref, qg_ref, kg_ref, seg_ref, cos_ref, sin_ref,
                  xa_ref, ga_ref, q_ref, k_ref, v_ref, *, rope):
    h = _norm_mod(x_ref[0], gain_ref[...], sc_ref[0], sh_ref[0])
    res = _dot(h.astype(BF16), w_ref[...])
    xa_ref[0] = res[:, :LRU_WIDTH]
    ga_ref[0] = res[:, LRU_WIDTH:2 * LRU_WIDTH]
    q0 = 2 * LRU_WIDTH
    seg = seg_ref[...]

    def head_norm_rope(z, gain):
        ms = _head_sum(z * z, seg) * (1.0 / HEAD_DIM)
        zn = z * lax.rsqrt(ms + NORM_EPS) * gain
        if not rope:
            return zn
        c = cos_ref[...]
        s = sin_ref[...]
        outs = []
        for t in range(zn.shape[1] // LANES):
            zt = zn[:, t * LANES:(t + 1) * LANES]
            outs.append(zt * c + _swap_pairs(zt) * s)
        return outs[0] if len(outs) == 1 else jnp.concatenate(outs, axis=1)

    q = head_norm_rope(res[:, q0:q0 + Q_WIDTH], qg_ref[...])
    q_ref[0] = (q * Q_PRESCALE).astype(BF16)
    k = head_norm_rope(res[:, q0 + Q_WIDTH:q0 + Q_WIDTH + KV_WIDTH], kg_ref[...])
    k_ref[0] = k.astype(BF16)
    v_ref[0] = res[:, q0 + Q_WIDTH + KV_WIDTH:].astype(BF16)


def _proj0(x, sc, sh, gain, w_in, q_gain, k_gain, cos_t, sin_t, rope):
    b, t, _ = x.shape
    tm = min(512, t)
    qg = jnp.tile(q_gain, N_Q_HEADS).reshape(1, Q_WIDTH)
    kg = jnp.tile(k_gain, N_KV_HEADS).reshape(1, KV_WIDTH)
    seg = _seg_ones(LANES)
    const = lambda shape: pl.BlockSpec(shape, lambda bi, i: (0,) * len(shape))
    tok = lambda w: pl.BlockSpec((1, tm, w), lambda bi, i: (bi, i, 0))
    per_b = pl.BlockSpec((1, 1, D_MODEL), lambda bi, i: (bi, 0, 0))
    return pl.pallas_call(
        functools.partial(_proj0_kernel, rope=rope),
        out_shape=(jax.ShapeDtypeStruct((b, t, LRU_WIDTH), F32),
                   jax.ShapeDtypeStruct((b, t, LRU_WIDTH), F32),
                   jax.ShapeDtypeStruct((b, t, Q_WIDTH), BF16),
                   jax.ShapeDtypeStruct((b, t, KV_WIDTH), BF16),
                   jax.ShapeDtypeStruct((b, t, KV_WIDTH), BF16)),
        grid=(b, t // tm),
        in_specs=[tok(D_MODEL), per_b, per_b, const((1, D_MODEL)), const((D_MODEL, IN_WIDTH)),
                  const((1, Q_WIDTH)), const((1, KV_WIDTH)), const((LANES, LANES)),
                  pl.BlockSpec((tm, LANES), lambda bi, i: (i, 0)),
                  pl.BlockSpec((tm, LANES), lambda bi, i: (i, 0))],
        out_specs=(tok(LRU_WIDTH), tok(LRU_WIDTH), tok(Q_WIDTH), tok(KV_WIDTH), tok(KV_WIDTH)),
        compiler_params=_cparams("parallel", "arbitrary"),
        name="l0_in_proj",
    )(x, sc, sh, gain.reshape(1, D_MODEL), w_in, qg, kg, seg, cos_t, sin_t)


def _rope_tables(t):
    n_rows = t // GRID_W
    row = jnp.repeat(jnp.arange(n_rows, dtype=F32), GRID_W)
    col = jnp.tile(jnp.arange(GRID_W, dtype=F32), n_rows)
    axis_dim = HEAD_DIM // 2
    inv_freq = ROPE_BASE ** (-jnp.arange(0, axis_dim, 2, dtype=F32) / axis_dim)
    ang = jnp.concatenate([row[:, None] * inv_freq, col[:, None] * inv_freq], axis=-1)
    cos = jnp.repeat(jnp.cos(ang), 2, axis=-1)
    sin = jnp.repeat(jnp.sin(ang), 2, axis=-1)
    sign = jnp.tile(jnp.asarray([-1.0, 1.0], F32), HEAD_DIM // 2)
    return jnp.tile(cos, (1, 2)), jnp.tile(sin * sign, (1, 2))


def _scan_rows(a, b, h0, rev):
    n = a.shape[0]
    groups = n // SUBLANES
    a = a.reshape(groups, SUBLANES, LANES)
    b = b.reshape(groups, SUBLANES, LANES)
    row = lax.broadcasted_iota(I32, a.shape, 1)
    for d in (1, 2, 4):
        shift = SUBLANES - d if rev else d
        m = row < SUBLANES - d if rev else row >= d
        a_s = pltpu.roll(a, shift, 1)
        b_s = pltpu.roll(b, shift, 1)
        b = jnp.where(m, a * b_s + b, b)
        a = jnp.where(m, a * a_s, a)
    outs = [None] * groups
    h = h0
    for g in (range(groups - 1, -1, -1) if rev else range(groups)):
        hg = a[g] * h + b[g]
        outs[g] = hg
        h = hg[0:1] if rev else hg[SUBLANES - 1:SUBLANES]
    return jnp.concatenate(outs, axis=0), h


def _rglru_kernel(xc_ref, gc_ref, xl_ref, gl_ref, cw_ref, cb_ref, wg_ref, bg_ref, lam_ref,
                  yc_ref, yl_ref, rec_c, rec_l, *, tt_c, tt_l):
    cw = cw_ref[...]
    cb = cb_ref[...]

    def coeffs(x_ref, t0, tt, d):
        n = x_ref.shape[1]
        main = x_ref[0, pl.ds(t0, tt), :]
        prev = x_ref[0, pl.ds(pl.multiple_of(jnp.maximum(t0 - SUBLANES, 0), SUBLANES), SUBLANES), :]
        prev = jnp.where(t0 > 0, prev, 0.0)
        nxt = x_ref[0, pl.ds(pl.multiple_of(jnp.minimum(t0 + tt, n - SUBLANES), SUBLANES), SUBLANES), :]
        nxt = jnp.where(t0 + tt < n, nxt, 0.0)
        xe = jnp.concatenate([prev, main, nxt], axis=0)
        o = SUBLANES - CONV_W // 2
        xc = cb
        for j in range(CONV_W):
            xc = xc + cw[j:j + 1] * xe[o + j:o + j + tt]
        g = _dot(xc.astype(BF16), wg_ref[d, 0]) + bg_ref[d, 0]
        i_gate = jax.nn.sigmoid(g[:, :LANES])
        r_gate = jax.nn.sigmoid(g[:, LANES:])
        log_a = LRU_C * r_gate * (-_softplus(-lam_ref[d, 0]))
        a = jnp.exp(log_a)
        bco = jnp.sqrt(-jnp.tanh(log_a) * (a * a + 1.0)) * (i_gate * xc)
        return a, bco

    def sweep(x_ref, tt, d, rev, h, emit):
        nch = x_ref.shape[1] // tt

        def body(i, h):
            ci = nch - 1 - i if rev else i
            t0 = pl.multiple_of(ci * tt, tt)
            a, bco = coeffs(x_ref, t0, tt, d)
            hs, h = _scan_rows(a, bco, h, rev)
            emit(t0, tt, hs)
            return h

        return lax.fori_loop(0, nch, body, h)

    def store_rec(rec):
        def emit(t0, tt, hs):
            rec[pl.ds(t0, tt), :] = hs
        return emit

    def store_out(rec, g_ref, y_ref):
        def emit(t0, tt, hs):
            tot = rec[pl.ds(t0, tt), :] + hs
            y_ref[0, pl.ds(t0, tt), :] = (tot * _gelu_tanh(g_ref[0, pl.ds(t0, tt), :])).astype(y_ref.dtype)
        return emit

    zero = jnp.zeros((1, LANES), F32)
    h = sweep(xc_ref, tt_c, 0, False, zero, store_rec(rec_c))
    sweep(xl_ref, tt_l, 0, False, h, store_rec(rec_l))
    h = sweep(xc_ref, tt_c, 1, True, zero, store_out(rec_c, gc_ref, yc_ref))
    sweep(xl_ref, tt_l, 1, True, h, store_out(rec_l, gl_ref, yl_ref))


def _rglru(xa_c, ga_c, xa_l, ga_l, conv_w, conv_b, lru_wi, lru_bi, lru_wr, lru_br, lru_lam):
    b, lc, _ = xa_c.shape
    t = xa_l.shape[1]
    nt = LRU_WIDTH // LANES
    per_tile = LANES // LRU_BLOCK

    def block_diag(w):
        w = w.reshape(2, nt, per_tile, LRU_BLOCK, LRU_BLOCK)
        eye = jnp.eye(per_tile, dtype=w.dtype)
        return jnp.einsum('dtpij,pq->dtpiqj', w, eye).reshape(2, nt, LANES, LANES)

    wg = jnp.concatenate([block_diag(lru_wi), block_diag(lru_wr)], axis=-1).astype(BF16)
    bg = jnp.concatenate([lru_bi.reshape(2, nt, 1, LANES), lru_br.reshape(2, nt, 1, LANES)], axis=-1)
    lam = lru_lam.reshape(2, nt, 1, LANES)
    tt_c = min(256, lc)
    tt_l = min(256, t)
    seq = lambda n: pl.BlockSpec((1, n, LANES), lambda bi, j: (bi, 0, j))
    return pl.pallas_call(
        functools.partial(_rglru_kernel, tt_c=tt_c, tt_l=tt_l),
        out_shape=(jax.ShapeDtypeStruct((b, lc, LRU_WIDTH), BF16),
                   jax.ShapeDtypeStruct((b, t, LRU_WIDTH), BF16)),
        grid=(b, nt),
        in_specs=[seq(lc), seq(lc), seq(t), seq(t),
                  pl.BlockSpec((CONV_W, LANES), lambda bi, j: (0, j)),
                  pl.BlockSpec((1, LANES), lambda bi, j: (0, j)),
                  pl.BlockSpec((2, 1, LANES, 2 * LANES), lambda bi, j: (0, j, 0, 0)),
                  pl.BlockSpec((2, 1, 1, 2 * LANES), lambda bi, j: (0, j, 0, 0)),
                  pl.BlockSpec((2, 1, 1, LANES), lambda bi, j: (0, j, 0, 0))],
        out_specs=(seq(lc), seq(t)),
        scratch_shapes=[pltpu.VMEM((lc, LANES), F32), pltpu.VMEM((t, LANES), F32)],
        compiler_params=_cparams("parallel", "arbitrary"),
        name="l0_rglru",
    )(xa_c, ga_c, xa_l, ga_l, conv_w, conv_b.reshape(1, LRU_WIDTH), wg, bg, lam)


def _attn_kernel(q_ref, *refs, n_seg, kv_chunk):
    kv = refs[:2 * n_seg]
    o_ref = refs[2 * n_seg]
    g = pl.program_id(1)
    q = q_ref[0]
    tq = q.shape[0]
    lane = lax.broadcasted_iota(I32, (1, KV_WIDTH), 1)
    mine = (lane >= g * HEAD_DIM) & (lane < (g + 1) * HEAD_DIM)
    rows = []
    for h in range(Q_GROUP):
        qh = q[:, h * HEAD_DIM:(h + 1) * HEAD_DIM]
        both = jnp.concatenate([qh] * N_KV_HEADS, axis=1)
        rows.append(jnp.where(mine, both, jnp.zeros_like(both)))
    qpad = jnp.concatenate(rows, axis=0)
    nq = Q_GROUP * tq
    m = jnp.full((1, nq), -jnp.inf, F32)
    acc = jnp.zeros((kv[1].shape[2], nq), F32)
    chunks = []
    for i in range(n_seg):
        tk = kv[2 * i].shape[1]
        ck = min(kv_chunk, tk)
        chunks += [(kv[2 * i], kv[2 * i + 1], c * ck, ck) for c in range(tk // ck)]
    scores = lambda ch: _dot_nt(ch[0][0, ch[2]:ch[2] + ch[3], :], qpad)
    ready = [scores(ch) for ch in chunks[:ATTN_LOOKAHEAD]]
    for j, (_, vt_ref, c0, ck) in enumerate(chunks):
        s = ready.pop(0)
        if j + ATTN_LOOKAHEAD < len(chunks):
            ready.append(scores(chunks[j + ATTN_LOOKAHEAD]))
        m_new = jnp.maximum(m, s.max(axis=0, keepdims=True))
        alpha = jnp.exp2(m - m_new)
        p = jnp.exp2(s - m_new)
        acc = alpha * acc + _dot(vt_ref[0, 0, :, c0:c0 + ck], p.astype(BF16))
        m = m_new
    out_t = acc[:HEAD_DIM] / acc[HEAD_DIM:HEAD_DIM + 1]
    outs = [out_t[:, h * tq:(h + 1) * tq].T for h in range(Q_GROUP)]
    o_ref[0] = jnp.concatenate(outs, axis=1).astype(o_ref.dtype)


def _attention(q, segs):
    b, t, _ = q.shape
    tq = min(ATTN_Q_TILE, t)
    gw = Q_GROUP * HEAD_DIM
    in_specs = [pl.BlockSpec((1, tq, gw), lambda bi, g, i: (bi, i, g))]
    args = [q]
    for k, vt in segs:
        tk = k.shape[1]
        in_specs.append(pl.BlockSpec((1, tk, KV_WIDTH), lambda bi, g, i: (bi, 0, 0)))
        in_specs.append(pl.BlockSpec((1, 1, ATTN_VT_ROWS, tk), lambda bi, g, i: (bi, g, 0, 0)))
        args += [k, vt]
    return pl.pallas_call(
        functools.partial(_attn_kernel, n_seg=len(segs), kv_chunk=ATTN_KV_CHUNK),
        out_shape=jax.ShapeDtypeStruct((b, t, Q_WIDTH), BF16),
        grid=(b, N_KV_HEADS, t // tq),
        in_specs=in_specs,
        out_specs=pl.BlockSpec((1, tq, gw), lambda bi, g, i: (bi, i, g)),
        compiler_params=_cparams("parallel", "parallel", "arbitrary"),
        name="l0_attention",
    )(*args)


def _kv_layout(k, v):
    b, t, _ = k.shape
    vt = v.reshape(b, t, N_KV_HEADS, HEAD_DIM).transpose(0, 2, 3, 1)
    extra = jnp.zeros((b, N_KV_HEADS, ATTN_VT_ROWS - HEAD_DIM, t), v.dtype).at[:, :, 0].set(1)
    return k, jnp.concatenate([vt, extra], axis=2)


def _resmm_kernel(x_ref, g_ref, *refs, n):
    acc = None
    for i in range(n):
        d = _dot(refs[i][0], refs[n + i][...])
        acc = d if acc is None else acc + d
    o_ref = refs[2 * n]
    o_ref[0] = x_ref[0] + g_ref[0] * acc


def _residual_matmul(x, gate, acts, weights):
    b, t, _ = x.shape
    tm = min(512, t)
    n = len(acts)
    tok = lambda w: pl.BlockSpec((1, tm, w), lambda bi, i: (bi, i, 0))
    in_specs = [tok(D_MODEL), pl.BlockSpec((1, 1, D_MODEL), lambda bi, i: (bi, 0, 0))]
    in_specs += [tok(a.shape[-1]) for a in acts]
    in_specs += [pl.BlockSpec(w.shape, lambda bi, i: (0, 0)) for w in weights]
    return pl.pallas_call(
        functools.partial(_resmm_kernel, n=n),
        out_shape=jax.ShapeDtypeStruct((b, t, D_MODEL), F32),
        grid=(b, t // tm),
        in_specs=in_specs,
        out_specs=tok(D_MODEL),
        compiler_params=_cparams("parallel", "arbitrary"),
        name="residual_proj",
    )(x, gate, *acts, *weights)


def _router_kernel(x_ref, sc_ref, sh_ref, gain_ref, wr_ref, h_ref, aff_ref):
    h = _norm_mod(x_ref[0], gain_ref[...], sc_ref[0], sh_ref[0])
    tm = h.shape[0]
    logits = _dot_f32(h, wr_ref[...], 2)
    lane = lax.broadcasted_iota(I32, (tm, ROUTE_PAD), 1)
    z = jnp.where(lane < N_EXPERTS, logits, -jnp.inf)
    e = jnp.exp(z - z.max(axis=1, keepdims=True))
    aff = e / e.sum(axis=1, keepdims=True)
    h_ref[0, :, :D_MODEL] = h.astype(BF16)
    hi = aff.astype(BF16).astype(F32)
    h_ref[0, :, D_MODEL:] = (hi + pltpu.roll(aff - hi, N_EXPERTS, 1)).astype(BF16)
    aff_ref[0] = aff.T[:N_EXPERTS]


def _router(x, sc, sh, gain, w_router):
    b, t, _ = x.shape
    tm = min(512, t)
    wr = jnp.pad(w_router, ((0, 0), (0, ROUTE_PAD - N_EXPERTS)))
    tok = lambda w: pl.BlockSpec((1, tm, w), lambda bi, i: (bi, i, 0))
    per_b = pl.BlockSpec((1, 1, D_MODEL), lambda bi, i: (bi, 0, 0))
    return pl.pallas_call(
        _router_kernel,
        out_shape=(jax.ShapeDtypeStruct((b, t, D_MODEL + ROUTE_PAD), BF16),
                   jax.ShapeDtypeStruct((b, N_EXPERTS, t), F32)),
        grid=(b, t // tm),
        in_specs=[tok(D_MODEL), per_b, per_b,
                  pl.BlockSpec((1, D_MODEL), lambda bi, i: (0, 0)),
                  pl.BlockSpec((D_MODEL, ROUTE_PAD), lambda bi, i: (0, 0))],
        out_specs=(tok(D_MODEL + ROUTE_PAD), pl.BlockSpec((1, N_EXPERTS, tm), lambda bi, i: (bi, 0, i))),
        compiler_params=_cparams("parallel", "arbitrary"),
        name="moe_router",
    )(x, sc, sh, gain.reshape(1, D_MODEL), wr)


def _cumsum_lanes(x01, tri):
    outs = []
    off = jnp.zeros((x01.shape[0], 1), F32)
    for j in range(x01.shape[1] // LANES):
        cj = _dot(x01[:, j * LANES:(j + 1) * LANES].astype(BF16), tri) + off
        outs.append(cj)
        off = cj[:, LANES - 1:LANES]
    return outs[0] if len(outs) == 1 else jnp.concatenate(outs, axis=1)


def _select_kernel(aff_ref, tri_ref, tile_ref, pos_ref, tab_ref, *, cap):
    aff = aff_ref[0]
    n = aff.shape[1]
    keys = pltpu.bitcast(aff, I32)

    def bit_step(i, tau):
        cand = tau | jnp.left_shift(jnp.int32(1), 30 - i)
        cnt = jnp.sum((keys >= cand).astype(I32), axis=1, keepdims=True)
        return jnp.where(cnt >= cap, cand, tau)

    tau = lax.fori_loop(0, 31, bit_step, jnp.zeros((N_EXPERTS, 1), I32))
    gt = keys > tau
    eq = keys == tau
    need = (cap - jnp.sum(gt.astype(I32), axis=1, keepdims=True)).astype(F32)
    tri = tri_ref[...]
    c_eq = _cumsum_lanes(eq.astype(F32), tri)
    sel = gt | (eq & (c_eq <= need))
    c_sel = _cumsum_lanes(sel.astype(F32), tri)
    pos_ref[0] = jnp.where(sel, c_sel - 1.0, -1.0)
    tab_ref[0] = _dot(sel.astype(BF16), tile_ref[...]).astype(I32)


def _select(aff_t, cap):
    b, _, n = aff_t.shape
    assert n % MOE_TILE == 0 and n // MOE_TILE <= LANES // 2
    tri = jnp.asarray(np.triu(np.ones((LANES, LANES), np.float32)), dtype=BF16)
    tok = np.arange(n)[:, None]
    j = np.arange(LANES // 2)[None, :]
    tile_tab = jnp.asarray(np.concatenate([tok < j * MOE_TILE, tok // MOE_TILE == j], axis=1), dtype=BF16)
    return pl.pallas_call(
        functools.partial(_select_kernel, cap=cap),
        out_shape=(jax.ShapeDtypeStruct((b, N_EXPERTS, n), F32),
                   jax.ShapeDtypeStruct((b, N_EXPERTS, LANES), I32)),
        grid=(b,),
        in_specs=[pl.BlockSpec((1, N_EXPERTS, n), lambda bi: (bi, 0, 0)),
                  pl.BlockSpec((LANES, LANES), lambda bi: (0, 0)),
                  pl.BlockSpec((n, LANES), lambda bi: (0, 0))],
        out_specs=(pl.BlockSpec((1, N_EXPERTS, n), lambda bi: (bi, 0, 0)),
                   pl.BlockSpec((1, N_EXPERTS, LANES), lambda bi: (bi, 0, 0))),
        compiler_params=_cparams("parallel"),
        name="moe_select",
    )(aff_t, tri, tile_tab)


def _tile_windows(tab_ref, pos_ref, j, win):
    slot = lax.broadcasted_iota(I32, (win, MOE_TILE), 0).astype(F32)
    out = []
    for e in range(N_EXPERTS):
        start = tab_ref[0, e, j]
        count = tab_ref[0, e, LANES // 2 + j]
        a0 = pl.multiple_of(lax.shift_left(lax.shift_right_logical(start, MOE_ALIGN_LOG2), MOE_ALIGN_LOG2),
                            1 << MOE_ALIGN_LOG2)
        p = pos_ref[0, e:e + 1, :]
        onehot = lambda first, p=p: (p == slot + jnp.asarray(first, F32)).astype(BF16)
        extra = jnp.maximum(start - a0 + count - 1, 0) // win
        out.append((a0, onehot, extra))
    return out


def _moe_gather_kernel(tab_ref, h_ref, pos_ref, o_ref, *, win):
    j = pl.program_id(2)

    @pl.when(j == 0)
    def _():
        o_ref[...] = jnp.zeros(o_ref.shape, o_ref.dtype)

    h = h_ref[0]
    wins = _tile_windows(tab_ref, pos_ref, j, win)
    x = _dot(jnp.concatenate([oh(a0) for a0, oh, _ in wins], axis=0), h).astype(BF16)
    for e, (a0, oh, extra) in enumerate(wins):
        o_ref[0, e, pl.ds(a0, win), :] = o_ref[0, e, pl.ds(a0, win), :] + x[e * win:(e + 1) * win]

        def more(k, c, e=e, a0=a0, oh=oh):
            ak = pl.multiple_of(a0 + (k + 1) * win, 1 << MOE_ALIGN_LOG2)
            o_ref[0, e, pl.ds(ak, win), :] = o_ref[0, e, pl.ds(ak, win), :] + _dot(oh(ak), h).astype(BF16)
            return c

        lax.fori_loop(0, extra, more, 0)


def _moe_gather(h_ext, pos, tab, cap):
    b, n, width = h_ext.shape
    cols = width // MOE_COL_SPLIT
    rows = cap + MOE_SLACK
    return pl.pallas_call(
        functools.partial(_moe_gather_kernel, win=MOE_WIN),
        out_shape=jax.ShapeDtypeStruct((b, N_EXPERTS, rows, width), BF16),
        grid=(b, MOE_COL_SPLIT, n // MOE_TILE),
        in_specs=[pl.BlockSpec((1, N_EXPERTS, LANES), lambda bi, c, j: (bi, 0, 0), memory_space=pltpu.SMEM),
                  pl.BlockSpec((1, MOE_TILE, cols), lambda bi, c, j: (bi, j, c)),
                  pl.BlockSpec((1, N_EXPERTS, MOE_TILE), lambda bi, c, j: (bi, 0, j))],
        out_specs=pl.BlockSpec((1, N_EXPERTS, rows, cols), lambda bi, c, j: (bi, 0, 0, c)),
        compiler_params=_cparams("parallel", "parallel", "arbitrary"),
        name="moe_gather",
    )(tab, h_ext, pos)


def _moe_ffn_kernel(x_ref, wg_ref, wu_ref, wd_ref, y_ref, acc_s, *, cap):
    e = pl.program_id(0)
    f = pl.program_id(2)
    nb = x_ref.shape[0]
    x = x_ref[:, 0].reshape(nb * cap, x_ref.shape[3])
    xb = x[:, :D_MODEL]

    @pl.when(f == 0)
    def _():
        acc_s[...] = jnp.zeros(acc_s.shape, F32)

    g = _dot(xb, wg_ref[0].astype(BF16))
    u = _dot(xb, wu_ref[0].astype(BF16))
    hid = (g * jax.nn.sigmoid(g)) * u
    acc_s[...] += _dot(hid.astype(BF16), wd_ref[0].astype(BF16))

    @pl.when(f == pl.num_programs(2) - 1)
    def _():
        lane = lax.broadcasted_iota(I32, (nb * cap, ROUTE_PAD), 1)
        mine = (lane == e) | (lane == e + N_EXPERTS)
        gate = jnp.sum(jnp.where(mine, x[:, D_MODEL:].astype(F32), 0.0), axis=1, keepdims=True)
        y_ref[:, 0, :cap] = (acc_s[...] * gate).astype(y_ref.dtype).reshape(nb, cap, D_MODEL)
        slack = y_ref.shape[2] - cap
        y_ref[:, 0, cap:] = jnp.zeros((nb, slack, D_MODEL), y_ref.dtype)


def _moe_ffn(xe, wg, wu, wd, cap):
    b, _, rows, width = xe.shape
    nb = b if b * cap <= MOE_FFN_ROWS else 1
    fc = MOE_FFN_SLICE
    return pl.pallas_call(
        functools.partial(_moe_ffn_kernel, cap=cap),
        out_shape=jax.ShapeDtypeStruct((b, N_EXPERTS, rows, D_MODEL), BF16),
        grid=(N_EXPERTS, b // nb, EXPERT_FF // fc),
        in_specs=[pl.BlockSpec((nb, 1, cap, width), lambda e, bi, f: (bi, e, 0, 0)),
                  pl.BlockSpec((1, D_MODEL, fc), lambda e, bi, f: (e, 0, f)),
                  pl.BlockSpec((1, D_MODEL, fc), lambda e, bi, f: (e, 0, f)),
                  pl.BlockSpec((1, fc, D_MODEL), lambda e, bi, f: (e, f, 0))],
        out_specs=pl.BlockSpec((nb, 1, rows, D_MODEL), lambda e, bi, f: (bi, e, 0, 0)),
        scratch_shapes=[pltpu.VMEM((nb * cap, D_MODEL), F32)],
        compiler_params=_cparams("arbitrary", "arbitrary", "arbitrary"),
        name="moe_experts",
    )(xe, wg, wu, wd)


def _moe_combine_kernel(tab_ref, x_ref, gt_ref, pos_ref, y_ref, o_ref, acc_s, *, win):
    j = pl.program_id(2)
    wins = _tile_windows(tab_ref, pos_ref, j, win)
    onehots = jnp.concatenate([oh(a0) for a0, oh, _ in wins], axis=0)
    rows = jnp.concatenate([y_ref[0, e, pl.ds(a0, win), :] for e, (a0, _, _) in enumerate(wins)], axis=0)
    acc_s[...] = _dot_tn(onehots, rows)
    for e, (a0, oh, extra) in enumerate(wins):
        def more(k, c, e=e, a0=a0, oh=oh):
            ak = pl.multiple_of(a0 + (k + 1) * win, 1 << MOE_ALIGN_LOG2)
            acc_s[...] += _dot_tn(oh(ak), y_ref[0, e, pl.ds(ak, win), :])
            return c

        lax.fori_loop(0, extra, more, 0)
    o_ref[0] = x_ref[0] + gt_ref[0] * acc_s[...]


def _moe_combine(x, gt, pos, tab, y):
    b, n, _ = x.shape
    rows = y.shape[2]
    cols = D_MODEL // 2
    tile = lambda: pl.BlockSpec((1, MOE_TILE, cols), lambda bi, c, j: (bi, j, c))
    return pl.pallas_call(
        functools.partial(_moe_combine_kernel, win=MOE_WIN),
        out_shape=jax.ShapeDtypeStruct((b, n, D_MODEL), F32),
        grid=(b, 2, n // MOE_TILE),
        in_specs=[pl.BlockSpec((1, N_EXPERTS, LANES), lambda bi, c, j: (bi, 0, 0), memory_space=pltpu.SMEM),
                  tile(),
                  pl.BlockSpec((1, 1, cols), lambda bi, c, j: (bi, 0, c)),
                  pl.BlockSpec((1, N_EXPERTS, MOE_TILE), lambda bi, c, j: (bi, 0, j)),
                  pl.BlockSpec((1, N_EXPERTS, rows, cols), lambda bi, c, j: (bi, 0, 0, c))],
        out_specs=tile(),
        scratch_shapes=[pltpu.VMEM((MOE_TILE, cols), F32)],
        compiler_params=_cparams("parallel", "parallel", "arbitrary"),
        name="moe_combine",
    )(tab, x, gt, pos, y)


def _moe(x, sc, sh, gt, gain, w_router, wg, wu, wd):
    n = x.shape[1]
    cap = max(1, CAPACITY_FACTOR * n // N_EXPERTS)
    h_ext, aff_t = _router(x, sc, sh, gain, w_router)
    pos, tab = _select(aff_t, cap)
    y = _moe_ffn(_moe_gather(h_ext, pos, tab, cap), wg, wu, wd, cap)
    return _moe_combine(x, gt, pos, tab, y)


def _rwkv_prep_kernel(xm_ref, xp_ref, xn_ref, sc_ref, sh_ref, gain_ref, mu_ref, wr_ref, wk_ref, wv_ref,
                      w1_ref, w2_ref, a1_ref, a2_ref, g1_ref, g2_ref, w0_ref, a0_ref, kks_ref, ka_ref, rk_ref,
                      seg_ref, r_ref, v_ref, g_ref, kk_ref, bonus_ref, lw0_ref, lw1_ref, kd0_ref, kd1_ref,
                      ag0_ref, ag1_ref):
    i = pl.program_id(1)
    last = pl.num_programs(1) - 1
    tm = xm_ref.shape[1]
    gain = gain_ref[...]
    sc = sc_ref[0]
    sh = sh_ref[0]
    xe = jnp.concatenate([xp_ref[0], xm_ref[0], xn_ref[0]], axis=0)
    he = _norm_mod(xe, gain, sc, sh)
    row = lax.broadcasted_iota(I32, (tm, 1), 0)
    h = he[SUBLANES:SUBLANES + tm]
    hm1 = jnp.where((row == 0) & (i == 0), 0.0, he[SUBLANES - 1:SUBLANES - 1 + tm])
    hp1 = jnp.where((row == tm - 1) & (i == last), 0.0, he[SUBLANES + 1:SUBLANES + 1 + tm])
    xx = 0.5 * (hm1 + hp1) - h
    mu = mu_ref[...]
    mix = lambda j: (h + xx * mu[j:j + 1]).astype(BF16)
    r = _dot(mix(0), wr_ref[...])
    k = _dot(mix(2), wk_ref[...])
    v = _dot(mix(3), wv_ref[...])
    tw = jnp.tanh(_dot(mix(1), w1_ref[...])).astype(BF16)
    la = _dot(mix(4), a1_ref[...]).astype(BF16)
    g = _dot(jax.nn.sigmoid(_dot(mix(5), g1_ref[...])).astype(BF16), g2_ref[...])
    seg = seg_ref[...]
    kkv = k * kks_ref[...]
    kk = kkv * lax.rsqrt(jnp.maximum(_head_sum(kkv * kkv, seg, 1), 1e-24))
    r_ref[0] = r.astype(r_ref.dtype)
    v_ref[0] = v.astype(v_ref.dtype)
    g_ref[0] = g.astype(g_ref.dtype)
    kk_ref[0] = kk.astype(kk_ref.dtype)
    bonus = None
    for d, (lw_ref, kd_ref, ag_ref) in enumerate(((lw0_ref, kd0_ref, ag0_ref), (lw1_ref, kd1_ref, ag1_ref))):
        w_pre = w0_ref[d] + _dot(tw, w2_ref[d])
        lw_ref[0] = jax.nn.sigmoid(w_pre) * np.float32(-np.exp(-0.5))
        a = jax.nn.sigmoid(a0_ref[d] + _dot(la, a2_ref[d]))
        kd = k * (1.0 + (a - 1.0) * ka_ref[...])
        kd_ref[0] = kd.astype(kd_ref.dtype)
        ag_ref[0] = a.astype(ag_ref.dtype)
        bd = _head_sum(r * kd * rk_ref[...], seg, 1) * v
        bonus = bd if bonus is None else bonus + bd
    bonus_ref[0] = bonus


def _rwkv_prep(x, sc, sh, gain, p):
    b, t, _ = x.shape
    tm = min(256, t)
    nb8 = tm // SUBLANES
    tok = lambda: pl.BlockSpec((1, tm, D_MODEL), lambda bi, i: (bi, i, 0))
    prev = pl.BlockSpec((1, SUBLANES, D_MODEL), lambda bi, i: (bi, jnp.maximum(i * nb8 - 1, 0), 0))
    nxt = pl.BlockSpec((1, SUBLANES, D_MODEL),
                       lambda bi, i: (bi, jnp.minimum((i + 1) * nb8, t // SUBLANES - 1), 0))
    per_b = pl.BlockSpec((1, 1, D_MODEL), lambda bi, i: (bi, 0, 0))
    const = lambda a: pl.BlockSpec(a.shape, lambda bi, i: (0,) * a.ndim)
    consts = [p['gain'], p['mu'], p['w_r'], p['w_k'], p['w_v'], p['w1'], p['w2'], p['a1'], p['a2'], p['g1'],
              p['g2'], p['w0'], p['a0'], p['kks'], p['ka'], p['rk'], p['seg']]
    bf = jax.ShapeDtypeStruct((b, t, D_MODEL), BF16)
    f32 = jax.ShapeDtypeStruct((b, t, D_MODEL), F32)
    return pl.pallas_call(
        _rwkv_prep_kernel,
        out_shape=(bf, bf, bf, bf, f32, f32, f32, bf, bf, bf, bf),
        grid=(b, t // tm),
        in_specs=[tok(), prev, nxt, per_b, per_b] + [const(a) for a in consts],
        out_specs=tuple(tok() for _ in range(11)),
        compiler_params=_cparams("parallel", "arbitrary"),
        name="l1_rwkv_prep",
    )(x, x, x, sc, sh, *consts)


def _rwkv_params(gain, mu, w_r, w_k, w_v, w0, w1, w2, a0, a1, a2, g1, g2, k_k, k_a, r_k):
    def pad_dir(w):
        z = jnp.zeros_like(w[0])
        return jnp.stack([jnp.concatenate([w[0], z], axis=0), jnp.concatenate([z, w[1]], axis=0)])

    row = lambda a: a.reshape(1, D_MODEL)
    return dict(
        gain=row(gain), mu=mu, w_r=w_r.astype(BF16), w_k=w_k.astype(BF16), w_v=w_v.astype(BF16),
        w1=jnp.concatenate([w1[0], w1[1]], axis=1).astype(BF16), w2=pad_dir(w2).astype(BF16),
        a1=jnp.concatenate([a1[0], a1[1]], axis=1).astype(BF16), a2=pad_dir(a2).astype(BF16),
        g1=jnp.pad(g1, ((0, 0), (0, GATE_LORA_PAD - GATE_LORA))).astype(BF16),
        g2=jnp.pad(g2, ((0, GATE_LORA_PAD - GATE_LORA), (0, 0))).astype(BF16),
        w0=w0.reshape(2, 1, D_MODEL), a0=a0.reshape(2, 1, D_MODEL),
        kks=row(k_k), ka=row(k_a), rk=row(r_k), seg=_seg_ones(LANES))


SOLVE_BASE = 8
WKV_SUB_CHUNKS = 4


def _solve_masks(n, top):
    rowi = lax.broadcasted_iota(I32, (n, n), 0)
    coli = lax.broadcasted_iota(I32, (n, n), 1)
    same = lambda shift: (rowi >> shift) == (coli >> shift)
    k = int(np.log2(SOLVE_BASE))
    levels = range(k, int(np.log2(top)))
    return (jnp.where(rowi == coli, 1.0, 0.0), same(k),
            [same(j + 1) & jnp.logical_not(same(j)) for j in levels])


def _solve_unit_tri(nmat, x, masks):
    eye, base, joins = masks
    m = jnp.where(base, nmat, 0.0).astype(BF16)
    tinv = eye + m.astype(F32)
    for _ in range(int(np.log2(SOLVE_BASE)) - 1):
        m = _dot(m, m).astype(BF16)
        yield None
        tinv = tinv + _dot(m, tinv.astype(BF16))
        yield None
    for join in joins:
        off = jnp.where(join, nmat, 0.0).astype(BF16)
        tb = tinv.astype(BF16)
        half = _dot(tb, off).astype(BF16)
        yield None
        tinv = tinv + _dot(half, tb)
        yield None
    yield _dot(tinv.astype(BF16), x.astype(BF16))


def _wkv_stage(r_ref, v_ref, kk_ref, lw_ref, kd_ref, ag_ref, tri, ws, zs, vs, gts, d, rev, chunk, rows, k):
    lw = lw_ref[0, rows]
    hi, lo = _split2(lw)
    cum = _dot(tri, hi) + _dot(tri, lo)
    g_in = jnp.exp(cum)
    g_ex = jnp.exp(cum - lw)
    g_inv = jnp.exp(-cum)
    kk = kk_ref[0, rows].astype(F32)
    a_t = -(kk * g_ex)
    r_t = r_ref[0, rows].astype(F32) * g_in
    b_t = kk * ag_ref[0, rows].astype(F32) * g_inv
    k_t = kd_ref[0, rows].astype(F32) * g_inv
    g_tot = jnp.exp(cum[0:1] if rev else cum[chunk - 1:chunk])
    first = (lax.broadcasted_iota(I32, (1, D_MODEL), 1) % LANES) < RWKV_HEAD
    stacked = lambda x: (jnp.where(first, x, jnp.zeros_like(x)), jnp.where(first, jnp.zeros_like(x), x))
    a2, r2, b2, k2 = (stacked(x.astype(BF16)) for x in (a_t, r_t, b_t, k_t))
    v2 = stacked(v_ref[0, rows])
    c2 = 2 * chunk
    for p in range(HEAD_PAIRS):
        sl = slice(p * LANES, (p + 1) * LANES)
        for h in range(2):
            ws[k, d, p, h * chunk:(h + 1) * chunk] = a2[h][:, sl]
            ws[k, d, p, c2 + h * chunk:c2 + (h + 1) * chunk] = r2[h][:, sl]
            zs[k, d, p, h * chunk:(h + 1) * chunk] = b2[h][:, sl]
            zs[k, d, p, c2 + h * chunk:c2 + (h + 1) * chunk] = k2[h][:, sl]
            vs[k, d, p, h * chunk:(h + 1) * chunk] = v2[h][:, sl]
        gts[k, d, p] = g_tot[:, sl]


def _wkv_masks(chunk, rev):
    n = 2 * chunk
    ti = lax.broadcasted_iota(I32, (n, 2 * n), 0) & (chunk - 1)
    tj = lax.broadcasted_iota(I32, (n, 2 * n), 1) & (chunk - 1)
    strict = ti < tj if rev else ti > tj
    incl = ti <= tj if rev else ti >= tj
    return strict, incl, _solve_masks(n, chunk)


def _wkv_chain(s_ref, y_ref, ws, zs, vs, gts, d, p, masks, chunk, rows, k):
    n = 2 * chunk
    w = ws[k, d, p]
    z = zs[k, d, p]
    v = vs[k, d, p]
    s_old = s_ref[d, p]
    pm = _dot_nt(w, z)
    yield
    wst = _dot_nt(w, s_old.astype(BF16))
    yield
    strict, incl, solve_masks = masks
    top = jnp.where(strict, pm[:n], 0.0)
    nmat = top[:, :n]
    ak = top[:, n:].astype(BF16)
    rbk = jnp.where(incl, pm[n:], 0.0).astype(BF16)
    x = wst[:n] + _dot(ak, v)
    yield
    u = None
    for u in _solve_unit_tri(nmat, x, solve_masks):
        yield
    uv = jnp.concatenate([u.astype(BF16), v], axis=0)
    y = wst[n:] + _dot(rbk, uv)
    yield
    if y_ref is not None:
        y_ref[0, rows, p * LANES:(p + 1) * LANES] = y[:chunk] + y[chunk:]
    ds = _dot_tn(uv, z)
    yield
    s_ref[d, p] = (s_old + ds) * gts[k, d, p]


def _wkv_kernel(*refs, chunk, emit_y, has_init):
    ins = refs[:12]
    tri_ref = refs[12]
    pos = 13
    if has_init:
        s0_ref = refs[pos]
        pos += 1
    yf_ref = yb_ref = None
    if emit_y:
        yf_ref, yb_ref = refs[pos:pos + 2]
        pos += 2
    sfin_ref = refs[pos]
    s_ref, ws, zs, vs, gts = refs[pos + 1:]
    i = pl.program_id(1)

    @pl.when(i == 0)
    def _():
        if has_init:
            s_ref[...] = s0_ref[0]
        else:
            s_ref[...] = jnp.zeros(s_ref.shape, F32)

    n_sub = ins[0].shape[1] // chunk

    def rows_of(d, k):
        j = n_sub - 1 - k if d else k
        return slice(j * chunk, (j + 1) * chunk)

    def stage(k):
        for d, rev in ((0, False), (1, True)):
            _wkv_stage(*ins[6 * d:6 * d + 6], tri_ref[d], ws, zs, vs, gts, d, rev, chunk, rows_of(d, k), k)

    masks = [_wkv_masks(chunk, False), _wkv_masks(chunk, True)]
    stage(0)
    for k in range(n_sub):
        chains = [_wkv_chain(s_ref, (yf_ref, yb_ref)[d], ws, zs, vs, gts, d, p, masks[d], chunk, rows_of(d, k), k)
                  for p in range(HEAD_PAIRS) for d in range(2)]
        first_round = True
        while chains:
            alive = []
            for ch in chains:
                try:
                    next(ch)
                    alive.append(ch)
                except StopIteration:
                    pass
            chains = alive
            if first_round and k + 1 < n_sub:
                stage(k + 1)
            first_round = False

    @pl.when(i == pl.num_programs(1) - 1)
    def _():
        sfin_ref[0] = s_ref[...]


def _wkv(r, v, kk, lw, kd, ag, s0, emit_y, chunk=64):
    b, t, _ = r.shape
    n_sub = min(WKV_SUB_CHUNKS, t // chunk)
    n = t // (n_sub * chunk)
    fwd = pl.BlockSpec((1, n_sub * chunk, D_MODEL), lambda bi, i: (bi, i, 0))
    bwd = pl.BlockSpec((1, n_sub * chunk, D_MODEL), lambda bi, i: (bi, n - 1 - i, 0))
    tri = jnp.asarray(np.stack([np.tril(np.ones((chunk, chunk), np.float32)),
                                np.triu(np.ones((chunk, chunk), np.float32))]), dtype=BF16)
    state_shape = (2, HEAD_PAIRS, 2 * RWKV_HEAD, LANES)
    state_spec = pl.BlockSpec((1,) + state_shape, lambda bi, i: (bi, 0, 0, 0, 0))
    args = [r, v, kk, lw[0], kd[0], ag[0], r, v, kk, lw[1], kd[1], ag[1], tri]
    in_specs = [fwd] * 6 + [bwd] * 6 + [pl.BlockSpec((2, chunk, chunk), lambda bi, i: (0, 0, 0))]
    if s0 is not None:
        args.append(s0)
        in_specs.append(state_spec)
    out_shape = [jax.ShapeDtypeStruct((b,) + state_shape, F32)]
    out_specs = [state_spec]
    if emit_y:
        out_shape = [jax.ShapeDtypeStruct((b, t, D_MODEL), F32)] * 2 + out_shape
        out_specs = [fwd, bwd] + out_specs
    res = pl.pallas_call(
        functools.partial(_wkv_kernel, chunk=chunk, emit_y=emit_y, has_init=s0 is not None),
        out_shape=tuple(out_shape),
        grid=(b, n),
        in_specs=in_specs,
        out_specs=tuple(out_specs),
        scratch_shapes=[pltpu.VMEM(state_shape, F32),
                        pltpu.VMEM((n_sub, 2, HEAD_PAIRS, 4 * chunk, LANES), BF16),
                        pltpu.VMEM((n_sub, 2, HEAD_PAIRS, 4 * chunk, LANES), BF16),
                        pltpu.VMEM((n_sub, 2, HEAD_PAIRS, 2 * chunk, LANES), BF16),
                        pltpu.VMEM((n_sub, 2, HEAD_PAIRS, 1, LANES), F32)],
        compiler_params=_cparams("parallel", "arbitrary"),
        name="l1_wkv_scan",
    )(*args)
    return res


def _readout_kernel(x_ref, gt_ref, yf_ref, yb_ref, bonus_ref, g_ref, gng_ref, gnb_ref, seg_ref, wo_ref, o_ref):
    seg = seg_ref[...]
    y = yf_ref[0] + yb_ref[0]
    mean = _head_sum(y, seg) * (1.0 / RWKV_HEAD)
    c = y - mean
    var = _head_sum(c * c, seg, 1) * (1.0 / RWKV_HEAD)
    yn = c * lax.rsqrt(var + GN_EPS) * gng_ref[...] + gnb_ref[...]
    out = (yn + bonus_ref[0]) * g_ref[0].astype(F32)
    o_ref[0] = x_ref[0] + gt_ref[0] * _dot(out.astype(BF16), wo_ref[...])


def _readout(x, gt, y_f, y_b, bonus, g, gn_g, gn_b, w_o):
    b, t, _ = x.shape
    tm = min(256, t)
    tok = pl.BlockSpec((1, tm, D_MODEL), lambda bi, i: (bi, i, 0))
    row = pl.BlockSpec((1, D_MODEL), lambda bi, i: (0, 0))
    return pl.pallas_call(
        _readout_kernel,
        out_shape=jax.ShapeDtypeStruct((b, t, D_MODEL), F32),
        grid=(b, t // tm),
        in_specs=[tok, pl.BlockSpec((1, 1, D_MODEL), lambda bi, i: (bi, 0, 0)), tok, tok, tok, tok, row, row,
                  pl.BlockSpec((LANES, LANES), lambda bi, i: (0, 0)),
                  pl.BlockSpec((D_MODEL, D_MODEL), lambda bi, i: (0, 0))],
        out_specs=tok,
        compiler_params=_cparams("parallel", "arbitrary"),
        name="l1_readout",
    )(x, gt, y_f, y_b, bonus, g, gn_g.reshape(1, D_MODEL), gn_b.reshape(1, D_MODEL), _seg_ones(LANES), w_o)


def _modulation(c, c_ctx, w_mod, b_mod):
    b = c.shape[0]
    cond = jnp.zeros((SUBLANES, D_MODEL), F32).at[:b].set(c).at[b].set(c_ctx)
    m = _ada(cond, w_mod, b_mod)
    lat = [m[:b, j * D_MODEL:(j + 1) * D_MODEL].reshape(b, 1, D_MODEL) for j in range(6)]
    ctx = [jnp.broadcast_to(m[b, j * D_MODEL:(j + 1) * D_MODEL].reshape(1, 1, D_MODEL), (b, 1, D_MODEL))
           for j in range(6)]
    return lat, ctx


def kernel(x, c, ctx, c_ctx, l0_w_mod, l0_b_mod, l0_norm_mix, l0_norm_ffn, l0_w_in, l0_conv_w, l0_conv_b, l0_lru_wi, l0_lru_bi, l0_lru_wr, l0_lru_br, l0_lru_lam, l0_q_gain, l0_k_gain, l0_w_out, l0_router, l0_we_gate, l0_we_up, l0_we_down, l1_w_mod, l1_b_mod, l1_norm_mix, l1_norm_ffn, l1_mu, l1_w_r, l1_w_k, l1_w_v, l1_w0, l1_w1, l1_w2, l1_a0, l1_a1, l1_a2, l1_g1, l1_g2, l1_k_k, l1_k_a, l1_r_k, l1_gn_g, l1_gn_b, l1_w_o, l1_router, l1_we_gate, l1_we_up, l1_we_down):
    t = x.shape[1]

    (sh1, sc1, gt1, sh2, sc2, gt2), (csh1, csc1, cgt1, csh2, csc2, cgt2) = _modulation(c, c_ctx, l0_w_mod, l0_b_mod)
    w_in = l0_w_in.astype(BF16)
    cos_t, sin_t = _rope_tables(t)
    xa_l, ga_l, q_l, k_l, v_l = _proj0(x, sc1, sh1, l0_norm_mix, w_in, l0_q_gain, l0_k_gain, cos_t, sin_t, True)
    lc = ctx.shape[1]
    xa_c, ga_c, q_c, k_c, v_c = _proj0(ctx, csc1, csh1, l0_norm_mix, w_in, l0_q_gain, l0_k_gain,
                                       cos_t[:lc], sin_t[:lc], False)
    ya_c, ya_l = _rglru(xa_c, ga_c, xa_l, ga_l, l0_conv_w, l0_conv_b, l0_lru_wi, l0_lru_bi, l0_lru_wr,
                        l0_lru_br, l0_lru_lam)
    seg_l = _kv_layout(k_l, v_l)
    seg_c = _kv_layout(k_c, v_c)
    yb_l = _attention(q_l, [seg_l, seg_c])
    yb_c = _attention(q_c, [seg_c])
    w_out = l0_w_out.astype(BF16)
    w_oa, w_ob = w_out[:LRU_WIDTH], w_out[LRU_WIDTH:]
    x = _residual_matmul(x, gt1, [ya_l, yb_l], [w_oa, w_ob])
    ctx = _residual_matmul(ctx, cgt1, [ya_c, yb_c], [w_oa, w_ob])
    x = _moe(x, sc2, sh2, gt2, l0_norm_ffn, l0_router, l0_we_gate, l0_we_up, l0_we_down)
    ctx = _moe(ctx, csc2, csh2, cgt2, l0_norm_ffn, l0_router, l0_we_gate, l0_we_up, l0_we_down)

    (sh1, sc1, gt1, sh2, sc2, gt2), (csh1, csc1, _, _, _, _) = _modulation(c, c_ctx, l1_w_mod, l1_b_mod)
    p = _rwkv_params(l1_norm_mix, l1_mu, l1_w_r, l1_w_k, l1_w_v, l1_w0, l1_w1, l1_w2, l1_a0, l1_a1, l1_a2,
                     l1_g1, l1_g2, l1_k_k, l1_k_a, l1_r_k)
    r_c, v_c, _, kk_c, _, lw0_c, lw1_c, kd0_c, kd1_c, ag0_c, ag1_c = _rwkv_prep(ctx, csc1, csh1, l1_norm_mix, p)
    r_l, v_l, g_l, kk_l, bonus_l, lw0, lw1, kd0, kd1, ag0, ag1 = _rwkv_prep(x, sc1, sh1, l1_norm_mix, p)
    (s_ctx,) = _wkv(r_c, v_c, kk_c, (lw0_c, lw1_c), (kd0_c, kd1_c), (ag0_c, ag1_c), None, False)
    y_f, y_b, _ = _wkv(r_l, v_l, kk_l, (lw0, lw1), (kd0, kd1), (ag0, ag1), s_ctx, True)
    x = _readout(x, gt1, y_f, y_b, bonus_l, g_l, l1_gn_g, l1_gn_b, l1_w_o.astype(BF16))
    x = _moe(x, sc2, sh2, gt2, l1_norm_ffn, l1_router, l1_we_gate, l1_we_up, l1_we_down)
    return x
```

```python
import functools

import jax
import jax.numpy as jnp
import numpy as np
from jax import lax
from jax.experimental import pallas as pl
from jax.experimental.pallas import tpu as pltpu

F32 = jnp.float32
BF16 = jnp.bfloat16
I32 = jnp.int32

D_MODEL = 1024
LANES = 128
SUBLANES = 8
GRID_W = 64
NORM_EPS = 1e-6
LRU_WIDTH = 512
LRU_BLOCKS = 8
LRU_BLOCK = LRU_WIDTH // LRU_BLOCKS
LRU_C = 8.0
CONV_W = 4
N_Q_HEADS = 8
N_KV_HEADS = 2
HEAD_DIM = 64
Q_GROUP = N_Q_HEADS // N_KV_HEADS
ROPE_BASE = 10000.0
ATTN_SCALE = HEAD_DIM ** -0.5
Q_PRESCALE = ATTN_SCALE * float(np.log2(np.e))
ATTN_KV_CHUNK = 1024
ATTN_Q_TILE = 128
ATTN_LOOKAHEAD = 2
ATTN_VT_ROWS = HEAD_DIM + 16
Q_WIDTH = N_Q_HEADS * HEAD_DIM
KV_WIDTH = N_KV_HEADS * HEAD_DIM
IN_WIDTH = 2 * LRU_WIDTH + Q_WIDTH + 2 * KV_WIDTH
RWKV_HEAD = 64
HEAD_PAIRS = D_MODEL // LANES
GATE_LORA = 160
GATE_LORA_PAD = 256
GN_EPS = 64e-5
N_EXPERTS = 16
EXPERT_FF = 2048
CAPACITY_FACTOR = 2
ROUTE_PAD = LANES
VMEM_LIMIT = 60 * 1024 * 1024
MOE_TILE = 256
MOE_ALIGN_LOG2 = 4
MOE_WIN = 48 + (1 << MOE_ALIGN_LOG2)
MOE_SLACK = 128
MOE_COL_SPLIT = 3
MOE_FFN_ROWS = 1024
MOE_FFN_SLICE = 1024


def _cparams(*sem):
    return pltpu.CompilerParams(dimension_semantics=sem, vmem_limit_bytes=VMEM_LIMIT)


def _dot(a, b):
    return jnp.dot(a, b, preferred_element_type=F32)


def _dot_nt(a, b):
    return lax.dot_general(a, b, (((1,), (1,)), ((), ())), preferred_element_type=F32)


def _dot_tn(a, b):
    return lax.dot_general(a, b, (((0,), (0,)), ((), ())), preferred_element_type=F32)


def _split2(x):
    hi = x.astype(BF16)
    lo = (x - hi.astype(F32)).astype(BF16)
    return hi, lo


def _split3(x):
    hi = x.astype(BF16)
    r = x - hi.astype(F32)
    mid = r.astype(BF16)
    lo = (r - mid.astype(F32)).astype(BF16)
    return hi, mid, lo


def _dot_f32(a, b, pieces=3):
    split = _split3 if pieces == 3 else _split2
    ap = split(a)
    bp = split(b)
    out = None
    for i in range(pieces):
        for j in range(pieces - i):
            d = _dot(ap[i], bp[j])
            out = d if out is None else out + d
    return out


def _softplus(x):
    return jnp.maximum(x, 0.0) + jnp.log1p(jnp.exp(-jnp.abs(x)))


def _gelu_tanh(x):
    c = np.float32(np.sqrt(2.0 / np.pi))
    return 0.5 * x * (1.0 + jnp.tanh(c * (x + 0.044715 * (x * x * x))))


def _norm_mod(x, gain, sc, sh):
    ms = jnp.mean(x * x, axis=-1, keepdims=True)
    y = x * lax.rsqrt(ms + NORM_EPS)
    return (y * gain) * (1.0 + sc) + sh


def _seg_ones(width):
    i = np.arange(width)[:, None] // HEAD_DIM
    j = np.arange(width)[None, :] // HEAD_DIM
    return jnp.asarray(i == j, dtype=BF16)


def _head_sum(x, seg, pieces=2):
    outs = []
    for c in range(x.shape[1] // LANES):
        xc = x[:, c * LANES:(c + 1) * LANES]
        if pieces == 1:
            outs.append(_dot(xc.astype(BF16), seg))
        else:
            hi, lo = _split2(xc)
            outs.append(_dot(hi, seg) + _dot(lo, seg))
    return outs[0] if len(outs) == 1 else jnp.concatenate(outs, axis=1)


def _ada_kernel(c_ref, w_ref, b_ref, o_ref):
    s = c_ref[...]
    s = s * jax.nn.sigmoid(s)
    o_ref[...] = _dot_f32(s, w_ref[...]) + b_ref[...]


def _ada(cond8, w_mod, b_mod):
    n = w_mod.shape[1]
    tn = 1536
    return pl.pallas_call(
        _ada_kernel,
        out_shape=jax.ShapeDtypeStruct((SUBLANES, n), F32),
        grid=(n // tn,),
        in_specs=[pl.BlockSpec((SUBLANES, D_MODEL), lambda j: (0, 0)),
                  pl.BlockSpec((D_MODEL, tn), lambda j: (0, j)),
                  pl.BlockSpec((1, tn), lambda j: (0, j))],
        out_specs=pl.BlockSpec((SUBLANES, tn), lambda j: (0, j)),
        compiler_params=_cparams("arbitrary"),
        name="ada_params",
    )(cond8, w_mod, b_mod.reshape(1, n))


def _swap_pairs(x):
    lane = lax.broadcasted_iota(I32, x.shape, 1)
    nxt = pltpu.roll(x, LANES - 1, 1)
    prv = pltpu.roll(x, 1, 1)
    return jnp.where(lane % 2 == 0, nxt, prv)


def _proj0_kernel(x_ref, sc_ref, sh_ref, gain_ref, w_ref, qg_ref, kg_ref, seg_ref, cos_ref, sin_ref,
                  xa_ref, ga_ref, q_ref, k_ref, v_ref, *, rope):
    h = _norm_mod(x_ref[0], gain_ref[...], sc_ref[0], sh_ref[0])
    res = _dot(h.astype(BF16), w_ref[...])
    xa_ref[0] = res[:, :LRU_WIDTH]
    ga_ref[0] = res[:, LRU_WIDTH:2 * LRU_WIDTH]
    q0 = 2 * LRU_WIDTH
    seg = seg_ref[...]

    def head_norm_rope(z, gain):
        ms = _head_sum(z * z, seg) * (1.0 / HEAD_DIM)
        zn = z * lax.rsqrt(ms + NORM_EPS) * gain
        if not rope:
            return zn
        c = cos_ref[...]
        s = sin_ref[...]
        outs = []
        for t in range(zn.shape[1] // LANES):
            zt = zn[:, t * LANES:(t + 1) * LANES]
            outs.append(zt * c + _swap_pairs(zt) * s)
        return outs[0] if len(outs) == 1 else jnp.concatenate(outs, axis=1)

    q = head_norm_rope(res[:, q0:q0 + Q_WIDTH], qg_ref[...])
    q_ref[0] = (q * Q_PRESCALE).astype(BF16)
    k = head_norm_rope(res[:, q0 + Q_WIDTH:q0 + Q_WIDTH + KV_WIDTH], kg_ref[...])
    k_ref[0] = k.astype(BF16)
    v_ref[0] = res[:, q0 + Q_WIDTH + KV_WIDTH:].astype(BF16)


def _proj0(x, sc, sh, gain, w_in, q_gain, k_gain, cos_t, sin_t, rope):
    b, t, _ = x.shape
    tm = min(512, t)
    qg = jnp.tile(q_gain, N_Q_HEADS).reshape(1, Q_WIDTH)
    kg = jnp.tile(k_gain, N_KV_HEADS).reshape(1, KV_WIDTH)
    seg = _seg_ones(LANES)
    const = lambda shape: pl.BlockSpec(shape, lambda bi, i: (0,) * len(shape))
    tok = lambda w: pl.BlockSpec((1, tm, w), lambda bi, i: (bi, i, 0))
    per_b = pl.BlockSpec((1, 1, D_MODEL), lambda bi, i: (bi, 0, 0))
    return pl.pallas_call(
        functools.partial(_proj0_kernel, rope=rope),
        out_shape=(jax.ShapeDtypeStruct((b, t, LRU_WIDTH), F32),
                   jax.ShapeDtypeStruct((b, t, LRU_WIDTH), F32),
                   jax.ShapeDtypeStruct((b, t, Q_WIDTH), BF16),
                   jax.ShapeDtypeStruct((b, t, KV_WIDTH), BF16),
                   jax.ShapeDtypeStruct((b, t, KV_WIDTH), BF16)),
        grid=(b, t // tm),
        in_specs=[tok(D_MODEL), per_b, per_b, const((1, D_MODEL)), const((D_MODEL, IN_WIDTH)),
                  const((1, Q_WIDTH)), const((1, KV_WIDTH)), const((LANES, LANES)),
                  pl.BlockSpec((tm, LANES), lambda bi, i: (i, 0)),
                  pl.BlockSpec((tm, LANES), lambda bi, i: (i, 0))],
        out_specs=(tok(LRU_WIDTH), tok(LRU_WIDTH), tok(Q_WIDTH), tok(KV_WIDTH), tok(KV_WIDTH)),
        compiler_params=_cparams("parallel", "arbitrary"),
        name="l0_in_proj",
    )(x, sc, sh, gain.reshape(1, D_MODEL), w_in, qg, kg, seg, cos_t, sin_t)


def _rope_tables(t):
    n_rows = t // GRID_W
    row = jnp.repeat(jnp.arange(n_rows, dtype=F32), GRID_W)
    col = jnp.tile(jnp.arange(GRID_W, dtype=F32), n_rows)
    axis_dim = HEAD_DIM // 2
    inv_freq = ROPE_BASE ** (-jnp.arange(0, axis_dim, 2, dtype=F32) / axis_dim)
    ang = jnp.concatenate([row[:, None] * inv_freq, col[:, None] * inv_freq], axis=-1)
    cos = jnp.repeat(jnp.cos(ang), 2, axis=-1)
    sin = jnp.repeat(jnp.sin(ang), 2, axis=-1)
    sign = jnp.tile(jnp.asarray([-1.0, 1.0], F32), HEAD_DIM // 2)
    return jnp.tile(cos, (1, 2)), jnp.tile(sin * sign, (1, 2))


def _scan_rows(a, b, h0, rev):
    n = a.shape[0]
    groups = n // SUBLANES
    a = a.reshape(groups, SUBLANES, LANES)
    b = b.reshape(groups, SUBLANES, LANES)
    row = lax.broadcasted_iota(I32, a.shape, 1)
    for d in (1, 2, 4):
        shift = SUBLANES - d if rev else d
        m = row < SUBLANES - d if rev else row >= d
        a_s = pltpu.roll(a, shift, 1)
        b_s = pltpu.roll(b, shift, 1)
        b = jnp.where(m, a * b_s + b, b)
        a = jnp.where(m, a * a_s, a)
    outs = [None] * groups
    h = h0
    for g in (range(groups - 1, -1, -1) if rev else range(groups)):
        hg = a[g] * h + b[g]
        outs[g] = hg
        h = hg[0:1] if rev else hg[SUBLANES - 1:SUBLANES]
    return jnp.concatenate(outs, axis=0), h


def _rglru_kernel(xc_ref, gc_ref, xl_ref, gl_ref, cw_ref, cb_ref, wg_ref, bg_ref, lam_ref,
                  yc_ref, yl_ref, rec_c, rec_l, *, tt_c, tt_l):
    cw = cw_ref[...]
    cb = cb_ref[...]

    def coeffs(x_ref, t0, tt, d):
        n = x_ref.shape[1]
        main = x_ref[0, pl.ds(t0, tt), :]
        prev = x_ref[0, pl.ds(pl.multiple_of(jnp.maximum(t0 - SUBLANES, 0), SUBLANES), SUBLANES), :]
        prev = jnp.where(t0 > 0, prev, 0.0)
        nxt = x_ref[0, pl.ds(pl.multiple_of(jnp.minimum(t0 + tt, n - SUBLANES), SUBLANES), SUBLANES), :]
        nxt = jnp.where(t0 + tt < n, nxt, 0.0)
        xe = jnp.concatenate([prev, main, nxt], axis=0)
        o = SUBLANES - CONV_W // 2
        xc = cb
        for j in range(CONV_W):
            xc = xc + cw[j:j + 1] * xe[o + j:o + j + tt]
        g = _dot(xc.astype(BF16), wg_ref[d, 0]) + bg_ref[d, 0]
        i_gate = jax.nn.sigmoid(g[:, :LANES])
        r_gate = jax.nn.sigmoid(g[:, LANES:])
        log_a = LRU_C * r_gate * (-_softplus(-lam_ref[d, 0]))
        a = jnp.exp(log_a)
        bco = jnp.sqrt(-jnp.tanh(log_a) * (a * a + 1.0)) * (i_gate * xc)
        return a, bco

    def sweep(x_ref, tt, d, rev, h, emit):
        nch = x_ref.shape[1] // tt

        def body(i, h):
            ci = nch - 1 - i if rev else i
            t0 = pl.multiple_of(ci * tt, tt)
            a, bco = coeffs(x_ref, t0, tt, d)
            hs, h = _scan_rows(a, bco, h, rev)
            emit(t0, tt, hs)
            return h

        return lax.fori_loop(0, nch, body, h)

    def store_rec(rec):
        def emit(t0, tt, hs):
            rec[pl.ds(t0, tt), :] = hs
        return emit

    def store_out(rec, g_ref, y_ref):
        def emit(t0, tt, hs):
            tot = rec[pl.ds(t0, tt), :] + hs
            y_ref[0, pl.ds(t0, tt), :] = (tot * _gelu_tanh(g_ref[0, pl.ds(t0, tt), :])).astype(y_ref.dtype)
        return emit

    zero = jnp.zeros((1, LANES), F32)
    h = sweep(xc_ref, tt_c, 0, False, zero, store_rec(rec_c))
    sweep(xl_ref, tt_l, 0, False, h, store_rec(rec_l))
    h = sweep(xc_ref, tt_c, 1, True, zero, store_out(rec_c, gc_ref, yc_ref))
    sweep(xl_ref, tt_l, 1, True, h, store_out(rec_l, gl_ref, yl_ref))


def _rglru(xa_c, ga_c, xa_l, ga_l, conv_w, conv_b, lru_wi, lru_bi, lru_wr, lru_br, lru_lam):
    b, lc, _ = xa_c.shape
    t = xa_l.shape[1]
    nt = LRU_WIDTH // LANES
    per_tile = LANES // LRU_BLOCK

    def block_diag(w):
        w = w.reshape(2, nt, per_tile, LRU_BLOCK, LRU_BLOCK)
        eye = jnp.eye(per_tile, dtype=w.dtype)
        return jnp.einsum('dtpij,pq->dtpiqj', w, eye).reshape(2, nt, LANES, LANES)

    wg = jnp.concatenate([block_diag(lru_wi), block_diag(lru_wr)], axis=-1).astype(BF16)
    bg = jnp.concatenate([lru_bi.reshape(2, nt, 1, LANES), lru_br.reshape(2, nt, 1, LANES)], axis=-1)
    lam = lru_lam.reshape(2, nt, 1, LANES)
    tt_c = min(256, lc)
    tt_l = min(256, t)
    seq = lambda n: pl.BlockSpec((1, n, LANES), lambda bi, j: (bi, 0, j))
    return pl.pallas_call(
        functools.partial(_rglru_kernel, tt_c=tt_c, tt_l=tt_l),
        out_shape=(jax.ShapeDtypeStruct((b, lc, LRU_WIDTH), BF16),
                   jax.ShapeDtypeStruct((b, t, LRU_WIDTH), BF16)),
        grid=(b, nt),
        in_specs=[seq(lc), seq(lc), seq(t), seq(t),
                  pl.BlockSpec((CONV_W, LANES), lambda bi, j: (0, j)),
                  pl.BlockSpec((1, LANES), lambda bi, j: (0, j)),
                  pl.BlockSpec((2, 1, LANES, 2 * LANES), lambda bi, j: (0, j, 0, 0)),
                  pl.BlockSpec((2, 1, 1, 2 * LANES), lambda bi, j: (0, j, 0, 0)),
                  pl.BlockSpec((2, 1, 1, LANES), lambda bi, j: (0, j, 0, 0))],
        out_specs=(seq(lc), seq(t)),
        scratch_shapes=[pltpu.VMEM((lc, LANES), F32), pltpu.VMEM((t, LANES), F32)],
        compiler_params=_cparams("parallel", "arbitrary"),
        name="l0_rglru",
    )(xa_c, ga_c, xa_l, ga_l, conv_w, conv_b.reshape(1, LRU_WIDTH), wg, bg, lam)


def _attn_kernel(q_ref, *refs, n_seg, kv_chunk):
    kv = refs[:2 * n_seg]
    o_ref = refs[2 * n_seg]
    g = pl.program_id(1)
    q = q_ref[0]
    tq = q.shape[0]
    lane = lax.broadcasted_iota(I32, (1, KV_WIDTH), 1)
    mine = (lane >= g * HEAD_DIM) & (lane < (g + 1) * HEAD_DIM)
    rows = []
    for h in range(Q_GROUP):
        qh = q[:, h * HEAD_DIM:(h + 1) * HEAD_DIM]
        both = jnp.concatenate([qh] * N_KV_HEADS, axis=1)
        rows.append(jnp.where(mine, both, jnp.zeros_like(both)))
    qpad = jnp.concatenate(rows, axis=0)
    nq = Q_GROUP * tq
    m = jnp.full((1, nq), -jnp.inf, F32)
    acc = jnp.zeros((kv[1].shape[2], nq), F32)
    chunks = []
    for i in range(n_seg):
        tk = kv[2 * i].shape[1]
        ck = min(kv_chunk, tk)
        chunks += [(kv[2 * i], kv[2 * i + 1], c * ck, ck) for c in range(tk // ck)]
    scores = lambda ch: _dot_nt(ch[0][0, ch[2]:ch[2] + ch[3], :], qpad)
    ready = [scores(ch) for ch in chunks[:ATTN_LOOKAHEAD]]
    for j, (_, vt_ref, c0, ck) in enumerate(chunks):
        s = ready.pop(0)
        if j + ATTN_LOOKAHEAD < len(chunks):
            ready.append(scores(chunks[j + ATTN_LOOKAHEAD]))
        m_new = jnp.maximum(m, s.max(axis=0, keepdims=True))
        alpha = jnp.exp2(m - m_new)
        p = jnp.exp2(s - m_new)
        acc = alpha * acc + _dot(vt_ref[0, 0, :, c0:c0 + ck], p.astype(BF16))
        m = m_new
    out_t = acc[:HEAD_DIM] / acc[HEAD_DIM:HEAD_DIM + 1]
    outs = [out_t[:, h * tq:(h + 1) * tq].T for h in range(Q_GROUP)]
    o_ref[0] = jnp.concatenate(outs, axis=1).astype(o_ref.dtype)


def _attention(q, segs):
    b, t, _ = q.shape
    tq = min(ATTN_Q_TILE, t)
    gw = Q_GROUP * HEAD_DIM
    in_specs = [pl.BlockSpec((1, tq, gw), lambda bi, g, i: (bi, i, g))]
    args = [q]
    for k, vt in segs:
        tk = k.shape[1]
        in_specs.append(pl.BlockSpec((1, tk, KV_WIDTH), lambda bi, g, i: (bi, 0, 0)))
        in_specs.append(pl.BlockSpec((1, 1, ATTN_VT_ROWS, tk), lambda bi, g, i: (bi, g, 0, 0)))
        args += [k, vt]
    return pl.pallas_call(
        functools.partial(_attn_kernel, n_seg=len(segs), kv_chunk=ATTN_KV_CHUNK),
        out_shape=jax.ShapeDtypeStruct((b, t, Q_WIDTH), BF16),
        grid=(b, N_KV_HEADS, t // tq),
        in_specs=in_specs,
        out_specs=pl.BlockSpec((1, tq, gw), lambda bi, g, i: (bi, i, g)),
        compiler_params=_cparams("parallel", "parallel", "arbitrary"),
        name="l0_attention",
    )(*args)


def _kv_layout(k, v):
    b, t, _ = k.shape
    vt = v.reshape(b, t, N_KV_HEADS, HEAD_DIM).transpose(0, 2, 3, 1)
    extra = jnp.zeros((b, N_KV_HEADS, ATTN_VT_ROWS - HEAD_DIM, t), v.dtype).at[:, :, 0].set(1)
    return k, jnp.concatenate([vt, extra], axis=2)


def _resmm_kernel(x_ref, g_ref, *refs, n):
    acc = None
    for i in range(n):
        d = _dot(refs[i][0], refs[n + i][...])
        acc = d if acc is None else acc + d
    o_ref = refs[2 * n]
    o_ref[0] = x_ref[0] + g_ref[0] * acc


def _residual_matmul(x, gate, acts, weights):
    b, t, _ = x.shape
    tm = min(512, t)
    n = len(acts)
    tok = lambda w: pl.BlockSpec((1, tm, w), lambda bi, i: (bi, i, 0))
    in_specs = [tok(D_MODEL), pl.BlockSpec((1, 1, D_MODEL), lambda bi, i: (bi, 0, 0))]
    in_specs += [tok(a.shape[-1]) for a in acts]
    in_specs += [pl.BlockSpec(w.shape, lambda bi, i: (0, 0)) for w in weights]
    return pl.pallas_call(
        functools.partial(_resmm_kernel, n=n),
        out_shape=jax.ShapeDtypeStruct((b, t, D_MODEL), F32),
        grid=(b, t // tm),
        in_specs=in_specs,
        out_specs=tok(D_MODEL),
        compiler_params=_cparams("parallel", "arbitrary"),
        name="residual_proj",
    )(x, gate, *acts, *weights)


def _router_kernel(x_ref, sc_ref, sh_ref, gain_ref, wr_ref, h_ref, aff_ref):
    h = _norm_mod(x_ref[0], gain_ref[...], sc_ref[0], sh_ref[0])
    tm = h.shape[0]
    logits = _dot_f32(h, wr_ref[...], 2)
    lane = lax.broadcasted_iota(I32, (tm, ROUTE_PAD), 1)
    z = jnp.where(lane < N_EXPERTS, logits, -jnp.inf)
    e = jnp.exp(z - z.max(axis=1, keepdims=True))
    aff = e / e.sum(axis=1, keepdims=True)
    h_ref[0, :, :D_MODEL] = h.astype(BF16)
    hi = aff.astype(BF16).astype(F32)
    h_ref[0, :, D_MODEL:] = (hi + pltpu.roll(aff - hi, N_EXPERTS, 1)).astype(BF16)
    aff_ref[0] = aff.T[:N_EXPERTS]


def _router(x, sc, sh, gain, w_router):
    b, t, _ = x.shape
    tm = min(512, t)
    wr = jnp.pad(w_router, ((0, 0), (0, ROUTE_PAD - N_EXPERTS)))
    tok = lambda w: pl.BlockSpec((1, tm, w), lambda bi, i: (bi, i, 0))
    per_b = pl.BlockSpec((1, 1, D_MODEL), lambda bi, i: (bi, 0, 0))
    return pl.pallas_call(
        _router_kernel,
        out_shape=(jax.ShapeDtypeStruct((b, t, D_MODEL + ROUTE_PAD), BF16),
                   jax.ShapeDtypeStruct((b, N_EXPERTS, t), F32)),
        grid=(b, t // tm),
        in_specs=[tok(D_MODEL), per_b, per_b,
                  pl.BlockSpec((1, D_MODEL), lambda bi, i: (0, 0)),
                  pl.BlockSpec((D_MODEL, ROUTE_PAD), lambda bi, i: (0, 0))],
        out_specs=(tok(D_MODEL + ROUTE_PAD), pl.BlockSpec((1, N_EXPERTS, tm), lambda bi, i: (bi, 0, i))),
        compiler_params=_cparams("parallel", "arbitrary"),
        name="moe_router",
    )(x, sc, sh, gain.reshape(1, D_MODEL), wr)


def _cumsum_lanes(x01, tri):
    outs = []
    off = jnp.zeros((x01.shape[0], 1), F32)
    for j in range(x01.shape[1] // LANES):
        cj = _dot(x01[:, j * LANES:(j + 1) * LANES].astype(BF16), tri) + off
        outs.append(cj)
        off = cj[:, LANES - 1:LANES]
    return outs[0] if len(outs) == 1 else jnp.concatenate(outs, axis=1)


def _select_kernel(aff_ref, tri_ref, tile_ref, pos_ref, tab_ref, *, cap):
    aff = aff_ref[0]
    n = aff.shape[1]
    keys = pltpu.bitcast(aff, I32)

    def bit_step(i, tau):
        cand = tau | jnp.left_shift(jnp.int32(1), 30 - i)
        cnt = jnp.sum((keys >= cand).astype(I32), axis=1, keepdims=True)
        return jnp.where(cnt >= cap, cand, tau)

    tau = lax.fori_loop(0, 31, bit_step, jnp.zeros((N_EXPERTS, 1), I32))
    gt = keys > tau
    eq = keys == tau
    need = (cap - jnp.sum(gt.astype(I32), axis=1, keepdims=True)).astype(F32)
    tri = tri_ref[...]
    c_eq = _cumsum_lanes(eq.astype(F32), tri)
    sel = gt | (eq & (c_eq <= need))
    c_sel = _cumsum_lanes(sel.astype(F32), tri)
    pos_ref[0] = jnp.where(sel, c_sel - 1.0, -1.0)
    tab_ref[0] = _dot(sel.astype(BF16), tile_ref[...]).astype(I32)


def _select(aff_t, cap):
    b, _, n = aff_t.shape
    assert n % MOE_TILE == 0 and n // MOE_TILE <= LANES // 2
    tri = jnp.asarray(np.triu(np.ones((LANES, LANES), np.float32)), dtype=BF16)
    tok = np.arange(n)[:, None]
    j = np.arange(LANES // 2)[None, :]
    tile_tab = jnp.asarray(np.concatenate([tok < j * MOE_TILE, tok // MOE_TILE == j], axis=1), dtype=BF16)
    return pl.pallas_call(
        functools.partial(_select_kernel, cap=cap),
        out_shape=(jax.ShapeDtypeStruct((b, N_EXPERTS, n), F32),
                   jax.ShapeDtypeStruct((b, N_EXPERTS, LANES), I32)),
        grid=(b,),
        in_specs=[pl.BlockSpec((1, N_EXPERTS, n), lambda bi: (bi, 0, 0)),
                  pl.BlockSpec((LANES, LANES), lambda bi: (0, 0)),
                  pl.BlockSpec((n, LANES), lambda bi: (0, 0))],
        out_specs=(pl.BlockSpec((1, N_EXPERTS, n), lambda bi: (bi, 0, 0)),
                   pl.BlockSpec((1, N_EXPERTS, LANES), lambda bi: (bi, 0, 0))),
        compiler_params=_cparams("parallel"),
        name="moe_select",
    )(aff_t, tri, tile_tab)


def _tile_windows(tab_ref, pos_ref, j, win):
    slot = lax.broadcasted_iota(I32, (win, MOE_TILE), 0).astype(F32)
    out = []
    for e in range(N_EXPERTS):
        start = tab_ref[0, e, j]
        count = tab_ref[0, e, LANES // 2 + j]
        a0 = pl.multiple_of(lax.shift_left(lax.shift_right_logical(start, MOE_ALIGN_LOG2), MOE_ALIGN_LOG2),
                            1 << MOE_ALIGN_LOG2)
        p = pos_ref[0, e:e + 1, :]
        onehot = lambda first, p=p: (p == slot + jnp.asarray(first, F32)).astype(BF16)
        extra = jnp.maximum(start - a0 + count - 1, 0) // win
        out.append((a0, onehot, extra))
    return out


def _moe_gather_kernel(tab_ref, h_ref, pos_ref, o_ref, *, win):
    j = pl.program_id(2)

    @pl.when(j == 0)
    def _():
        o_ref[...] = jnp.zeros(o_ref.shape, o_ref.dtype)

    h = h_ref[0]
    wins = _tile_windows(tab_ref, pos_ref, j, win)
    x = _dot(jnp.concatenate([oh(a0) for a0, oh, _ in wins], axis=0), h).astype(BF16)
    for e, (a0, oh, extra) in enumerate(wins):
        o_ref[0, e, pl.ds(a0, win), :] = o_ref[0, e, pl.ds(a0, win), :] + x[e * win:(e + 1) * win]

        def more(k, c, e=e, a0=a0, oh=oh):
            ak = pl.multiple_of(a0 + (k + 1) * win, 1 << MOE_ALIGN_LOG2)
            o_ref[0, e, pl.ds(ak, win), :] = o_ref[0, e, pl.ds(ak, win), :] + _dot(oh(ak), h).astype(BF16)
            return c

        lax.fori_loop(0, extra, more, 0)


def _moe_gather(h_ext, pos, tab, cap):
    b, n, width = h_ext.shape
    cols = width // MOE_COL_SPLIT
    rows = cap + MOE_SLACK
    assert MOE_SLACK >= MOE_WIN and MOE_WIN % (1 << MOE_ALIGN_LOG2) == 0 and cols % LANES == 0
    return pl.pallas_call(
        functools.partial(_moe_gather_kernel, win=MOE_WIN),
        out_shape=jax.ShapeDtypeStruct((b, N_EXPERTS, rows, width), BF16),
        grid=(b, MOE_COL_SPLIT, n // MOE_TILE),
        in_specs=[pl.BlockSpec((1, N_EXPERTS, LANES), lambda bi, c, j: (bi, 0, 0), memory_space=pltpu.SMEM),
                  pl.BlockSpec((1, MOE_TILE, cols), lambda bi, c, j: (bi, j, c)),
                  pl.BlockSpec((1, N_EXPERTS, MOE_TILE), lambda bi, c, j: (bi, 0, j))],
        out_specs=pl.BlockSpec((1, N_EXPERTS, rows, cols), lambda bi, c, j: (bi, 0, 0, c)),
        compiler_params=_cparams("parallel", "parallel", "arbitrary"),
        name="moe_gather",
    )(tab, h_ext, pos)


def _moe_ffn_kernel(x_ref, wg_ref, wu_ref, wd_ref, y_ref, acc_s, *, cap):
    e = pl.program_id(0)
    f = pl.program_id(2)
    nb = x_ref.shape[0]
    x = x_ref[:, 0].reshape(nb * cap, x_ref.shape[3])
    xb = x[:, :D_MODEL]

    @pl.when(f == 0)
    def _():
        acc_s[...] = jnp.zeros(acc_s.shape, F32)

    g = _dot(xb, wg_ref[0].astype(BF16))
    u = _dot(xb, wu_ref[0].astype(BF16))
    hid = (g * jax.nn.sigmoid(g)) * u
    acc_s[...] += _dot(hid.astype(BF16), wd_ref[0].astype(BF16))

    @pl.when(f == pl.num_programs(2) - 1)
    def _():
        lane = lax.broadcasted_iota(I32, (nb * cap, ROUTE_PAD), 1)
        mine = (lane == e) | (lane == e + N_EXPERTS)
        gate = jnp.sum(jnp.where(mine, x[:, D_MODEL:].astype(F32), 0.0), axis=1, keepdims=True)
        y_ref[:, 0, :cap] = (acc_s[...] * gate).astype(y_ref.dtype).reshape(nb, cap, D_MODEL)
        slack = y_ref.shape[2] - cap
        y_ref[:, 0, cap:] = jnp.zeros((nb, slack, D_MODEL), y_ref.dtype)


def _moe_ffn(xe, wg, wu, wd, cap):
    b, _, rows, width = xe.shape
    nb = b if b * cap <= MOE_FFN_ROWS else 1
    fc = MOE_FFN_SLICE
    return pl.pallas_call(
        functools.partial(_moe_ffn_kernel, cap=cap),
        out_shape=jax.ShapeDtypeStruct((b, N_EXPERTS, rows, D_MODEL), BF16),
        grid=(N_EXPERTS, b // nb, EXPERT_FF // fc),
        in_specs=[pl.BlockSpec((nb, 1, cap, width), lambda e, bi, f: (bi, e, 0, 0)),
                  pl.BlockSpec((1, D_MODEL, fc), lambda e, bi, f: (e, 0, f)),
                  pl.BlockSpec((1, D_MODEL, fc), lambda e, bi, f: (e, 0, f)),
                  pl.BlockSpec((1, fc, D_MODEL), lambda e, bi, f: (e, f, 0))],
        out_specs=pl.BlockSpec((nb, 1, rows, D_MODEL), lambda e, bi, f: (bi, e, 0, 0)),
        scratch_shapes=[pltpu.VMEM((nb * cap, D_MODEL), F32)],
        compiler_params=_cparams("arbitrary", "arbitrary", "arbitrary"),
        name="moe_experts",
    )(xe, wg, wu, wd)


def _moe_combine_kernel(tab_ref, x_ref, gt_ref, pos_ref, y_ref, o_ref, acc_s, *, win):
    j = pl.program_id(2)
    wins = _tile_windows(tab_ref, pos_ref, j, win)
    onehots = jnp.concatenate([oh(a0) for a0, oh, _ in wins], axis=0)
    rows = jnp.concatenate([y_ref[0, e, pl.ds(a0, win), :] for e, (a0, _, _) in enumerate(wins)], axis=0)
    acc_s[...] = _dot_tn(onehots, rows)
    for e, (a0, oh, extra) in enumerate(wins):
        def more(k, c, e=e, a0=a0, oh=oh):
            ak = pl.multiple_of(a0 + (k + 1) * win, 1 << MOE_ALIGN_LOG2)
            acc_s[...] += _dot_tn(oh(ak), y_ref[0, e, pl.ds(ak, win), :])
            return c

        lax.fori_loop(0, extra, more, 0)
    o_ref[0] = x_ref[0] + gt_ref[0] * acc_s[...]


def _moe_combine(x, gt, pos, tab, y):
    b, n, _ = x.shape
    rows = y.shape[2]
    cols = D_MODEL // 2
    tile = lambda: pl.BlockSpec((1, MOE_TILE, cols), lambda bi, c, j: (bi, j, c))
    return pl.pallas_call(
        functools.partial(_moe_combine_kernel, win=MOE_WIN),
        out_shape=jax.ShapeDtypeStruct((b, n, D_MODEL), F32),
        grid=(b, 2, n // MOE_TILE),
        in_specs=[pl.BlockSpec((1, N_EXPERTS, LANES), lambda bi, c, j: (bi, 0, 0), memory_space=pltpu.SMEM),
                  tile(),
                  pl.BlockSpec((1, 1, cols), lambda bi, c, j: (bi, 0, c)),
                  pl.BlockSpec((1, N_EXPERTS, MOE_TILE), lambda bi, c, j: (bi, 0, j)),
                  pl.BlockSpec((1, N_EXPERTS, rows, cols), lambda bi, c, j: (bi, 0, 0, c))],
        out_specs=tile(),
        scratch_shapes=[pltpu.VMEM((MOE_TILE, cols), F32)],
        compiler_params=_cparams("parallel", "parallel", "arbitrary"),
        name="moe_combine",
    )(tab, x, gt, pos, y)


def _moe(x, sc, sh, gt, gain, w_router, wg, wu, wd):
    n = x.shape[1]
    cap = max(1, CAPACITY_FACTOR * n // N_EXPERTS)
    h_ext, aff_t = _router(x, sc, sh, gain, w_router)
    pos, tab = _select(aff_t, cap)
    y = _moe_ffn(_moe_gather(h_ext, pos, tab, cap), wg, wu, wd, cap)
    return _moe_combine(x, gt, pos, tab, y)


def _rwkv_prep_kernel(xm_ref, xp_ref, xn_ref, sc_ref, sh_ref, gain_ref, mu_ref, wr_ref, wk_ref, wv_ref,
                      w1_ref, w2_ref, a1_ref, a2_ref, g1_ref, g2_ref, w0_ref, a0_ref, kks_ref, ka_ref, rk_ref,
                      seg_ref, r_ref, v_ref, g_ref, kk_ref, bonus_ref, lw0_ref, lw1_ref, kd0_ref, kd1_ref,
                      ag0_ref, ag1_ref):
    i = pl.program_id(1)
    last = pl.num_programs(1) - 1
    tm = xm_ref.shape[1]
    gain = gain_ref[...]
    sc = sc_ref[0]
    sh = sh_ref[0]
    xe = jnp.concatenate([xp_ref[0], xm_ref[0], xn_ref[0]], axis=0)
    he = _norm_mod(xe, gain, sc, sh)
    row = lax.broadcasted_iota(I32, (tm, 1), 0)
    h = he[SUBLANES:SUBLANES + tm]
    hm1 = jnp.where((row == 0) & (i == 0), 0.0, he[SUBLANES - 1:SUBLANES - 1 + tm])
    hp1 = jnp.where((row == tm - 1) & (i == last), 0.0, he[SUBLANES + 1:SUBLANES + 1 + tm])
    xx = 0.5 * (hm1 + hp1) - h
    mu = mu_ref[...]
    mix = lambda j: (h + xx * mu[j:j + 1]).astype(BF16)
    r = _dot(mix(0), wr_ref[...])
    k = _dot(mix(2), wk_ref[...])
    v = _dot(mix(3), wv_ref[...])
    tw = jnp.tanh(_dot(mix(1), w1_ref[...])).astype(BF16)
    la = _dot(mix(4), a1_ref[...]).astype(BF16)
    g = _dot(jax.nn.sigmoid(_dot(mix(5), g1_ref[...])).astype(BF16), g2_ref[...])
    seg = seg_ref[...]
    kkv = k * kks_ref[...]
    kk = kkv * lax.rsqrt(jnp.maximum(_head_sum(kkv * kkv, seg, 1), 1e-24))
    r_ref[0] = r.astype(r_ref.dtype)
    v_ref[0] = v.astype(v_ref.dtype)
    g_ref[0] = g.astype(g_ref.dtype)
    kk_ref[0] = kk.astype(kk_ref.dtype)
    bonus = None
    for d, (lw_ref, kd_ref, ag_ref) in enumerate(((lw0_ref, kd0_ref, ag0_ref), (lw1_ref, kd1_ref, ag1_ref))):
        w_pre = w0_ref[d] + _dot(tw, w2_ref[d])
        lw_ref[0] = jax.nn.sigmoid(w_pre) * np.float32(-np.exp(-0.5))
        a = jax.nn.sigmoid(a0_ref[d] + _dot(la, a2_ref[d]))
        kd = k * (1.0 + (a - 1.0) * ka_ref[...])
        kd_ref[0] = kd.astype(kd_ref.dtype)
        ag_ref[0] = a.astype(ag_ref.dtype)
        bd = _head_sum(r * kd * rk_ref[...], seg, 1) * v
        bonus = bd if bonus is None else bonus + bd
    bonus_ref[0] = bonus


def _rwkv_prep(x, sc, sh, gain, p):
    b, t, _ = x.shape
    tm = min(256, t)
    nb8 = tm // SUBLANES
    tok = lambda: pl.BlockSpec((1, tm, D_MODEL), lambda bi, i: (bi, i, 0))
    prev = pl.BlockSpec((1, SUBLANES, D_MODEL), lambda bi, i: (bi, jnp.maximum(i * nb8 - 1, 0), 0))
    nxt = pl.BlockSpec((1, SUBLANES, D_MODEL),
                       lambda bi, i: (bi, jnp.minimum((i + 1) * nb8, t // SUBLANES - 1), 0))
    per_b = pl.BlockSpec((1, 1, D_MODEL), lambda bi, i: (bi, 0, 0))
    const = lambda a: pl.BlockSpec(a.shape, lambda bi, i: (0,) * a.ndim)
    consts = [p['gain'], p['mu'], p['w_r'], p['w_k'], p['w_v'], p['w1'], p['w2'], p['a1'], p['a2'], p['g1'],
              p['g2'], p['w0'], p['a0'], p['kks'], p['ka'], p['rk'], p['seg']]
    bf = jax.ShapeDtypeStruct((b, t, D_MODEL), BF16)
    f32 = jax.ShapeDtypeStruct((b, t, D_MODEL), F32)
    return pl.pallas_call(
        _rwkv_prep_kernel,
        out_shape=(bf, bf, bf, bf, f32, f32, f32, bf, bf, bf, bf),
        grid=(b, t // tm),
        in_specs=[tok(), prev, nxt, per_b, per_b] + [const(a) for a in consts],
        out_specs=tuple(tok() for _ in range(11)),
        compiler_params=_cparams("parallel", "arbitrary"),
        name="l1_rwkv_prep",
    )(x, x, x, sc, sh, *consts)


def _rwkv_params(gain, mu, w_r, w_k, w_v, w0, w1, w2, a0, a1, a2, g1, g2, k_k, k_a, r_k):
    def pad_dir(w):
        z = jnp.zeros_like(w[0])
        return jnp.stack([jnp.concatenate([w[0], z], axis=0), jnp.concatenate([z, w[1]], axis=0)])

    row = lambda a: a.reshape(1, D_MODEL)
    return dict(
        gain=row(gain), mu=mu, w_r=w_r.astype(BF16), w_k=w_k.astype(BF16), w_v=w_v.astype(BF16),
        w1=jnp.concatenate([w1[0], w1[1]], axis=1).astype(BF16), w2=pad_dir(w2).astype(BF16),
        a1=jnp.concatenate([a1[0], a1[1]], axis=1).astype(BF16), a2=pad_dir(a2).astype(BF16),
        g1=jnp.pad(g1, ((0, 0), (0, GATE_LORA_PAD - GATE_LORA))).astype(BF16),
        g2=jnp.pad(g2, ((0, GATE_LORA_PAD - GATE_LORA), (0, 0))).astype(BF16),
        w0=w0.reshape(2, 1, D_MODEL), a0=a0.reshape(2, 1, D_MODEL),
        kks=row(k_k), ka=row(k_a), rk=row(r_k), seg=_seg_ones(LANES))


SOLVE_BASE = 8
WKV_SUB_CHUNKS = 4


def _solve_masks(n, top):
    rowi = lax.broadcasted_iota(I32, (n, n), 0)
    coli = lax.broadcasted_iota(I32, (n, n), 1)
    same = lambda shift: (rowi >> shift) == (coli >> shift)
    k = int(np.log2(SOLVE_BASE))
    levels = range(k, int(np.log2(top)))
    return (jnp.where(rowi == coli, 1.0, 0.0), same(k),
            [same(j + 1) & jnp.logical_not(same(j)) for j in levels])


def _solve_unit_tri(nmat, x, masks):
    eye, base, joins = masks
    m = jnp.where(base, nmat, 0.0).astype(BF16)
    tinv = eye + m.astype(F32)
    for _ in range(int(np.log2(SOLVE_BASE)) - 1):
        m = _dot(m, m).astype(BF16)
        yield None
        tinv = tinv + _dot(m, tinv.astype(BF16))
        yield None
    for join in joins:
        off = jnp.where(join, nmat, 0.0).astype(BF16)
        tb = tinv.astype(BF16)
        half = _dot(tb, off).astype(BF16)
        yield None
        tinv = tinv + _dot(half, tb)
        yield None
    yield _dot(tinv.astype(BF16), x.astype(BF16))


def _wkv_stage(r_ref, v_ref, kk_ref, lw_ref, kd_ref, ag_ref, tri, ws, zs, vs, gts, d, rev, chunk, rows, k):
    lw = lw_ref[0, rows]
    hi, lo = _split2(lw)
    cum = _dot(tri, hi) + _dot(tri, lo)
    g_in = jnp.exp(cum)
    g_ex = jnp.exp(cum - lw)
    g_inv = jnp.exp(-cum)
    kk = kk_ref[0, rows].astype(F32)
    a_t = -(kk * g_ex)
    r_t = r_ref[0, rows].astype(F32) * g_in
    b_t = kk * ag_ref[0, rows].astype(F32) * g_inv
    k_t = kd_ref[0, rows].astype(F32) * g_inv
    g_tot = jnp.exp(cum[0:1] if rev else cum[chunk - 1:chunk])
    first = (lax.broadcasted_iota(I32, (1, D_MODEL), 1) % LANES) < RWKV_HEAD
    stacked = lambda x: (jnp.where(first, x, jnp.zeros_like(x)), jnp.where(first, jnp.zeros_like(x), x))
    a2, r2, b2, k2 = (stacked(x.astype(BF16)) for x in (a_t, r_t, b_t, k_t))
    v2 = stacked(v_ref[0, rows])
    c2 = 2 * chunk
    for p in range(HEAD_PAIRS):
        sl = slice(p * LANES, (p + 1) * LANES)
        for h in range(2):
            ws[k, d, p, h * chunk:(h + 1) * chunk] = a2[h][:, sl]
            ws[k, d, p, c2 + h * chunk:c2 + (h + 1) * chunk] = r2[h][:, sl]
            zs[k, d, p, h * chunk:(h + 1) * chunk] = b2[h][:, sl]
            zs[k, d, p, c2 + h * chunk:c2 + (h + 1) * chunk] = k2[h][:, sl]
            vs[k, d, p, h * chunk:(h + 1) * chunk] = v2[h][:, sl]
        gts[k, d, p] = g_tot[:, sl]


def _wkv_masks(chunk, rev):
    n = 2 * chunk
    ti = lax.broadcasted_iota(I32, (n, 2 * n), 0) & (chunk - 1)
    tj = lax.broadcasted_iota(I32, (n, 2 * n), 1) & (chunk - 1)
    strict = ti < tj if rev else ti > tj
    incl = ti <= tj if rev else ti >= tj
    return strict, incl, _solve_masks(n, chunk)


def _wkv_chain(s_ref, y_ref, ws, zs, vs, gts, d, p, masks, chunk, rows, k):
    n = 2 * chunk
    w = ws[k, d, p]
    z = zs[k, d, p]
    v = vs[k, d, p]
    s_old = s_ref[d, p]
    pm = _dot_nt(w, z)
    yield
    wst = _dot_nt(w, s_old.astype(BF16))
    yield
    strict, incl, solve_masks = masks
    top = jnp.where(strict, pm[:n], 0.0)
    nmat = top[:, :n]
    ak = top[:, n:].astype(BF16)
    rbk = jnp.where(incl, pm[n:], 0.0).astype(BF16)
    x = wst[:n] + _dot(ak, v)
    yield
    u = None
    for u in _solve_unit_tri(nmat, x, solve_masks):
        yield
    uv = jnp.concatenate([u.astype(BF16), v], axis=0)
    y = wst[n:] + _dot(rbk, uv)
    yield
    if y_ref is not None:
        y_ref[0, rows, p * LANES:(p + 1) * LANES] = y[:chunk] + y[chunk:]
    ds = _dot_tn(uv, z)
    yield
    s_ref[d, p] = (s_old + ds) * gts[k, d, p]


def _wkv_kernel(*refs, chunk, emit_y, has_init):
    ins = refs[:12]
    tri_ref = refs[12]
    pos = 13
    if has_init:
        s0_ref = refs[pos]
        pos += 1
    yf_ref = yb_ref = None
    if emit_y:
        yf_ref, yb_ref = refs[pos:pos + 2]
        pos += 2
    sfin_ref = refs[pos]
    s_ref, ws, zs, vs, gts = refs[pos + 1:]
    i = pl.program_id(1)

    @pl.when(i == 0)
    def _():
        if has_init:
            s_ref[...] = s0_ref[0]
        else:
            s_ref[...] = jnp.zeros(s_ref.shape, F32)

    n_sub = ins[0].shape[1] // chunk

    def rows_of(d, k):
        j = n_sub - 1 - k if d else k
        return slice(j * chunk, (j + 1) * chunk)

    def stage(k):
        for d, rev in ((0, False), (1, True)):
            _wkv_stage(*ins[6 * d:6 * d + 6], tri_ref[d], ws, zs, vs, gts, d, rev, chunk, rows_of(d, k), k)

    masks = [_wkv_masks(chunk, False), _wkv_masks(chunk, True)]
    stage(0)
    for k in range(n_sub):
        chains = [_wkv_chain(s_ref, (yf_ref, yb_ref)[d], ws, zs, vs, gts, d, p, masks[d], chunk, rows_of(d, k), k)
                  for p in range(HEAD_PAIRS) for d in range(2)]
        first_round = True
        while chains:
            alive = []
            for ch in chains:
                try:
                    next(ch)
                    alive.append(ch)
                except StopIteration:
                    pass
            chains = alive
            if first_round and k + 1 < n_sub:
                stage(k + 1)
            first_round = False

    @pl.when(i == pl.num_programs(1) - 1)
    def _():
        sfin_ref[0] = s_ref[...]


def _wkv(r, v, kk, lw, kd, ag, s0, emit_y, chunk=64):
    b, t, _ = r.shape
    n_sub = min(WKV_SUB_CHUNKS, t // chunk)
    n = t // (n_sub * chunk)
    assert chunk == RWKV_HEAD and t == n * n_sub * chunk
    fwd = pl.BlockSpec((1, n_sub * chunk, D_MODEL), lambda bi, i: (bi, i, 0))
    bwd = pl.BlockSpec((1, n_sub * chunk, D_MODEL), lambda bi, i: (bi, n - 1 - i, 0))
    tri = jnp.asarray(np.stack([np.tril(np.ones((chunk, chunk), np.float32)),
                                np.triu(np.ones((chunk, chunk), np.float32))]), dtype=BF16)
    state_shape = (2, HEAD_PAIRS, 2 * RWKV_HEAD, LANES)
    state_spec = pl.BlockSpec((1,) + state_shape, lambda bi, i: (bi, 0, 0, 0, 0))
    args = [r, v, kk, lw[0], kd[0], ag[0], r, v, kk, lw[1], kd[1], ag[1], tri]
    in_specs = [fwd] * 6 + [bwd] * 6 + [pl.BlockSpec((2, chunk, chunk), lambda bi, i: (0, 0, 0))]
    if s0 is not None:
        args.append(s0)
        in_specs.append(state_spec)
    out_shape = [jax.ShapeDtypeStruct((b,) + state_shape, F32)]
    out_specs = [state_spec]
    if emit_y:
        out_shape = [jax.ShapeDtypeStruct((b, t, D_MODEL), F32)] * 2 + out_shape
        out_specs = [fwd, bwd] + out_specs
    res = pl.pallas_call(
        functools.partial(_wkv_kernel, chunk=chunk, emit_y=emit_y, has_init=s0 is not None),
        out_shape=tuple(out_shape),
        grid=(b, n),
        in_specs=in_specs,
        out_specs=tuple(out_specs),
        scratch_shapes=[pltpu.VMEM(state_shape, F32),
                        pltpu.VMEM((n_sub, 2, HEAD_PAIRS, 4 * chunk, LANES), BF16),
                        pltpu.VMEM((n_sub, 2, HEAD_PAIRS, 4 * chunk, LANES), BF16),
                        pltpu.VMEM((n_sub, 2, HEAD_PAIRS, 2 * chunk, LANES), BF16),
                        pltpu.VMEM((n_sub, 2, HEAD_PAIRS, 1, LANES), F32)],
        compiler_params=_cparams("parallel", "arbitrary"),
        name="l1_wkv_scan",
    )(*args)
    return res


def _readout_kernel(x_ref, gt_ref, yf_ref, yb_ref, bonus_ref, g_ref, gng_ref, gnb_ref, seg_ref, wo_ref, o_ref):
    seg = seg_ref[...]
    y = yf_ref[0] + yb_ref[0]
    mean = _head_sum(y, seg) * (1.0 / RWKV_HEAD)
    c = y - mean
    var = _head_sum(c * c, seg, 1) * (1.0 / RWKV_HEAD)
    yn = c * lax.rsqrt(var + GN_EPS) * gng_ref[...] + gnb_ref[...]
    out = (yn + bonus_ref[0]) * g_ref[0].astype(F32)
    o_ref[0] = x_ref[0] + gt_ref[0] * _dot(out.astype(BF16), wo_ref[...])


def _readout(x, gt, y_f, y_b, bonus, g, gn_g, gn_b, w_o):
    b, t, _ = x.shape
    tm = min(256, t)
    tok = pl.BlockSpec((1, tm, D_MODEL), lambda bi, i: (bi, i, 0))
    row = pl.BlockSpec((1, D_MODEL), lambda bi, i: (0, 0))
    return pl.pallas_call(
        _readout_kernel,
        out_shape=jax.ShapeDtypeStruct((b, t, D_MODEL), F32),
        grid=(b, t // tm),
        in_specs=[tok, pl.BlockSpec((1, 1, D_MODEL), lambda bi, i: (bi, 0, 0)), tok, tok, tok, tok, row, row,
                  pl.BlockSpec((LANES, LANES), lambda bi, i: (0, 0)),
                  pl.BlockSpec((D_MODEL, D_MODEL), lambda bi, i: (0, 0))],
        out_specs=tok,
        compiler_params=_cparams("parallel", "arbitrary"),
        name="l1_readout",
    )(x, gt, y_f, y_b, bonus, g, gn_g.reshape(1, D_MODEL), gn_b.reshape(1, D_MODEL), _seg_ones(LANES), w_o)


def _modulation(c, c_ctx, w_mod, b_mod):
    b = c.shape[0]
    cond = jnp.zeros((SUBLANES, D_MODEL), F32).at[:b].set(c).at[b].set(c_ctx)
    m = _ada(cond, w_mod, b_mod)
    lat = [m[:b, j * D_MODEL:(j + 1) * D_MODEL].reshape(b, 1, D_MODEL) for j in range(6)]
    ctx = [jnp.broadcast_to(m[b, j * D_MODEL:(j + 1) * D_MODEL].reshape(1, 1, D_MODEL), (b, 1, D_MODEL))
           for j in range(6)]
    return lat, ctx


def kernel(x, c, ctx, c_ctx, l0_w_mod, l0_b_mod, l0_norm_mix, l0_norm_ffn, l0_w_in, l0_conv_w, l0_conv_b, l0_lru_wi, l0_lru_bi, l0_lru_wr, l0_lru_br, l0_lru_lam, l0_q_gain, l0_k_gain, l0_w_out, l0_router, l0_we_gate, l0_we_up, l0_we_down, l1_w_mod, l1_b_mod, l1_norm_mix, l1_norm_ffn, l1_mu, l1_w_r, l1_w_k, l1_w_v, l1_w0, l1_w1, l1_w2, l1_a0, l1_a1, l1_a2, l1_g1, l1_g2, l1_k_k, l1_k_a, l1_r_k, l1_gn_g, l1_gn_b, l1_w_o, l1_router, l1_we_gate, l1_we_up, l1_we_down):
    t = x.shape[1]

    (sh1, sc1, gt1, sh2, sc2, gt2), (csh1, csc1, cgt1, csh2, csc2, cgt2) = _modulation(c, c_ctx, l0_w_mod, l0_b_mod)
    w_in = l0_w_in.astype(BF16)
    cos_t, sin_t = _rope_tables(t)
    xa_l, ga_l, q_l, k_l, v_l = _proj0(x, sc1, sh1, l0_norm_mix, w_in, l0_q_gain, l0_k_gain, cos_t, sin_t, True)
    lc = ctx.shape[1]
    xa_c, ga_c, q_c, k_c, v_c = _proj0(ctx, csc1, csh1, l0_norm_mix, w_in, l0_q_gain, l0_k_gain,
                                       cos_t[:lc], sin_t[:lc], False)
    ya_c, ya_l = _rglru(xa_c, ga_c, xa_l, ga_l, l0_conv_w, l0_conv_b, l0_lru_wi, l0_lru_bi, l0_lru_wr,
                        l0_lru_br, l0_lru_lam)
    seg_l = _kv_layout(k_l, v_l)
    seg_c = _kv_layout(k_c, v_c)
    yb_l = _attention(q_l, [seg_l, seg_c])
    yb_c = _attention(q_c, [seg_c])
    w_out = l0_w_out.astype(BF16)
    w_oa, w_ob = w_out[:LRU_WIDTH], w_out[LRU_WIDTH:]
    x = _residual_matmul(x, gt1, [ya_l, yb_l], [w_oa, w_ob])
    ctx = _residual_matmul(ctx, cgt1, [ya_c, yb_c], [w_oa, w_ob])
    x = _moe(x, sc2, sh2, gt2, l0_norm_ffn, l0_router, l0_we_gate, l0_we_up, l0_we_down)
    ctx = _moe(ctx, csc2, csh2, cgt2, l0_norm_ffn, l0_router, l0_we_gate, l0_we_up, l0_we_down)

    (sh1, sc1, gt1, sh2, sc2, gt2), (csh1, csc1, _, _, _, _) = _modulation(c, c_ctx, l1_w_mod, l1_b_mod)
    p = _rwkv_params(l1_norm_mix, l1_mu, l1_w_r, l1_w_k, l1_w_v, l1_w0, l1_w1, l1_w2, l1_a0, l1_a1, l1_a2,
                     l1_g1, l1_g2, l1_k_k, l1_k_a, l1_r_k)
    r_c, v_c, _, kk_c, _, lw0_c, lw1_c, kd0_c, kd1_c, ag0_c, ag1_c = _rwkv_prep(ctx, csc1, csh1, l1_norm_mix, p)
    r_l, v_l, g_l, kk_l, bonus_l, lw0, lw1, kd0, kd1, ag0, ag1 = _rwkv_prep(x, sc1, sh1, l1_norm_mix, p)
    (s_ctx,) = _wkv(r_c, v_c, kk_c, (lw0_c, lw1_c), (kd0_c, kd1_c), (ag0_c, ag1_c), None, False)
    y_f, y_b, _ = _wkv(r_l, v_l, kk_l, (lw0, lw1), (kd0, kd1), (ag0, ag1), s_ctx, True)
    x = _readout(x, gt1, y_f, y_b, bonus_l, g_l, l1_gn_g, l1_gn_b, l1_w_o.astype(BF16))
    x = _moe(x, sc2, sh2, gt2, l1_norm_ffn, l1_router, l1_we_gate, l1_we_up, l1_we_down)
    return x
```

```python
import functools

import jax
import jax.numpy as jnp
import numpy as np
from jax import lax
from jax.experimental import pallas as pl
from jax.experimental.pallas import tpu as pltpu

F32 = jnp.float32
BF16 = jnp.bfloat16
I32 = jnp.int32

D_MODEL = 1024
LANES = 128
SUBLANES = 8
GRID_W = 64
NORM_EPS = 1e-6
LRU_WIDTH = 512
LRU_BLOCKS = 8
LRU_BLOCK = LRU_WIDTH // LRU_BLOCKS
LRU_C = 8.0
CONV_W = 4
N_Q_HEADS = 8
N_KV_HEADS = 2
HEAD_DIM = 64
Q_GROUP = N_Q_HEADS // N_KV_HEADS
ROPE_BASE = 10000.0
ATTN_SCALE = HEAD_DIM ** -0.5
Q_PRESCALE = ATTN_SCALE * float(np.log2(np.e))
ATTN_KV_CHUNK = 1024
ATTN_Q_TILE = 128
ATTN_LOOKAHEAD = 2
ATTN_VT_ROWS = HEAD_DIM + 16
Q_WIDTH = N_Q_HEADS * HEAD_DIM
KV_WIDTH = N_KV_HEADS * HEAD_DIM
IN_WIDTH = 2 * LRU_WIDTH + Q_WIDTH + 2 * KV_WIDTH
RWKV_HEAD = 64
HEAD_PAIRS = D_MODEL // LANES
GATE_LORA = 160
GATE_LORA_PAD = 256
GN_EPS = 64e-5
N_EXPERTS = 16
EXPERT_FF = 2048
CAPACITY_FACTOR = 2
ROUTE_PAD = LANES
VMEM_LIMIT = 60 * 1024 * 1024
MOE_TILE = 256
MOE_ALIGN_LOG2 = 4
MOE_WIN = 48 + (1 << MOE_ALIGN_LOG2)
MOE_SLACK = 128
MOE_COL_SPLIT = 3
MOE_FFN_ROWS = 1024
MOE_FFN_SLICE = 1024


def _cparams(*sem):
    return pltpu.CompilerParams(dimension_semantics=sem, vmem_limit_bytes=VMEM_LIMIT)


def _dot(a, b):
    return jnp.dot(a, b, preferred_element_type=F32)


def _dot_nt(a, b):
    return lax.dot_general(a, b, (((1,), (1,)), ((), ())), preferred_element_type=F32)


def _dot_tn(a, b):
    return lax.dot_general(a, b, (((0,), (0,)), ((), ())), preferred_element_type=F32)


def _split2(x):
    hi = x.astype(BF16)
    lo = (x - hi.astype(F32)).astype(BF16)
    return hi, lo


def _split3(x):
    hi = x.astype(BF16)
    r = x - hi.astype(F32)
    mid = r.astype(BF16)
    lo = (r - mid.astype(F32)).astype(BF16)
    return hi, mid, lo


def _dot_f32(a, b, pieces=3):
    split = _split3 if pieces == 3 else _split2
    ap = split(a)
    bp = split(b)
    out = None
    for i in range(pieces):
        for j in range(pieces - i):
            d = _dot(ap[i], bp[j])
            out = d if out is None else out + d
    return out


def _softplus(x):
    return jnp.maximum(x, 0.0) + jnp.log1p(jnp.exp(-jnp.abs(x)))


def _gelu_tanh(x):
    c = np.float32(np.sqrt(2.0 / np.pi))
    return 0.5 * x * (1.0 + jnp.tanh(c * (x + 0.044715 * (x * x * x))))


def _norm_mod(x, gain, sc, sh):
    ms = jnp.mean(x * x, axis=-1, keepdims=True)
    y = x * lax.rsqrt(ms + NORM_EPS)
    return (y * gain) * (1.0 + sc) + sh


def _seg_ones(width):
    i = np.arange(width)[:, None] // HEAD_DIM
    j = np.arange(width)[None, :] // HEAD_DIM
    return jnp.asarray(i == j, dtype=BF16)


def _head_sum(x, seg, pieces=2):
    outs = []
    for c in range(x.shape[1] // LANES):
        xc = x[:, c * LANES:(c + 1) * LANES]
        if pieces == 1:
            outs.append(_dot(xc.astype(BF16), seg))
        else:
            hi, lo = _split2(xc)
            outs.append(_dot(hi, seg) + _dot(lo, seg))
    return outs[0] if len(outs) == 1 else jnp.concatenate(outs, axis=1)


def _ada_kernel(c_ref, w_ref, b_ref, o_ref):
    s = c_ref[...]
    s = s * jax.nn.sigmoid(s)
    o_ref[...] = _dot_f32(s, w_ref[...]) + b_ref[...]


def _ada(cond8, w_mod, b_mod):
    n = w_mod.shape[1]
    tn = 1536
    return pl.pallas_call(
        _ada_kernel,
        out_shape=jax.ShapeDtypeStruct((SUBLANES, n), F32),
        grid=(n // tn,),
        in_specs=[pl.BlockSpec((SUBLANES, D_MODEL), lambda j: (0, 0)),
                  pl.BlockSpec((D_MODEL, tn), lambda j: (0, j)),
                  pl.BlockSpec((1, tn), lambda j: (0, j))],
        out_specs=pl.BlockSpec((SUBLANES, tn), lambda j: (0, j)),
        compiler_params=_cparams("arbitrary"),
        name="ada_params",
    )(cond8, w_mod, b_mod.reshape(1, n))


def _swap_pairs(x):
    lane = lax.broadcasted_iota(I32, x.shape, 1)
    nxt = pltpu.roll(x, LANES - 1, 1)
    prv = pltpu.roll(x, 1, 1)
    return jnp.where(lane % 2 == 0, nxt, prv)


def _proj0_kernel(x_ref, sc_ref, sh_ref, gain_ref, w_ref, qg_ref, kg_ref, seg_ref, cos_ref, sin_ref,
                  xa_ref, ga_ref, q_ref, k_ref, v_ref, *, rope):
    h = _norm_mod(x_ref[0], gain_ref[...], sc_ref[0], sh_ref[0])
    res = _dot(h.astype(BF16), w_ref[...])
    xa_ref[0] = res[:, :LRU_WIDTH]
    ga_ref[0] = res[:, LRU_WIDTH:2 * LRU_WIDTH]
    q0 = 2 * LRU_WIDTH
    seg = seg_ref[...]

    def head_norm_rope(z, gain):
        ms = _head_sum(z * z, seg) * (1.0 / HEAD_DIM)
        zn = z * lax.rsqrt(ms + NORM_EPS) * gain
        if not rope:
            return zn
        c = cos_ref[...]
        s = sin_ref[...]
        outs = []
        for t in range(zn.shape[1] // LANES):
            zt = zn[:, t * LANES:(t + 1) * LANES]
            outs.append(zt * c + _swap_pairs(zt) * s)
        return outs[0] if len(outs) == 1 else jnp.concatenate(outs, axis=1)

    q = head_norm_rope(res[:, q0:q0 + Q_WIDTH], qg_ref[...])
    q_ref[0] = (q * Q_PRESCALE).astype(BF16)
    k = head_norm_rope(res[:, q0 + Q_WIDTH:q0 + Q_WIDTH + KV_WIDTH], kg_ref[...])
    k_ref[0] = k.astype(BF16)
    v_ref[0] = res[:, q0 + Q_WIDTH + KV_WIDTH:].astype(BF16)


def _proj0(x, sc, sh, gain, w_in, q_gain, k_gain, cos_t, sin_t, rope):
    b, t, _ = x.shape
    tm = min(512, t)
    qg = jnp.tile(q_gain, N_Q_HEADS).reshape(1, Q_WIDTH)
    kg = jnp.tile(k_gain, N_KV_HEADS).reshape(1, KV_WIDTH)
    seg = _seg_ones(LANES)
    const = lambda shape: pl.BlockSpec(shape, lambda bi, i: (0,) * len(shape))
    tok = lambda w: pl.BlockSpec((1, tm, w), lambda bi, i: (bi, i, 0))
    per_b = pl.BlockSpec((1, 1, D_MODEL), lambda bi, i: (bi, 0, 0))
    return pl.pallas_call(
        functools.partial(_proj0_kernel, rope=rope),
        out_shape=(jax.ShapeDtypeStruct((b, t, LRU_WIDTH), F32),
                   jax.ShapeDtypeStruct((b, t, LRU_WIDTH), F32),
                   jax.ShapeDtypeStruct((b, t, Q_WIDTH), BF16),
                   jax.ShapeDtypeStruct((b, t, KV_WIDTH), BF16),
                   jax.ShapeDtypeStruct((b, t, KV_WIDTH), BF16)),
        grid=(b, t // tm),
        in_specs=[tok(D_MODEL), per_b, per_b, const((1, D_MODEL)), const((D_MODEL, IN_WIDTH)),
                  const((1, Q_WIDTH)), const((1, KV_WIDTH)), const((LANES, LANES)),
                  pl.BlockSpec((tm, LANES), lambda bi, i: (i, 0)),
                  pl.BlockSpec((tm, LANES), lambda bi, i: (i, 0))],
        out_specs=(tok(LRU_WIDTH), tok(LRU_WIDTH), tok(Q_WIDTH), tok(KV_WIDTH), tok(KV_WIDTH)),
        compiler_params=_cparams("parallel", "arbitrary"),
        name="l0_in_proj",
    )(x, sc, sh, gain.reshape(1, D_MODEL), w_in, qg, kg, seg, cos_t, sin_t)


def _rope_tables(t):
    n_rows = t // GRID_W
    row = jnp.repeat(jnp.arange(n_rows, dtype=F32), GRID_W)
    col = jnp.tile(jnp.arange(GRID_W, dtype=F32), n_rows)
    axis_dim = HEAD_DIM // 2
    inv_freq = ROPE_BASE ** (-jnp.arange(0, axis_dim, 2, dtype=F32) / axis_dim)
    ang = jnp.concatenate([row[:, None] * inv_freq, col[:, None] * inv_freq], axis=-1)
    cos = jnp.repeat(jnp.cos(ang), 2, axis=-1)
    sin = jnp.repeat(jnp.sin(ang), 2, axis=-1)
    sign = jnp.tile(jnp.asarray([-1.0, 1.0], F32), HEAD_DIM // 2)
    return jnp.tile(cos, (1, 2)), jnp.tile(sin * sign, (1, 2))


def _scan_rows(a, b, h0, rev):
    n = a.shape[0]
    groups = n // SUBLANES
    a = a.reshape(groups, SUBLANES, LANES)
    b = b.reshape(groups, SUBLANES, LANES)
    row = lax.broadcasted_iota(I32, a.shape, 1)
    for d in (1, 2, 4):
        shift = SUBLANES - d if rev else d
        m = row < SUBLANES - d if rev else row >= d
        a_s = pltpu.roll(a, shift, 1)
        b_s = pltpu.roll(b, shift, 1)
        b = jnp.where(m, a * b_s + b, b)
        a = jnp.where(m, a * a_s, a)
    outs = [None] * groups
    h = h0
    for g in (range(groups - 1, -1, -1) if rev else range(groups)):
        hg = a[g] * h + b[g]
        outs[g] = hg
        h = hg[0:1] if rev else hg[SUBLANES - 1:SUBLANES]
    return jnp.concatenate(outs, axis=0), h


def _rglru_kernel(xc_ref, gc_ref, xl_ref, gl_ref, cw_ref, cb_ref, wg_ref, bg_ref, lam_ref,
                  yc_ref, yl_ref, rec_c, rec_l, *, tt_c, tt_l):
    cw = cw_ref[...]
    cb = cb_ref[...]

    def coeffs(x_ref, t0, tt, d):
        n = x_ref.shape[1]
        main = x_ref[0, pl.ds(t0, tt), :]
        prev = x_ref[0, pl.ds(pl.multiple_of(jnp.maximum(t0 - SUBLANES, 0), SUBLANES), SUBLANES), :]
        prev = jnp.where(t0 > 0, prev, 0.0)
        nxt = x_ref[0, pl.ds(pl.multiple_of(jnp.minimum(t0 + tt, n - SUBLANES), SUBLANES), SUBLANES), :]
        nxt = jnp.where(t0 + tt < n, nxt, 0.0)
        xe = jnp.concatenate([prev, main, nxt], axis=0)
        o = SUBLANES - CONV_W // 2
        xc = cb
        for j in range(CONV_W):
            xc = xc + cw[j:j + 1] * xe[o + j:o + j + tt]
        g = _dot(xc.astype(BF16), wg_ref[d, 0]) + bg_ref[d, 0]
        i_gate = jax.nn.sigmoid(g[:, :LANES])
        r_gate = jax.nn.sigmoid(g[:, LANES:])
        log_a = LRU_C * r_gate * (-_softplus(-lam_ref[d, 0]))
        a = jnp.exp(log_a)
        bco = jnp.sqrt(-jnp.tanh(log_a) * (a * a + 1.0)) * (i_gate * xc)
        return a, bco

    def sweep(x_ref, tt, d, rev, h, emit):
        nch = x_ref.shape[1] // tt

        def body(i, h):
            ci = nch - 1 - i if rev else i
            t0 = pl.multiple_of(ci * tt, tt)
            a, bco = coeffs(x_ref, t0, tt, d)
            hs, h = _scan_rows(a, bco, h, rev)
            emit(t0, tt, hs)
            return h

        return lax.fori_loop(0, nch, body, h)

    def store_rec(rec):
        def emit(t0, tt, hs):
            rec[pl.ds(t0, tt), :] = hs
        return emit

    def store_out(rec, g_ref, y_ref):
        def emit(t0, tt, hs):
            tot = rec[pl.ds(t0, tt), :] + hs
            y_ref[0, pl.ds(t0, tt), :] = (tot * _gelu_tanh(g_ref[0, pl.ds(t0, tt), :])).astype(y_ref.dtype)
        return emit

    zero = jnp.zeros((1, LANES), F32)
    h = sweep(xc_ref, tt_c, 0, False, zero, store_rec(rec_c))
    sweep(xl_ref, tt_l, 0, False, h, store_rec(rec_l))
    h = sweep(xc_ref, tt_c, 1, True, zero, store_out(rec_c, gc_ref, yc_ref))
    sweep(xl_ref, tt_l, 1, True, h, store_out(rec_l, gl_ref, yl_ref))


def _rglru(xa_c, ga_c, xa_l, ga_l, conv_w, conv_b, lru_wi, lru_bi, lru_wr, lru_br, lru_lam):
    b, lc, _ = xa_c.shape
    t = xa_l.shape[1]
    nt = LRU_WIDTH // LANES
    per_tile = LANES // LRU_BLOCK

    def block_diag(w):
        w = w.reshape(2, nt, per_tile, LRU_BLOCK, LRU_BLOCK)
        eye = jnp.eye(per_tile, dtype=w.dtype)
        return jnp.einsum('dtpij,pq->dtpiqj', w, eye).reshape(2, nt, LANES, LANES)

    wg = jnp.concatenate([block_diag(lru_wi), block_diag(lru_wr)], axis=-1).astype(BF16)
    bg = jnp.concatenate([lru_bi.reshape(2, nt, 1, LANES), lru_br.reshape(2, nt, 1, LANES)], axis=-1)
    lam = lru_lam.reshape(2, nt, 1, LANES)
    tt_c = min(256, lc)
    tt_l = min(256, t)
    seq = lambda n: pl.BlockSpec((1, n, LANES), lambda bi, j: (bi, 0, j))
    return pl.pallas_call(
        functools.partial(_rglru_kernel, tt_c=tt_c, tt_l=tt_l),
        out_shape=(jax.ShapeDtypeStruct((b, lc, LRU_WIDTH), BF16),
                   jax.ShapeDtypeStruct((b, t, LRU_WIDTH), BF16)),
        grid=(b, nt),
        in_specs=[seq(lc), seq(lc), seq(t), seq(t),
                  pl.BlockSpec((CONV_W, LANES), lambda bi, j: (0, j)),
                  pl.BlockSpec((1, LANES), lambda bi, j: (0, j)),
                  pl.BlockSpec((2, 1, LANES, 2 * LANES), lambda bi, j: (0, j, 0, 0)),
                  pl.BlockSpec((2, 1, 1, 2 * LANES), lambda bi, j: (0, j, 0, 0)),
                  pl.BlockSpec((2, 1, 1, LANES), lambda bi, j: (0, j, 0, 0))],
        out_specs=(seq(lc), seq(t)),
        scratch_shapes=[pltpu.VMEM((lc, LANES), F32), pltpu.VMEM((t, LANES), F32)],
        compiler_params=_cparams("parallel", "arbitrary"),
        name="l0_rglru",
    )(xa_c, ga_c, xa_l, ga_l, conv_w, conv_b.reshape(1, LRU_WIDTH), wg, bg, lam)


def _attn_kernel(q_ref, *refs, n_seg, kv_chunk):
    kv = refs[:2 * n_seg]
    o_ref = refs[2 * n_seg]
    g = pl.program_id(1)
    q = q_ref[0]
    tq = q.shape[0]
    lane = lax.broadcasted_iota(I32, (1, KV_WIDTH), 1)
    mine = (lane >= g * HEAD_DIM) & (lane < (g + 1) * HEAD_DIM)
    rows = []
    for h in range(Q_GROUP):
        qh = q[:, h * HEAD_DIM:(h + 1) * HEAD_DIM]
        both = jnp.concatenate([qh] * N_KV_HEADS, axis=1)
        rows.append(jnp.where(mine, both, jnp.zeros_like(both)))
    qpad = jnp.concatenate(rows, axis=0)
    nq = Q_GROUP * tq
    m = jnp.full((1, nq), -jnp.inf, F32)
    acc = jnp.zeros((kv[1].shape[2], nq), F32)
    chunks = []
    for i in range(n_seg):
        tk = kv[2 * i].shape[1]
        ck = min(kv_chunk, tk)
        chunks += [(kv[2 * i], kv[2 * i + 1], c * ck, ck) for c in range(tk // ck)]
    scores = lambda ch: _dot_nt(ch[0][0, ch[2]:ch[2] + ch[3], :], qpad)
    ready = [scores(ch) for ch in chunks[:ATTN_LOOKAHEAD]]
    for j, (_, vt_ref, c0, ck) in enumerate(chunks):
        s = ready.pop(0)
        if j + ATTN_LOOKAHEAD < len(chunks):
            ready.append(scores(chunks[j + ATTN_LOOKAHEAD]))
        m_new = jnp.maximum(m, s.max(axis=0, keepdims=True))
        alpha = jnp.exp2(m - m_new)
        p = jnp.exp2(s - m_new)
        acc = alpha * acc + _dot(vt_ref[0, 0, :, c0:c0 + ck], p.astype(BF16))
        m = m_new
    out_t = acc[:HEAD_DIM] / acc[HEAD_DIM:HEAD_DIM + 1]
    outs = [out_t[:, h * tq:(h + 1) * tq].T for h in range(Q_GROUP)]
    o_ref[0] = jnp.concatenate(outs, axis=1).astype(o_ref.dtype)


def _attention(q, segs):
    b, t, _ = q.shape
    tq = min(ATTN_Q_TILE, t)
    gw = Q_GROUP * HEAD_DIM
    in_specs = [pl.BlockSpec((1, tq, gw), lambda bi, g, i: (bi, i, g))]
    args = [q]
    for k, vt in segs:
        tk = k.shape[1]
        in_specs.append(pl.BlockSpec((1, tk, KV_WIDTH), lambda bi, g, i: (bi, 0, 0)))
        in_specs.append(pl.BlockSpec((1, 1, ATTN_VT_ROWS, tk), lambda bi, g, i: (bi, g, 0, 0)))
        args += [k, vt]
    return pl.pallas_call(
        functools.partial(_attn_kernel, n_seg=len(segs), kv_chunk=ATTN_KV_CHUNK),
        out_shape=jax.ShapeDtypeStruct((b, t, Q_WIDTH), BF16),
        grid=(b, N_KV_HEADS, t // tq),
        in_specs=in_specs,
        out_specs=pl.BlockSpec((1, tq, gw), lambda bi, g, i: (bi, i, g)),
        compiler_params=_cparams("parallel", "parallel", "arbitrary"),
        name="l0_attention",
    )(*args)


def _kv_layout(k, v):
    b, t, _ = k.shape
    vt = v.reshape(b, t, N_KV_HEADS, HEAD_DIM).transpose(0, 2, 3, 1)
    extra = jnp.zeros((b, N_KV_HEADS, ATTN_VT_ROWS - HEAD_DIM, t), v.dtype).at[:, :, 0].set(1)
    return k, jnp.concatenate([vt, extra], axis=2)


def _resmm_kernel(x_ref, g_ref, *refs, n):
    acc = None
    for i in range(n):
        d = _dot(refs[i][0], refs[n + i][...])
        acc = d if acc is None else acc + d
    o_ref = refs[2 * n]
    o_ref[0] = x_ref[0] + g_ref[0] * acc


def _residual_matmul(x, gate, acts, weights):
    b, t, _ = x.shape
    tm = min(512, t)
    n = len(acts)
    tok = lambda w: pl.BlockSpec((1, tm, w), lambda bi, i: (bi, i, 0))
    in_specs = [tok(D_MODEL), pl.BlockSpec((1, 1, D_MODEL), lambda bi, i: (bi, 0, 0))]
    in_specs += [tok(a.shape[-1]) for a in acts]
    in_specs += [pl.BlockSpec(w.shape, lambda bi, i: (0, 0)) for w in weights]
    return pl.pallas_call(
        functools.partial(_resmm_kernel, n=n),
        out_shape=jax.ShapeDtypeStruct((b, t, D_MODEL), F32),
        grid=(b, t // tm),
        in_specs=in_specs,
        out_specs=tok(D_MODEL),
        compiler_params=_cparams("parallel", "arbitrary"),
        name="residual_proj",
    )(x, gate, *acts, *weights)


def _router_kernel(x_ref, sc_ref, sh_ref, gain_ref, wr_ref, h_ref, aff_ref):
    h = _norm_mod(x_ref[0], gain_ref[...], sc_ref[0], sh_ref[0])
    tm = h.shape[0]
    logits = _dot_f32(h, wr_ref[...], 2)
    lane = lax.broadcasted_iota(I32, (tm, ROUTE_PAD), 1)
    z = jnp.where(lane < N_EXPERTS, logits, -jnp.inf)
    e = jnp.exp(z - z.max(axis=1, keepdims=True))
    aff = e / e.sum(axis=1, keepdims=True)
    h_ref[0, :, :D_MODEL] = h.astype(BF16)
    hi = aff.astype(BF16).astype(F32)
    h_ref[0, :, D_MODEL:] = (hi + pltpu.roll(aff - hi, N_EXPERTS, 1)).astype(BF16)
    aff_ref[0] = aff.T[:N_EXPERTS]


def _router(x, sc, sh, gain, w_router):
    b, t, _ = x.shape
    tm = min(512, t)
    wr = jnp.pad(w_router, ((0, 0), (0, ROUTE_PAD - N_EXPERTS)))
    tok = lambda w: pl.BlockSpec((1, tm, w), lambda bi, i: (bi, i, 0))
    per_b = pl.BlockSpec((1, 1, D_MODEL), lambda bi, i: (bi, 0, 0))
    return pl.pallas_call(
        _router_kernel,
        out_shape=(jax.ShapeDtypeStruct((b, t, D_MODEL + ROUTE_PAD), BF16),
                   jax.ShapeDtypeStruct((b, N_EXPERTS, t), F32)),
        grid=(b, t // tm),
        in_specs=[tok(D_MODEL), per_b, per_b,
                  pl.BlockSpec((1, D_MODEL), lambda bi, i: (0, 0)),
                  pl.BlockSpec((D_MODEL, ROUTE_PAD), lambda bi, i: (0, 0))],
        out_specs=(tok(D_MODEL + ROUTE_PAD), pl.BlockSpec((1, N_EXPERTS, tm), lambda bi, i: (bi, 0, i))),
        compiler_params=_cparams("parallel", "arbitrary"),
        name="moe_router",
    )(x, sc, sh, gain.reshape(1, D_MODEL), wr)


def _cumsum_lanes(x01, tri):
    outs = []
    off = jnp.zeros((x01.shape[0], 1), F32)
    for j in range(x01.shape[1] // LANES):
        cj = _dot(x01[:, j * LANES:(j + 1) * LANES].astype(BF16), tri) + off
        outs.append(cj)
        off = cj[:, LANES - 1:LANES]
    return outs[0] if len(outs) == 1 else jnp.concatenate(outs, axis=1)


def _select_kernel(aff_ref, tri_ref, tile_ref, pos_ref, tab_ref, *, cap):
    aff = aff_ref[0]
    n = aff.shape[1]
    keys = pltpu.bitcast(aff, I32)

    def bit_step(i, tau):
        cand = tau | jnp.left_shift(jnp.int32(1), 30 - i)
        cnt = jnp.sum((keys >= cand).astype(I32), axis=1, keepdims=True)
        return jnp.where(cnt >= cap, cand, tau)

    tau = lax.fori_loop(0, 31, bit_step, jnp.zeros((N_EXPERTS, 1), I32))
    gt = keys > tau
    eq = keys == tau
    need = (cap - jnp.sum(gt.astype(I32), axis=1, keepdims=True)).astype(F32)
    tri = tri_ref[...]
    c_eq = _cumsum_lanes(eq.astype(F32), tri)
    sel = gt | (eq & (c_eq <= need))
    c_sel = _cumsum_lanes(sel.astype(F32), tri)
    pos_ref[0] = jnp.where(sel, c_sel - 1.0, -1.0)
    tab_ref[0] = _dot(sel.astype(BF16), tile_ref[...]).astype(I32)


def _select(aff_t, cap):
    b, _, n = aff_t.shape
    assert n % MOE_TILE == 0 and n // MOE_TILE <= LANES // 2
    tri = jnp.asarray(np.triu(np.ones((LANES, LANES), np.float32)), dtype=BF16)
    tok = np.arange(n)[:, None]
    j = np.arange(LANES // 2)[None, :]
    tile_tab = jnp.asarray(np.concatenate([tok < j * MOE_TILE, tok // MOE_TILE == j], axis=1), dtype=BF16)
    return pl.pallas_call(
        functools.partial(_select_kernel, cap=cap),
        out_shape=(jax.ShapeDtypeStruct((b, N_EXPERTS, n), F32),
                   jax.ShapeDtypeStruct((b, N_EXPERTS, LANES), I32)),
        grid=(b,),
        in_specs=[pl.BlockSpec((1, N_EXPERTS, n), lambda bi: (bi, 0, 0)),
                  pl.BlockSpec((LANES, LANES), lambda bi: (0, 0)),
                  pl.BlockSpec((n, LANES), lambda bi: (0, 0))],
        out_specs=(pl.BlockSpec((1, N_EXPERTS, n), lambda bi: (bi, 0, 0)),
                   pl.BlockSpec((1, N_EXPERTS, LANES), lambda bi: (bi, 0, 0))),
        compiler_params=_cparams("parallel"),
        name="moe_select",
    )(aff_t, tri, tile_tab)


def _tile_windows(tab_ref, pos_ref, j, win):
    slot = lax.broadcasted_iota(I32, (win, MOE_TILE), 0).astype(F32)
    out = []
    for e in range(N_EXPERTS):
        start = tab_ref[0, e, j]
        count = tab_ref[0, e, LANES // 2 + j]
        a0 = pl.multiple_of(lax.shift_left(lax.shift_right_logical(start, MOE_ALIGN_LOG2), MOE_ALIGN_LOG2),
                            1 << MOE_ALIGN_LOG2)
        p = pos_ref[0, e:e + 1, :]
        onehot = lambda first, p=p: (p == slot + jnp.asarray(first, F32)).astype(BF16)
        extra = jnp.maximum(start - a0 + count - 1, 0) // win
        out.append((a0, onehot, extra))
    return out


def _moe_gather_kernel(tab_ref, h_ref, pos_ref, o_ref, *, win):
    j = pl.program_id(2)

    @pl.when(j == 0)
    def _():
        o_ref[...] = jnp.zeros(o_ref.shape, o_ref.dtype)

    h = h_ref[0]
    wins = _tile_windows(tab_ref, pos_ref, j, win)
    x = _dot(jnp.concatenate([oh(a0) for a0, oh, _ in wins], axis=0), h).astype(BF16)
    for e, (a0, oh, extra) in enumerate(wins):
        o_ref[0, e, pl.ds(a0, win), :] = o_ref[0, e, pl.ds(a0, win), :] + x[e * win:(e + 1) * win]

        def more(k, c, e=e, a0=a0, oh=oh):
            ak = pl.multiple_of(a0 + (k + 1) * win, 1 << MOE_ALIGN_LOG2)
            o_ref[0, e, pl.ds(ak, win), :] = o_ref[0, e, pl.ds(ak, win), :] + _dot(oh(ak), h).astype(BF16)
            return c

        lax.fori_loop(0, extra, more, 0)


def _moe_gather(h_ext, pos, tab, cap):
    b, n, width = h_ext.shape
    cols = width // MOE_COL_SPLIT
    rows = cap + MOE_SLACK
    assert MOE_SLACK >= MOE_WIN and MOE_WIN % (1 << MOE_ALIGN_LOG2) == 0 and cols % LANES == 0
    return pl.pallas_call(
        functools.partial(_moe_gather_kernel, win=MOE_WIN),
        out_shape=jax.ShapeDtypeStruct((b, N_EXPERTS, rows, width), BF16),
        grid=(b, MOE_COL_SPLIT, n // MOE_TILE),
        in_specs=[pl.BlockSpec((1, N_EXPERTS, LANES), lambda bi, c, j: (bi, 0, 0), memory_space=pltpu.SMEM),
                  pl.BlockSpec((1, MOE_TILE, cols), lambda bi, c, j: (bi, j, c)),
                  pl.BlockSpec((1, N_EXPERTS, MOE_TILE), lambda bi, c, j: (bi, 0, j))],
        out_specs=pl.BlockSpec((1, N_EXPERTS, rows, cols), lambda bi, c, j: (bi, 0, 0, c)),
        compiler_params=_cparams("parallel", "parallel", "arbitrary"),
        name="moe_gather",
    )(tab, h_ext, pos)


def _moe_ffn_kernel(x_ref, wg_ref, wu_ref, wd_ref, y_ref, acc_s, *, cap):
    e = pl.program_id(0)
    f = pl.program_id(2)
    nb = x_ref.shape[0]
    x = x_ref[:, 0].reshape(nb * cap, x_ref.shape[3])
    xb = x[:, :D_MODEL]

    @pl.when(f == 0)
    def _():
        acc_s[...] = jnp.zeros(acc_s.shape, F32)

    g = _dot(xb, wg_ref[0].astype(BF16))
    u = _dot(xb, wu_ref[0].astype(BF16))
    hid = (g * jax.nn.sigmoid(g)) * u
    acc_s[...] += _dot(hid.astype(BF16), wd_ref[0].astype(BF16))

    @pl.when(f == pl.num_programs(2) - 1)
    def _():
        lane = lax.broadcasted_iota(I32, (nb * cap, ROUTE_PAD), 1)
        mine = (lane == e) | (lane == e + N_EXPERTS)
        gate = jnp.sum(jnp.where(mine, x[:, D_MODEL:].astype(F32), 0.0), axis=1, keepdims=True)
        y_ref[:, 0, :cap] = (acc_s[...] * gate).astype(y_ref.dtype).reshape(nb, cap, D_MODEL)
        slack = y_ref.shape[2] - cap
        y_ref[:, 0, cap:] = jnp.zeros((nb, slack, D_MODEL), y_ref.dtype)


def _moe_ffn(xe, wg, wu, wd, cap):
    b, _, rows, width = xe.shape
    nb = b if b * cap <= MOE_FFN_ROWS else 1
    fc = MOE_FFN_SLICE
    return pl.pallas_call(
        functools.partial(_moe_ffn_kernel, cap=cap),
        out_shape=jax.ShapeDtypeStruct((b, N_EXPERTS, rows, D_MODEL), BF16),
        grid=(N_EXPERTS, b // nb, EXPERT_FF // fc),
        in_specs=[pl.BlockSpec((nb, 1, cap, width), lambda e, bi, f: (bi, e, 0, 0)),
                  pl.BlockSpec((1, D_MODEL, fc), lambda e, bi, f: (e, 0, f)),
                  pl.BlockSpec((1, D_MODEL, fc), lambda e, bi, f: (e, 0, f)),
                  pl.BlockSpec((1, fc, D_MODEL), lambda e, bi, f: (e, f, 0))],
        out_specs=pl.BlockSpec((nb, 1, rows, D_MODEL), lambda e, bi, f: (bi, e, 0, 0)),
        scratch_shapes=[pltpu.VMEM((nb * cap, D_MODEL), F32)],
        compiler_params=_cparams("arbitrary", "arbitrary", "arbitrary"),
        name="moe_experts",
    )(xe, wg, wu, wd)


def _moe_combine_kernel(tab_ref, x_ref, gt_ref, pos_ref, y_ref, o_ref, acc_s, *, win):
    j = pl.program_id(2)
    wins = _tile_windows(tab_ref, pos_ref, j, win)
    onehots = jnp.concatenate([oh(a0) for a0, oh, _ in wins], axis=0)
    rows = jnp.concatenate([y_ref[0, e, pl.ds(a0, win), :] for e, (a0, _, _) in enumerate(wins)], axis=0)
    acc_s[...] = _dot_tn(onehots, rows)
    for e, (a0, oh, extra) in enumerate(wins):
        def more(k, c, e=e, a0=a0, oh=oh):
            ak = pl.multiple_of(a0 + (k + 1) * win, 1 << MOE_ALIGN_LOG2)
            acc_s[...] += _dot_tn(oh(ak), y_ref[0, e, pl.ds(ak, win), :])
            return c

        lax.fori_loop(0, extra, more, 0)
    o_ref[0] = x_ref[0] + gt_ref[0] * acc_s[...]


def _moe_combine(x, gt, pos, tab, y):
    b, n, _ = x.shape
    rows = y.shape[2]
    cols = D_MODEL // 2
    tile = lambda: pl.BlockSpec((1, MOE_TILE, cols), lambda bi, c, j: (bi, j, c))
    return pl.pallas_call(
        functools.partial(_moe_combine_kernel, win=MOE_WIN),
        out_shape=jax.ShapeDtypeStruct((b, n, D_MODEL), F32),
        grid=(b, 2, n // MOE_TILE),
        in_specs=[pl.BlockSpec((1, N_EXPERTS, LANES), lambda bi, c, j: (bi, 0, 0), memory_space=pltpu.SMEM),
                  tile(),
                  pl.BlockSpec((1, 1, cols), lambda bi, c, j: (bi, 0, c)),
                  pl.BlockSpec((1, N_EXPERTS, MOE_TILE), lambda bi, c, j: (bi, 0, j)),
                  pl.BlockSpec((1, N_EXPERTS, rows, cols), lambda bi, c, j: (bi, 0, 0, c))],
        out_specs=tile(),
        scratch_shapes=[pltpu.VMEM((MOE_TILE, cols), F32)],
        compiler_params=_cparams("parallel", "parallel", "arbitrary"),
        name="moe_combine",
    )(tab, x, gt, pos, y)


def _moe(x, sc, sh, gt, gain, w_router, wg, wu, wd):
    n = x.shape[1]
    cap = max(1, CAPACITY_FACTOR * n // N_EXPERTS)
    h_ext, aff_t = _router(x, sc, sh, gain, w_router)
    pos, tab = _select(aff_t, cap)
    y = _moe_ffn(_moe_gather(h_ext, pos, tab, cap), wg, wu, wd, cap)
    return _moe_combine(x, gt, pos, tab, y)


def _rwkv_prep_kernel(xm_ref, xp_ref, xn_ref, sc_ref, sh_ref, gain_ref, mu_ref, wr_ref, wk_ref, wv_ref,
                      w1_ref, w2_ref, a1_ref, a2_ref, g1_ref, g2_ref, w0_ref, a0_ref, kks_ref, ka_ref, rk_ref,
                      seg_ref, r_ref, v_ref, g_ref, kk_ref, bonus_ref, lw0_ref, lw1_ref, kd0_ref, kd1_ref,
                      ag0_ref, ag1_ref):
    i = pl.program_id(1)
    last = pl.num_programs(1) - 1
    tm = xm_ref.shape[1]
    gain = gain_ref[...]
    sc = sc_ref[0]
    sh = sh_ref[0]
    xe = jnp.concatenate([xp_ref[0], xm_ref[0], xn_ref[0]], axis=0)
    he = _norm_mod(xe, gain, sc, sh)
    row = lax.broadcasted_iota(I32, (tm, 1), 0)
    h = he[SUBLANES:SUBLANES + tm]
    hm1 = jnp.where((row == 0) & (i == 0), 0.0, he[SUBLANES - 1:SUBLANES - 1 + tm])
    hp1 = jnp.where((row == tm - 1) & (i == last), 0.0, he[SUBLANES + 1:SUBLANES + 1 + tm])
    xx = 0.5 * (hm1 + hp1) - h
    mu = mu_ref[...]
    mix = lambda j: (h + xx * mu[j:j + 1]).astype(BF16)
    r = _dot(mix(0), wr_ref[...])
    k = _dot(mix(2), wk_ref[...])
    v = _dot(mix(3), wv_ref[...])
    tw = jnp.tanh(_dot(mix(1), w1_ref[...])).astype(BF16)
    la = _dot(mix(4), a1_ref[...]).astype(BF16)
    g = _dot(jax.nn.sigmoid(_dot(mix(5), g1_ref[...])).astype(BF16), g2_ref[...])
    seg = seg_ref[...]
    kkv = k * kks_ref[...]
    kk = kkv * lax.rsqrt(jnp.maximum(_head_sum(kkv * kkv, seg, 1), 1e-24))
    r_ref[0] = r.astype(r_ref.dtype)
    v_ref[0] = v.astype(v_ref.dtype)
    g_ref[0] = g.astype(g_ref.dtype)
    kk_ref[0] = kk.astype(kk_ref.dtype)
    bonus = None
    for d, (lw_ref, kd_ref, ag_ref) in enumerate(((lw0_ref, kd0_ref, ag0_ref), (lw1_ref, kd1_ref, ag1_ref))):
        w_pre = w0_ref[d] + _dot(tw, w2_ref[d])
        lw_ref[0] = jax.nn.sigmoid(w_pre) * np.float32(-np.exp(-0.5))
        a = jax.nn.sigmoid(a0_ref[d] + _dot(la, a2_ref[d]))
        kd = k * (1.0 + (a - 1.0) * ka_ref[...])
        kd_ref[0] = kd.astype(kd_ref.dtype)
        ag_ref[0] = a.astype(ag_ref.dtype)
        bd = _head_sum(r * kd * rk_ref[...], seg, 1) * v
        bonus = bd if bonus is None else bonus + bd
    bonus_ref[0] = bonus


def _rwkv_prep(x, sc, sh, gain, p):
    b, t, _ = x.shape
    tm = min(512, t)
    nb8 = tm // SUBLANES
    tok = lambda: pl.BlockSpec((1, tm, D_MODEL), lambda bi, i: (bi, i, 0))
    prev = pl.BlockSpec((1, SUBLANES, D_MODEL), lambda bi, i: (bi, jnp.maximum(i * nb8 - 1, 0), 0))
    nxt = pl.BlockSpec((1, SUBLANES, D_MODEL),
                       lambda bi, i: (bi, jnp.minimum((i + 1) * nb8, t // SUBLANES - 1), 0))
    per_b = pl.BlockSpec((1, 1, D_MODEL), lambda bi, i: (bi, 0, 0))
    const = lambda a: pl.BlockSpec(a.shape, lambda bi, i: (0,) * a.ndim, pipeline_mode=pl.Buffered(1))
    consts = [p['gain'], p['mu'], p['w_r'], p['w_k'], p['w_v'], p['w1'], p['w2'], p['a1'], p['a2'], p['g1'],
              p['g2'], p['w0'], p['a0'], p['kks'], p['ka'], p['rk'], p['seg']]
    bf = jax.ShapeDtypeStruct((b, t, D_MODEL), BF16)
    f32 = jax.ShapeDtypeStruct((b, t, D_MODEL), F32)
    return pl.pallas_call(
        _rwkv_prep_kernel,
        out_shape=(bf, bf, bf, bf, f32, f32, f32, bf, bf, bf, bf),
        grid=(b, t // tm),
        in_specs=[tok(), prev, nxt, per_b, per_b] + [const(a) for a in consts],
        out_specs=tuple(tok() for _ in range(11)),
        compiler_params=_cparams("parallel", "arbitrary"),
        name="l1_rwkv_prep",
    )(x, x, x, sc, sh, *consts)


def _rwkv_params(gain, mu, w_r, w_k, w_v, w0, w1, w2, a0, a1, a2, g1, g2, k_k, k_a, r_k):
    def pad_dir(w):
        z = jnp.zeros_like(w[0])
        return jnp.stack([jnp.concatenate([w[0], z], axis=0), jnp.concatenate([z, w[1]], axis=0)])

    row = lambda a: a.reshape(1, D_MODEL)
    return dict(
        gain=row(gain), mu=mu, w_r=w_r.astype(BF16), w_k=w_k.astype(BF16), w_v=w_v.astype(BF16),
        w1=jnp.concatenate([w1[0], w1[1]], axis=1).astype(BF16), w2=pad_dir(w2).astype(BF16),
        a1=jnp.concatenate([a1[0], a1[1]], axis=1).astype(BF16), a2=pad_dir(a2).astype(BF16),
        g1=jnp.pad(g1, ((0, 0), (0, GATE_LORA_PAD - GATE_LORA))).astype(BF16),
        g2=jnp.pad(g2, ((0, GATE_LORA_PAD - GATE_LORA), (0, 0))).astype(BF16),
        w0=w0.reshape(2, 1, D_MODEL), a0=a0.reshape(2, 1, D_MODEL),
        kks=row(k_k), ka=row(k_a), rk=row(r_k), seg=_seg_ones(LANES))


SOLVE_BASE = 8
WKV_SUB_CHUNKS = 4


def _solve_masks(n, top):
    rowi = lax.broadcasted_iota(I32, (n, n), 0)
    coli = lax.broadcasted_iota(I32, (n, n), 1)
    same = lambda shift: (rowi >> shift) == (coli >> shift)
    k = int(np.log2(SOLVE_BASE))
    levels = range(k, int(np.log2(top)))
    return (jnp.where(rowi == coli, 1.0, 0.0), same(k),
            [same(j + 1) & jnp.logical_not(same(j)) for j in levels])


def _solve_unit_tri(nmat, x, masks):
    eye, base, joins = masks
    m = jnp.where(base, nmat, 0.0).astype(BF16)
    tinv = eye + m.astype(F32)
    for _ in range(int(np.log2(SOLVE_BASE)) - 1):
        m = _dot(m, m).astype(BF16)
        yield None
        tinv = tinv + _dot(m, tinv.astype(BF16))
        yield None
    for join in joins:
        off = jnp.where(join, nmat, 0.0).astype(BF16)
        tb = tinv.astype(BF16)
        half = _dot(tb, off).astype(BF16)
        yield None
        tinv = tinv + _dot(half, tb)
        yield None
    yield _dot(tinv.astype(BF16), x.astype(BF16))


def _wkv_stage(r_ref, v_ref, kk_ref, lw_ref, kd_ref, ag_ref, tri, ws, zs, vs, gts, d, rev, chunk, rows, k):
    lw = lw_ref[0, rows]
    hi, lo = _split2(lw)
    cum = _dot(tri, hi) + _dot(tri, lo)
    g_in = jnp.exp(cum)
    g_ex = jnp.exp(cum - lw)
    g_inv = jnp.exp(-cum)
    kk = kk_ref[0, rows].astype(F32)
    a_t = -(kk * g_ex)
    r_t = r_ref[0, rows].astype(F32) * g_in
    b_t = kk * ag_ref[0, rows].astype(F32) * g_inv
    k_t = kd_ref[0, rows].astype(F32) * g_inv
    g_tot = jnp.exp(cum[0:1] if rev else cum[chunk - 1:chunk])
    first = (lax.broadcasted_iota(I32, (1, D_MODEL), 1) % LANES) < RWKV_HEAD
    stacked = lambda x: (jnp.where(first, x, jnp.zeros_like(x)), jnp.where(first, jnp.zeros_like(x), x))
    a2, r2, b2, k2 = (stacked(x.astype(BF16)) for x in (a_t, r_t, b_t, k_t))
    v2 = stacked(v_ref[0, rows])
    c2 = 2 * chunk
    for p in range(HEAD_PAIRS):
        sl = slice(p * LANES, (p + 1) * LANES)
        for h in range(2):
            ws[k, d, p, h * chunk:(h + 1) * chunk] = a2[h][:, sl]
            ws[k, d, p, c2 + h * chunk:c2 + (h + 1) * chunk] = r2[h][:, sl]
            zs[k, d, p, h * chunk:(h + 1) * chunk] = b2[h][:, sl]
            zs[k, d, p, c2 + h * chunk:c2 + (h + 1) * chunk] = k2[h][:, sl]
            vs[k, d, p, h * chunk:(h + 1) * chunk] = v2[h][:, sl]
        gts[k, d, p] = g_tot[:, sl]


def _wkv_masks(chunk, rev):
    n = 2 * chunk
    ti = lax.broadcasted_iota(I32, (n, 2 * n), 0) & (chunk - 1)
    tj = lax.broadcasted_iota(I32, (n, 2 * n), 1) & (chunk - 1)
    strict = ti < tj if rev else ti > tj
    incl = ti <= tj if rev else ti >= tj
    return strict, incl, _solve_masks(n, chunk)


def _wkv_chain(s_ref, y_ref, ws, zs, vs, gts, d, p, masks, chunk, rows, k):
    n = 2 * chunk
    w = ws[k, d, p]
    z = zs[k, d, p]
    v = vs[k, d, p]
    s_old = s_ref[d, p]
    pm = _dot_nt(w, z)
    yield
    wst = _dot_nt(w, s_old.astype(BF16))
    yield
    strict, incl, solve_masks = masks
    top = jnp.where(strict, pm[:n], 0.0)
    nmat = top[:, :n]
    ak = top[:, n:].astype(BF16)
    rbk = jnp.where(incl, pm[n:], 0.0).astype(BF16)
    x = wst[:n] + _dot(ak, v)
    yield
    u = None
    for u in _solve_unit_tri(nmat, x, solve_masks):
        yield
    uv = jnp.concatenate([u.astype(BF16), v], axis=0)
    y = wst[n:] + _dot(rbk, uv)
    yield
    if y_ref is not None:
        y_ref[0, rows, p * LANES:(p + 1) * LANES] = y[:chunk] + y[chunk:]
    ds = _dot_tn(uv, z)
    yield
    s_ref[d, p] = (s_old + ds) * gts[k, d, p]


def _wkv_kernel(*refs, chunk, emit_y, has_init):
    ins = refs[:12]
    tri_ref = refs[12]
    pos = 13
    if has_init:
        s0_ref = refs[pos]
        pos += 1
    yf_ref = yb_ref = None
    if emit_y:
        yf_ref, yb_ref = refs[pos:pos + 2]
        pos += 2
    sfin_ref = refs[pos]
    s_ref, ws, zs, vs, gts = refs[pos + 1:]
    i = pl.program_id(1)

    @pl.when(i == 0)
    def _():
        if has_init:
            s_ref[...] = s0_ref[0]
        else:
            s_ref[...] = jnp.zeros(s_ref.shape, F32)

    n_sub = ins[0].shape[1] // chunk

    def rows_of(d, k):
        j = n_sub - 1 - k if d else k
        return slice(j * chunk, (j + 1) * chunk)

    def stage(k):
        for d, rev in ((0, False), (1, True)):
            _wkv_stage(*ins[6 * d:6 * d + 6], tri_ref[d], ws, zs, vs, gts, d, rev, chunk, rows_of(d, k), k)

    masks = [_wkv_masks(chunk, False), _wkv_masks(chunk, True)]
    stage(0)
    for k in range(n_sub):
        chains = [_wkv_chain(s_ref, (yf_ref, yb_ref)[d], ws, zs, vs, gts, d, p, masks[d], chunk, rows_of(d, k), k)
                  for p in range(HEAD_PAIRS) for d in range(2)]
        first_round = True
        while chains:
            alive = []
            for ch in chains:
                try:
                    next(ch)
                    alive.append(ch)
                except StopIteration:
                    pass
            chains = alive
            if first_round and k + 1 < n_sub:
                stage(k + 1)
            first_round = False

    @pl.when(i == pl.num_programs(1) - 1)
    def _():
        sfin_ref[0] = s_ref[...]


def _wkv(r, v, kk, lw, kd, ag, s0, emit_y, chunk=64):
    b, t, _ = r.shape
    n_sub = min(WKV_SUB_CHUNKS, t // chunk)
    n = t // (n_sub * chunk)
    assert chunk == RWKV_HEAD and t == n * n_sub * chunk
    fwd = pl.BlockSpec((1, n_sub * chunk, D_MODEL), lambda bi, i: (bi, i, 0))
    bwd = pl.BlockSpec((1, n_sub * chunk, D_MODEL), lambda bi, i: (bi, n - 1 - i, 0))
    tri = jnp.asarray(np.stack([np.tril(np.ones((chunk, chunk), np.float32)),
                                np.triu(np.ones((chunk, chunk), np.float32))]), dtype=BF16)
    state_shape = (2, HEAD_PAIRS, 2 * RWKV_HEAD, LANES)
    state_spec = pl.BlockSpec((1,) + state_shape, lambda bi, i: (bi, 0, 0, 0, 0))
    args = [r, v, kk, lw[0], kd[0], ag[0], r, v, kk, lw[1], kd[1], ag[1], tri]
    in_specs = [fwd] * 6 + [bwd] * 6 + [pl.BlockSpec((2, chunk, chunk), lambda bi, i: (0, 0, 0))]
    if s0 is not None:
        args.append(s0)
        in_specs.append(state_spec)
    out_shape = [jax.ShapeDtypeStruct((b,) + state_shape, F32)]
    out_specs = [state_spec]
    if emit_y:
        out_shape = [jax.ShapeDtypeStruct((b, t, D_MODEL), F32)] * 2 + out_shape
        out_specs = [fwd, bwd] + out_specs
    res = pl.pallas_call(
        functools.partial(_wkv_kernel, chunk=chunk, emit_y=emit_y, has_init=s0 is not None),
        out_shape=tuple(out_shape),
        grid=(b, n),
        in_specs=in_specs,
        out_specs=tuple(out_specs),
        scratch_shapes=[pltpu.VMEM(state_shape, F32),
                        pltpu.VMEM((n_sub, 2, HEAD_PAIRS, 4 * chunk, LANES), BF16),
                        pltpu.VMEM((n_sub, 2, HEAD_PAIRS, 4 * chunk, LANES), BF16),
                        pltpu.VMEM((n_sub, 2, HEAD_PAIRS, 2 * chunk, LANES), BF16),
                        pltpu.VMEM((n_sub, 2, HEAD_PAIRS, 1, LANES), F32)],
        compiler_params=_cparams("parallel", "arbitrary"),
        name="l1_wkv_scan",
    )(*args)
    return res


def _readout_kernel(x_ref, gt_ref, yf_ref, yb_ref, bonus_ref, g_ref, gng_ref, gnb_ref, seg_ref, wo_ref, o_ref):
    seg = seg_ref[...]
    y = yf_ref[0] + yb_ref[0]
    mean = _head_sum(y, seg) * (1.0 / RWKV_HEAD)
    c = y - mean
    var = _head_sum(c * c, seg, 1) * (1.0 / RWKV_HEAD)
    yn = c * lax.rsqrt(var + GN_EPS) * gng_ref[...] + gnb_ref[...]
    out = (yn + bonus_ref[0]) * g_ref[0].astype(F32)
    o_ref[0] = x_ref[0] + gt_ref[0] * _dot(out.astype(BF16), wo_ref[...])


def _readout(x, gt, y_f, y_b, bonus, g, gn_g, gn_b, w_o):
    b, t, _ = x.shape
    tm = min(256, t)
    tok = pl.BlockSpec((1, tm, D_MODEL), lambda bi, i: (bi, i, 0))
    row = pl.BlockSpec((1, D_MODEL), lambda bi, i: (0, 0))
    return pl.pallas_call(
        _readout_kernel,
        out_shape=jax.ShapeDtypeStruct((b, t, D_MODEL), F32),
        grid=(b, t // tm),
        in_specs=[tok, pl.BlockSpec((1, 1, D_MODEL), lambda bi, i: (bi, 0, 0)), tok, tok, tok, tok, row, row,
                  pl.BlockSpec((LANES, LANES), lambda bi, i: (0, 0)),
                  pl.BlockSpec((D_MODEL, D_MODEL), lambda bi, i: (0, 0))],
        out_specs=tok,
        compiler_params=_cparams("parallel", "arbitrary"),
        name="l1_readout",
    )(x, gt, y_f, y_b, bonus, g, gn_g.reshape(1, D_MODEL), gn_b.reshape(1, D_MODEL), _seg_ones(LANES), w_o)


def _modulation(c, c_ctx, w_mod, b_mod):
    b = c.shape[0]
    cond = jnp.zeros((SUBLANES, D_MODEL), F32).at[:b].set(c).at[b].set(c_ctx)
    m = _ada(cond, w_mod, b_mod)
    lat = [m[:b, j * D_MODEL:(j + 1) * D_MODEL].reshape(b, 1, D_MODEL) for j in range(6)]
    ctx = [jnp.broadcast_to(m[b, j * D_MODEL:(j + 1) * D_MODEL].reshape(1, 1, D_MODEL), (b, 1, D_MODEL))
           for j in range(6)]
    return lat, ctx


def kernel(x, c, ctx, c_ctx, l0_w_mod, l0_b_mod, l0_norm_mix, l0_norm_ffn, l0_w_in, l0_conv_w, l0_conv_b, l0_lru_wi, l0_lru_bi, l0_lru_wr, l0_lru_br, l0_lru_lam, l0_q_gain, l0_k_gain, l0_w_out, l0_router, l0_we_gate, l0_we_up, l0_we_down, l1_w_mod, l1_b_mod, l1_norm_mix, l1_norm_ffn, l1_mu, l1_w_r, l1_w_k, l1_w_v, l1_w0, l1_w1, l1_w2, l1_a0, l1_a1, l1_a2, l1_g1, l1_g2, l1_k_k, l1_k_a, l1_r_k, l1_gn_g, l1_gn_b, l1_w_o, l1_router, l1_we_gate, l1_we_up, l1_we_down):
    t = x.shape[1]

    (sh1, sc1, gt1, sh2, sc2, gt2), (csh1, csc1, cgt1, csh2, csc2, cgt2) = _modulation(c, c_ctx, l0_w_mod, l0_b_mod)
    w_in = l0_w_in.astype(BF16)
    cos_t, sin_t = _rope_tables(t)
    xa_l, ga_l, q_l, k_l, v_l = _proj0(x, sc1, sh1, l0_norm_mix, w_in, l0_q_gain, l0_k_gain, cos_t, sin_t, True)
    lc = ctx.shape[1]
    xa_c, ga_c, q_c, k_c, v_c = _proj0(ctx, csc1, csh1, l0_norm_mix, w_in, l0_q_gain, l0_k_gain,
                                       cos_t[:lc], sin_t[:lc], False)
    ya_c, ya_l = _rglru(xa_c, ga_c, xa_l, ga_l, l0_conv_w, l0_conv_b, l0_lru_wi, l0_lru_bi, l0_lru_wr,
                        l0_lru_br, l0_lru_lam)
    seg_l = _kv_layout(k_l, v_l)
    seg_c = _kv_layout(k_c, v_c)
    yb_l = _attention(q_l, [seg_l, seg_c])
    yb_c = _attention(q_c, [seg_c])
    w_out = l0_w_out.astype(BF16)
    w_oa, w_ob = w_out[:LRU_WIDTH], w_out[LRU_WIDTH:]
    x = _residual_matmul(x, gt1, [ya_l, yb_l], [w_oa, w_ob])
    ctx = _residual_matmul(ctx, cgt1, [ya_c, yb_c], [w_oa, w_ob])
    x = _moe(x, sc2, sh2, gt2, l0_norm_ffn, l0_router, l0_we_gate, l0_we_up, l0_we_down)
    ctx = _moe(ctx, csc2, csh2, cgt2, l0_norm_ffn, l0_router, l0_we_gate, l0_we_up, l0_we_down)

    (sh1, sc1, gt1, sh2, sc2, gt2), (csh1, csc1, _, _, _, _) = _modulation(c, c_ctx, l1_w_mod, l1_b_mod)
    p = _rwkv_params(l1_norm_mix, l1_mu, l1_w_r, l1_w_k, l1_w_v, l1_w0, l1_w1, l1_w2, l1_a0, l1_a1, l1_a2,
                     l1_g1, l1_g2, l1_k_k, l1_k_a, l1_r_k)
    r_c, v_c, _, kk_c, _, lw0_c, lw1_c, kd0_c, kd1_c, ag0_c, ag1_c = _rwkv_prep(ctx, csc1, csh1, l1_norm_mix, p)
    r_l, v_l, g_l, kk_l, bonus_l, lw0, lw1, kd0, kd1, ag0, ag1 = _rwkv_prep(x, sc1, sh1, l1_norm_mix, p)
    (s_ctx,) = _wkv(r_c, v_c, kk_c, (lw0_c, lw1_c), (kd0_c, kd1_c), (ag0_c, ag1_c), None, False)
    y_f, y_b, _ = _wkv(r_l, v_l, kk_l, (lw0, lw1), (kd0, kd1), (ag0, ag1), s_ctx, True)
    x = _readout(x, gt1, y_f, y_b, bonus_l, g_l, l1_gn_g, l1_gn_b, l1_w_o.astype(BF16))
    x = _moe(x, sc2, sh2, gt2, l1_norm_ffn, l1_router, l1_we_gate, l1_we_up, l1_we_down)
    return x
```
